```python
import jax, jax.numpy as jnp
from jax import lax
import numpy as np

D_MODEL = 1024
BATCH = 8
SEQ = 4096
DEPTH = 2

GRID_W = 64
CTX_LEN = 256

CONV_W = D_MODEL // 4
CONV_TAPS = 3
RET_W = D_MODEL // 2
RET_HEADS = 8
RET_DK = RET_W // RET_HEADS
RET_DV = RET_W // RET_HEADS
RET_CHUNK = 128
SGU_W = D_MODEL // 4
SGU_GROUPS = 4
SGU_CHUNK = 128
MIX_W = CONV_W + RET_W + SGU_W

OFF_CONV_B = 0
OFF_CONV_C = OFF_CONV_B + CONV_W
OFF_CONV_X = OFF_CONV_C + CONV_W
OFF_RET_Q = OFF_CONV_X + CONV_W
OFF_RET_K = OFF_RET_Q + RET_W
OFF_RET_V = OFF_RET_K + RET_W
OFF_RET_G = OFF_RET_V + RET_W
OFF_SGU_U = OFF_RET_G + RET_W
OFF_SGU_V = OFF_SGU_U + SGU_W
IN_COLS = OFF_SGU_V + SGU_W
IN_SPLITS = (OFF_CONV_C, OFF_CONV_X, OFF_RET_Q, OFF_RET_K, OFF_RET_V, OFF_RET_G, OFF_SGU_U, OFF_SGU_V)

N_GROUPS = 4
EXPERTS_PER_GROUP = 8
N_EXPERTS = N_GROUPS * EXPERTS_PER_GROUP
TOP_K_IN_GROUP = 2
EXPERT_HIDDEN = D_MODEL // 2
DISPATCH_BLOCK = 128

N_MOD = 6
LN_EPS = 1e-5
DEEPNORM_ALPHA = (2 * DEPTH) ** 0.25
DEEPNORM_BETA = (8 * DEPTH) ** -0.25

kernel_name = "hybrid_dit_conv_retention_sgu_hmoe"


def _standardize(x):
    xf = x.astype(jnp.float32)
    mu = jnp.mean(xf, axis=-1, keepdims=True)
    var = jnp.mean(jnp.square(xf - mu), axis=-1, keepdims=True)
    return (xf - mu) * lax.rsqrt(var + LN_EPS)


def layer_norm(x, gain, bias):
    return (_standardize(x) * gain + bias).astype(x.dtype)


def conv3(z, w, axis):
    pad = [(0, 0)] * z.ndim
    pad[axis] = (1, 1)
    zp = jnp.pad(z, pad)
    n = z.shape[axis]
    out = w[0] * lax.slice_in_dim(zp, 0, n, axis=axis)
    out = out + w[1] * lax.slice_in_dim(zp, 1, n + 1, axis=axis)
    return out + w[2] * lax.slice_in_dim(zp, 2, n + 2, axis=axis)


def short_conv_grid(z, w):
    bsz, length, cw = z.shape
    rows = length // GRID_W
    zg = z.reshape(bsz, rows, GRID_W, cw)
    half = cw // 2
    yh = conv3(zg[..., :half], w[:, :half], axis=2)
    yv = conv3(zg[..., half:], w[:, half:], axis=1)
    return jnp.concatenate([yh, yv], axis=-1).reshape(bsz, length, cw)


def retention_scan(q, k, v, log_gamma, s0):
    f32 = jnp.float32
    bsz, length = q.shape[:2]
    nc = length // RET_CHUNK
    qc = q.astype(f32).reshape(bsz, nc, RET_CHUNK, RET_HEADS, RET_DK)
    kc = k.astype(f32).reshape(bsz, nc, RET_CHUNK, RET_HEADS, RET_DK)
    vc = v.astype(f32).reshape(bsz, nc, RET_CHUNK, RET_HEADS, RET_DV)
    lg = log_gamma.astype(f32)
    pos = jnp.arange(RET_CHUNK, dtype=f32)
    rel = pos[:, None] - pos[None, :]
    intra_decay = jnp.where(rel >= 0, jnp.exp(lg[:, None, None] * jnp.maximum(rel, 0.0)), 0.0)
    scores = jnp.einsum('bnihd,bnjhd->bnhij', qc, kc) * intra_decay
    intra = jnp.einsum('bnhij,bnjhe->bnihe', scores, vc)
    k_decay = jnp.exp((RET_CHUNK - 1 - pos)[:, None] * lg)
    chunk_kv = jnp.einsum('bnjhd,jh,bnjhe->nbhde', kc, k_decay, vc)
    chunk_decay = jnp.exp(RET_CHUNK * lg)[None, :, None, None]

    def step(state, kv):
        return chunk_decay * state + kv, state

    s_final, s_prev = lax.scan(step, s0.astype(f32), chunk_kv)
    q_decay = jnp.exp((pos + 1.0)[:, None] * lg)
    cross = jnp.einsum('bnihd,ih,nbhde->bnihe', qc, q_decay, s_prev)
    o = (intra + cross).reshape(bsz, length, RET_HEADS, RET_DV)
    return o, s_final


def retention_final_state(k, v, log_gamma):
    f32 = jnp.float32
    length = k.shape[1]
    pos = jnp.arange(length, dtype=f32)
    w = jnp.exp((length - 1 - pos)[:, None] * log_gamma.astype(f32))
    return jnp.einsum('blhd,lh,blhe->bhde', k.astype(f32), w, v.astype(f32))


def spatial_gating(u, v, w_s, b_s):
    bsz, length, width = v.shape
    nc = length // SGU_CHUNK
    vn = _standardize(v).astype(v.dtype)
    vn = vn.reshape(bsz, nc, SGU_CHUNK, SGU_GROUPS, width // SGU_GROUPS)
    mixed = jnp.einsum('gpq,bnqgc->bnpgc', w_s, vn) + b_s.T[:, :, None]
    return u * mixed.reshape(bsz, length, width)


def hybrid_mixer(h, w_in, w_out, conv_w, lg_f, lg_b, sgu_w, sgu_b, s0_f, s0_b, on_grid):
    bsz, length, _ = h.shape
    p = h @ w_in
    cb, cc, cx, q, k, v, g, u, vv = jnp.split(p, IN_SPLITS, axis=-1)
    z = cc * cx
    conv = short_conv_grid(z, conv_w) if on_grid else conv3(z, conv_w, axis=1)
    ya = cb * conv
    q = q.reshape(bsz, length, RET_HEADS, RET_DK)
    k = k.reshape(bsz, length, RET_HEADS, RET_DK) * (RET_DK ** -0.5)
    v = v.reshape(bsz, length, RET_HEADS, RET_DV)
    o_f, s_f = retention_scan(q, k, v, lg_f, s0_f)
    o_b, s_b = retention_scan(q[:, ::-1], k[:, ::-1], v[:, ::-1], lg_b, s0_b)
    o = _standardize(o_f + o_b[:, ::-1]).astype(h.dtype)
    yb = jax.nn.silu(g) * o.reshape(bsz, length, RET_W)
    yc = spatial_gating(u, vv, sgu_w, sgu_b)
    y = jnp.concatenate([ya, yb, yc], axis=-1) @ w_out
    return y, s_f, s_b


def context_retention_states(hc, w_in, lg_f, lg_b):
    bsz, length, _ = hc.shape
    k = (hc @ w_in[:, OFF_RET_K:OFF_RET_V]).reshape(bsz, length, RET_HEADS, RET_DK) * (RET_DK ** -0.5)
    v = (hc @ w_in[:, OFF_RET_V:OFF_RET_G]).reshape(bsz, length, RET_HEADS, RET_DV)
    return retention_final_state(k, v, lg_f), retention_final_state(k[:, ::-1], v[:, ::-1], lg_b)


def grouped_experts(xf, expert, token, gate, w_gate, w_up, w_down):
    n_tok, d = xf.shape
    m = expert.shape[0]
    order = jnp.argsort(expert)
    se = expert[order]
    counts = jnp.bincount(expert, length=N_EXPERTS)
    starts = jnp.cumsum(counts) - counts
    pcounts = (counts + DISPATCH_BLOCK - 1) // DISPATCH_BLOCK * DISPATCH_BLOCK
    pends = jnp.cumsum(pcounts)
    pstarts = pends - pcounts
    dest = pstarts[se] + (jnp.arange(m) - starts[se])
    n_blocks = -(-m // DISPATCH_BLOCK) + N_EXPERTS
    n_slots = n_blocks * DISPATCH_BLOCK
    slot_tok = jnp.full((n_slots,), n_tok, dtype=jnp.int32).at[dest].set(token[order].astype(jnp.int32))
    slot_gate = jnp.zeros((n_slots,), xf.dtype).at[dest].set(gate[order])
    blk_e = jnp.minimum(jnp.searchsorted(pends, jnp.arange(n_blocks) * DISPATCH_BLOCK, side='right'), N_EXPERTS - 1)
    xpad = jnp.concatenate([xf, jnp.zeros((1, d), xf.dtype)], axis=0)
    xb = xpad[slot_tok].reshape(n_blocks, DISPATCH_BLOCK, d)

    def expert_block(args):
        xblk, e = args
        return (jax.nn.silu(xblk @ w_gate[e]) * (xblk @ w_up[e])) @ w_down[e]

    yb = lax.map(expert_block, (xb, blk_e)).reshape(n_slots, d)
    y = jax.ops.segment_sum(yb * slot_gate[:, None], slot_tok, num_segments=n_tok + 1)
    return y[:n_tok]


def hier_moe(h, rg_w, rg_b, re_w, re_b, w_gate, w_up, w_down):
    bsz, length, d = h.shape
    xf = h.reshape(bsz * length, d)
    n_tok = xf.shape[0]
    pg = jax.nn.softmax((xf @ rg_w).astype(jnp.float32) + rg_b.astype(jnp.float32), axis=-1)
    g_idx = jnp.argmax(pg, axis=-1)
    g_prob = jnp.max(pg, axis=-1, keepdims=True)
    le_all = jnp.einsum('nd,gde->nge', xf, re_w).astype(jnp.float32) + re_b.astype(jnp.float32)
    le = le_all[jnp.arange(n_tok), g_idx]
    pe = jax.nn.softmax(le, axis=-1)
    top_p, top_i = lax.top_k(pe, TOP_K_IN_GROUP)
    gate = g_prob * top_p / jnp.sum(top_p, axis=-1, keepdims=True)
    expert = g_idx[:, None] * EXPERTS_PER_GROUP + top_i
    token = jnp.broadcast_to(jnp.arange(n_tok)[:, None], expert.shape)
    y = grouped_experts(xf, expert.reshape(-1), token.reshape(-1), gate.reshape(-1).astype(xf.dtype),
                        w_gate, w_up, w_down)
    return y.reshape(bsz, length, d)


def setup_inputs(seed: int = 0) -> dict:
    key = jax.random.key(seed)
    ks = jax.random.split(key, 32)
    f32 = jnp.float32

    def nrm(k, shape, scale):
        return jax.random.normal(k, shape, f32) * scale

    gamma_logit = jnp.asarray(np.log(2.0 ** (5 + np.arange(RET_HEADS)) - 1.0), f32)
    d = D_MODEL
    return {
        'x': nrm(ks[0], (BATCH, SEQ, d), 1.0),
        'c': nrm(ks[1], (BATCH, d), 1.0),
        'ctx': nrm(ks[2], (BATCH, CTX_LEN, d), 1.0),
        'c_ctx': nrm(ks[3], (d,), 1.0),
        'w_ada': nrm(ks[4], (DEPTH, d, N_MOD * d), 0.5 * d ** -0.5),
        'b_ada': nrm(ks[5], (DEPTH, N_MOD * d), 0.02),
        'w_in': nrm(ks[6], (DEPTH, d, IN_COLS), d ** -0.5),
        'conv_w': nrm(ks[7], (DEPTH, CONV_TAPS, CONV_W), CONV_TAPS ** -0.5),
        'ret_decay_fwd': gamma_logit + nrm(ks[8], (DEPTH, RET_HEADS), 0.1),
        'ret_decay_bwd': gamma_logit + nrm(ks[9], (DEPTH, RET_HEADS), 0.1),
        'sgu_w': nrm(ks[10], (DEPTH, SGU_GROUPS, SGU_CHUNK, SGU_CHUNK), SGU_CHUNK ** -0.5),
        'sgu_b': 1.0 + nrm(ks[11], (DEPTH, SGU_GROUPS, SGU_CHUNK), 0.02),
        'w_out': nrm(ks[12], (DEPTH, MIX_W, d), DEEPNORM_BETA * MIX_W ** -0.5),
        'ln1_g': 1.0 + nrm(ks[13], (DEPTH, d), 0.02),
        'ln1_b': nrm(ks[14], (DEPTH, d), 0.02),
        'router_group_w': nrm(ks[15], (DEPTH, d, N_GROUPS), d ** -0.5),
        'router_group_b': nrm(ks[16], (DEPTH, N_GROUPS), 0.01),
        'router_expert_w': nrm(ks[17], (DEPTH, N_GROUPS, d, EXPERTS_PER_GROUP), d ** -0.5),
        'router_expert_b': nrm(ks[18], (DEPTH, N_GROUPS, EXPERTS_PER_GROUP), 0.01),
        'moe_w_gate': nrm(ks[19], (DEPTH, N_EXPERTS, d, EXPERT_HIDDEN), d ** -0.5),
        'moe_w_up': nrm(ks[20], (DEPTH, N_EXPERTS, d, EXPERT_HIDDEN), d ** -0.5),
        'moe_w_down': nrm(ks[21], (DEPTH, N_EXPERTS, EXPERT_HIDDEN, d), DEEPNORM_BETA * EXPERT_HIDDEN ** -0.5),
        'ln2_g': 1.0 + nrm(ks[22], (DEPTH, d), 0.02),
        'ln2_b': nrm(ks[23], (DEPTH, d), 0.02),
    }


def reference(x, c, ctx, c_ctx, w_ada, b_ada, w_in, conv_w, ret_decay_fwd, ret_decay_bwd, sgu_w, sgu_b,
              w_out, ln1_g, ln1_b, router_group_w, router_group_b, router_expert_w, router_expert_b,
              moe_w_gate, moe_w_up, moe_w_down, ln2_g, ln2_b):
    alpha = DEEPNORM_ALPHA
    bsz = x.shape[0]
    for l in range(DEPTH):
        last = l == DEPTH - 1
        mod = (jax.nn.silu(c) @ w_ada[l] + b_ada[l])[:, None, :]
        sh1, sc1, g1, sh2, sc2, g2 = jnp.split(mod, N_MOD, axis=-1)
        mod_c = jax.nn.silu(c_ctx) @ w_ada[l] + b_ada[l]
        csh1, csc1, cg1, csh2, csc2, cg2 = jnp.split(mod_c, N_MOD)
        lg_f = jax.nn.log_sigmoid(ret_decay_fwd[l].astype(jnp.float32))
        lg_b = jax.nn.log_sigmoid(ret_decay_bwd[l].astype(jnp.float32))

        hc = ctx * (1.0 + csc1) + csh1
        if last:
            s_f, s_b = context_retention_states(hc, w_in[l], lg_f, lg_b)
        else:
            s_zero = jnp.zeros((bsz, RET_HEADS, RET_DK, RET_DV), jnp.float32)
            yc, s_f, s_b = hybrid_mixer(hc, w_in[l], w_out[l], conv_w[l], lg_f, lg_b, sgu_w[l], sgu_b[l],
                                        s_zero, s_zero, on_grid=False)
            ctx = layer_norm(alpha * ctx + cg1 * yc, ln1_g[l], ln1_b[l])
            hm = ctx * (1.0 + csc2) + csh2
            ctx = layer_norm(alpha * ctx + cg2 * hier_moe(hm, router_group_w[l], router_group_b[l],
                                                          router_expert_w[l], router_expert_b[l],
                                                          moe_w_gate[l], moe_w_up[l], moe_w_down[l]),
                             ln2_g[l], ln2_b[l])

        h = x * (1.0 + sc1) + sh1
        y, _, _ = hybrid_mixer(h, w_in[l], w_out[l], conv_w[l], lg_f, lg_b, sgu_w[l], sgu_b[l],
                               s_f, s_b, on_grid=True)
        x = layer_norm(alpha * x + g1 * y, ln1_g[l], ln1_b[l])
        hm = x * (1.0 + sc2) + sh2
        x = layer_norm(alpha * x + g2 * hier_moe(hm, router_group_w[l], router_group_b[l],
                                                 router_expert_w[l], router_expert_b[l],
                                                 moe_w_gate[l], moe_w_up[l], moe_w_down[l]),
                       ln2_g[l], ln2_b[l])
    return x
```

```python
import functools

import jax
import jax.numpy as jnp
from jax import lax
from jax.experimental import pallas as pl
from jax.experimental.pallas import tpu as pltpu

F32 = jnp.float32
BF16 = jnp.bfloat16

D_MODEL = 1024
DEPTH = 2
GRID_W = 64
CONV_W = 256
RET_W = 512
RET_HEADS = 8
RET_DK = 64
CHUNK = 128
SGU_W = 256
SGU_GROUPS = 4
IN_COLS = 3 * CONV_W + 4 * RET_W + 2 * SGU_W
N_GROUPS = 4
EXPERTS_PER_GROUP = 8
N_EXPERTS = N_GROUPS * EXPERTS_PER_GROUP
EXPERT_HIDDEN = 512
N_MOD = 6
MOD_ROWS = 8
LN_EPS = 1e-5
ALPHA = (2 * DEPTH) ** 0.25

TILE = 256
HALO = GRID_W
BLOCK_M = 256
ROUTE_W = 128
VMEM_LIMIT = 56 * 1024 * 1024


def _params(n_axes):
    return pltpu.CompilerParams(dimension_semantics=("arbitrary",) * n_axes, vmem_limit_bytes=VMEM_LIMIT)


def _standardize(v):
    mu = jnp.mean(v, axis=-1, keepdims=True)
    var = jnp.mean(jnp.square(v - mu), axis=-1, keepdims=True)
    return (v - mu) * lax.rsqrt(var + LN_EPS)


def _silu(v):
    return v * jax.nn.sigmoid(v)


def _ada_kernel(c_ref, w_ref, b_ref, o_ref):
    a = _silu(c_ref[...]).astype(BF16)
    o_ref[0] = jnp.dot(a, w_ref[0].astype(BF16), preferred_element_type=F32) + b_ref[0]


def _ada_call(cond, w_ada, b_ada):
    rows = cond.shape[0]
    cols = w_ada.shape[-1]
    tn = 1536
    return pl.pallas_call(
        _ada_kernel,
        grid=(DEPTH, cols // tn),
        in_specs=[
            pl.BlockSpec((rows, D_MODEL), lambda l, j: (0, 0)),
            pl.BlockSpec((1, D_MODEL, tn), lambda l, j: (l, 0, j)),
            pl.BlockSpec((1, 1, tn), lambda l, j: (l, 0, j)),
        ],
        out_specs=pl.BlockSpec((1, rows, tn), lambda l, j: (l, 0, j)),
        out_shape=jax.ShapeDtypeStruct((DEPTH, rows, cols), F32),
        compiler_params=_params(2),
    )(cond, w_ada, b_ada.reshape(DEPTH, 1, cols))


def _inproj_kernel(x_ref, mod_ref, w_ref, pc_ref, q_ref, k_ref, v_ref, g_ref, su_ref, sv_ref):
    mod = mod_ref[0, 0, 0]
    h = (x_ref[0] * (1.0 + mod[1:2]) + mod[0:1]).astype(BF16)

    def proj(lo, hi):
        return jnp.dot(h, w_ref[0, :, lo:hi], preferred_element_type=F32)

    o = 3 * CONV_W
    pc_ref[0] = proj(0, o)
    q_ref[0] = proj(o, o + RET_W)
    k_ref[0] = proj(o + RET_W, o + 2 * RET_W) * (RET_DK ** -0.5)
    v_ref[0] = proj(o + 2 * RET_W, o + 3 * RET_W)
    g_ref[0] = proj(o + 3 * RET_W, o + 4 * RET_W)
    o += 4 * RET_W
    su_ref[0] = proj(o, o + SGU_W)
    sv_ref[0] = proj(o + SGU_W, o + 2 * SGU_W)


def _inproj_call(layer, xa, mod, w_in_bf, n_lat_tiles):
    bsz, length, _ = xa.shape
    widths = (3 * CONV_W, RET_W, RET_W, RET_W, RET_W, SGU_W, SGU_W)
    return pl.pallas_call(
        _inproj_kernel,
        grid=(bsz, length // TILE),
        in_specs=[
            pl.BlockSpec((1, TILE, D_MODEL), lambda b, i: (b, i, 0)),
            pl.BlockSpec((1, 1, 1, MOD_ROWS, D_MODEL), lambda b, i: (layer, b, jnp.minimum(i // n_lat_tiles, 1), 0, 0)),
            pl.BlockSpec((1, D_MODEL, IN_COLS), lambda b, i: (layer, 0, 0)),
        ],
        out_specs=[pl.BlockSpec((1, TILE, w), lambda b, i: (b, i, 0)) for w in widths],
        out_shape=[jax.ShapeDtypeStruct((bsz, length, w), F32) for w in widths],
        compiler_params=_params(2),
    )(xa, mod, w_in_bf)


def _state_kernel(kf_ref, vf_ref, kb_ref, vb_ref, lgf_ref, lgb_ref, lgfc_ref, lgbc_ref, sf_ref, sb_ref, stf, stb):
    @pl.when(pl.program_id(1) == 0)
    def _():
        stf[...] = jnp.zeros_like(stf)
        stb[...] = jnp.zeros_like(stb)

    sf_ref[0, 0] = stf[...]
    sb_ref[0, 0] = stb[...]
    pos = lax.broadcasted_iota(jnp.int32, (CHUNK, 1), 0).astype(F32)
    kf = (kf_ref[0] * jnp.exp((CHUNK - 1.0 - pos) * lgf_ref[0])).astype(BF16)
    kb = (kb_ref[0] * jnp.exp(pos * lgb_ref[0])).astype(BF16)
    vf = vf_ref[0].astype(BF16)
    vb = vb_ref[0].astype(BF16)
    decay_f = jnp.exp(CHUNK * lgfc_ref[0])
    decay_b = jnp.exp(CHUNK * lgbc_ref[0])
    contract_rows = (((0,), (0,)), ((), ()))
    for h in range(RET_HEADS):
        sl = slice(h * RET_DK, (h + 1) * RET_DK)
        stf[sl, :] = decay_f[sl] * stf[sl, :] + lax.dot_general(kf[:, sl], vf[:, sl], contract_rows,
                                                                 preferred_element_type=F32)
        stb[sl, :] = decay_b[sl] * stb[sl, :] + lax.dot_general(kb[:, sl], vb[:, sl], contract_rows,
                                                                 preferred_element_type=F32)


def _state_call(layer, k, v, lg_rows, lg_cols, n_lat_chunks):
    bsz, length, _ = k.shape
    nc = length // CHUNK

    def fwd(s):
        return (s + n_lat_chunks) % nc

    def bwd(s):
        return nc - 1 - s

    kv_spec = lambda order: pl.BlockSpec((1, CHUNK, RET_W), lambda b, s: (b, order(s), 0))
    lg_row = lambda d: pl.BlockSpec((1, 1, RET_W), lambda b, s: (2 * layer + d, 0, 0))
    lg_col = lambda d: pl.BlockSpec((1, RET_W, 1), lambda b, s: (2 * layer + d, 0, 0))
    st_spec = lambda order: pl.BlockSpec((1, 1, RET_W, RET_DK), lambda b, s: (b, order(s), 0, 0))
    return pl.pallas_call(
        _state_kernel,
        grid=(bsz, nc),
        in_specs=[kv_spec(fwd), kv_spec(fwd), kv_spec(bwd), kv_spec(bwd), lg_row(0), lg_row(1), lg_col(0), lg_col(1)],
        out_specs=[st_spec(fwd), st_spec(bwd)],
        out_shape=[jax.ShapeDtypeStruct((bsz, nc, RET_W, RET_DK), F32)] * 2,
        scratch_shapes=[pltpu.VMEM((RET_W, RET_DK), F32), pltpu.VMEM((RET_W, RET_DK), F32)],
        compiler_params=_params(2),
    )(k, v, k, v, lg_rows, lg_rows, lg_cols, lg_cols)


def _route(logits):
    lane = lax.broadcasted_iota(jnp.int32, logits.shape, 1)
    big = jnp.int32(ROUTE_W)
    neg = jnp.float32(-jnp.inf)

    def top(vals):
        m = jnp.max(vals, axis=-1, keepdims=True)
        idx = jnp.min(jnp.where(vals == m, lane, big), axis=-1, keepdims=True)
        return m, idx

    gl = jnp.where(lane < N_GROUPS, logits, neg)
    gmax, gidx = top(gl)
    g_prob = 1.0 / jnp.sum(jnp.exp(gl - gmax), axis=-1, keepdims=True)
    lo = N_GROUPS + EXPERTS_PER_GROUP * gidx
    el = jnp.where((lane >= lo) & (lane < lo + EXPERTS_PER_GROUP), logits, neg)
    m1, i1 = top(el)
    m2, i2 = top(jnp.where(lane == i1, neg, el))
    e2 = jnp.exp(m2 - m1)
    gate1 = g_prob * (1.0 / (1.0 + e2))
    gate2 = g_prob * (e2 / (1.0 + e2))
    out = jnp.where(lane == 0, (i1 - N_GROUPS).astype(F32), 0.0)
    out = jnp.where(lane == 1, (i2 - N_GROUPS).astype(F32), out)
    out = jnp.where(lane == 2, gate1, out)
    return jnp.where(lane == 3, gate2, out)


def _mix_kernel(n_lat_tiles, lgf_ref, lgb_ref, pc_ref, hp_ref, hn_ref, q_ref, k_ref, v_ref, g_ref, su_ref, sv_ref,
                sf_ref, sb_ref, x_ref, mod_ref, convw_ref, sguw_ref, sgub_ref, wout_ref, ln_ref, wr_ref, br_ref,
                x1_ref, hm_ref, route_ref, ycat):
    i = pl.program_id(1)
    is_ctx = i >= n_lat_tiles
    row = lax.broadcasted_iota(jnp.int32, (TILE, 1), 0)

    pc = pc_ref[0]
    z = pc[:, CONV_W:2 * CONV_W] * pc[:, 2 * CONV_W:3 * CONV_W]
    line_mask = jnp.where(is_ctx, TILE - 1, GRID_W - 1)
    first = (row & line_mask) == 0
    last = (row & line_mask) == line_mask
    z_prev = jnp.where(first, 0.0, pltpu.roll(z, 1, 0))
    z_next = jnp.where(last, 0.0, pltpu.roll(z, TILE - 1, 0))
    hp = hp_ref[0]
    hn = hn_ref[0]
    z_top = jnp.where(i == 0, 0.0, hp[:, CONV_W:2 * CONV_W] * hp[:, 2 * CONV_W:3 * CONV_W])
    z_bot = jnp.where(i == n_lat_tiles - 1, 0.0, hn[:, CONV_W:2 * CONV_W] * hn[:, 2 * CONV_W:3 * CONV_W])
    zcat = jnp.concatenate([z_top, z, z_bot], axis=0)
    z_up = zcat[0:TILE]
    z_down = zcat[2 * HALO:2 * HALO + TILE]
    along_seq = lax.broadcasted_iota(jnp.int32, (1, CONV_W), 1) < jnp.where(is_ctx, CONV_W, CONV_W // 2)
    cw = convw_ref[0]
    conv = cw[0:1] * jnp.where(along_seq, z_prev, z_up) + cw[1:2] * z
    conv = conv + cw[2:3] * jnp.where(along_seq, z_next, z_down)
    ycat[:, 0:CONV_W] = (pc[:, 0:CONV_W] * conv).astype(BF16)

    vn = _standardize(sv_ref[0]).astype(BF16)
    group = lax.broadcasted_iota(jnp.int32, (1, SGU_W), 1) // (SGU_W // SGU_GROUPS)
    for c in range(TILE // CHUNK):
        rows = slice(c * CHUNK, (c + 1) * CHUNK)
        mixed = jnp.zeros((CHUNK, SGU_W), F32)
        for gi in range(SGU_GROUPS):
            m = jnp.dot(sguw_ref[0, gi], vn[rows], preferred_element_type=F32)
            mixed = jnp.where(group == gi, m, mixed)
        ycat[rows, CONV_W + RET_W:] = (su_ref[0, rows, :] * (mixed + sgub_ref[0])).astype(BF16)

    pos_i = lax.broadcasted_iota(jnp.int32, (CHUNK, 1), 0).astype(F32)
    rel = pos_i - lax.broadcasted_iota(jnp.int32, (1, CHUNK), 1).astype(F32)
    for h in range(RET_HEADS):
        lgf = lgf_ref[h]
        lgb = lgb_ref[h]
        decay = jnp.where(rel > 0, jnp.exp(lgf * jnp.maximum(rel, 0.0)),
                          jnp.where(rel < 0, jnp.exp(lgb * jnp.maximum(-rel, 0.0)), 2.0))
        q_decay_f = jnp.exp(lgf * (pos_i + 1.0))
        q_decay_b = jnp.exp(lgb * (CHUNK - pos_i))
        cols = slice(h * RET_DK, (h + 1) * RET_DK)
        for c in range(TILE // CHUNK):
            rows = slice(c * CHUNK, (c + 1) * CHUNK)
            qh = q_ref[0, rows, cols]
            kh = k_ref[0, rows, cols].astype(BF16)
            vh = v_ref[0, rows, cols].astype(BF16)
            scores = lax.dot_general(qh.astype(BF16), kh, (((1,), (1,)), ((), ())), preferred_element_type=F32)
            o = jnp.dot((scores * decay).astype(BF16), vh, preferred_element_type=F32)
            o = o + jnp.dot((qh * q_decay_f).astype(BF16), sf_ref[0, c, cols, :].astype(BF16),
                            preferred_element_type=F32)
            o = o + jnp.dot((qh * q_decay_b).astype(BF16), sb_ref[0, c, cols, :].astype(BF16),
                            preferred_element_type=F32)
            gh = g_ref[0, rows, cols]
            ycat[rows, CONV_W + h * RET_DK:CONV_W + (h + 1) * RET_DK] = (_silu(gh) * _standardize(o)).astype(BF16)

    y = jnp.dot(ycat[...], wout_ref[0], preferred_element_type=F32)
    mod = mod_ref[0, 0, 0]
    ln = ln_ref[0]
    x1 = _standardize(ALPHA * x_ref[0] + mod[2:3] * y) * ln[0:1] + ln[1:2]
    x1_ref[0] = x1
    hm = x1 * (1.0 + mod[4:5]) + mod[3:4]
    hm_ref[0] = hm
    logits = jnp.dot(hm.astype(BF16), wr_ref[0], preferred_element_type=F32) + br_ref[0]
    route_ref[0] = _route(logits)


def _mix_call(layer, n_lat_tiles, n_proc, lgf, lgb, pc, q, k, v, g, su, sv, sf, sb, xa, mod, conv_w, sgu_w_bf, sgu_bias,
              w_out_bf, ln1, w_route_bf, b_route):
    bsz, length, _ = xa.shape
    halos_per_tile = TILE // HALO
    n_halo = length // HALO
    cpt = TILE // CHUNK
    tok = lambda w: pl.BlockSpec((1, TILE, w), lambda b, i, *_: (b, i, 0))
    per_layer = lambda *shape: pl.BlockSpec((1,) + shape, lambda b, i, *_: (layer,) + (0,) * len(shape))
    grid_spec = pltpu.PrefetchScalarGridSpec(
        num_scalar_prefetch=2,
        grid=(bsz, n_proc),
        in_specs=[
            tok(3 * CONV_W),
            pl.BlockSpec((1, HALO, 3 * CONV_W), lambda b, i, *_: (b, jnp.maximum(i * halos_per_tile - 1, 0), 0)),
            pl.BlockSpec((1, HALO, 3 * CONV_W),
                         lambda b, i, *_: (b, jnp.minimum((i + 1) * halos_per_tile, n_halo - 1), 0)),
            tok(RET_W), tok(RET_W), tok(RET_W), tok(RET_W), tok(SGU_W), tok(SGU_W),
            pl.BlockSpec((1, cpt, RET_W, RET_DK), lambda b, i, *_: (b, i, 0, 0)),
            pl.BlockSpec((1, cpt, RET_W, RET_DK), lambda b, i, *_: (b, i, 0, 0)),
            tok(D_MODEL),
            pl.BlockSpec((1, 1, 1, MOD_ROWS, D_MODEL),
                         lambda b, i, *_: (layer, b, jnp.minimum(i // n_lat_tiles, 1), 0, 0)),
            per_layer(8, CONV_W),
            per_layer(SGU_GROUPS, CHUNK, CHUNK),
            per_layer(CHUNK, SGU_W),
            per_layer(D_MODEL, D_MODEL),
            per_layer(8, D_MODEL),
            per_layer(D_MODEL, ROUTE_W),
            per_layer(1, ROUTE_W),
        ],
        out_specs=[tok(D_MODEL), tok(D_MODEL), tok(ROUTE_W)],
        scratch_shapes=[pltpu.VMEM((TILE, D_MODEL), BF16)],
    )
    return pl.pallas_call(
        functools.partial(_mix_kernel, n_lat_tiles),
        grid_spec=grid_spec,
        out_shape=[jax.ShapeDtypeStruct((bsz, n_proc * TILE, D_MODEL), F32),
                   jax.ShapeDtypeStruct((bsz, n_proc * TILE, D_MODEL), F32),
                   jax.ShapeDtypeStruct((bsz, n_proc * TILE, ROUTE_W), F32)],
        compiler_params=_params(2),
    )(lgf, lgb, pc, pc, pc, q, k, v, g, su, sv, sf, sb, xa, mod, conv_w, sgu_w_bf, sgu_bias, w_out_bf, ln1,
      w_route_bf, b_route)


def _dispatch_kernel(dest_ref, hm_ref, xs_in_ref, xs_ref, sem):
    del xs_in_ref
    base = pl.program_id(0) * (2 * TILE)

    def issue(r, carry):
        for kk in range(2):
            pltpu.make_async_copy(hm_ref.at[pl.ds(r, 1)], xs_ref.at[pl.ds(dest_ref[base + 2 * r + kk], 1)], sem).start()
        return carry

    lax.fori_loop(0, TILE, issue, 0, unroll=8)

    def drain(r, carry):
        for _ in range(2):
            pltpu.make_async_copy(hm_ref.at[pl.ds(0, 1)], xs_ref.at[pl.ds(0, 1)], sem).wait()
        return carry

    lax.fori_loop(0, TILE, drain, 0, unroll=8)


def _dispatch_call(dest, hm_flat, n_slots):
    n_tok = hm_flat.shape[0]
    grid_spec = pltpu.PrefetchScalarGridSpec(
        num_scalar_prefetch=1,
        grid=(n_tok // TILE,),
        in_specs=[pl.BlockSpec((TILE, D_MODEL), lambda i, *_: (i, 0)),
                  pl.BlockSpec(memory_space=pl.ANY)],
        out_specs=pl.BlockSpec(memory_space=pl.ANY),
        scratch_shapes=[pltpu.SemaphoreType.DMA],
    )
    return pl.pallas_call(
        _dispatch_kernel,
        grid_spec=grid_spec,
        out_shape=jax.ShapeDtypeStruct((n_slots, D_MODEL), F32),
        input_output_aliases={2: 0},
        compiler_params=_params(1),
    )(dest, hm_flat, jnp.zeros((n_slots, D_MODEL), F32))


def _expert_kernel(blk_e_ref, n_used_ref, xs_ref, wg_ref, wu_ref, wd_ref, ys_ref, wg_bf, wu_bf, wd_bf):
    i = pl.program_id(0)
    changed = (i == 0) | (blk_e_ref[i] != blk_e_ref[jnp.maximum(i - 1, 0)])

    @pl.when(changed)
    def _():
        wg_bf[...] = wg_ref[0, 0].astype(BF16)
        wu_bf[...] = wu_ref[0, 0].astype(BF16)
        wd_bf[...] = wd_ref[0, 0].astype(BF16)

    @pl.when(i < n_used_ref[0])
    def _():
        xb = xs_ref[...].astype(BF16)
        gate = jnp.dot(xb, wg_bf[...], preferred_element_type=F32)
        up = jnp.dot(xb, wu_bf[...], preferred_element_type=F32)
        ys_ref[...] = jnp.dot((_silu(gate) * up).astype(BF16), wd_bf[...], preferred_element_type=F32)

    @pl.when(i >= n_used_ref[0])
    def _():
        ys_ref[...] = jnp.zeros_like(ys_ref)


def _expert_call(layer, blk_e, n_used, xs, w_gate, w_up, w_down):
    n_blocks = xs.shape[0] // BLOCK_M
    grid_spec = pltpu.PrefetchScalarGridSpec(
        num_scalar_prefetch=2,
        grid=(n_blocks,),
        in_specs=[
            pl.BlockSpec((BLOCK_M, D_MODEL), lambda i, be, nu: (jnp.minimum(i, nu[0] - 1), 0)),
            pl.BlockSpec((1, 1, D_MODEL, EXPERT_HIDDEN), lambda i, be, nu: (layer, be[i], 0, 0)),
            pl.BlockSpec((1, 1, D_MODEL, EXPERT_HIDDEN), lambda i, be, nu: (layer, be[i], 0, 0)),
            pl.BlockSpec((1, 1, EXPERT_HIDDEN, D_MODEL), lambda i, be, nu: (layer, be[i], 0, 0)),
        ],
        out_specs=pl.BlockSpec((BLOCK_M, D_MODEL), lambda i, be, nu: (i, 0)),
        scratch_shapes=[pltpu.VMEM((D_MODEL, EXPERT_HIDDEN), BF16), pltpu.VMEM((D_MODEL, EXPERT_HIDDEN), BF16),
                        pltpu.VMEM((EXPERT_HIDDEN, D_MODEL), BF16)],
    )
    return pl.pallas_call(
        _expert_kernel,
        grid_spec=grid_spec,
        out_shape=jax.ShapeDtypeStruct(xs.shape, F32),
        compiler_params=_params(1),
    )(blk_e, n_used, xs, w_gate, w_up, w_down)


def _combine_kernel(tiles_per_batch, dest_ref, ys_ref, route_ref, x1_ref, mod_ref, ln_ref, out_ref, buf, sem):
    step = pl.program_id(0) * tiles_per_batch + pl.program_id(1)
    slot = step % 2
    has_next = step + 1 < pl.num_programs(0) * tiles_per_batch

    def gather(tile, to_slot):
        base = tile * (2 * TILE)

        def issue(r, carry):
            for kk in range(2):
                pltpu.make_async_copy(ys_ref.at[pl.ds(dest_ref[base + 2 * r + kk], 1)],
                                      buf.at[to_slot, kk, pl.ds(r, 1)], sem.at[to_slot]).start()
            return carry

        lax.fori_loop(0, TILE, issue, 0, unroll=8)

    @pl.when(step == 0)
    def _():
        gather(step, slot)

    @pl.when(has_next)
    def _():
        gather(step + 1, 1 - slot)

    def drain(r, carry):
        for kk in range(2):
            pltpu.make_async_copy(ys_ref.at[pl.ds(0, 1)], buf.at[slot, kk, pl.ds(0, 1)], sem.at[slot]).wait()
        return carry

    lax.fori_loop(0, TILE, drain, 0, unroll=8)

    route = route_ref[0]
    y = route[:, 2:3] * buf[slot, 0] + route[:, 3:4] * buf[slot, 1]
    mod = mod_ref[0, 0, 0]
    ln = ln_ref[0]
    out_ref[0] = _standardize(ALPHA * x1_ref[0] + mod[5:6] * y) * ln[0:1] + ln[1:2]


def _combine_call(layer, n_lat_tiles, dest, ys, route, x1, mod, ln2):
    bsz, length, _ = x1.shape
    tiles_per_batch = length // TILE
    tok = lambda w: pl.BlockSpec((1, TILE, w), lambda b, i, *_: (b, i, 0))
    grid_spec = pltpu.PrefetchScalarGridSpec(
        num_scalar_prefetch=1,
        grid=(bsz, tiles_per_batch),
        in_specs=[
            pl.BlockSpec(memory_space=pl.ANY),
            tok(ROUTE_W),
            tok(D_MODEL),
            pl.BlockSpec((1, 1, 1, MOD_ROWS, D_MODEL),
                         lambda b, i, *_: (layer, b, jnp.minimum(i // n_lat_tiles, 1), 0, 0)),
            pl.BlockSpec((1, 8, D_MODEL), lambda b, i, *_: (layer, 0, 0)),
        ],
        out_specs=tok(D_MODEL),
        scratch_shapes=[pltpu.VMEM((2, 2, TILE, D_MODEL), F32), pltpu.SemaphoreType.DMA((2,))],
    )
    return pl.pallas_call(
        functools.partial(_combine_kernel, tiles_per_batch),
        grid_spec=grid_spec,
        out_shape=jax.ShapeDtypeStruct(x1.shape, F32),
        compiler_params=_params(2),
    )(dest, ys, route, x1, mod, ln2)


def _dispatch_plan(route):
    expert = route[..., 0:2].astype(jnp.int32).reshape(-1)
    m = expert.shape[0]
    onehot = (expert[:, None] == jnp.arange(N_EXPERTS, dtype=jnp.int32)[None, :]).astype(jnp.int32)
    running = jnp.cumsum(onehot, axis=0)
    rank = jnp.sum(onehot * running, axis=1) - 1
    counts = running[-1]
    pcounts = (counts + BLOCK_M - 1) // BLOCK_M * BLOCK_M
    pends = jnp.cumsum(pcounts)
    pstarts = pends - pcounts
    dest = (pstarts[expert] + rank).astype(jnp.int32)
    n_blocks = m // BLOCK_M + N_EXPERTS
    blk_e = jnp.minimum(jnp.searchsorted(pends, jnp.arange(n_blocks, dtype=jnp.int32) * BLOCK_M, side='right'),
                        N_EXPERTS - 1).astype(jnp.int32)
    n_used = (pends[-1:] // BLOCK_M).astype(jnp.int32)
    return dest, blk_e, n_used, n_blocks * BLOCK_M


def _pad_rows(a, rows):
    return jnp.pad(a, [(0, 0)] * (a.ndim - 2) + [(0, rows - a.shape[-2]), (0, 0)])


def kernel(x, c, ctx, c_ctx, w_ada, b_ada, w_in, conv_w, ret_decay_fwd, ret_decay_bwd, sgu_w, sgu_b, w_out, ln1_g, ln1_b,
           router_group_w, router_group_b, router_expert_w, router_expert_b, moe_w_gate, moe_w_up, moe_w_down, ln2_g,
           ln2_b):
    bsz, seq, d = x.shape
    ctx_len = ctx.shape[1]
    assert d == D_MODEL and ctx_len == TILE and seq % TILE == 0 and seq % GRID_W == 0
    n_lat_tiles = seq // TILE
    n_lat_chunks = seq // CHUNK

    cond = _pad_rows(jnp.concatenate([c, c_ctx[None, :]], axis=0), 16)
    ada = _ada_call(cond, w_ada, b_ada)
    mod_lat = ada[:, :bsz].reshape(DEPTH, bsz, N_MOD, d)
    mod_ctx = jnp.broadcast_to(ada[:, bsz].reshape(DEPTH, 1, N_MOD, d), mod_lat.shape)
    mod = _pad_rows(jnp.stack([mod_lat, mod_ctx], axis=2), MOD_ROWS)

    w_in_bf = w_in.astype(BF16)
    w_out_bf = w_out.astype(BF16)
    sgu_w_bf = sgu_w.astype(BF16)
    sgu_bias = jnp.repeat(jnp.swapaxes(sgu_b, 1, 2), SGU_W // SGU_GROUPS, axis=2)
    conv_w8 = _pad_rows(conv_w, 8)
    ln1 = _pad_rows(jnp.stack([ln1_g, ln1_b], axis=1), 8)
    ln2 = _pad_rows(jnp.stack([ln2_g, ln2_b], axis=1), 8)
    w_route = jnp.concatenate([router_group_w, jnp.swapaxes(router_expert_w, 1, 2).reshape(DEPTH, d, N_EXPERTS)], axis=2)
    w_route_bf = jnp.pad(w_route, ((0, 0), (0, 0), (0, ROUTE_W - w_route.shape[2]))).astype(BF16)
    b_route = jnp.concatenate([router_group_b, router_expert_b.reshape(DEPTH, N_EXPERTS)], axis=1)
    b_route = jnp.pad(b_route, ((0, 0), (0, ROUTE_W - b_route.shape[1])))[:, None, :].astype(F32)
    lg = jnp.stack([jax.nn.log_sigmoid(ret_decay_fwd.astype(F32)), jax.nn.log_sigmoid(ret_decay_bwd.astype(F32))],
                   axis=1)
    lg_lanes = jnp.repeat(lg, RET_DK, axis=2).reshape(DEPTH * 2, RET_W)
    lg_rows = lg_lanes[:, None, :]
    lg_cols = lg_lanes[:, :, None]

    xa = jnp.concatenate([x, ctx], axis=1)
    for layer in range(DEPTH):
        last = layer == DEPTH - 1
        pc, q, k, v, g, su, sv = _inproj_call(layer, xa, mod, w_in_bf, n_lat_tiles)
        sf, sb = _state_call(layer, k, v, lg_rows, lg_cols, n_lat_chunks)
        n_proc = n_lat_tiles if last else n_lat_tiles + 1
        x1, hm, route = _mix_call(layer, n_lat_tiles, n_proc, lg[layer, 0], lg[layer, 1], pc, q, k, v, g, su, sv, sf, sb, xa,
                                  mod, conv_w8, sgu_w_bf, sgu_bias, w_out_bf, ln1, w_route_bf, b_route)
        dest, blk_e, n_used, n_slots = _dispatch_plan(route)
        xs = _dispatch_call(dest, hm.reshape(-1, d), n_slots)
        ys = _expert_call(layer, blk_e, n_used, xs, moe_w_gate, moe_w_up, moe_w_down)
        xa = _combine_call(layer, n_lat_tiles, dest, ys, route, x1, mod, ln2)
    return xa
```

```python
import functools

import jax
import jax.numpy as jnp
import numpy as np
from jax import lax
from jax.experimental import pallas as pl
from jax.experimental.pallas import tpu as pltpu

F32 = jnp.float32
BF16 = jnp.bfloat16

D_MODEL = 1024
DEPTH = 2
GRID_W = 64
CONV_W = 256
RET_W = 512
RET_HEADS = 8
RET_DK = 64
CHUNK = 128
SGU_W = 256
SGU_GROUPS = 4
IN_COLS = 3 * CONV_W + 4 * RET_W + 2 * SGU_W
N_GROUPS = 4
EXPERTS_PER_GROUP = 8
N_EXPERTS = N_GROUPS * EXPERTS_PER_GROUP
EXPERT_HIDDEN = 512
N_MOD = 6
MOD_ROWS = 8
LN_EPS = 1e-5
ALPHA = (2 * DEPTH) ** 0.25

TILE = 256
HALO = GRID_W
BLOCK_M = 128
ROUTE_W = 128
ROW_W = D_MODEL + ROUTE_W
PAIRS_PER_GROUP = EXPERTS_PER_GROUP * (EXPERTS_PER_GROUP - 1) // 2
N_CLASSES = N_GROUPS * PAIRS_PER_GROUP
RANK_TILE = 512
CLASS_LO = np.array([g * EXPERTS_PER_GROUP + lo for g in range(N_GROUPS) for lo in range(EXPERTS_PER_GROUP)
                     for hi in range(lo + 1, EXPERTS_PER_GROUP)] + [N_EXPERTS - 2] * (ROUTE_W - N_CLASSES), np.int32)
CLASS_HI = np.array([g * EXPERTS_PER_GROUP + hi for g in range(N_GROUPS) for lo in range(EXPERTS_PER_GROUP)
                     for hi in range(lo + 1, EXPERTS_PER_GROUP)] + [N_EXPERTS - 1] * (ROUTE_W - N_CLASSES), np.int32)
VMEM_LIMIT = 56 * 1024 * 1024


def _params(n_axes):
    return pltpu.CompilerParams(dimension_semantics=("arbitrary",) * n_axes, vmem_limit_bytes=VMEM_LIMIT)


def _standardize(v):
    mu = jnp.mean(v, axis=-1, keepdims=True)
    var = jnp.mean(jnp.square(v - mu), axis=-1, keepdims=True)
    return (v - mu) * lax.rsqrt(var + LN_EPS)


def _silu(v):
    return v * jax.nn.sigmoid(v)


def _ada_kernel(c_ref, w_ref, b_ref, o_ref):
    a = _silu(c_ref[...]).astype(BF16)
    o_ref[0] = jnp.dot(a, w_ref[0].astype(BF16), preferred_element_type=F32) + b_ref[0]


def _ada_call(cond, w_ada, b_ada):
    rows = cond.shape[0]
    cols = w_ada.shape[-1]
    tn = 1536
    return pl.pallas_call(
        _ada_kernel,
        grid=(DEPTH, cols // tn),
        in_specs=[
            pl.BlockSpec((rows, D_MODEL), lambda l, j: (0, 0)),
            pl.BlockSpec((1, D_MODEL, tn), lambda l, j: (l, 0, j)),
            pl.BlockSpec((1, 1, tn), lambda l, j: (l, 0, j)),
        ],
        out_specs=pl.BlockSpec((1, rows, tn), lambda l, j: (l, 0, j)),
        out_shape=jax.ShapeDtypeStruct((DEPTH, rows, cols), F32),
        compiler_params=_params(2),
    )(cond, w_ada, b_ada.reshape(DEPTH, 1, cols))


def _inproj_kernel(x_ref, mod_ref, w_ref, pc_ref, q_ref, k_ref, v_ref, g_ref, su_ref, sv_ref):
    mod = mod_ref[0, 0, 0]
    h = (x_ref[0] * (1.0 + mod[1:2]) + mod[0:1]).astype(BF16)

    def proj(lo, hi):
        return jnp.dot(h, w_ref[0, :, lo:hi], preferred_element_type=F32)

    o = 3 * CONV_W
    pc_ref[0] = proj(0, o)
    q_ref[0] = proj(o, o + RET_W)
    k_ref[0] = proj(o + RET_W, o + 2 * RET_W) * (RET_DK ** -0.5)
    v_ref[0] = proj(o + 2 * RET_W, o + 3 * RET_W)
    g_ref[0] = proj(o + 3 * RET_W, o + 4 * RET_W)
    o += 4 * RET_W
    su_ref[0] = proj(o, o + SGU_W)
    sv_ref[0] = proj(o + SGU_W, o + 2 * SGU_W)


def _inproj_call(layer, xa, mod, w_in_bf, n_lat_tiles):
    bsz, length, _ = xa.shape
    widths = (3 * CONV_W, RET_W, RET_W, RET_W, RET_W, SGU_W, SGU_W)
    return pl.pallas_call(
        _inproj_kernel,
        grid=(bsz, length // TILE),
        in_specs=[
            pl.BlockSpec((1, TILE, D_MODEL), lambda b, i: (b, i, 0)),
            pl.BlockSpec((1, 1, 1, MOD_ROWS, D_MODEL), lambda b, i: (layer, b, jnp.minimum(i // n_lat_tiles, 1), 0, 0)),
            pl.BlockSpec((1, D_MODEL, IN_COLS), lambda b, i: (layer, 0, 0)),
        ],
        out_specs=[pl.BlockSpec((1, TILE, w), lambda b, i: (b, i, 0)) for w in widths],
        out_shape=[jax.ShapeDtypeStruct((bsz, length, w), F32) for w in widths],
        compiler_params=_params(2),
    )(xa, mod, w_in_bf)


def _state_kernel(kf_ref, vf_ref, kb_ref, vb_ref, lgf_ref, lgb_ref, lgfc_ref, lgbc_ref, sf_ref, sb_ref, stf, stb):
    @pl.when(pl.program_id(1) == 0)
    def _():
        stf[...] = jnp.zeros_like(stf)
        stb[...] = jnp.zeros_like(stb)

    sf_ref[0, 0] = stf[...]
    sb_ref[0, 0] = stb[...]
    pos = lax.broadcasted_iota(jnp.int32, (CHUNK, 1), 0).astype(F32)
    kf = (kf_ref[0] * jnp.exp((CHUNK - 1.0 - pos) * lgf_ref[0])).astype(BF16)
    kb = (kb_ref[0] * jnp.exp(pos * lgb_ref[0])).astype(BF16)
    vf = vf_ref[0].astype(BF16)
    vb = vb_ref[0].astype(BF16)
    decay_f = jnp.exp(CHUNK * lgfc_ref[0])
    decay_b = jnp.exp(CHUNK * lgbc_ref[0])
    contract_rows = (((0,), (0,)), ((), ()))
    for h in range(RET_HEADS):
        sl = slice(h * RET_DK, (h + 1) * RET_DK)
        stf[sl, :] = decay_f[sl] * stf[sl, :] + lax.dot_general(kf[:, sl], vf[:, sl], contract_rows,
                                                                 preferred_element_type=F32)
        stb[sl, :] = decay_b[sl] * stb[sl, :] + lax.dot_general(kb[:, sl], vb[:, sl], contract_rows,
                                                                 preferred_element_type=F32)


def _state_call(layer, k, v, lg_rows, lg_cols, n_lat_chunks):
    bsz, length, _ = k.shape
    nc = length // CHUNK

    def fwd(s):
        return (s + n_lat_chunks) % nc

    def bwd(s):
        return nc - 1 - s

    kv_spec = lambda order: pl.BlockSpec((1, CHUNK, RET_W), lambda b, s: (b, order(s), 0))
    lg_row = lambda d: pl.BlockSpec((1, 1, RET_W), lambda b, s: (2 * layer + d, 0, 0))
    lg_col = lambda d: pl.BlockSpec((1, RET_W, 1), lambda b, s: (2 * layer + d, 0, 0))
    st_spec = lambda order: pl.BlockSpec((1, 1, RET_W, RET_DK), lambda b, s: (b, order(s), 0, 0))
    return pl.pallas_call(
        _state_kernel,
        grid=(bsz, nc),
        in_specs=[kv_spec(fwd), kv_spec(fwd), kv_spec(bwd), kv_spec(bwd), lg_row(0), lg_row(1), lg_col(0), lg_col(1)],
        out_specs=[st_spec(fwd), st_spec(bwd)],
        out_shape=[jax.ShapeDtypeStruct((bsz, nc, RET_W, RET_DK), F32)] * 2,
        scratch_shapes=[pltpu.VMEM((RET_W, RET_DK), F32), pltpu.VMEM((RET_W, RET_DK), F32)],
        compiler_params=_params(2),
    )(k, v, k, v, lg_rows, lg_rows, lg_cols, lg_cols)


def _route(logits):
    lane = lax.broadcasted_iota(jnp.int32, logits.shape, 1)
    big = jnp.int32(ROUTE_W)
    neg = jnp.float32(-jnp.inf)

    def top(vals):
        m = jnp.max(vals, axis=-1, keepdims=True)
        idx = jnp.min(jnp.where(vals == m, lane, big), axis=-1, keepdims=True)
        return m, idx

    gl = jnp.where(lane < N_GROUPS, logits, neg)
    gmax, gidx = top(gl)
    g_prob = 1.0 / jnp.sum(jnp.exp(gl - gmax), axis=-1, keepdims=True)
    lo = N_GROUPS + EXPERTS_PER_GROUP * gidx
    el = jnp.where((lane >= lo) & (lane < lo + EXPERTS_PER_GROUP), logits, neg)
    m1, i1 = top(el)
    m2, i2 = top(jnp.where(lane == i1, neg, el))
    e2 = jnp.exp(m2 - m1)
    gate1 = g_prob * (1.0 / (1.0 + e2))
    gate2 = g_prob * (e2 / (1.0 + e2))
    first_is_lo = i1 < i2
    e_lo = jnp.minimum(i1, i2) - lo
    e_hi = jnp.maximum(i1, i2) - lo
    pair = e_lo * (EXPERTS_PER_GROUP - 1) - ((e_lo * (e_lo - 1)) >> 1) + (e_hi - e_lo - 1)
    out = jnp.where(lane == 0, (gidx * PAIRS_PER_GROUP + pair).astype(F32), 0.0)
    out = jnp.where(lane == 1, jnp.where(first_is_lo, gate1, gate2), out)
    return jnp.where(lane == 2, jnp.where(first_is_lo, gate2, gate1), out)


def _mix_kernel(n_lat_tiles, lgf_ref, lgb_ref, pc_ref, hp_ref, hn_ref, q_ref, k_ref, v_ref, g_ref, su_ref, sv_ref,
                sf_ref, sb_ref, x_ref, mod_ref, convw_ref, sguw_ref, sgub_ref, wout_ref, ln_ref, wr_ref, br_ref,
                x1_ref, hmx_ref, ycat):
    i = pl.program_id(1)
    is_ctx = i >= n_lat_tiles
    row = lax.broadcasted_iota(jnp.int32, (TILE, 1), 0)

    pc = pc_ref[0]
    z = pc[:, CONV_W:2 * CONV_W] * pc[:, 2 * CONV_W:3 * CONV_W]
    line_mask = jnp.where(is_ctx, TILE - 1, GRID_W - 1)
    first = (row & line_mask) == 0
    last = (row & line_mask) == line_mask
    z_prev = jnp.where(first, 0.0, pltpu.roll(z, 1, 0))
    z_next = jnp.where(last, 0.0, pltpu.roll(z, TILE - 1, 0))
    hp = hp_ref[0]
    hn = hn_ref[0]
    z_top = jnp.where(i == 0, 0.0, hp[:, CONV_W:2 * CONV_W] * hp[:, 2 * CONV_W:3 * CONV_W])
    z_bot = jnp.where(i == n_lat_tiles - 1, 0.0, hn[:, CONV_W:2 * CONV_W] * hn[:, 2 * CONV_W:3 * CONV_W])
    zcat = jnp.concatenate([z_top, z, z_bot], axis=0)
    z_up = zcat[0:TILE]
    z_down = zcat[2 * HALO:2 * HALO + TILE]
    along_seq = lax.broadcasted_iota(jnp.int32, (1, CONV_W), 1) < jnp.where(is_ctx, CONV_W, CONV_W // 2)
    cw = convw_ref[0]
    conv = cw[0:1] * jnp.where(along_seq, z_prev, z_up) + cw[1:2] * z
    conv = conv + cw[2:3] * jnp.where(along_seq, z_next, z_down)
    ycat[:, 0:CONV_W] = (pc[:, 0:CONV_W] * conv).astype(BF16)

    vn = _standardize(sv_ref[0]).astype(BF16)
    group = lax.broadcasted_iota(jnp.int32, (1, SGU_W), 1) // (SGU_W // SGU_GROUPS)
    for c in range(TILE // CHUNK):
        rows = slice(c * CHUNK, (c + 1) * CHUNK)
        mixed = jnp.zeros((CHUNK, SGU_W), F32)
        for gi in range(SGU_GROUPS):
            m = jnp.dot(sguw_ref[0, gi], vn[rows], preferred_element_type=F32)
            mixed = jnp.where(group == gi, m, mixed)
        ycat[rows, CONV_W + RET_W:] = (su_ref[0, rows, :] * (mixed + sgub_ref[0])).astype(BF16)

    pos_i = lax.broadcasted_iota(jnp.int32, (CHUNK, 1), 0).astype(F32)
    rel = pos_i - lax.broadcasted_iota(jnp.int32, (1, CHUNK), 1).astype(F32)
    for h in range(RET_HEADS):
        lgf = lgf_ref[h]
        lgb = lgb_ref[h]
        decay = jnp.where(rel > 0, jnp.exp(lgf * jnp.maximum(rel, 0.0)),
                          jnp.where(rel < 0, jnp.exp(lgb * jnp.maximum(-rel, 0.0)), 2.0))
        q_decay_f = jnp.exp(lgf * (pos_i + 1.0))
        q_decay_b = jnp.exp(lgb * (CHUNK - pos_i))
        cols = slice(h * RET_DK, (h + 1) * RET_DK)
        for c in range(TILE // CHUNK):
            rows = slice(c * CHUNK, (c + 1) * CHUNK)
            qh = q_ref[0, rows, cols]
            kh = k_ref[0, rows, cols].astype(BF16)
            vh = v_ref[0, rows, cols].astype(BF16)
            scores = lax.dot_general(qh.astype(BF16), kh, (((1,), (1,)), ((), ())), preferred_element_type=F32)
            o = jnp.dot((scores * decay).astype(BF16), vh, preferred_element_type=F32)
            o = o + jnp.dot((qh * q_decay_f).astype(BF16), sf_ref[0, c, cols, :].astype(BF16),
                            preferred_element_type=F32)
            o = o + jnp.dot((qh * q_decay_b).astype(BF16), sb_ref[0, c, cols, :].astype(BF16),
                            preferred_element_type=F32)
            gh = g_ref[0, rows, cols]
            ycat[rows, CONV_W + h * RET_DK:CONV_W + (h + 1) * RET_DK] = (_silu(gh) * _standardize(o)).astype(BF16)

    y = jnp.dot(ycat[...], wout_ref[0], preferred_element_type=F32)
    mod = mod_ref[0, 0, 0]
    ln = ln_ref[0]
    x1 = _standardize(ALPHA * x_ref[0] + mod[2:3] * y) * ln[0:1] + ln[1:2]
    x1_ref[0] = x1
    hm = x1 * (1.0 + mod[4:5]) + mod[3:4]
    hmx_ref[0, :, 0:D_MODEL] = hm
    logits = jnp.dot(hm.astype(BF16), wr_ref[0], preferred_element_type=F32) + br_ref[0]
    hmx_ref[0, :, D_MODEL:] = _route(logits)


def _mix_call(layer, n_lat_tiles, n_proc, lgf, lgb, pc, q, k, v, g, su, sv, sf, sb, xa, mod, conv_w, sgu_w_bf, sgu_bias,
              w_out_bf, ln1, w_route_bf, b_route):
    bsz, length, _ = xa.shape
    halos_per_tile = TILE // HALO
    n_halo = length // HALO
    cpt = TILE // CHUNK
    tok = lambda w: pl.BlockSpec((1, TILE, w), lambda b, i, *_: (b, i, 0))
    per_layer = lambda *shape: pl.BlockSpec((1,) + shape, lambda b, i, *_: (layer,) + (0,) * len(shape))
    grid_spec = pltpu.PrefetchScalarGridSpec(
        num_scalar_prefetch=2,
        grid=(bsz, n_proc),
        in_specs=[
            tok(3 * CONV_W),
            pl.BlockSpec((1, HALO, 3 * CONV_W), lambda b, i, *_: (b, jnp.maximum(i * halos_per_tile - 1, 0), 0)),
            pl.BlockSpec((1, HALO, 3 * CONV_W),
                         lambda b, i, *_: (b, jnp.minimum((i + 1) * halos_per_tile, n_halo - 1), 0)),
            tok(RET_W), tok(RET_W), tok(RET_W), tok(RET_W), tok(SGU_W), tok(SGU_W),
            pl.BlockSpec((1, cpt, RET_W, RET_DK), lambda b, i, *_: (b, i, 0, 0)),
            pl.BlockSpec((1, cpt, RET_W, RET_DK), lambda b, i, *_: (b, i, 0, 0)),
            tok(D_MODEL),
            pl.BlockSpec((1, 1, 1, MOD_ROWS, D_MODEL),
                         lambda b, i, *_: (layer, b, jnp.minimum(i // n_lat_tiles, 1), 0, 0)),
            per_layer(8, CONV_W),
            per_layer(SGU_GROUPS, CHUNK, CHUNK),
            per_layer(CHUNK, SGU_W),
            per_layer(D_MODEL, D_MODEL),
            per_layer(8, D_MODEL),
            per_layer(D_MODEL, ROUTE_W),
            per_layer(1, ROUTE_W),
        ],
        out_specs=[tok(D_MODEL), tok(ROW_W)],
        scratch_shapes=[pltpu.VMEM((TILE, D_MODEL), BF16)],
    )
    return pl.pallas_call(
        functools.partial(_mix_kernel, n_lat_tiles),
        grid_spec=grid_spec,
        out_shape=[jax.ShapeDtypeStruct((bsz, n_proc * TILE, D_MODEL), F32),
                   jax.ShapeDtypeStruct((bsz, n_proc * TILE, ROW_W), F32)],
        compiler_params=_params(2),
    )(lgf, lgb, pc, pc, pc, q, k, v, g, su, sv, sf, sb, xa, mod, conv_w, sgu_w_bf, sgu_bias, w_out_bf, ln1,
      w_route_bf, b_route)


def _rank_kernel(route_ref, class_ref, rank_ref, counts_ref, running):
    @pl.when(pl.program_id(0) == 0)
    def _():
        running[...] = jnp.zeros_like(running)

    lane = lax.broadcasted_iota(jnp.int32, (RANK_TILE, ROUTE_W), 1)
    cls = route_ref[:, 0:1].astype(jnp.int32)
    class_ref[...] = cls
    onehot = lane == cls
    earlier = (lax.broadcasted_iota(jnp.int32, (RANK_TILE, RANK_TILE), 0)
               > lax.broadcasted_iota(jnp.int32, (RANK_TILE, RANK_TILE), 1))
    before = jnp.dot(earlier.astype(BF16), onehot.astype(BF16), preferred_element_type=F32) + running[...]
    rank_ref[...] = jnp.sum(jnp.where(onehot, before, 0.0), axis=-1, keepdims=True).astype(jnp.int32)
    running[...] += jnp.sum(onehot.astype(F32), axis=0, keepdims=True)
    counts_ref[...] = running[...]


def _rank_call(hmx_flat):
    n_tok = hmx_flat.shape[0]
    return pl.pallas_call(
        _rank_kernel,
        grid=(n_tok // RANK_TILE,),
        in_specs=[pl.BlockSpec((RANK_TILE, ROUTE_W), lambda i: (i, D_MODEL // ROUTE_W))],
        out_specs=[pl.BlockSpec((RANK_TILE, 1), lambda i: (i, 0)), pl.BlockSpec((RANK_TILE, 1), lambda i: (i, 0)),
                   pl.BlockSpec((1, ROUTE_W), lambda i: (0, 0))],
        out_shape=[jax.ShapeDtypeStruct((n_tok, 1), jnp.int32), jax.ShapeDtypeStruct((n_tok, 1), jnp.int32),
                   jax.ShapeDtypeStruct((1, ROUTE_W), F32)],
        scratch_shapes=[pltpu.VMEM((1, ROUTE_W), F32)],
        compiler_params=_params(1),
    )(hmx_flat)


def _dispatch_kernel(dest_ref, pad_end_ref, pad_len_ref, n_used_ref, hmx_ref, xs_ref, zeros, sem, pad_sem):
    step = pl.program_id(0)
    base = step * TILE
    half = BLOCK_M // 2

    def for_each_pad_copy(fn):
        def per_class(c, carry):
            off = pad_end_ref[c]
            n = pad_len_ref[c]
            for shift in range(BLOCK_M.bit_length() - 2, -1, -1):
                bit = 1 << shift
                off = off - (n & bit)

                @pl.when((n & bit) != 0)
                def _():
                    if bit >= 8:
                        fn(pltpu.make_async_copy(zeros.at[pl.ds(0, bit)], xs_ref.at[pl.ds(pl.multiple_of(off, 8), bit)],
                                                 pad_sem))
                    else:
                        for j in range(bit):
                            fn(pltpu.make_async_copy(zeros.at[pl.ds(0, 1)], xs_ref.at[pl.ds(off + j, 1)], pad_sem))

            return carry

        lax.fori_loop(0, N_CLASSES, per_class, 0)

        def per_half_block(j, carry):
            fn(pltpu.make_async_copy(zeros, xs_ref.at[pl.ds(pl.multiple_of(j * half, 8), half)], pad_sem))
            return carry

        lax.fori_loop(n_used_ref[0] * 2, xs_ref.shape[0] // half, per_half_block, 0)

    @pl.when(step == 0)
    def _():
        zeros[...] = jnp.zeros_like(zeros)
        for_each_pad_copy(lambda cp: cp.start())

    def issue(r, carry):
        pltpu.make_async_copy(hmx_ref.at[pl.ds(r, 1)], xs_ref.at[pl.ds(dest_ref[base + r], 1)], sem).start()
        return carry

    lax.fori_loop(0, TILE, issue, 0, unroll=8)

    def drain(r, carry):
        pltpu.make_async_copy(hmx_ref.at[pl.ds(0, 1)], xs_ref.at[pl.ds(0, 1)], sem).wait()
        return carry

    lax.fori_loop(0, TILE, drain, 0, unroll=8)

    @pl.when(step == pl.num_programs(0) - 1)
    def _():
        for_each_pad_copy(lambda cp: cp.wait())


def _dispatch_call(dest, pad_end, pad_len, n_used, hmx_flat, n_slots):
    n_tok = hmx_flat.shape[0]
    grid_spec = pltpu.PrefetchScalarGridSpec(
        num_scalar_prefetch=4,
        grid=(n_tok // TILE,),
        in_specs=[pl.BlockSpec((TILE, ROW_W), lambda i, *_: (i, 0))],
        out_specs=pl.BlockSpec(memory_space=pl.ANY),
        scratch_shapes=[pltpu.VMEM((BLOCK_M // 2, ROW_W), F32), pltpu.SemaphoreType.DMA, pltpu.SemaphoreType.DMA],
    )
    return pl.pallas_call(
        _dispatch_kernel,
        grid_spec=grid_spec,
        out_shape=jax.ShapeDtypeStruct((n_slots, ROW_W), F32),
        compiler_params=_params(1),
    )(dest, pad_end, pad_len, n_used, hmx_flat)


def _expert_kernel(n_used_ref, xs_ref, wg_lo, wu_lo, wd_lo, wg_hi, wu_hi, wd_hi, ys_ref):
    used = pl.program_id(0) < n_used_ref[0]

    @pl.when(used)
    def _():
        xb = xs_ref[:, 0:D_MODEL].astype(BF16)
        gates = xs_ref[:, D_MODEL:]

        def mlp(wg, wu, wd):
            gate = jnp.dot(xb, wg[0, 0], preferred_element_type=F32)
            up = jnp.dot(xb, wu[0, 0], preferred_element_type=F32)
            return jnp.dot((_silu(gate) * up).astype(BF16), wd[0, 0], preferred_element_type=F32)

        ys_ref[...] = mlp(wg_lo, wu_lo, wd_lo) * gates[:, 1:2] + mlp(wg_hi, wu_hi, wd_hi) * gates[:, 2:3]

    @pl.when(jnp.logical_not(used))
    def _():
        ys_ref[...] = jnp.zeros_like(ys_ref)


def _expert_call(layer, blk_lo, blk_hi, n_used, xs, w_gate_bf, w_up_bf, w_down_bf):
    n_blocks = xs.shape[0] // BLOCK_M
    up_spec = lambda which: pl.BlockSpec((1, 1, D_MODEL, EXPERT_HIDDEN),
                                         lambda i, lo, hi, nu: (layer, (lo, hi)[which][i], 0, 0))
    down_spec = lambda which: pl.BlockSpec((1, 1, EXPERT_HIDDEN, D_MODEL),
                                           lambda i, lo, hi, nu: (layer, (lo, hi)[which][i], 0, 0))
    grid_spec = pltpu.PrefetchScalarGridSpec(
        num_scalar_prefetch=3,
        grid=(n_blocks,),
        in_specs=[pl.BlockSpec((BLOCK_M, ROW_W), lambda i, lo, hi, nu: (jnp.minimum(i, nu[0] - 1), 0)),
                  up_spec(0), up_spec(0), down_spec(0), up_spec(1), up_spec(1), down_spec(1)],
        out_specs=pl.BlockSpec((BLOCK_M, D_MODEL), lambda i, *_: (i, 0)),
    )
    kern = lambda lo, hi, nu, *refs: _expert_kernel(nu, *refs)
    return pl.pallas_call(
        kern,
        grid_spec=grid_spec,
        out_shape=jax.ShapeDtypeStruct((xs.shape[0], D_MODEL), F32),
        compiler_params=_params(1),
    )(blk_lo, blk_hi, n_used, xs, w_gate_bf, w_up_bf, w_down_bf, w_gate_bf, w_up_bf, w_down_bf)


def _combine_kernel(tiles_per_batch, dest_ref, ys_ref, x1_ref, mod_ref, ln_ref, out_ref, buf, sem):
    step = pl.program_id(0) * tiles_per_batch + pl.program_id(1)
    slot = step % 2
    has_next = step + 1 < pl.num_programs(0) * tiles_per_batch

    def gather(tile, to_slot):
        base = tile * TILE

        def issue(r, carry):
            pltpu.make_async_copy(ys_ref.at[pl.ds(dest_ref[base + r], 1)], buf.at[to_slot, pl.ds(r, 1)],
                                  sem.at[to_slot]).start()
            return carry

        lax.fori_loop(0, TILE, issue, 0, unroll=8)

    @pl.when(step == 0)
    def _():
        gather(step, slot)

    @pl.when(has_next)
    def _():
        gather(step + 1, 1 - slot)

    def drain(r, carry):
        pltpu.make_async_copy(ys_ref.at[pl.ds(0, 1)], buf.at[slot, pl.ds(0, 1)], sem.at[slot]).wait()
        return carry

    lax.fori_loop(0, TILE, drain, 0, unroll=8)

    mod = mod_ref[0, 0, 0]
    ln = ln_ref[0]
    out_ref[0] = _standardize(ALPHA * x1_ref[0] + mod[5:6] * buf[slot]) * ln[0:1] + ln[1:2]


def _combine_call(layer, n_lat_tiles, dest, ys, x1, mod, ln2):
    bsz, length, _ = x1.shape
    tiles_per_batch = length // TILE
    tok = lambda w: pl.BlockSpec((1, TILE, w), lambda b, i, *_: (b, i, 0))
    grid_spec = pltpu.PrefetchScalarGridSpec(
        num_scalar_prefetch=1,
        grid=(bsz, tiles_per_batch),
        in_specs=[
            pl.BlockSpec(memory_space=pl.ANY),
            tok(D_MODEL),
            pl.BlockSpec((1, 1, 1, MOD_ROWS, D_MODEL),
                         lambda b, i, *_: (layer, b, jnp.minimum(i // n_lat_tiles, 1), 0, 0)),
            pl.BlockSpec((1, 8, D_MODEL), lambda b, i, *_: (layer, 0, 0)),
        ],
        out_specs=tok(D_MODEL),
        scratch_shapes=[pltpu.VMEM((2, TILE, D_MODEL), F32), pltpu.SemaphoreType.DMA((2,))],
    )
    return pl.pallas_call(
        functools.partial(_combine_kernel, tiles_per_batch),
        grid_spec=grid_spec,
        out_shape=jax.ShapeDtypeStruct(x1.shape, F32),
        compiler_params=_params(2),
    )(dest, ys, x1, mod, ln2)


def _dispatch_plan(class_of_token, rank, counts):
    n_tok = class_of_token.shape[0]
    counts = counts.astype(jnp.int32)
    pcounts = (counts + BLOCK_M - 1) // BLOCK_M * BLOCK_M
    pends = jnp.cumsum(pcounts)
    pstarts = pends - pcounts
    classes = jnp.arange(ROUTE_W, dtype=jnp.int32)
    dest = jnp.sum(jnp.where(class_of_token[:, None] == classes[None, :], pstarts[None, :], 0), axis=1) + rank
    n_blocks = n_tok // BLOCK_M + N_CLASSES
    blk_start = jnp.arange(n_blocks, dtype=jnp.int32) * BLOCK_M
    blk_class = jnp.minimum(jnp.sum((pends[None, :] <= blk_start[:, None]).astype(jnp.int32), axis=1), N_CLASSES - 1)
    n_used = pends[-1:] // BLOCK_M
    return (dest.astype(jnp.int32), pends, pcounts - counts, jnp.asarray(CLASS_LO)[blk_class],
            jnp.asarray(CLASS_HI)[blk_class], n_used.astype(jnp.int32), n_blocks * BLOCK_M)


def _pad_rows(a, rows):
    return jnp.pad(a, [(0, 0)] * (a.ndim - 2) + [(0, rows - a.shape[-2]), (0, 0)])


def kernel(x, c, ctx, c_ctx, w_ada, b_ada, w_in, conv_w, ret_decay_fwd, ret_decay_bwd, sgu_w, sgu_b, w_out, ln1_g, ln1_b,
           router_group_w, router_group_b, router_expert_w, router_expert_b, moe_w_gate, moe_w_up, moe_w_down, ln2_g,
           ln2_b):
    bsz, seq, d = x.shape
    ctx_len = ctx.shape[1]
    assert d == D_MODEL and ctx_len == TILE and seq % TILE == 0 and seq % GRID_W == 0
    n_lat_tiles = seq // TILE
    n_lat_chunks = seq // CHUNK

    cond = _pad_rows(jnp.concatenate([c, c_ctx[None, :]], axis=0), 16)
    ada = _ada_call(cond, w_ada, b_ada)
    mod_lat = ada[:, :bsz].reshape(DEPTH, bsz, N_MOD, d)
    mod_ctx = jnp.broadcast_to(ada[:, bsz].reshape(DEPTH, 1, N_MOD, d), mod_lat.shape)
    mod = _pad_rows(jnp.stack([mod_lat, mod_ctx], axis=2), MOD_ROWS)

    w_in_bf = w_in.astype(BF16)
    w_out_bf = w_out.astype(BF16)
    sgu_w_bf = sgu_w.astype(BF16)
    w_gate_bf = moe_w_gate.astype(BF16)
    w_up_bf = moe_w_up.astype(BF16)
    w_down_bf = moe_w_down.astype(BF16)
    sgu_bias = jnp.repeat(jnp.swapaxes(sgu_b, 1, 2), SGU_W // SGU_GROUPS, axis=2)
    conv_w8 = _pad_rows(conv_w, 8)
    ln1 = _pad_rows(jnp.stack([ln1_g, ln1_b], axis=1), 8)
    ln2 = _pad_rows(jnp.stack([ln2_g, ln2_b], axis=1), 8)
    w_route = jnp.concatenate([router_group_w, jnp.swapaxes(router_expert_w, 1, 2).reshape(DEPTH, d, N_EXPERTS)], axis=2)
    w_route_bf = jnp.pad(w_route, ((0, 0), (0, 0), (0, ROUTE_W - w_route.shape[2]))).astype(BF16)
    b_route = jnp.concatenate([router_group_b, router_expert_b.reshape(DEPTH, N_EXPERTS)], axis=1)
    b_route = jnp.pad(b_route, ((0, 0), (0, ROUTE_W - b_route.shape[1])))[:, None, :].astype(F32)
    lg = jnp.stack([jax.nn.log_sigmoid(ret_decay_fwd.astype(F32)), jax.nn.log_sigmoid(ret_decay_bwd.astype(F32))],
                   axis=1)
    lg_lanes = jnp.repeat(lg, RET_DK, axis=2).reshape(DEPTH * 2, RET_W)
    lg_rows = lg_lanes[:, None, :]
    lg_cols = lg_lanes[:, :, None]

    xa = jnp.concatenate([x, ctx], axis=1)
    for layer in range(DEPTH):
        last = layer == DEPTH - 1
        pc, q, k, v, g, su, sv = _inproj_call(layer, xa, mod, w_in_bf, n_lat_tiles)
        sf, sb = _state_call(layer, k, v, lg_rows, lg_cols, n_lat_chunks)
        n_proc = n_lat_tiles if last else n_lat_tiles + 1
        x1, hmx = _mix_call(layer, n_lat_tiles, n_proc, lg[layer, 0], lg[layer, 1], pc, q, k, v, g, su, sv, sf, sb, xa,
                            mod, conv_w8, sgu_w_bf, sgu_bias, w_out_bf, ln1, w_route_bf, b_route)
        hmx_flat = hmx.reshape(-1, ROW_W)
        class_of_token, rank, counts = _rank_call(hmx_flat)
        dest, pad_end, pad_len, blk_lo, blk_hi, n_used, n_slots = _dispatch_plan(class_of_token[:, 0], rank[:, 0],
                                                                                  counts[0])
        xs = _dispatch_call(dest, pad_end, pad_len, n_used, hmx_flat, n_slots)
        ys = _expert_call(layer, blk_lo, blk_hi, n_used, xs, w_gate_bf, w_up_bf, w_down_bf)
        xa = _combine_call(layer, n_lat_tiles, dest, ys, x1, mod, ln2)
    return xa
```

```python
import functools

import jax
import jax.numpy as jnp
import numpy as np
from jax import lax
from jax.experimental import pallas as pl
from jax.experimental.pallas import tpu as pltpu

F32 = jnp.float32
BF16 = jnp.bfloat16

D_MODEL = 1024
DEPTH = 2
GRID_W = 64
CONV_W = 256
RET_W = 512
RET_HEADS = 8
RET_DK = 64
CHUNK = 128
SGU_W = 256
SGU_GROUPS = 4
IN_COLS = 3 * CONV_W + 4 * RET_W + 2 * SGU_W
N_GROUPS = 4
EXPERTS_PER_GROUP = 8
N_EXPERTS = N_GROUPS * EXPERTS_PER_GROUP
EXPERT_HIDDEN = 512
N_MOD = 6
MOD_ROWS = 8
LN_EPS = 1e-5
ALPHA = (2 * DEPTH) ** 0.25

TILE = 256
HALO = GRID_W
BLOCK_M = 256
ROUTE_W = 128
ROW_W = D_MODEL + ROUTE_W
PAIRS_PER_GROUP = EXPERTS_PER_GROUP * (EXPERTS_PER_GROUP - 1) // 2
N_CLASSES = N_GROUPS * PAIRS_PER_GROUP
RANK_TILE = 512
CLASS_LO = np.array([g * EXPERTS_PER_GROUP + lo for g in range(N_GROUPS) for lo in range(EXPERTS_PER_GROUP)
                     for hi in range(lo + 1, EXPERTS_PER_GROUP)] + [N_EXPERTS - 2] * (ROUTE_W - N_CLASSES), np.int32)
CLASS_HI = np.array([g * EXPERTS_PER_GROUP + hi for g in range(N_GROUPS) for lo in range(EXPERTS_PER_GROUP)
                     for hi in range(lo + 1, EXPERTS_PER_GROUP)] + [N_EXPERTS - 1] * (ROUTE_W - N_CLASSES), np.int32)
VMEM_LIMIT = 56 * 1024 * 1024


def _params(n_axes):
    return pltpu.CompilerParams(dimension_semantics=("arbitrary",) * n_axes, vmem_limit_bytes=VMEM_LIMIT)


def _standardize(v):
    mu = jnp.mean(v, axis=-1, keepdims=True)
    var = jnp.mean(jnp.square(v - mu), axis=-1, keepdims=True)
    return (v - mu) * lax.rsqrt(var + LN_EPS)


def _silu(v):
    return v * jax.nn.sigmoid(v)


def _ada_kernel(c_ref, w_ref, b_ref, o_ref):
    a = _silu(c_ref[...]).astype(BF16)
    o_ref[0] = jnp.dot(a, w_ref[0].astype(BF16), preferred_element_type=F32) + b_ref[0]


def _ada_call(cond, w_ada, b_ada):
    rows = cond.shape[0]
    cols = w_ada.shape[-1]
    tn = 1536
    return pl.pallas_call(
        _ada_kernel,
        grid=(DEPTH, cols // tn),
        in_specs=[
            pl.BlockSpec((rows, D_MODEL), lambda l, j: (0, 0)),
            pl.BlockSpec((1, D_MODEL, tn), lambda l, j: (l, 0, j)),
            pl.BlockSpec((1, 1, tn), lambda l, j: (l, 0, j)),
        ],
        out_specs=pl.BlockSpec((1, rows, tn), lambda l, j: (l, 0, j)),
        out_shape=jax.ShapeDtypeStruct((DEPTH, rows, cols), F32),
        compiler_params=_params(2),
    )(cond, w_ada, b_ada.reshape(DEPTH, 1, cols))


def _inproj_kernel(x_ref, mod_ref, w_ref, pc_ref, q_ref, k_ref, v_ref, g_ref, su_ref, sv_ref):
    mod = mod_ref[0, 0, 0]
    h = (x_ref[0] * (1.0 + mod[1:2]) + mod[0:1]).astype(BF16)

    def proj(lo, hi):
        return jnp.dot(h, w_ref[0, :, lo:hi], preferred_element_type=F32)

    o = 3 * CONV_W
    pc_ref[0] = proj(0, o)
    q_ref[0] = proj(o, o + RET_W)
    k_ref[0] = proj(o + RET_W, o + 2 * RET_W) * (RET_DK ** -0.5)
    v_ref[0] = proj(o + 2 * RET_W, o + 3 * RET_W)
    g_ref[0] = proj(o + 3 * RET_W, o + 4 * RET_W)
    o += 4 * RET_W
    su_ref[0] = proj(o, o + SGU_W)
    sv_ref[0] = proj(o + SGU_W, o + 2 * SGU_W)


def _inproj_call(layer, xa, mod, w_in_bf, n_lat_tiles):
    bsz, length, _ = xa.shape
    widths = (3 * CONV_W, RET_W, RET_W, RET_W, RET_W, SGU_W, SGU_W)
    return pl.pallas_call(
        _inproj_kernel,
        grid=(bsz, length // TILE),
        in_specs=[
            pl.BlockSpec((1, TILE, D_MODEL), lambda b, i: (b, i, 0)),
            pl.BlockSpec((1, 1, 1, MOD_ROWS, D_MODEL), lambda b, i: (layer, b, jnp.minimum(i // n_lat_tiles, 1), 0, 0)),
            pl.BlockSpec((1, D_MODEL, IN_COLS), lambda b, i: (layer, 0, 0)),
        ],
        out_specs=[pl.BlockSpec((1, TILE, w), lambda b, i: (b, i, 0)) for w in widths],
        out_shape=[jax.ShapeDtypeStruct((bsz, length, w), F32) for w in widths],
        compiler_params=_params(2),
    )(xa, mod, w_in_bf)


def _state_kernel(kf_ref, vf_ref, kb_ref, vb_ref, lgf_ref, lgb_ref, lgfc_ref, lgbc_ref, sf_ref, sb_ref, stf, stb):
    @pl.when(pl.program_id(1) == 0)
    def _():
        stf[...] = jnp.zeros_like(stf)
        stb[...] = jnp.zeros_like(stb)

    sf_ref[0, 0] = stf[...]
    sb_ref[0, 0] = stb[...]
    pos = lax.broadcasted_iota(jnp.int32, (CHUNK, 1), 0).astype(F32)
    kf = (kf_ref[0] * jnp.exp((CHUNK - 1.0 - pos) * lgf_ref[0])).astype(BF16)
    kb = (kb_ref[0] * jnp.exp(pos * lgb_ref[0])).astype(BF16)
    vf = vf_ref[0].astype(BF16)
    vb = vb_ref[0].astype(BF16)
    decay_f = jnp.exp(CHUNK * lgfc_ref[0])
    decay_b = jnp.exp(CHUNK * lgbc_ref[0])
    contract_rows = (((0,), (0,)), ((), ()))
    for h in range(RET_HEADS):
        sl = slice(h * RET_DK, (h + 1) * RET_DK)
        stf[sl, :] = decay_f[sl] * stf[sl, :] + lax.dot_general(kf[:, sl], vf[:, sl], contract_rows,
                                                                 preferred_element_type=F32)
        stb[sl, :] = decay_b[sl] * stb[sl, :] + lax.dot_general(kb[:, sl], vb[:, sl], contract_rows,
                                                                 preferred_element_type=F32)


def _state_call(layer, k, v, lg_rows, lg_cols, n_lat_chunks):
    bsz, length, _ = k.shape
    nc = length // CHUNK

    def fwd(s):
        return (s + n_lat_chunks) % nc

    def bwd(s):
        return nc - 1 - s

    kv_spec = lambda order: pl.BlockSpec((1, CHUNK, RET_W), lambda b, s: (b, order(s), 0))
    lg_row = lambda d: pl.BlockSpec((1, 1, RET_W), lambda b, s: (2 * layer + d, 0, 0))
    lg_col = lambda d: pl.BlockSpec((1, RET_W, 1), lambda b, s: (2 * layer + d, 0, 0))
    st_spec = lambda order: pl.BlockSpec((1, 1, RET_W, RET_DK), lambda b, s: (b, order(s), 0, 0))
    return pl.pallas_call(
        _state_kernel,
        grid=(bsz, nc),
        in_specs=[kv_spec(fwd), kv_spec(fwd), kv_spec(bwd), kv_spec(bwd), lg_row(0), lg_row(1), lg_col(0), lg_col(1)],
        out_specs=[st_spec(fwd), st_spec(bwd)],
        out_shape=[jax.ShapeDtypeStruct((bsz, nc, RET_W, RET_DK), F32)] * 2,
        scratch_shapes=[pltpu.VMEM((RET_W, RET_DK), F32), pltpu.VMEM((RET_W, RET_DK), F32)],
        compiler_params=_params(2),
    )(k, v, k, v, lg_rows, lg_rows, lg_cols, lg_cols)


def _route(logits):
    lane = lax.broadcasted_iota(jnp.int32, logits.shape, 1)
    big = jnp.int32(ROUTE_W)
    neg = jnp.float32(-jnp.inf)

    def top(vals):
        m = jnp.max(vals, axis=-1, keepdims=True)
        idx = jnp.min(jnp.where(vals == m, lane, big), axis=-1, keepdims=True)
        return m, idx

    gl = jnp.where(lane < N_GROUPS, logits, neg)
    gmax, gidx = top(gl)
    g_prob = 1.0 / jnp.sum(jnp.exp(gl - gmax), axis=-1, keepdims=True)
    lo = N_GROUPS + EXPERTS_PER_GROUP * gidx
    el = jnp.where((lane >= lo) & (lane < lo + EXPERTS_PER_GROUP), logits, neg)
    m1, i1 = top(el)
    m2, i2 = top(jnp.where(lane == i1, neg, el))
    e2 = jnp.exp(m2 - m1)
    gate1 = g_prob * (1.0 / (1.0 + e2))
    gate2 = g_prob * (e2 / (1.0 + e2))
    first_is_lo = i1 < i2
    e_lo = jnp.minimum(i1, i2) - lo
    e_hi = jnp.maximum(i1, i2) - lo
    pair = e_lo * (EXPERTS_PER_GROUP - 1) - ((e_lo * (e_lo - 1)) >> 1) + (e_hi - e_lo - 1)
    out = jnp.where(lane == 0, (gidx * PAIRS_PER_GROUP + pair).astype(F32), 0.0)
    out = jnp.where(lane == 1, jnp.where(first_is_lo, gate1, gate2), out)
    return jnp.where(lane == 2, jnp.where(first_is_lo, gate2, gate1), out)


def _mix_kernel(n_lat_tiles, lgf_ref, lgb_ref, pc_ref, hp_ref, hn_ref, q_ref, k_ref, v_ref, g_ref, su_ref, sv_ref,
                sf_ref, sb_ref, x_ref, mod_ref, convw_ref, sguw_ref, sgub_ref, wout_ref, ln_ref, wr_ref, br_ref,
                x1_ref, hmx_ref, ycat):
    i = pl.program_id(1)
    is_ctx = i >= n_lat_tiles
    row = lax.broadcasted_iota(jnp.int32, (TILE, 1), 0)

    pc = pc_ref[0]
    z = pc[:, CONV_W:2 * CONV_W] * pc[:, 2 * CONV_W:3 * CONV_W]
    line_mask = jnp.where(is_ctx, TILE - 1, GRID_W - 1)
    first = (row & line_mask) == 0
    last = (row & line_mask) == line_mask
    z_prev = jnp.where(first, 0.0, pltpu.roll(z, 1, 0))
    z_next = jnp.where(last, 0.0, pltpu.roll(z, TILE - 1, 0))
    hp = hp_ref[0]
    hn = hn_ref[0]
    z_top = jnp.where(i == 0, 0.0, hp[:, CONV_W:2 * CONV_W] * hp[:, 2 * CONV_W:3 * CONV_W])
    z_bot = jnp.where(i == n_lat_tiles - 1, 0.0, hn[:, CONV_W:2 * CONV_W] * hn[:, 2 * CONV_W:3 * CONV_W])
    zcat = jnp.concatenate([z_top, z, z_bot], axis=0)
    z_up = zcat[0:TILE]
    z_down = zcat[2 * HALO:2 * HALO + TILE]
    along_seq = lax.broadcasted_iota(jnp.int32, (1, CONV_W), 1) < jnp.where(is_ctx, CONV_W, CONV_W // 2)
    cw = convw_ref[0]
    conv = cw[0:1] * jnp.where(along_seq, z_prev, z_up) + cw[1:2] * z
    conv = conv + cw[2:3] * jnp.where(along_seq, z_next, z_down)
    ycat[:, 0:CONV_W] = (pc[:, 0:CONV_W] * conv).astype(BF16)

    vn = _standardize(sv_ref[0]).astype(BF16)
    group = lax.broadcasted_iota(jnp.int32, (1, SGU_W), 1) // (SGU_W // SGU_GROUPS)
    for c in range(TILE // CHUNK):
        rows = slice(c * CHUNK, (c + 1) * CHUNK)
        mixed = jnp.zeros((CHUNK, SGU_W), F32)
        for gi in range(SGU_GROUPS):
            m = jnp.dot(sguw_ref[0, gi], vn[rows], preferred_element_type=F32)
            mixed = jnp.where(group == gi, m, mixed)
        ycat[rows, CONV_W + RET_W:] = (su_ref[0, rows, :] * (mixed + sgub_ref[0])).astype(BF16)

    pos_i = lax.broadcasted_iota(jnp.int32, (CHUNK, 1), 0).astype(F32)
    rel = pos_i - lax.broadcasted_iota(jnp.int32, (1, CHUNK), 1).astype(F32)
    for h in range(RET_HEADS):
        lgf = lgf_ref[h]
        lgb = lgb_ref[h]
        decay = jnp.where(rel > 0, jnp.exp(lgf * jnp.maximum(rel, 0.0)),
                          jnp.where(rel < 0, jnp.exp(lgb * jnp.maximum(-rel, 0.0)), 2.0))
        q_decay_f = jnp.exp(lgf * (pos_i + 1.0))
        q_decay_b = jnp.exp(lgb * (CHUNK - pos_i))
        cols = slice(h * RET_DK, (h + 1) * RET_DK)
        for c in range(TILE // CHUNK):
            rows = slice(c * CHUNK, (c + 1) * CHUNK)
            qh = q_ref[0, rows, cols]
            kh = k_ref[0, rows, cols].astype(BF16)
            vh = v_ref[0, rows, cols].astype(BF16)
            scores = lax.dot_general(qh.astype(BF16), kh, (((1,), (1,)), ((), ())), preferred_element_type=F32)
            o = jnp.dot((scores * decay).astype(BF16), vh, preferred_element_type=F32)
            o = o + jnp.dot((qh * q_decay_f).astype(BF16), sf_ref[0, c, cols, :].astype(BF16),
                            preferred_element_type=F32)
            o = o + jnp.dot((qh * q_decay_b).astype(BF16), sb_ref[0, c, cols, :].astype(BF16),
                            preferred_element_type=F32)
            gh = g_ref[0, rows, cols]
            ycat[rows, CONV_W + h * RET_DK:CONV_W + (h + 1) * RET_DK] = (_silu(gh) * _standardize(o)).astype(BF16)

    y = jnp.dot(ycat[...], wout_ref[0], preferred_element_type=F32)
    mod = mod_ref[0, 0, 0]
    ln = ln_ref[0]
    x1 = _standardize(ALPHA * x_ref[0] + mod[2:3] * y) * ln[0:1] + ln[1:2]
    x1_ref[0] = x1
    hm = x1 * (1.0 + mod[4:5]) + mod[3:4]
    hmx_ref[0, :, 0:D_MODEL] = hm
    logits = jnp.dot(hm.astype(BF16), wr_ref[0], preferred_element_type=F32) + br_ref[0]
    hmx_ref[0, :, D_MODEL:] = _route(logits)


def _mix_call(layer, n_lat_tiles, n_proc, lgf, lgb, pc, q, k, v, g, su, sv, sf, sb, xa, mod, conv_w, sgu_w_bf, sgu_bias,
              w_out_bf, ln1, w_route_bf, b_route):
    bsz, length, _ = xa.shape
    halos_per_tile = TILE // HALO
    n_halo = length // HALO
    cpt = TILE // CHUNK
    tok = lambda w: pl.BlockSpec((1, TILE, w), lambda b, i, *_: (b, i, 0))
    per_layer = lambda *shape: pl.BlockSpec((1,) + shape, lambda b, i, *_: (layer,) + (0,) * len(shape))
    grid_spec = pltpu.PrefetchScalarGridSpec(
        num_scalar_prefetch=2,
        grid=(bsz, n_proc),
        in_specs=[
            tok(3 * CONV_W),
            pl.BlockSpec((1, HALO, 3 * CONV_W), lambda b, i, *_: (b, jnp.maximum(i * halos_per_tile - 1, 0), 0)),
            pl.BlockSpec((1, HALO, 3 * CONV_W),
                         lambda b, i, *_: (b, jnp.minimum((i + 1) * halos_per_tile, n_halo - 1), 0)),
            tok(RET_W), tok(RET_W), tok(RET_W), tok(RET_W), tok(SGU_W), tok(SGU_W),
            pl.BlockSpec((1, cpt, RET_W, RET_DK), lambda b, i, *_: (b, i, 0, 0)),
            pl.BlockSpec((1, cpt, RET_W, RET_DK), lambda b, i, *_: (b, i, 0, 0)),
            tok(D_MODEL),
            pl.BlockSpec((1, 1, 1, MOD_ROWS, D_MODEL),
                         lambda b, i, *_: (layer, b, jnp.minimum(i // n_lat_tiles, 1), 0, 0)),
            per_layer(8, CONV_W),
            per_layer(SGU_GROUPS, CHUNK, CHUNK),
            per_layer(CHUNK, SGU_W),
            per_layer(D_MODEL, D_MODEL),
            per_layer(8, D_MODEL),
            per_layer(D_MODEL, ROUTE_W),
            per_layer(1, ROUTE_W),
        ],
        out_specs=[tok(D_MODEL), tok(ROW_W)],
        scratch_shapes=[pltpu.VMEM((TILE, D_MODEL), BF16)],
    )
    return pl.pallas_call(
        functools.partial(_mix_kernel, n_lat_tiles),
        grid_spec=grid_spec,
        out_shape=[jax.ShapeDtypeStruct((bsz, n_proc * TILE, D_MODEL), F32),
                   jax.ShapeDtypeStruct((bsz, n_proc * TILE, ROW_W), F32)],
        compiler_params=_params(2),
    )(lgf, lgb, pc, pc, pc, q, k, v, g, su, sv, sf, sb, xa, mod, conv_w, sgu_w_bf, sgu_bias, w_out_bf, ln1,
      w_route_bf, b_route)


def _rank_kernel(route_ref, class_ref, rank_ref, counts_ref, running):
    @pl.when(pl.program_id(0) == 0)
    def _():
        running[...] = jnp.zeros_like(running)

    lane = lax.broadcasted_iota(jnp.int32, (RANK_TILE, ROUTE_W), 1)
    cls = route_ref[:, 0:1].astype(jnp.int32)
    class_ref[...] = cls
    onehot = lane == cls
    earlier = (lax.broadcasted_iota(jnp.int32, (RANK_TILE, RANK_TILE), 0)
               > lax.broadcasted_iota(jnp.int32, (RANK_TILE, RANK_TILE), 1))
    before = jnp.dot(earlier.astype(BF16), onehot.astype(BF16), preferred_element_type=F32) + running[...]
    rank_ref[...] = jnp.sum(jnp.where(onehot, before, 0.0), axis=-1, keepdims=True).astype(jnp.int32)
    running[...] += jnp.sum(onehot.astype(F32), axis=0, keepdims=True)
    counts_ref[...] = running[...]


def _rank_call(hmx_flat):
    n_tok = hmx_flat.shape[0]
    return pl.pallas_call(
        _rank_kernel,
        grid=(n_tok // RANK_TILE,),
        in_specs=[pl.BlockSpec((RANK_TILE, ROUTE_W), lambda i: (i, D_MODEL // ROUTE_W))],
        out_specs=[pl.BlockSpec((RANK_TILE, 1), lambda i: (i, 0)), pl.BlockSpec((RANK_TILE, 1), lambda i: (i, 0)),
                   pl.BlockSpec((1, ROUTE_W), lambda i: (0, 0))],
        out_shape=[jax.ShapeDtypeStruct((n_tok, 1), jnp.int32), jax.ShapeDtypeStruct((n_tok, 1), jnp.int32),
                   jax.ShapeDtypeStruct((1, ROUTE_W), F32)],
        scratch_shapes=[pltpu.VMEM((1, ROUTE_W), F32)],
        compiler_params=_params(1),
    )(hmx_flat)


def _dispatch_kernel(dest_ref, pad_end_ref, pad_len_ref, n_used_ref, hmx_ref, xs_ref, zeros, sem, pad_sem):
    step = pl.program_id(0)
    base = step * TILE
    half = BLOCK_M // 2

    def for_each_pad_copy(fn):
        def per_class(c, carry):
            off = pad_end_ref[c]
            n = pad_len_ref[c]
            for shift in range(BLOCK_M.bit_length() - 2, -1, -1):
                bit = 1 << shift
                off = off - (n & bit)

                @pl.when((n & bit) != 0)
                def _():
                    if bit >= 8:
                        fn(pltpu.make_async_copy(zeros.at[pl.ds(0, bit)], xs_ref.at[pl.ds(pl.multiple_of(off, 8), bit)],
                                                 pad_sem))
                    else:
                        for j in range(bit):
                            fn(pltpu.make_async_copy(zeros.at[pl.ds(0, 1)], xs_ref.at[pl.ds(off + j, 1)], pad_sem))

            return carry

        lax.fori_loop(0, N_CLASSES, per_class, 0)

        def per_half_block(j, carry):
            fn(pltpu.make_async_copy(zeros, xs_ref.at[pl.ds(pl.multiple_of(j * half, 8), half)], pad_sem))
            return carry

        lax.fori_loop(n_used_ref[0] * 2, xs_ref.shape[0] // half, per_half_block, 0)

    @pl.when(step == 0)
    def _():
        zeros[...] = jnp.zeros_like(zeros)
        for_each_pad_copy(lambda cp: cp.start())

    for r in range(TILE):
        pltpu.make_async_copy(hmx_ref.at[pl.ds(r, 1)], xs_ref.at[pl.ds(dest_ref[base + r], 1)], sem).start()
    for r in range(TILE):
        pltpu.make_async_copy(hmx_ref.at[pl.ds(0, 1)], xs_ref.at[pl.ds(0, 1)], sem).wait()

    @pl.when(step == pl.num_programs(0) - 1)
    def _():
        for_each_pad_copy(lambda cp: cp.wait())


def _dispatch_call(dest, pad_end, pad_len, n_used, hmx_flat, n_slots):
    n_tok = hmx_flat.shape[0]
    grid_spec = pltpu.PrefetchScalarGridSpec(
        num_scalar_prefetch=4,
        grid=(n_tok // TILE,),
        in_specs=[pl.BlockSpec((TILE, ROW_W), lambda i, *_: (i, 0))],
        out_specs=pl.BlockSpec(memory_space=pl.ANY),
        scratch_shapes=[pltpu.VMEM((BLOCK_M // 2, ROW_W), F32), pltpu.SemaphoreType.DMA, pltpu.SemaphoreType.DMA],
    )
    return pl.pallas_call(
        _dispatch_kernel,
        grid_spec=grid_spec,
        out_shape=jax.ShapeDtypeStruct((n_slots, ROW_W), F32),
        compiler_params=_params(1),
    )(dest, pad_end, pad_len, n_used, hmx_flat)


def _expert_kernel(n_used_ref, xs_ref, wg_lo, wu_lo, wd_lo, wg_hi, wu_hi, wd_hi, ys_ref):
    used = pl.program_id(0) < n_used_ref[0]

    @pl.when(used)
    def _():
        xb = xs_ref[:, 0:D_MODEL].astype(BF16)
        gates = xs_ref[:, D_MODEL:]

        def mlp(wg, wu, wd):
            gate = jnp.dot(xb, wg[0, 0], preferred_element_type=F32)
            up = jnp.dot(xb, wu[0, 0], preferred_element_type=F32)
            return jnp.dot((_silu(gate) * up).astype(BF16), wd[0, 0], preferred_element_type=F32)

        ys_ref[...] = mlp(wg_lo, wu_lo, wd_lo) * gates[:, 1:2] + mlp(wg_hi, wu_hi, wd_hi) * gates[:, 2:3]

    @pl.when(jnp.logical_not(used))
    def _():
        ys_ref[...] = jnp.zeros_like(ys_ref)


def _expert_call(layer, blk_lo, blk_hi, n_used, xs, w_gate_bf, w_up_bf, w_down_bf):
    n_blocks = xs.shape[0] // BLOCK_M
    up_spec = lambda which: pl.BlockSpec((1, 1, D_MODEL, EXPERT_HIDDEN),
                                         lambda i, lo, hi, nu: (layer, (lo, hi)[which][i], 0, 0))
    down_spec = lambda which: pl.BlockSpec((1, 1, EXPERT_HIDDEN, D_MODEL),
                                           lambda i, lo, hi, nu: (layer, (lo, hi)[which][i], 0, 0))
    grid_spec = pltpu.PrefetchScalarGridSpec(
        num_scalar_prefetch=3,
        grid=(n_blocks,),
        in_specs=[pl.BlockSpec((BLOCK_M, ROW_W), lambda i, lo, hi, nu: (jnp.minimum(i, nu[0] - 1), 0)),
                  up_spec(0), up_spec(0), down_spec(0), up_spec(1), up_spec(1), down_spec(1)],
        out_specs=pl.BlockSpec((BLOCK_M, D_MODEL), lambda i, *_: (i, 0)),
    )
    kern = lambda lo, hi, nu, *refs: _expert_kernel(nu, *refs)
    return pl.pallas_call(
        kern,
        grid_spec=grid_spec,
        out_shape=jax.ShapeDtypeStruct((xs.shape[0], D_MODEL), F32),
        compiler_params=_params(1),
    )(blk_lo, blk_hi, n_used, xs, w_gate_bf, w_up_bf, w_down_bf, w_gate_bf, w_up_bf, w_down_bf)


def _combine_kernel(tiles_per_batch, dest_ref, ys_ref, x1_ref, mod_ref, ln_ref, out_ref, buf, sem):
    step = pl.program_id(0) * tiles_per_batch + pl.program_id(1)
    slot = step % 2
    has_next = step + 1 < pl.num_programs(0) * tiles_per_batch

    def gather(tile, to_slot):
        base = tile * TILE

        for r in range(TILE):
            pltpu.make_async_copy(ys_ref.at[pl.ds(dest_ref[base + r], 1)], buf.at[to_slot, pl.ds(r, 1)],
                                  sem.at[to_slot]).start()

    @pl.when(step == 0)
    def _():
        gather(step, slot)

    @pl.when(has_next)
    def _():
        gather(step + 1, 1 - slot)

    for r in range(TILE):
        pltpu.make_async_copy(ys_ref.at[pl.ds(0, 1)], buf.at[slot, pl.ds(0, 1)], sem.at[slot]).wait()

    mod = mod_ref[0, 0, 0]
    ln = ln_ref[0]
    out_ref[0] = _standardize(ALPHA * x1_ref[0] + mod[5:6] * buf[slot]) * ln[0:1] + ln[1:2]


def _combine_call(layer, n_lat_tiles, dest, ys, x1, mod, ln2):
    bsz, length, _ = x1.shape
    tiles_per_batch = length // TILE
    tok = lambda w: pl.BlockSpec((1, TILE, w), lambda b, i, *_: (b, i, 0))
    grid_spec = pltpu.PrefetchScalarGridSpec(
        num_scalar_prefetch=1,
        grid=(bsz, tiles_per_batch),
        in_specs=[
            pl.BlockSpec(memory_space=pl.ANY),
            tok(D_MODEL),
            pl.BlockSpec((1, 1, 1, MOD_ROWS, D_MODEL),
                         lambda b, i, *_: (layer, b, jnp.minimum(i // n_lat_tiles, 1), 0, 0)),
            pl.BlockSpec((1, 8, D_MODEL), lambda b, i, *_: (layer, 0, 0)),
        ],
        out_specs=tok(D_MODEL),
        scratch_shapes=[pltpu.VMEM((2, TILE, D_MODEL), F32), pltpu.SemaphoreType.DMA((2,))],
    )
    return pl.pallas_call(
        functools.partial(_combine_kernel, tiles_per_batch),
        grid_spec=grid_spec,
        out_shape=jax.ShapeDtypeStruct(x1.shape, F32),
        compiler_params=_params(2),
    )(dest, ys, x1, mod, ln2)


def _dispatch_plan(class_of_token, rank, counts):
    n_tok = class_of_token.shape[0]
    counts = counts.astype(jnp.int32)
    pcounts = (counts + BLOCK_M - 1) // BLOCK_M * BLOCK_M
    pends = jnp.cumsum(pcounts)
    pstarts = pends - pcounts
    classes = jnp.arange(ROUTE_W, dtype=jnp.int32)
    dest = jnp.sum(jnp.where(class_of_token[:, None] == classes[None, :], pstarts[None, :], 0), axis=1) + rank
    n_blocks = n_tok // BLOCK_M + N_CLASSES
    blk_start = jnp.arange(n_blocks, dtype=jnp.int32) * BLOCK_M
    blk_class = jnp.minimum(jnp.sum((pends[None, :] <= blk_start[:, None]).astype(jnp.int32), axis=1), N_CLASSES - 1)
    n_used = pends[-1:] // BLOCK_M
    return (dest.astype(jnp.int32), pends, pcounts - counts, jnp.asarray(CLASS_LO)[blk_class],
            jnp.asarray(CLASS_HI)[blk_class], n_used.astype(jnp.int32), n_blocks * BLOCK_M)


def _pad_rows(a, rows):
    return jnp.pad(a, [(0, 0)] * (a.ndim - 2) + [(0, rows - a.shape[-2]), (0, 0)])


def kernel(x, c, ctx, c_ctx, w_ada, b_ada, w_in, conv_w, ret_decay_fwd, ret_decay_bwd, sgu_w, sgu_b, w_out, ln1_g, ln1_b,
           router_group_w, router_group_b, router_expert_w, router_expert_b, moe_w_gate, moe_w_up, moe_w_down, ln2_g,
           ln2_b):
    bsz, seq, d = x.shape
    ctx_len = ctx.shape[1]
    assert d == D_MODEL and ctx_len == TILE and seq % TILE == 0 and seq % GRID_W == 0
    n_lat_tiles = seq // TILE
    n_lat_chunks = seq // CHUNK

    cond = _pad_rows(jnp.concatenate([c, c_ctx[None, :]], axis=0), 16)
    ada = _ada_call(cond, w_ada, b_ada)
    mod_lat = ada[:, :bsz].reshape(DEPTH, bsz, N_MOD, d)
    mod_ctx = jnp.broadcast_to(ada[:, bsz].reshape(DEPTH, 1, N_MOD, d), mod_lat.shape)
    mod = _pad_rows(jnp.stack([mod_lat, mod_ctx], axis=2), MOD_ROWS)

    w_in_bf = w_in.astype(BF16)
    w_out_bf = w_out.astype(BF16)
    sgu_w_bf = sgu_w.astype(BF16)
    w_gate_bf = moe_w_gate.astype(BF16)
    w_up_bf = moe_w_up.astype(BF16)
    w_down_bf = moe_w_down.astype(BF16)
    sgu_bias = jnp.repeat(jnp.swapaxes(sgu_b, 1, 2), SGU_W // SGU_GROUPS, axis=2)
    conv_w8 = _pad_rows(conv_w, 8)
    ln1 = _pad_rows(jnp.stack([ln1_g, ln1_b], axis=1), 8)
    ln2 = _pad_rows(jnp.stack([ln2_g, ln2_b], axis=1), 8)
    w_route = jnp.concatenate([router_group_w, jnp.swapaxes(router_expert_w, 1, 2).reshape(DEPTH, d, N_EXPERTS)], axis=2)
    w_route_bf = jnp.pad(w_route, ((0, 0), (0, 0), (0, ROUTE_W - w_route.shape[2]))).astype(BF16)
    b_route = jnp.concatenate([router_group_b, router_expert_b.reshape(DEPTH, N_EXPERTS)], axis=1)
    b_route = jnp.pad(b_route, ((0, 0), (0, ROUTE_W - b_route.shape[1])))[:, None, :].astype(F32)
    lg = jnp.stack([jax.nn.log_sigmoid(ret_decay_fwd.astype(F32)), jax.nn.log_sigmoid(ret_decay_bwd.astype(F32))],
                   axis=1)
    lg_lanes = jnp.repeat(lg, RET_DK, axis=2).reshape(DEPTH * 2, RET_W)
    lg_rows = lg_lanes[:, None, :]
    lg_cols = lg_lanes[:, :, None]

    xa = jnp.concatenate([x, ctx], axis=1)
    for layer in range(DEPTH):
        last = layer == DEPTH - 1
        pc, q, k, v, g, su, sv = _inproj_call(layer, xa, mod, w_in_bf, n_lat_tiles)
        sf, sb = _state_call(layer, k, v, lg_rows, lg_cols, n_lat_chunks)
        n_proc = n_lat_tiles if last else n_lat_tiles + 1
        x1, hmx = _mix_call(layer, n_lat_tiles, n_proc, lg[layer, 0], lg[layer, 1], pc, q, k, v, g, su, sv, sf, sb, xa,
                            mod, conv_w8, sgu_w_bf, sgu_bias, w_out_bf, ln1, w_route_bf, b_route)
        hmx_flat = hmx.reshape(-1, ROW_W)
        class_of_token, rank, counts = _rank_call(hmx_flat)
        dest, pad_end, pad_len, blk_lo, blk_hi, n_used, n_slots = _dispatch_plan(class_of_token[:, 0], rank[:, 0],
                                                                                  counts[0])
        xs = _dispatch_call(dest, pad_end, pad_len, n_used, hmx_flat, n_slots)
        ys = _expert_call(layer, blk_lo, blk_hi, n_used, xs, w_gate_bf, w_up_bf, w_down_bf)
        xa = _combine_call(layer, n_lat_tiles, dest, ys, x1, mod, ln2)
    return xa
```

```python
import functools

import jax
import jax.numpy as jnp
import numpy as np
from jax import lax
from jax.experimental import pallas as pl
from jax.experimental.pallas import tpu as pltpu

F32 = jnp.float32
BF16 = jnp.bfloat16

D_MODEL = 1024
DEPTH = 2
GRID_W = 64
CONV_W = 256
RET_W = 512
RET_HEADS = 8
RET_DK = 64
PAIR_W = 2 * RET_DK
CHUNK = 128
SGU_W = 256
SGU_GROUPS = 4
IN_COLS = 3 * CONV_W + 4 * RET_W + 2 * SGU_W
N_GROUPS = 4
EXPERTS_PER_GROUP = 8
N_EXPERTS = N_GROUPS * EXPERTS_PER_GROUP
EXPERT_HIDDEN = 512
N_MOD = 6
MOD_ROWS = 8
LN_EPS = 1e-5
ALPHA = (2 * DEPTH) ** 0.25

TILE = 256
HALO = GRID_W
BLOCK_M = 256
ROUTE_W = 128
ROW_W = D_MODEL + ROUTE_W
PAIRS_PER_GROUP = EXPERTS_PER_GROUP * (EXPERTS_PER_GROUP - 1) // 2
N_CLASSES = N_GROUPS * PAIRS_PER_GROUP
RANK_TILE = 512
CLASS_LO = np.array([g * EXPERTS_PER_GROUP + lo for g in range(N_GROUPS) for lo in range(EXPERTS_PER_GROUP)
                     for hi in range(lo + 1, EXPERTS_PER_GROUP)] + [N_EXPERTS - 2] * (ROUTE_W - N_CLASSES), np.int32)
CLASS_HI = np.array([g * EXPERTS_PER_GROUP + hi for g in range(N_GROUPS) for lo in range(EXPERTS_PER_GROUP)
                     for hi in range(lo + 1, EXPERTS_PER_GROUP)] + [N_EXPERTS - 1] * (ROUTE_W - N_CLASSES), np.int32)
VMEM_LIMIT = 56 * 1024 * 1024


def _params(n_axes):
    return pltpu.CompilerParams(dimension_semantics=("arbitrary",) * n_axes, vmem_limit_bytes=VMEM_LIMIT)


def _standardize(v):
    mu = jnp.mean(v, axis=-1, keepdims=True)
    var = jnp.mean(jnp.square(v - mu), axis=-1, keepdims=True)
    return (v - mu) * lax.rsqrt(var + LN_EPS)


def _silu(v):
    return v * jax.nn.sigmoid(v)


def _ada_kernel(c_ref, w_ref, b_ref, o_ref):
    a = _silu(c_ref[...]).astype(BF16)
    o_ref[0] = jnp.dot(a, w_ref[0].astype(BF16), preferred_element_type=F32) + b_ref[0]


def _ada_call(cond, w_ada, b_ada):
    rows = cond.shape[0]
    cols = w_ada.shape[-1]
    tn = 1536
    return pl.pallas_call(
        _ada_kernel,
        grid=(DEPTH, cols // tn),
        in_specs=[
            pl.BlockSpec((rows, D_MODEL), lambda l, j: (0, 0)),
            pl.BlockSpec((1, D_MODEL, tn), lambda l, j: (l, 0, j)),
            pl.BlockSpec((1, 1, tn), lambda l, j: (l, 0, j)),
        ],
        out_specs=pl.BlockSpec((1, rows, tn), lambda l, j: (l, 0, j)),
        out_shape=jax.ShapeDtypeStruct((DEPTH, rows, cols), F32),
        compiler_params=_params(2),
    )(cond, w_ada, b_ada.reshape(DEPTH, 1, cols))


def _inproj_kernel(x_ref, mod_ref, w_ref, lgf_ref, lgb_ref, pc_ref, q_ref, k_ref, v_ref, g_ref, su_ref, sv_ref, kv_ref):
    mod = mod_ref[0, 0, 0]
    h = (x_ref[0] * (1.0 + mod[1:2]) + mod[0:1]).astype(BF16)

    def proj(lo, hi):
        return jnp.dot(h, w_ref[0, :, lo:hi], preferred_element_type=F32)

    o = 3 * CONV_W
    pc_ref[0] = proj(0, o)
    q_ref[0] = proj(o, o + RET_W)
    k = proj(o + RET_W, o + 2 * RET_W) * (RET_DK ** -0.5)
    k_ref[0] = k
    v = proj(o + 2 * RET_W, o + 3 * RET_W)
    v_ref[0] = v
    g_ref[0] = proj(o + 3 * RET_W, o + 4 * RET_W)
    o += 4 * RET_W
    su_ref[0] = proj(o, o + SGU_W)
    sv_ref[0] = proj(o + SGU_W, o + 2 * SGU_W)

    pos = lax.broadcasted_iota(jnp.int32, (CHUNK, 1), 0).astype(F32)
    k_decay = (jnp.exp((CHUNK - 1.0 - pos) * lgf_ref[0]), jnp.exp(pos * lgb_ref[0]))
    contract_rows = (((0,), (0,)), ((), ()))
    for c in range(TILE // CHUNK):
        rows = slice(c * CHUNK, (c + 1) * CHUNK)
        vc = v[rows].astype(BF16)
        for d in range(2):
            kc = (k[rows] * k_decay[d]).astype(BF16)
            for p in range(RET_W // PAIR_W):
                cols = slice(p * PAIR_W, (p + 1) * PAIR_W)
                kv_ref[d, 0, c, cols, :] = lax.dot_general(kc[:, cols], vc[:, cols], contract_rows,
                                                           preferred_element_type=F32)


def _inproj_call(layer, xa, mod, w_in_bf, lg_rows, n_lat_tiles):
    bsz, length, _ = xa.shape
    widths = (3 * CONV_W, RET_W, RET_W, RET_W, RET_W, SGU_W, SGU_W)
    cpt = TILE // CHUNK
    return pl.pallas_call(
        _inproj_kernel,
        grid=(bsz, length // TILE),
        in_specs=[
            pl.BlockSpec((1, TILE, D_MODEL), lambda b, i: (b, i, 0)),
            pl.BlockSpec((1, 1, 1, MOD_ROWS, D_MODEL), lambda b, i: (layer, b, jnp.minimum(i // n_lat_tiles, 1), 0, 0)),
            pl.BlockSpec((1, D_MODEL, IN_COLS), lambda b, i: (layer, 0, 0)),
            pl.BlockSpec((1, 1, RET_W), lambda b, i: (2 * layer, 0, 0)),
            pl.BlockSpec((1, 1, RET_W), lambda b, i: (2 * layer + 1, 0, 0)),
        ],
        out_specs=[pl.BlockSpec((1, TILE, w), lambda b, i: (b, i, 0)) for w in widths]
        + [pl.BlockSpec((2, 1, cpt, RET_W, PAIR_W), lambda b, i: (0, b, i, 0, 0))],
        out_shape=[jax.ShapeDtypeStruct((bsz, length, w), F32) for w in widths]
        + [jax.ShapeDtypeStruct((2, bsz, length // CHUNK, RET_W, PAIR_W), F32)],
        compiler_params=_params(2),
    )(xa, mod, w_in_bf, lg_rows, lg_rows)


def _scan_kernel(n_lat_chunks, kv_ref, lgc_ref, s_ref, state):
    nc = kv_ref.shape[2]
    backward = pl.program_id(1)
    decay = jnp.exp(CHUNK * lgc_ref[0])
    row_head = lax.broadcasted_iota(jnp.int32, (RET_W, PAIR_W), 0) % PAIR_W // RET_DK
    lane_head = lax.broadcasted_iota(jnp.int32, (RET_W, PAIR_W), 1) // RET_DK
    own_head = row_head == lane_head
    state[...] = jnp.zeros_like(state)

    def step(s, carry):
        c = jnp.where(backward == 1, nc - 1 - s, (s + n_lat_chunks) % nc)
        s_ref[0, 0, c] = jnp.where(own_head, state[...], 0.0).astype(BF16)
        state[...] = decay * state[...] + kv_ref[0, 0, c]
        return carry

    lax.fori_loop(0, nc, step, 0)


def _scan_call(layer, kv, lg_cols, n_lat_chunks):
    _, bsz, nc, _, _ = kv.shape
    blk = (1, 1, nc, RET_W, PAIR_W)
    return pl.pallas_call(
        functools.partial(_scan_kernel, n_lat_chunks),
        grid=(bsz, 2),
        in_specs=[pl.BlockSpec(blk, lambda b, d: (d, b, 0, 0, 0)),
                  pl.BlockSpec((1, RET_W, 1), lambda b, d: (2 * layer + d, 0, 0))],
        out_specs=pl.BlockSpec(blk, lambda b, d: (d, b, 0, 0, 0)),
        out_shape=jax.ShapeDtypeStruct(kv.shape, BF16),
        scratch_shapes=[pltpu.VMEM((RET_W, PAIR_W), F32)],
        compiler_params=_params(2),
    )(kv, lg_cols)


def _route(logits):
    lane = lax.broadcasted_iota(jnp.int32, logits.shape, 1)
    big = jnp.int32(ROUTE_W)
    neg = jnp.float32(-jnp.inf)

    def top(vals):
        m = jnp.max(vals, axis=-1, keepdims=True)
        idx = jnp.min(jnp.where(vals == m, lane, big), axis=-1, keepdims=True)
        return m, idx

    gl = jnp.where(lane < N_GROUPS, logits, neg)
    gmax, gidx = top(gl)
    g_prob = 1.0 / jnp.sum(jnp.exp(gl - gmax), axis=-1, keepdims=True)
    lo = N_GROUPS + EXPERTS_PER_GROUP * gidx
    el = jnp.where((lane >= lo) & (lane < lo + EXPERTS_PER_GROUP), logits, neg)
    m1, i1 = top(el)
    m2, i2 = top(jnp.where(lane == i1, neg, el))
    e2 = jnp.exp(m2 - m1)
    gate1 = g_prob * (1.0 / (1.0 + e2))
    gate2 = g_prob * (e2 / (1.0 + e2))
    first_is_lo = i1 < i2
    e_lo = jnp.minimum(i1, i2) - lo
    e_hi = jnp.maximum(i1, i2) - lo
    pair = e_lo * (EXPERTS_PER_GROUP - 1) - ((e_lo * (e_lo - 1)) >> 1) + (e_hi - e_lo - 1)
    out = jnp.where(lane == 0, (gidx * PAIRS_PER_GROUP + pair).astype(F32), 0.0)
    out = jnp.where(lane == 1, jnp.where(first_is_lo, gate1, gate2), out)
    return jnp.where(lane == 2, jnp.where(first_is_lo, gate2, gate1), out)


def _mix_kernel(n_lat_tiles, lgf_ref, lgb_ref, pc_ref, hp_ref, hn_ref, q_ref, k_ref, v_ref, g_ref, su_ref, sv_ref,
                sf_ref, sb_ref, x_ref, mod_ref, convw_ref, sguw_ref, sgub_ref, wout_ref, ln_ref, wr_ref, br_ref,
                lgfr_ref, lgbr_ref, x1_ref, hmx_ref, ycat, dec, qdec):
    i = pl.program_id(1)
    is_ctx = i >= n_lat_tiles
    row = lax.broadcasted_iota(jnp.int32, (TILE, 1), 0)

    pc = pc_ref[0]
    z = pc[:, CONV_W:2 * CONV_W] * pc[:, 2 * CONV_W:3 * CONV_W]
    line_mask = jnp.where(is_ctx, TILE - 1, GRID_W - 1)
    first = (row & line_mask) == 0
    last = (row & line_mask) == line_mask
    z_prev = jnp.where(first, 0.0, pltpu.roll(z, 1, 0))
    z_next = jnp.where(last, 0.0, pltpu.roll(z, TILE - 1, 0))
    hp = hp_ref[0]
    hn = hn_ref[0]
    z_top = jnp.where(i == 0, 0.0, hp[:, CONV_W:2 * CONV_W] * hp[:, 2 * CONV_W:3 * CONV_W])
    z_bot = jnp.where(i == n_lat_tiles - 1, 0.0, hn[:, CONV_W:2 * CONV_W] * hn[:, 2 * CONV_W:3 * CONV_W])
    zcat = jnp.concatenate([z_top, z, z_bot], axis=0)
    z_up = zcat[0:TILE]
    z_down = zcat[2 * HALO:2 * HALO + TILE]
    along_seq = lax.broadcasted_iota(jnp.int32, (1, CONV_W), 1) < jnp.where(is_ctx, CONV_W, CONV_W // 2)
    cw = convw_ref[0]
    conv = cw[0:1] * jnp.where(along_seq, z_prev, z_up) + cw[1:2] * z
    conv = conv + cw[2:3] * jnp.where(along_seq, z_next, z_down)
    ycat[:, 0:CONV_W] = (pc[:, 0:CONV_W] * conv).astype(BF16)

    vn = _standardize(sv_ref[0]).astype(BF16)
    group = lax.broadcasted_iota(jnp.int32, (1, SGU_W), 1) // (SGU_W // SGU_GROUPS)
    for c in range(TILE // CHUNK):
        rows = slice(c * CHUNK, (c + 1) * CHUNK)
        mixed = jnp.zeros((CHUNK, SGU_W), F32)
        for gi in range(SGU_GROUPS):
            m = jnp.dot(sguw_ref[0, gi], vn[rows], preferred_element_type=F32)
            mixed = jnp.where(group == gi, m, mixed)
        ycat[rows, CONV_W + RET_W:] = (su_ref[0, rows, :] * (mixed + sgub_ref[0])).astype(BF16)

    @pl.when((pl.program_id(0) == 0) & (i == 0))
    def _():
        pos = lax.broadcasted_iota(jnp.int32, (CHUNK, 1), 0).astype(F32)
        rel = pos - lax.broadcasted_iota(jnp.int32, (1, CHUNK), 1).astype(F32)
        for h in range(RET_HEADS):
            dec[h // 2, (h % 2) * CHUNK:(h % 2 + 1) * CHUNK, :] = jnp.where(
                rel > 0, jnp.exp(lgf_ref[h] * jnp.maximum(rel, 0.0)),
                jnp.where(rel < 0, jnp.exp(lgb_ref[h] * jnp.maximum(-rel, 0.0)), 2.0))
        qdec[0] = jnp.exp(lgfr_ref[0] * (pos + 1.0))
        qdec[1] = jnp.exp(lgbr_ref[0] * (CHUNK - pos))

    low_head = lax.broadcasted_iota(jnp.int32, (1, PAIR_W), 1) < RET_DK

    def half_norm(o):
        def half_mean(t):
            lo = jnp.sum(jnp.where(low_head, t, 0.0), axis=-1, keepdims=True)
            hi = jnp.sum(jnp.where(low_head, 0.0, t), axis=-1, keepdims=True)
            return jnp.where(low_head, lo, hi) * (1.0 / RET_DK)

        centred = o - half_mean(o)
        return centred * lax.rsqrt(half_mean(jnp.square(centred)) + LN_EPS)

    for c in range(TILE // CHUNK):
        rows = slice(c * CHUNK, (c + 1) * CHUNK)
        for p in range(RET_W // PAIR_W):
            cols = slice(p * PAIR_W, (p + 1) * PAIR_W)
            qp = q_ref[0, rows, cols]
            vp = v_ref[0, rows, cols]
            q_split = jnp.concatenate([jnp.where(low_head, qp, 0.0), jnp.where(low_head, 0.0, qp)], axis=0)
            scores = lax.dot_general(q_split.astype(BF16), k_ref[0, rows, cols].astype(BF16),
                                     (((1,), (1,)), ((), ())), preferred_element_type=F32) * dec[p]
            lhs = jnp.concatenate([scores[0:CHUNK].astype(BF16), scores[CHUNK:].astype(BF16),
                                   (qp * qdec[0, :, cols]).astype(BF16), (qp * qdec[1, :, cols]).astype(BF16)], axis=1)
            rhs = jnp.concatenate([jnp.where(low_head, vp, 0.0).astype(BF16), jnp.where(low_head, 0.0, vp).astype(BF16),
                                   sf_ref[0, 0, c, cols, :], sb_ref[0, 0, c, cols, :]], axis=0)
            o = jnp.dot(lhs, rhs, preferred_element_type=F32)
            ycat[rows, CONV_W + p * PAIR_W:CONV_W + (p + 1) * PAIR_W] = (
                _silu(g_ref[0, rows, cols]) * half_norm(o)).astype(BF16)

    y = jnp.dot(ycat[...], wout_ref[0], preferred_element_type=F32)
    mod = mod_ref[0, 0, 0]
    ln = ln_ref[0]
    x1 = _standardize(ALPHA * x_ref[0] + mod[2:3] * y) * ln[0:1] + ln[1:2]
    x1_ref[0] = x1
    hm = x1 * (1.0 + mod[4:5]) + mod[3:4]
    hmx_ref[0, :, 0:D_MODEL] = hm
    logits = jnp.dot(hm.astype(BF16), wr_ref[0], preferred_element_type=F32) + br_ref[0]
    hmx_ref[0, :, D_MODEL:] = _route(logits)


def _mix_call(layer, n_lat_tiles, n_proc, lgf, lgb, pc, q, k, v, g, su, sv, states, xa, mod, conv_w, sgu_w_bf, sgu_bias,
              w_out_bf, ln1, w_route_bf, b_route, lg_rows):
    bsz, length, _ = xa.shape
    halos_per_tile = TILE // HALO
    n_halo = length // HALO
    cpt = TILE // CHUNK
    tok = lambda w: pl.BlockSpec((1, TILE, w), lambda b, i, *_: (b, i, 0))
    per_layer = lambda *shape: pl.BlockSpec((1,) + shape, lambda b, i, *_: (layer,) + (0,) * len(shape))
    grid_spec = pltpu.PrefetchScalarGridSpec(
        num_scalar_prefetch=2,
        grid=(bsz, n_proc),
        in_specs=[
            tok(3 * CONV_W),
            pl.BlockSpec((1, HALO, 3 * CONV_W), lambda b, i, *_: (b, jnp.maximum(i * halos_per_tile - 1, 0), 0)),
            pl.BlockSpec((1, HALO, 3 * CONV_W),
                         lambda b, i, *_: (b, jnp.minimum((i + 1) * halos_per_tile, n_halo - 1), 0)),
            tok(RET_W), tok(RET_W), tok(RET_W), tok(RET_W), tok(SGU_W), tok(SGU_W),
            pl.BlockSpec((1, 1, cpt, RET_W, PAIR_W), lambda b, i, *_: (0, b, i, 0, 0)),
            pl.BlockSpec((1, 1, cpt, RET_W, PAIR_W), lambda b, i, *_: (1, b, i, 0, 0)),
            tok(D_MODEL),
            pl.BlockSpec((1, 1, 1, MOD_ROWS, D_MODEL),
                         lambda b, i, *_: (layer, b, jnp.minimum(i // n_lat_tiles, 1), 0, 0)),
            per_layer(8, CONV_W),
            per_layer(SGU_GROUPS, CHUNK, CHUNK),
            per_layer(CHUNK, SGU_W),
            per_layer(D_MODEL, D_MODEL),
            per_layer(8, D_MODEL),
            per_layer(D_MODEL, ROUTE_W),
            per_layer(1, ROUTE_W),
            pl.BlockSpec((1, 1, RET_W), lambda b, i, *_: (2 * layer, 0, 0)),
            pl.BlockSpec((1, 1, RET_W), lambda b, i, *_: (2 * layer + 1, 0, 0)),
        ],
        out_specs=[tok(D_MODEL), tok(ROW_W)],
        scratch_shapes=[pltpu.VMEM((TILE, D_MODEL), BF16),
                        pltpu.VMEM((RET_W // PAIR_W, 2 * CHUNK, CHUNK), F32),
                        pltpu.VMEM((2, CHUNK, RET_W), F32)],
    )
    return pl.pallas_call(
        functools.partial(_mix_kernel, n_lat_tiles),
        grid_spec=grid_spec,
        out_shape=[jax.ShapeDtypeStruct((bsz, n_proc * TILE, D_MODEL), F32),
                   jax.ShapeDtypeStruct((bsz, n_proc * TILE, ROW_W), F32)],
        compiler_params=_params(2),
    )(lgf, lgb, pc, pc, pc, q, k, v, g, su, sv, states, states, xa, mod, conv_w, sgu_w_bf, sgu_bias, w_out_bf, ln1,
      w_route_bf, b_route, lg_rows, lg_rows)


def _rank_kernel(route_ref, class_ref, rank_ref, counts_ref, running):
    @pl.when(pl.program_id(0) == 0)
    def _():
        running[...] = jnp.zeros_like(running)

    lane = lax.broadcasted_iota(jnp.int32, (RANK_TILE, ROUTE_W), 1)
    cls = route_ref[:, 0:1].astype(jnp.int32)
    class_ref[...] = cls
    onehot = lane == cls
    earlier = (lax.broadcasted_iota(jnp.int32, (RANK_TILE, RANK_TILE), 0)
               > lax.broadcasted_iota(jnp.int32, (RANK_TILE, RANK_TILE), 1))
    before = jnp.dot(earlier.astype(BF16), onehot.astype(BF16), preferred_element_type=F32) + running[...]
    rank_ref[...] = jnp.sum(jnp.where(onehot, before, 0.0), axis=-1, keepdims=True).astype(jnp.int32)
    running[...] += jnp.sum(onehot.astype(F32), axis=0, keepdims=True)
    counts_ref[...] = running[...]


def _rank_call(hmx_flat):
    n_tok = hmx_flat.shape[0]
    return pl.pallas_call(
        _rank_kernel,
        grid=(n_tok // RANK_TILE,),
        in_specs=[pl.BlockSpec((RANK_TILE, ROUTE_W), lambda i: (i, D_MODEL // ROUTE_W))],
        out_specs=[pl.BlockSpec((RANK_TILE, 1), lambda i: (i, 0)), pl.BlockSpec((RANK_TILE, 1), lambda i: (i, 0)),
                   pl.BlockSpec((1, ROUTE_W), lambda i: (0, 0))],
        out_shape=[jax.ShapeDtypeStruct((n_tok, 1), jnp.int32), jax.ShapeDtypeStruct((n_tok, 1), jnp.int32),
                   jax.ShapeDtypeStruct((1, ROUTE_W), F32)],
        scratch_shapes=[pltpu.VMEM((1, ROUTE_W), F32)],
        compiler_params=_params(1),
    )(hmx_flat)


def _dispatch_kernel(dest_ref, pad_end_ref, pad_len_ref, n_used_ref, hmx_ref, xs_ref, zeros, sem, pad_sem):
    step = pl.program_id(0)
    base = step * TILE
    half = BLOCK_M // 2

    def for_each_pad_copy(fn):
        def per_class(c, carry):
            off = pad_end_ref[c]
            n = pad_len_ref[c]
            for shift in range(BLOCK_M.bit_length() - 2, -1, -1):
                bit = 1 << shift
                off = off - (n & bit)

                @pl.when((n & bit) != 0)
                def _():
                    if bit >= 8:
                        fn(pltpu.make_async_copy(zeros.at[pl.ds(0, bit)], xs_ref.at[pl.ds(pl.multiple_of(off, 8), bit)],
                                                 pad_sem))
                    else:
                        for j in range(bit):
                            fn(pltpu.make_async_copy(zeros.at[pl.ds(0, 1)], xs_ref.at[pl.ds(off + j, 1)], pad_sem))

            return carry

        lax.fori_loop(0, N_CLASSES, per_class, 0)

        def per_half_block(j, carry):
            fn(pltpu.make_async_copy(zeros, xs_ref.at[pl.ds(pl.multiple_of(j * half, 8), half)], pad_sem))
            return carry

        lax.fori_loop(n_used_ref[0] * 2, xs_ref.shape[0] // half, per_half_block, 0)

    @pl.when(step == 0)
    def _():
        zeros[...] = jnp.zeros_like(zeros)
        for_each_pad_copy(lambda cp: cp.start())

    for r in range(TILE):
        pltpu.make_async_copy(hmx_ref.at[pl.ds(r, 1)], xs_ref.at[pl.ds(dest_ref[base + r], 1)], sem).start()
    for r in range(TILE):
        pltpu.make_async_copy(hmx_ref.at[pl.ds(0, 1)], xs_ref.at[pl.ds(0, 1)], sem).wait()

    @pl.when(step == pl.num_programs(0) - 1)
    def _():
        for_each_pad_copy(lambda cp: cp.wait())


def _dispatch_call(dest, pad_end, pad_len, n_used, hmx_flat, n_slots):
    n_tok = hmx_flat.shape[0]
    grid_spec = pltpu.PrefetchScalarGridSpec(
        num_scalar_prefetch=4,
        grid=(n_tok // TILE,),
        in_specs=[pl.BlockSpec((TILE, ROW_W), lambda i, *_: (i, 0))],
        out_specs=pl.BlockSpec(memory_space=pl.ANY),
        scratch_shapes=[pltpu.VMEM((BLOCK_M // 2, ROW_W), F32), pltpu.SemaphoreType.DMA, pltpu.SemaphoreType.DMA],
    )
    return pl.pallas_call(
        _dispatch_kernel,
        grid_spec=grid_spec,
        out_shape=jax.ShapeDtypeStruct((n_slots, ROW_W), F32),
        compiler_params=_params(1),
    )(dest, pad_end, pad_len, n_used, hmx_flat)


def _expert_kernel(n_used_ref, xs_ref, wg_lo, wu_lo, wd_lo, wg_hi, wu_hi, wd_hi, ys_ref):
    used = pl.program_id(0) < n_used_ref[0]

    @pl.when(used)
    def _():
        xb = xs_ref[:, 0:D_MODEL].astype(BF16)
        gates = xs_ref[:, D_MODEL:]

        def mlp(wg, wu, wd):
            gate = jnp.dot(xb, wg[0, 0], preferred_element_type=F32)
            up = jnp.dot(xb, wu[0, 0], preferred_element_type=F32)
            return jnp.dot((_silu(gate) * up).astype(BF16), wd[0, 0], preferred_element_type=F32)

        ys_ref[...] = mlp(wg_lo, wu_lo, wd_lo) * gates[:, 1:2] + mlp(wg_hi, wu_hi, wd_hi) * gates[:, 2:3]

    @pl.when(jnp.logical_not(used))
    def _():
        ys_ref[...] = jnp.zeros_like(ys_ref)


def _expert_call(layer, blk_lo, blk_hi, n_used, xs, w_gate_bf, w_up_bf, w_down_bf):
    n_blocks = xs.shape[0] // BLOCK_M
    up_spec = lambda which: pl.BlockSpec((1, 1, D_MODEL, EXPERT_HIDDEN),
                                         lambda i, lo, hi, nu: (layer, (lo, hi)[which][i], 0, 0))
    down_spec = lambda which: pl.BlockSpec((1, 1, EXPERT_HIDDEN, D_MODEL),
                                           lambda i, lo, hi, nu: (layer, (lo, hi)[which][i], 0, 0))
    grid_spec = pltpu.PrefetchScalarGridSpec(
        num_scalar_prefetch=3,
        grid=(n_blocks,),
        in_specs=[pl.BlockSpec((BLOCK_M, ROW_W), lambda i, lo, hi, nu: (jnp.minimum(i, nu[0] - 1), 0)),
                  up_spec(0), up_spec(0), down_spec(0), up_spec(1), up_spec(1), down_spec(1)],
        out_specs=pl.BlockSpec((BLOCK_M, D_MODEL), lambda i, *_: (i, 0)),
    )
    kern = lambda lo, hi, nu, *refs: _expert_kernel(nu, *refs)
    return pl.pallas_call(
        kern,
        grid_spec=grid_spec,
        out_shape=jax.ShapeDtypeStruct((xs.shape[0], D_MODEL), F32),
        compiler_params=_params(1),
    )(blk_lo, blk_hi, n_used, xs, w_gate_bf, w_up_bf, w_down_bf, w_gate_bf, w_up_bf, w_down_bf)


def _combine_kernel(tiles_per_batch, dest_ref, ys_ref, x1_ref, mod_ref, ln_ref, out_ref, buf, sem):
    step = pl.program_id(0) * tiles_per_batch + pl.program_id(1)
    slot = step % 2
    has_next = step + 1 < pl.num_programs(0) * tiles_per_batch

    def gather(tile, to_slot):
        base = tile * TILE

        for r in range(TILE):
            pltpu.make_async_copy(ys_ref.at[pl.ds(dest_ref[base + r], 1)], buf.at[to_slot, pl.ds(r, 1)],
                                  sem.at[to_slot]).start()

    @pl.when(step == 0)
    def _():
        gather(step, slot)

    @pl.when(has_next)
    def _():
        gather(step + 1, 1 - slot)

    for r in range(TILE):
        pltpu.make_async_copy(ys_ref.at[pl.ds(0, 1)], buf.at[slot, pl.ds(0, 1)], sem.at[slot]).wait()

    mod = mod_ref[0, 0, 0]
    ln = ln_ref[0]
    out_ref[0] = _standardize(ALPHA * x1_ref[0] + mod[5:6] * buf[slot]) * ln[0:1] + ln[1:2]


def _combine_call(layer, n_lat_tiles, dest, ys, x1, mod, ln2):
    bsz, length, _ = x1.shape
    tiles_per_batch = length // TILE
    tok = lambda w: pl.BlockSpec((1, TILE, w), lambda b, i, *_: (b, i, 0))
    grid_spec = pltpu.PrefetchScalarGridSpec(
        num_scalar_prefetch=1,
        grid=(bsz, tiles_per_batch),
        in_specs=[
            pl.BlockSpec(memory_space=pl.ANY),
            tok(D_MODEL),
            pl.BlockSpec((1, 1, 1, MOD_ROWS, D_MODEL),
                         lambda b, i, *_: (layer, b, jnp.minimum(i // n_lat_tiles, 1), 0, 0)),
            pl.BlockSpec((1, 8, D_MODEL), lambda b, i, *_: (layer, 0, 0)),
        ],
        out_specs=tok(D_MODEL),
        scratch_shapes=[pltpu.VMEM((2, TILE, D_MODEL), F32), pltpu.SemaphoreType.DMA((2,))],
    )
    return pl.pallas_call(
        functools.partial(_combine_kernel, tiles_per_batch),
        grid_spec=grid_spec,
        out_shape=jax.ShapeDtypeStruct(x1.shape, F32),
        compiler_params=_params(2),
    )(dest, ys, x1, mod, ln2)


def _dispatch_plan(class_of_token, rank, counts):
    n_tok = class_of_token.shape[0]
    counts = counts.astype(jnp.int32)
    pcounts = (counts + BLOCK_M - 1) // BLOCK_M * BLOCK_M
    pends = jnp.cumsum(pcounts)
    pstarts = pends - pcounts
    classes = jnp.arange(ROUTE_W, dtype=jnp.int32)
    dest = jnp.sum(jnp.where(class_of_token[:, None] == classes[None, :], pstarts[None, :], 0), axis=1) + rank
    n_blocks = n_tok // BLOCK_M + N_CLASSES
    blk_start = jnp.arange(n_blocks, dtype=jnp.int32) * BLOCK_M
    blk_class = jnp.minimum(jnp.sum((pends[None, :] <= blk_start[:, None]).astype(jnp.int32), axis=1), N_CLASSES - 1)
    n_used = pends[-1:] // BLOCK_M
    return (dest.astype(jnp.int32), pends, pcounts - counts, jnp.asarray(CLASS_LO)[blk_class],
            jnp.asarray(CLASS_HI)[blk_class], n_used.astype(jnp.int32), n_blocks * BLOCK_M)


def _pad_rows(a, rows):
    return jnp.pad(a, [(0, 0)] * (a.ndim - 2) + [(0, rows - a.shape[-2]), (0, 0)])


def kernel(x, c, ctx, c_ctx, w_ada, b_ada, w_in, conv_w, ret_decay_fwd, ret_decay_bwd, sgu_w, sgu_b, w_out, ln1_g, ln1_b,
           router_group_w, router_group_b, router_expert_w, router_expert_b, moe_w_gate, moe_w_up, moe_w_down, ln2_g,
           ln2_b):
    bsz, seq, d = x.shape
    ctx_len = ctx.shape[1]
    assert d == D_MODEL and ctx_len == TILE and seq % TILE == 0 and seq % GRID_W == 0
    n_lat_tiles = seq // TILE
    n_lat_chunks = seq // CHUNK

    cond = _pad_rows(jnp.concatenate([c, c_ctx[None, :]], axis=0), 16)
    ada = _ada_call(cond, w_ada, b_ada)
    mod_lat = ada[:, :bsz].reshape(DEPTH, bsz, N_MOD, d)
    mod_ctx = jnp.broadcast_to(ada[:, bsz].reshape(DEPTH, 1, N_MOD, d), mod_lat.shape)
    mod = _pad_rows(jnp.stack([mod_lat, mod_ctx], axis=2), MOD_ROWS)

    w_in_bf = w_in.astype(BF16)
    w_out_bf = w_out.astype(BF16)
    sgu_w_bf = sgu_w.astype(BF16)
    w_gate_bf = moe_w_gate.astype(BF16)
    w_up_bf = moe_w_up.astype(BF16)
    w_down_bf = moe_w_down.astype(BF16)
    sgu_bias = jnp.repeat(jnp.swapaxes(sgu_b, 1, 2), SGU_W // SGU_GROUPS, axis=2)
    conv_w8 = _pad_rows(conv_w, 8)
    ln1 = _pad_rows(jnp.stack([ln1_g, ln1_b], axis=1), 8)
    ln2 = _pad_rows(jnp.stack([ln2_g, ln2_b], axis=1), 8)
    w_route = jnp.concatenate([router_group_w, jnp.swapaxes(router_expert_w, 1, 2).reshape(DEPTH, d, N_EXPERTS)], axis=2)
    w_route_bf = jnp.pad(w_route, ((0, 0), (0, 0), (0, ROUTE_W - w_route.shape[2]))).astype(BF16)
    b_route = jnp.concatenate([router_group_b, router_expert_b.reshape(DEPTH, N_EXPERTS)], axis=1)
    b_route = jnp.pad(b_route, ((0, 0), (0, ROUTE_W - b_route.shape[1])))[:, None, :].astype(F32)
    lg = jnp.stack([jax.nn.log_sigmoid(ret_decay_fwd.astype(F32)), jax.nn.log_sigmoid(ret_decay_bwd.astype(F32))],
                   axis=1)
    lg_lanes = jnp.repeat(lg, RET_DK, axis=2).reshape(DEPTH * 2, RET_W)
    lg_rows = lg_lanes[:, None, :]
    lg_cols = lg_lanes[:, :, None]

    xa = jnp.concatenate([x, ctx], axis=1)
    for layer in range(DEPTH):
        last = layer == DEPTH - 1
        pc, q, k, v, g, su, sv, kv = _inproj_call(layer, xa, mod, w_in_bf, lg_rows, n_lat_tiles)
        states = _scan_call(layer, kv, lg_cols, n_lat_chunks)
        n_proc = n_lat_tiles if last else n_lat_tiles + 1
        x1, hmx = _mix_call(layer, n_lat_tiles, n_proc, lg[layer, 0], lg[layer, 1], pc, q, k, v, g, su, sv, states, xa,
                            mod, conv_w8, sgu_w_bf, sgu_bias, w_out_bf, ln1, w_route_bf, b_route, lg_rows)
        hmx_flat = hmx.reshape(-1, ROW_W)
        class_of_token, rank, counts = _rank_call(hmx_flat)
        dest, pad_end, pad_len, blk_lo, blk_hi, n_used, n_slots = _dispatch_plan(class_of_token[:, 0], rank[:, 0],
                                                                                  counts[0])
        xs = _dispatch_call(dest, pad_end, pad_len, n_used, hmx_flat, n_slots)
        ys = _expert_call(layer, blk_lo, blk_hi, n_used, xs, w_gate_bf, w_up_bf, w_down_bf)
        xa = _combine_call(layer, n_lat_tiles, dest, ys, x1, mod, ln2)
    return xa
```

```python
import functools
import math

import jax
import jax.numpy as jnp
import numpy as np
from jax import lax
from jax.experimental import pallas as pl
from jax.experimental.pallas import tpu as pltpu

F32 = jnp.float32
BF16 = jnp.bfloat16

D_MODEL = 1024
DEPTH = 2
GRID_W = 64
CONV_W = 256
RET_W = 512
RET_HEADS = 8
RET_DK = 64
PAIR_W = 2 * RET_DK
CHUNK = 128
SGU_W = 256
SGU_GROUPS = 4
IN_COLS = 3 * CONV_W + 4 * RET_W + 2 * SGU_W
N_GROUPS = 4
EXPERTS_PER_GROUP = 8
N_EXPERTS = N_GROUPS * EXPERTS_PER_GROUP
EXPERT_HIDDEN = 512
N_MOD = 6
MOD_ROWS = 8
LN_EPS = 1e-5
ALPHA = (2 * DEPTH) ** 0.25

TILE = 256
HALO = GRID_W
BLOCK_M = 256
ROUTE_W = 128
ROUTE_ROWS = 48
EXPERT_ROW0 = 8
RANK_TILES = 8
PAIRS_PER_GROUP = EXPERTS_PER_GROUP * (EXPERTS_PER_GROUP - 1) // 2
N_CLASSES = N_GROUPS * PAIRS_PER_GROUP
CLASS_LO = np.array([g * EXPERTS_PER_GROUP + lo for g in range(N_GROUPS) for lo in range(EXPERTS_PER_GROUP)
                     for hi in range(lo + 1, EXPERTS_PER_GROUP)] + [N_EXPERTS - 2] * (ROUTE_W - N_CLASSES), np.int32)
CLASS_HI = np.array([g * EXPERTS_PER_GROUP + hi for g in range(N_GROUPS) for lo in range(EXPERTS_PER_GROUP)
                     for hi in range(lo + 1, EXPERTS_PER_GROUP)] + [N_EXPERTS - 1] * (ROUTE_W - N_CLASSES), np.int32)
VMEM_LIMIT = 56 * 1024 * 1024


def _params(n_axes):
    return pltpu.CompilerParams(dimension_semantics=("arbitrary",) * n_axes, vmem_limit_bytes=VMEM_LIMIT)


def _standardize(v):
    mu = jnp.mean(v, axis=-1, keepdims=True)
    var = jnp.mean(jnp.square(v - mu), axis=-1, keepdims=True)
    return (v - mu) * lax.rsqrt(var + LN_EPS)


def _silu(v):
    return v * jax.nn.sigmoid(v)


def _ada_kernel(c_ref, w_ref, b_ref, o_ref):
    a = _silu(c_ref[...]).astype(BF16)
    o_ref[0] = jnp.dot(a, w_ref[0].astype(BF16), preferred_element_type=F32) + b_ref[0]


def _ada_call(cond, w_ada, b_ada):
    rows = cond.shape[0]
    cols = w_ada.shape[-1]
    tn = 1536
    return pl.pallas_call(
        _ada_kernel,
        grid=(DEPTH, cols // tn),
        in_specs=[
            pl.BlockSpec((rows, D_MODEL), lambda l, j: (0, 0)),
            pl.BlockSpec((1, D_MODEL, tn), lambda l, j: (l, 0, j)),
            pl.BlockSpec((1, 1, tn), lambda l, j: (l, 0, j)),
        ],
        out_specs=pl.BlockSpec((1, rows, tn), lambda l, j: (l, 0, j)),
        out_shape=jax.ShapeDtypeStruct((DEPTH, rows, cols), F32),
        compiler_params=_params(2),
    )(cond, w_ada, b_ada.reshape(DEPTH, 1, cols))


def _inproj_kernel(x_ref, mod_ref, w_ref, lgf_ref, lgb_ref, pc_ref, q_ref, k_ref, v_ref, g_ref, su_ref, sv_ref, kv_ref):
    mod = mod_ref[0, 0, 0]
    h = (x_ref[0] * (1.0 + mod[1:2]) + mod[0:1]).astype(BF16)

    def proj(lo, hi):
        return jnp.dot(h, w_ref[0, :, lo:hi], preferred_element_type=F32)

    o = 3 * CONV_W
    pc_ref[0] = proj(0, o)
    q_ref[0] = proj(o, o + RET_W)
    k = proj(o + RET_W, o + 2 * RET_W) * (RET_DK ** -0.5)
    k_ref[0] = k
    v = proj(o + 2 * RET_W, o + 3 * RET_W)
    v_ref[0] = v
    g_ref[0] = proj(o + 3 * RET_W, o + 4 * RET_W)
    o += 4 * RET_W
    su_ref[0] = proj(o, o + SGU_W)
    sv_ref[0] = proj(o + SGU_W, o + 2 * SGU_W)

    pos = lax.broadcasted_iota(jnp.int32, (CHUNK, 1), 0).astype(F32)
    k_decay = (jnp.exp((CHUNK - 1.0 - pos) * lgf_ref[0]), jnp.exp(pos * lgb_ref[0]))
    contract_rows = (((0,), (0,)), ((), ()))
    for c in range(TILE // CHUNK):
        rows = slice(c * CHUNK, (c + 1) * CHUNK)
        vc = v[rows].astype(BF16)
        for d in range(2):
            kc = (k[rows] * k_decay[d]).astype(BF16)
            for p in range(RET_W // PAIR_W):
                cols = slice(p * PAIR_W, (p + 1) * PAIR_W)
                kv_ref[d, 0, c, cols, :] = lax.dot_general(kc[:, cols], vc[:, cols], contract_rows,
                                                           preferred_element_type=F32)


def _inproj_call(layer, xa, mod, w_in_bf, lg_rows, n_lat_tiles):
    bsz, length, _ = xa.shape
    widths = (3 * CONV_W, RET_W, RET_W, RET_W, RET_W, SGU_W, SGU_W)
    cpt = TILE // CHUNK
    return pl.pallas_call(
        _inproj_kernel,
        grid=(bsz, length // TILE),
        in_specs=[
            pl.BlockSpec((1, TILE, D_MODEL), lambda b, i: (b, i, 0)),
            pl.BlockSpec((1, 1, 1, MOD_ROWS, D_MODEL), lambda b, i: (layer, b, jnp.minimum(i // n_lat_tiles, 1), 0, 0)),
            pl.BlockSpec((1, D_MODEL, IN_COLS), lambda b, i: (layer, 0, 0)),
            pl.BlockSpec((1, 1, RET_W), lambda b, i: (2 * layer, 0, 0)),
            pl.BlockSpec((1, 1, RET_W), lambda b, i: (2 * layer + 1, 0, 0)),
        ],
        out_specs=[pl.BlockSpec((1, TILE, w), lambda b, i: (b, i, 0)) for w in widths]
        + [pl.BlockSpec((2, 1, cpt, RET_W, PAIR_W), lambda b, i: (0, b, i, 0, 0))],
        out_shape=[jax.ShapeDtypeStruct((bsz, length, w), F32) for w in widths]
        + [jax.ShapeDtypeStruct((2, bsz, length // CHUNK, RET_W, PAIR_W), F32)],
        compiler_params=_params(2),
    )(xa, mod, w_in_bf, lg_rows, lg_rows)


def _scan_kernel(n_lat_chunks, kv_ref, lgc_ref, s_ref, state):
    nc = kv_ref.shape[2]
    backward = pl.program_id(1)
    decay = jnp.exp(CHUNK * lgc_ref[0])
    row_head = lax.broadcasted_iota(jnp.int32, (RET_W, PAIR_W), 0) % PAIR_W // RET_DK
    lane_head = lax.broadcasted_iota(jnp.int32, (RET_W, PAIR_W), 1) // RET_DK
    own_head = row_head == lane_head
    state[...] = jnp.zeros_like(state)

    def step(s, carry):
        c = jnp.where(backward == 1, nc - 1 - s, (s + n_lat_chunks) % nc)
        s_ref[0, 0, c] = jnp.where(own_head, state[...], 0.0).astype(BF16)
        state[...] = decay * state[...] + kv_ref[0, 0, c]
        return carry

    lax.fori_loop(0, nc, step, 0)


def _scan_call(layer, kv, lg_cols, n_lat_chunks):
    _, bsz, nc, _, _ = kv.shape
    blk = (1, 1, nc, RET_W, PAIR_W)
    return pl.pallas_call(
        functools.partial(_scan_kernel, n_lat_chunks),
        grid=(bsz, 2),
        in_specs=[pl.BlockSpec(blk, lambda b, d: (d, b, 0, 0, 0)),
                  pl.BlockSpec((1, RET_W, 1), lambda b, d: (2 * layer + d, 0, 0))],
        out_specs=pl.BlockSpec(blk, lambda b, d: (d, b, 0, 0, 0)),
        out_shape=jax.ShapeDtypeStruct(kv.shape, BF16),
        scratch_shapes=[pltpu.VMEM((RET_W, PAIR_W), F32)],
        compiler_params=_params(2),
    )(kv, lg_cols)


def _route_class(lt):
    assert EXPERTS_PER_GROUP == 8 and N_GROUPS <= 8
    sub = lax.broadcasted_iota(jnp.int32, (8, lt.shape[1]), 0)
    neg = jnp.float32(-jnp.inf)

    def top(vals):
        m = jnp.max(vals, axis=0, keepdims=True)
        return jnp.min(jnp.where(vals == m, sub, 8), axis=0, keepdims=True)

    gidx = top(jnp.where(sub < N_GROUPS, lt[0:8], neg))
    pair = jnp.zeros_like(gidx)
    for g in range(N_GROUPS):
        e = lt[EXPERT_ROW0 + 8 * g:EXPERT_ROW0 + 8 * (g + 1)]
        i1 = top(e)
        i2 = top(jnp.where(sub == i1, neg, e))
        e_lo = jnp.minimum(i1, i2)
        e_hi = jnp.maximum(i1, i2)
        pair_g = e_lo * (EXPERTS_PER_GROUP - 1) - ((e_lo * (e_lo - 1)) >> 1) + (e_hi - e_lo - 1)
        pair = jnp.where(gidx == g, pair_g, pair)
    return gidx * PAIRS_PER_GROUP + pair


def _mix_kernel(n_lat_tiles, lgf_ref, lgb_ref, pc_ref, hp_ref, hn_ref, q_ref, k_ref, v_ref, g_ref, su_ref, sv_ref,
                sf_ref, sb_ref, x_ref, mod_ref, convw_ref, sguw_ref, sgub_ref, wout_ref, ln_ref, wr_ref, br_ref,
                lgfr_ref, lgbr_ref, x1_ref, hm_ref, route_ref, ycat, dec, qdec):
    i = pl.program_id(1)
    is_ctx = i >= n_lat_tiles
    row = lax.broadcasted_iota(jnp.int32, (TILE, 1), 0)

    @pl.when((pl.program_id(0) == 0) & (i == 0))
    def _():
        pos = lax.broadcasted_iota(jnp.int32, (CHUNK, 1), 0).astype(F32)
        rel = pos - lax.broadcasted_iota(jnp.int32, (1, CHUNK), 1).astype(F32)
        for h in range(RET_HEADS):
            dec[h // 2, (h % 2) * CHUNK:(h % 2 + 1) * CHUNK, :] = jnp.where(
                rel > 0, jnp.exp(lgf_ref[h] * jnp.maximum(rel, 0.0)),
                jnp.where(rel < 0, jnp.exp(lgb_ref[h] * jnp.maximum(-rel, 0.0)), 2.0))
        qdec[0] = jnp.exp(lgfr_ref[0] * (pos + 1.0))
        qdec[1] = jnp.exp(lgbr_ref[0] * (CHUNK - pos))

    low_head = lax.broadcasted_iota(jnp.int32, (1, PAIR_W), 1) < RET_DK

    def half_norm(o):
        def half_mean(t):
            lo = jnp.sum(jnp.where(low_head, t, 0.0), axis=-1, keepdims=True)
            hi = jnp.sum(jnp.where(low_head, 0.0, t), axis=-1, keepdims=True)
            return jnp.where(low_head, lo, hi) * (1.0 / RET_DK)

        centred = o - half_mean(o)
        return centred * lax.rsqrt(half_mean(jnp.square(centred)) + LN_EPS)

    tiles = [(slice(c * CHUNK, (c + 1) * CHUNK), slice(p * PAIR_W, (p + 1) * PAIR_W), c, p)
             for c in range(TILE // CHUNK) for p in range(RET_W // PAIR_W)]
    scores = []
    for rows, cols, c, p in tiles:
        qp = q_ref[0, rows, cols]
        q_split = jnp.concatenate([jnp.where(low_head, qp, 0.0), jnp.where(low_head, 0.0, qp)], axis=0)
        scores.append(lax.dot_general(q_split.astype(BF16), k_ref[0, rows, cols].astype(BF16),
                                      (((1,), (1,)), ((), ())), preferred_element_type=F32))

    pc = pc_ref[0]
    z = pc[:, CONV_W:2 * CONV_W] * pc[:, 2 * CONV_W:3 * CONV_W]
    line_mask = jnp.where(is_ctx, TILE - 1, GRID_W - 1)
    first = (row & line_mask) == 0
    last = (row & line_mask) == line_mask
    z_prev = jnp.where(first, 0.0, pltpu.roll(z, 1, 0))
    z_next = jnp.where(last, 0.0, pltpu.roll(z, TILE - 1, 0))
    hp = hp_ref[0]
    hn = hn_ref[0]
    z_top = jnp.where(i == 0, 0.0, hp[:, CONV_W:2 * CONV_W] * hp[:, 2 * CONV_W:3 * CONV_W])
    z_bot = jnp.where(i == n_lat_tiles - 1, 0.0, hn[:, CONV_W:2 * CONV_W] * hn[:, 2 * CONV_W:3 * CONV_W])
    zcat = jnp.concatenate([z_top, z, z_bot], axis=0)
    z_up = zcat[0:TILE]
    z_down = zcat[2 * HALO:2 * HALO + TILE]
    along_seq = lax.broadcasted_iota(jnp.int32, (1, CONV_W), 1) < jnp.where(is_ctx, CONV_W, CONV_W // 2)
    cw = convw_ref[0]
    conv = cw[0:1] * jnp.where(along_seq, z_prev, z_up) + cw[1:2] * z
    conv = conv + cw[2:3] * jnp.where(along_seq, z_next, z_down)
    ycat[:, 0:CONV_W] = (pc[:, 0:CONV_W] * conv).astype(BF16)

    outs = []
    for (rows, cols, c, p), sc in zip(tiles, scores):
        qp = q_ref[0, rows, cols]
        vp = v_ref[0, rows, cols]
        sc = sc * dec[p]
        lhs = jnp.concatenate([sc[0:CHUNK].astype(BF16), sc[CHUNK:].astype(BF16),
                               (qp * qdec[0, :, cols]).astype(BF16), (qp * qdec[1, :, cols]).astype(BF16)], axis=1)
        rhs = jnp.concatenate([jnp.where(low_head, vp, 0.0).astype(BF16), jnp.where(low_head, 0.0, vp).astype(BF16),
                               sf_ref[0, 0, c, cols, :], sb_ref[0, 0, c, cols, :]], axis=0)
        outs.append(jnp.dot(lhs, rhs, preferred_element_type=F32))

    vn = _standardize(sv_ref[0]).astype(BF16)
    group = lax.broadcasted_iota(jnp.int32, (1, SGU_W), 1) // (SGU_W // SGU_GROUPS)
    for c in range(TILE // CHUNK):
        rows = slice(c * CHUNK, (c + 1) * CHUNK)
        mixed = jnp.zeros((CHUNK, SGU_W), F32)
        for gi in range(SGU_GROUPS):
            m = jnp.dot(sguw_ref[0, gi], vn[rows], preferred_element_type=F32)
            mixed = jnp.where(group == gi, m, mixed)
        ycat[rows, CONV_W + RET_W:] = (su_ref[0, rows, :] * (mixed + sgub_ref[0])).astype(BF16)

    for (rows, cols, c, p), o in zip(tiles, outs):
        ycat[rows, CONV_W + p * PAIR_W:CONV_W + (p + 1) * PAIR_W] = (
            _silu(g_ref[0, rows, cols]) * half_norm(o)).astype(BF16)

    y = jnp.dot(ycat[...], wout_ref[0], preferred_element_type=F32)
    mod = mod_ref[0, 0, 0]
    ln = ln_ref[0]
    x1 = _standardize(ALPHA * x_ref[0] + mod[2:3] * y) * ln[0:1] + ln[1:2]
    x1_ref[0] = x1
    hm = x1 * (1.0 + mod[4:5]) + mod[3:4]
    hm_ref[0] = hm
    lt = lax.dot_general(wr_ref[0], hm.astype(BF16), (((1,), (1,)), ((), ())), preferred_element_type=F32) + br_ref[0]
    route_ref[0, 0] = jnp.concatenate([_route_class(lt).astype(F32), jnp.zeros((7, TILE), F32)], axis=0)


def _mix_call(layer, n_lat_tiles, n_proc, lgf, lgb, pc, q, k, v, g, su, sv, states, xa, mod, conv_w, sgu_w_bf, sgu_bias,
              w_out_bf, ln1, w_route_t, b_route_t, lg_rows):
    bsz, length, _ = xa.shape
    halos_per_tile = TILE // HALO
    n_halo = length // HALO
    cpt = TILE // CHUNK
    tok = lambda w: pl.BlockSpec((1, TILE, w), lambda b, i, *_: (b, i, 0))
    per_layer = lambda *shape: pl.BlockSpec((1,) + shape, lambda b, i, *_: (layer,) + (0,) * len(shape))
    grid_spec = pltpu.PrefetchScalarGridSpec(
        num_scalar_prefetch=2,
        grid=(bsz, n_proc),
        in_specs=[
            tok(3 * CONV_W),
            pl.BlockSpec((1, HALO, 3 * CONV_W), lambda b, i, *_: (b, jnp.maximum(i * halos_per_tile - 1, 0), 0)),
            pl.BlockSpec((1, HALO, 3 * CONV_W),
                         lambda b, i, *_: (b, jnp.minimum((i + 1) * halos_per_tile, n_halo - 1), 0)),
            tok(RET_W), tok(RET_W), tok(RET_W), tok(RET_W), tok(SGU_W), tok(SGU_W),
            pl.BlockSpec((1, 1, cpt, RET_W, PAIR_W), lambda b, i, *_: (0, b, i, 0, 0)),
            pl.BlockSpec((1, 1, cpt, RET_W, PAIR_W), lambda b, i, *_: (1, b, i, 0, 0)),
            tok(D_MODEL),
            pl.BlockSpec((1, 1, 1, MOD_ROWS, D_MODEL),
                         lambda b, i, *_: (layer, b, jnp.minimum(i // n_lat_tiles, 1), 0, 0)),
            per_layer(8, CONV_W),
            per_layer(SGU_GROUPS, CHUNK, CHUNK),
            per_layer(CHUNK, SGU_W),
            per_layer(D_MODEL, D_MODEL),
            per_layer(8, D_MODEL),
            per_layer(ROUTE_ROWS, D_MODEL),
            per_layer(ROUTE_ROWS, 1),
            pl.BlockSpec((1, 1, RET_W), lambda b, i, *_: (2 * layer, 0, 0)),
            pl.BlockSpec((1, 1, RET_W), lambda b, i, *_: (2 * layer + 1, 0, 0)),
        ],
        out_specs=[tok(D_MODEL), tok(D_MODEL), pl.BlockSpec((1, 1, 8, TILE), lambda b, i, *_: (b, i, 0, 0))],
        scratch_shapes=[pltpu.VMEM((TILE, D_MODEL), BF16),
                        pltpu.VMEM((RET_W // PAIR_W, 2 * CHUNK, CHUNK), F32),
                        pltpu.VMEM((2, CHUNK, RET_W), F32)],
    )
    return pl.pallas_call(
        functools.partial(_mix_kernel, n_lat_tiles),
        grid_spec=grid_spec,
        out_shape=[jax.ShapeDtypeStruct((bsz, n_proc * TILE, D_MODEL), F32),
                   jax.ShapeDtypeStruct((bsz, n_proc * TILE, D_MODEL), F32),
                   jax.ShapeDtypeStruct((bsz, n_proc, 8, TILE), F32)],
        compiler_params=_params(2),
    )(lgf, lgb, pc, pc, pc, q, k, v, g, su, sv, states, states, xa, mod, conv_w, sgu_w_bf, sgu_bias, w_out_bf, ln1,
      w_route_t, b_route_t, lg_rows, lg_rows)


def _rank_kernel(route_ref, class_ref, rank_ref, counts_ref, running):
    @pl.when(pl.program_id(0) == 0)
    def _():
        running[...] = jnp.zeros_like(running)

    sub = lax.broadcasted_iota(jnp.int32, (ROUTE_W, TILE), 0)
    earlier = (lax.broadcasted_iota(jnp.int32, (TILE, TILE), 0)
               < lax.broadcasted_iota(jnp.int32, (TILE, TILE), 1)).astype(BF16)
    for t in range(route_ref.shape[0]):
        cls = route_ref[t, 0:1, :].astype(jnp.int32)
        class_ref[t] = cls
        onehot = sub == cls
        before = jnp.dot(onehot.astype(BF16), earlier, preferred_element_type=F32) + running[...]
        rank_ref[t] = jnp.sum(jnp.where(onehot, before, 0.0), axis=0, keepdims=True).astype(jnp.int32)
        running[...] += jnp.sum(onehot.astype(F32), axis=1, keepdims=True)
    counts_ref[...] = jnp.broadcast_to(running[...], counts_ref.shape)


def _rank_call(route):
    n_tiles = route.shape[0]
    per_step = math.gcd(n_tiles, RANK_TILES)
    per_tile = pl.BlockSpec((per_step, 1, TILE), lambda i: (i, 0, 0))
    return pl.pallas_call(
        _rank_kernel,
        grid=(n_tiles // per_step,),
        in_specs=[pl.BlockSpec((per_step, 8, TILE), lambda i: (i, 0, 0))],
        out_specs=[per_tile, per_tile, pl.BlockSpec((ROUTE_W, ROUTE_W), lambda i: (0, 0))],
        out_shape=[jax.ShapeDtypeStruct((n_tiles, 1, TILE), jnp.int32), jax.ShapeDtypeStruct((n_tiles, 1, TILE), jnp.int32),
                   jax.ShapeDtypeStruct((ROUTE_W, ROUTE_W), F32)],
        scratch_shapes=[pltpu.VMEM((ROUTE_W, 1), F32)],
        compiler_params=_params(1),
    )(route)


def _dispatch_kernel(dest_ref, pad_end_ref, pad_len_ref, n_used_ref, hm_ref, xs_ref, zeros, sem, pad_sem):
    step = pl.program_id(0)
    base = step * TILE
    half = BLOCK_M // 2

    def for_each_pad_copy(fn):
        def per_class(c, carry):
            off = pad_end_ref[c]
            n = pad_len_ref[c]
            for shift in range(BLOCK_M.bit_length() - 2, -1, -1):
                bit = 1 << shift
                off = off - (n & bit)

                @pl.when((n & bit) != 0)
                def _():
                    if bit >= 8:
                        fn(pltpu.make_async_copy(zeros.at[pl.ds(0, bit)], xs_ref.at[pl.ds(pl.multiple_of(off, 8), bit)],
                                                 pad_sem))
                    else:
                        for j in range(bit):
                            fn(pltpu.make_async_copy(zeros.at[pl.ds(0, 1)], xs_ref.at[pl.ds(off + j, 1)], pad_sem))

            return carry

        lax.fori_loop(0, N_CLASSES, per_class, 0)

        def per_half_block(j, carry):
            fn(pltpu.make_async_copy(zeros, xs_ref.at[pl.ds(pl.multiple_of(j * half, 8), half)], pad_sem))
            return carry

        lax.fori_loop(n_used_ref[0] * 2, xs_ref.shape[0] // half, per_half_block, 0)

    @pl.when(step == 0)
    def _():
        zeros[...] = jnp.zeros_like(zeros)
        for_each_pad_copy(lambda cp: cp.start())

    for r in range(TILE):
        pltpu.make_async_copy(hm_ref.at[pl.ds(r, 1)], xs_ref.at[pl.ds(dest_ref[base + r], 1)], sem).start()
    for r in range(TILE):
        pltpu.make_async_copy(hm_ref.at[pl.ds(0, 1)], xs_ref.at[pl.ds(0, 1)], sem).wait()

    @pl.when(step == pl.num_programs(0) - 1)
    def _():
        for_each_pad_copy(lambda cp: cp.wait())


def _dispatch_call(dest, pad_end, pad_len, n_used, hm_flat, n_slots):
    n_tok = hm_flat.shape[0]
    grid_spec = pltpu.PrefetchScalarGridSpec(
        num_scalar_prefetch=4,
        grid=(n_tok // TILE,),
        in_specs=[pl.BlockSpec((TILE, D_MODEL), lambda i, *_: (i, 0))],
        out_specs=pl.BlockSpec(memory_space=pl.ANY),
        scratch_shapes=[pltpu.VMEM((BLOCK_M // 2, D_MODEL), F32), pltpu.SemaphoreType.DMA, pltpu.SemaphoreType.DMA],
    )
    return pl.pallas_call(
        _dispatch_kernel,
        grid_spec=grid_spec,
        out_shape=jax.ShapeDtypeStruct((n_slots, D_MODEL), F32),
        compiler_params=_params(1),
    )(dest, pad_end, pad_len, n_used, hm_flat)


def _expert_kernel(blk_lo_ref, blk_hi_ref, n_used_ref, xs_ref, wr_ref, br_ref, wg_lo, wu_lo, wd_lo, wg_hi, wu_hi, wd_hi,
                   ys_ref):
    i = pl.program_id(0)
    used = i < n_used_ref[0]

    @pl.when(used)
    def _():
        xb = xs_ref[...].astype(BF16)

        logits = jnp.dot(xb, wr_ref[0], preferred_element_type=F32) + br_ref[0]
        lane = lax.broadcasted_iota(jnp.int32, logits.shape, 1)
        gl = jnp.where(lane < N_GROUPS, logits, -jnp.inf)
        g_prob = 1.0 / jnp.sum(jnp.exp(gl - jnp.max(gl, axis=-1, keepdims=True)), axis=-1, keepdims=True)
        l_lo = jnp.sum(jnp.where(lane == N_GROUPS + blk_lo_ref[i], logits, 0.0), axis=-1, keepdims=True)
        l_hi = jnp.sum(jnp.where(lane == N_GROUPS + blk_hi_ref[i], logits, 0.0), axis=-1, keepdims=True)
        m = jnp.maximum(l_lo, l_hi)
        p_lo = jnp.exp(l_lo - m)
        p_hi = jnp.exp(l_hi - m)

        def mlp(wg, wu, wd):
            gate = jnp.dot(xb, wg[0, 0], preferred_element_type=F32)
            up = jnp.dot(xb, wu[0, 0], preferred_element_type=F32)
            return jnp.dot((_silu(gate) * up).astype(BF16), wd[0, 0], preferred_element_type=F32)

        ys_ref[...] = (mlp(wg_lo, wu_lo, wd_lo) * (g_prob * (p_lo / (p_lo + p_hi)))
                       + mlp(wg_hi, wu_hi, wd_hi) * (g_prob * (p_hi / (p_lo + p_hi))))

    @pl.when(jnp.logical_not(used))
    def _():
        ys_ref[...] = jnp.zeros_like(ys_ref)


def _expert_call(layer, blk_lo, blk_hi, n_used, xs, w_route_bf, b_route, w_gate_bf, w_up_bf, w_down_bf):
    n_blocks = xs.shape[0] // BLOCK_M
    up_spec = lambda which: pl.BlockSpec((1, 1, D_MODEL, EXPERT_HIDDEN),
                                         lambda i, lo, hi, nu: (layer, (lo, hi)[which][i], 0, 0))
    down_spec = lambda which: pl.BlockSpec((1, 1, EXPERT_HIDDEN, D_MODEL),
                                           lambda i, lo, hi, nu: (layer, (lo, hi)[which][i], 0, 0))
    grid_spec = pltpu.PrefetchScalarGridSpec(
        num_scalar_prefetch=3,
        grid=(n_blocks,),
        in_specs=[pl.BlockSpec((BLOCK_M, D_MODEL), lambda i, lo, hi, nu: (jnp.minimum(i, nu[0] - 1), 0)),
                  pl.BlockSpec((1, D_MODEL, ROUTE_W), lambda i, *_: (layer, 0, 0)),
                  pl.BlockSpec((1, 1, ROUTE_W), lambda i, *_: (layer, 0, 0)),
                  up_spec(0), up_spec(0), down_spec(0), up_spec(1), up_spec(1), down_spec(1)],
        out_specs=pl.BlockSpec((BLOCK_M, D_MODEL), lambda i, *_: (i, 0)),
    )
    return pl.pallas_call(
        _expert_kernel,
        grid_spec=grid_spec,
        out_shape=jax.ShapeDtypeStruct((xs.shape[0], D_MODEL), F32),
        compiler_params=_params(1),
    )(blk_lo, blk_hi, n_used, xs, w_route_bf, b_route, w_gate_bf, w_up_bf, w_down_bf, w_gate_bf, w_up_bf, w_down_bf)


def _combine_kernel(tiles_per_batch, dest_ref, ys_ref, x1_ref, mod_ref, ln_ref, out_ref, buf, sem):
    step = pl.program_id(0) * tiles_per_batch + pl.program_id(1)
    slot = step % 2
    has_next = step + 1 < pl.num_programs(0) * tiles_per_batch

    def gather(tile, to_slot):
        base = tile * TILE

        for r in range(TILE):
            pltpu.make_async_copy(ys_ref.at[pl.ds(dest_ref[base + r], 1)], buf.at[to_slot, pl.ds(r, 1)],
                                  sem.at[to_slot]).start()

    @pl.when(step == 0)
    def _():
        gather(step, slot)

    @pl.when(has_next)
    def _():
        gather(step + 1, 1 - slot)

    for r in range(TILE):
        pltpu.make_async_copy(ys_ref.at[pl.ds(0, 1)], buf.at[slot, pl.ds(0, 1)], sem.at[slot]).wait()

    mod = mod_ref[0, 0, 0]
    ln = ln_ref[0]
    out_ref[0] = _standardize(ALPHA * x1_ref[0] + mod[5:6] * buf[slot]) * ln[0:1] + ln[1:2]


def _combine_call(layer, n_lat_tiles, dest, ys, x1, mod, ln2):
    bsz, length, _ = x1.shape
    tiles_per_batch = length // TILE
    tok = lambda w: pl.BlockSpec((1, TILE, w), lambda b, i, *_: (b, i, 0))
    grid_spec = pltpu.PrefetchScalarGridSpec(
        num_scalar_prefetch=1,
        grid=(bsz, tiles_per_batch),
        in_specs=[
            pl.BlockSpec(memory_space=pl.ANY),
            tok(D_MODEL),
            pl.BlockSpec((1, 1, 1, MOD_ROWS, D_MODEL),
                         lambda b, i, *_: (layer, b, jnp.minimum(i // n_lat_tiles, 1), 0, 0)),
            pl.BlockSpec((1, 8, D_MODEL), lambda b, i, *_: (layer, 0, 0)),
        ],
        out_specs=tok(D_MODEL),
        scratch_shapes=[pltpu.VMEM((2, TILE, D_MODEL), F32), pltpu.SemaphoreType.DMA((2,))],
    )
    return pl.pallas_call(
        functools.partial(_combine_kernel, tiles_per_batch),
        grid_spec=grid_spec,
        out_shape=jax.ShapeDtypeStruct(x1.shape, F32),
        compiler_params=_params(2),
    )(dest, ys, x1, mod, ln2)


def _dispatch_plan(class_of_token, rank, counts):
    n_tok = class_of_token.shape[0]
    counts = counts.astype(jnp.int32)
    pcounts = (counts + BLOCK_M - 1) // BLOCK_M * BLOCK_M
    pends = jnp.cumsum(pcounts)
    pstarts = pends - pcounts
    classes = jnp.arange(ROUTE_W, dtype=jnp.int32)
    dest = jnp.sum(jnp.where(class_of_token[:, None] == classes[None, :], pstarts[None, :], 0), axis=1) + rank
    n_blocks = n_tok // BLOCK_M + N_CLASSES
    blk_start = jnp.arange(n_blocks, dtype=jnp.int32) * BLOCK_M
    blk_class = jnp.minimum(jnp.sum((pends[None, :] <= blk_start[:, None]).astype(jnp.int32), axis=1), N_CLASSES - 1)
    n_used = pends[-1:] // BLOCK_M
    return (dest.astype(jnp.int32), pends, pcounts - counts, jnp.asarray(CLASS_LO)[blk_class],
            jnp.asarray(CLASS_HI)[blk_class], n_used.astype(jnp.int32), n_blocks * BLOCK_M)


def _pad_rows(a, rows):
    return jnp.pad(a, [(0, 0)] * (a.ndim - 2) + [(0, rows - a.shape[-2]), (0, 0)])


def kernel(x, c, ctx, c_ctx, w_ada, b_ada, w_in, conv_w, ret_decay_fwd, ret_decay_bwd, sgu_w, sgu_b, w_out, ln1_g, ln1_b,
           router_group_w, router_group_b, router_expert_w, router_expert_b, moe_w_gate, moe_w_up, moe_w_down, ln2_g,
           ln2_b):
    bsz, seq, d = x.shape
    ctx_len = ctx.shape[1]
    assert d == D_MODEL and ctx_len == TILE and seq % TILE == 0 and seq % GRID_W == 0
    n_lat_tiles = seq // TILE
    n_lat_chunks = seq // CHUNK

    cond = _pad_rows(jnp.concatenate([c, c_ctx[None, :]], axis=0), 16)
    ada = _ada_call(cond, w_ada, b_ada)
    mod_lat = ada[:, :bsz].reshape(DEPTH, bsz, N_MOD, d)
    mod_ctx = jnp.broadcast_to(ada[:, bsz].reshape(DEPTH, 1, N_MOD, d), mod_lat.shape)
    mod = _pad_rows(jnp.stack([mod_lat, mod_ctx], axis=2), MOD_ROWS)

    w_in_bf = w_in.astype(BF16)
    w_out_bf = w_out.astype(BF16)
    sgu_w_bf = sgu_w.astype(BF16)
    w_gate_bf = moe_w_gate.astype(BF16)
    w_up_bf = moe_w_up.astype(BF16)
    w_down_bf = moe_w_down.astype(BF16)
    sgu_bias = jnp.repeat(jnp.swapaxes(sgu_b, 1, 2), SGU_W // SGU_GROUPS, axis=2)
    conv_w8 = _pad_rows(conv_w, 8)
    ln1 = _pad_rows(jnp.stack([ln1_g, ln1_b], axis=1), 8)
    ln2 = _pad_rows(jnp.stack([ln2_g, ln2_b], axis=1), 8)
    w_route = jnp.concatenate([router_group_w, jnp.swapaxes(router_expert_w, 1, 2).reshape(DEPTH, d, N_EXPERTS)], axis=2)
    w_route_bf = jnp.pad(w_route, ((0, 0), (0, 0), (0, ROUTE_W - w_route.shape[2]))).astype(BF16)
    b_route = jnp.concatenate([router_group_b, router_expert_b.reshape(DEPTH, N_EXPERTS)], axis=1)
    b_route = jnp.pad(b_route, ((0, 0), (0, ROUTE_W - b_route.shape[1])))[:, None, :].astype(F32)
    w_route_t = jnp.concatenate([_pad_rows(jnp.swapaxes(w_route[:, :, :N_GROUPS], 1, 2), EXPERT_ROW0),
                                 jnp.swapaxes(w_route[:, :, N_GROUPS:], 1, 2)], axis=1)
    w_route_t = _pad_rows(w_route_t, ROUTE_ROWS).astype(BF16)
    b_route_t = jnp.concatenate([_pad_rows(b_route[:, 0, :N_GROUPS, None], EXPERT_ROW0),
                                 b_route[:, 0, N_GROUPS:N_GROUPS + N_EXPERTS, None]], axis=1)
    b_route_t = _pad_rows(b_route_t, ROUTE_ROWS)
    lg = jnp.stack([jax.nn.log_sigmoid(ret_decay_fwd.astype(F32)), jax.nn.log_sigmoid(ret_decay_bwd.astype(F32))],
                   axis=1)
    lg_lanes = jnp.repeat(lg, RET_DK, axis=2).reshape(DEPTH * 2, RET_W)
    lg_rows = lg_lanes[:, None, :]
    lg_cols = lg_lanes[:, :, None]

    xa = jnp.concatenate([x, ctx], axis=1)
    for layer in range(DEPTH):
        last = layer == DEPTH - 1
        pc, q, k, v, g, su, sv, kv = _inproj_call(layer, xa, mod, w_in_bf, lg_rows, n_lat_tiles)
        states = _scan_call(layer, kv, lg_cols, n_lat_chunks)
        n_proc = n_lat_tiles if last else n_lat_tiles + 1
        x1, hm, route = _mix_call(layer, n_lat_tiles, n_proc, lg[layer, 0], lg[layer, 1], pc, q, k, v, g, su, sv, states,
                                  xa, mod, conv_w8, sgu_w_bf, sgu_bias, w_out_bf, ln1, w_route_t, b_route_t, lg_rows)
        class_of_token, rank, counts = _rank_call(route.reshape(-1, 8, TILE))
        dest, pad_end, pad_len, blk_lo, blk_hi, n_used, n_slots = _dispatch_plan(class_of_token.reshape(-1),
                                                                                  rank.reshape(-1), counts[:, 0])
        xs = _dispatch_call(dest, pad_end, pad_len, n_used, hm.reshape(-1, D_MODEL), n_slots)
        ys = _expert_call(layer, blk_lo, blk_hi, n_used, xs, w_route_bf, b_route, w_gate_bf, w_up_bf, w_down_bf)
        xa = _combine_call(layer, n_lat_tiles, dest, ys, x1, mod, ln2)
    return xa
```

```python
import functools
import math

import jax
import jax.numpy as jnp
import numpy as np
from jax import lax
from jax.experimental import pallas as pl
from jax.experimental.pallas import tpu as pltpu

F32 = jnp.float32
BF16 = jnp.bfloat16

D_MODEL = 1024
DEPTH = 2
GRID_W = 64
CONV_W = 256
RET_W = 512
RET_HEADS = 8
RET_DK = 64
PAIR_W = 2 * RET_DK
CHUNK = 128
SGU_W = 256
SGU_GROUPS = 4
IN_COLS = 3 * CONV_W + 4 * RET_W + 2 * SGU_W
N_GROUPS = 4
EXPERTS_PER_GROUP = 8
N_EXPERTS = N_GROUPS * EXPERTS_PER_GROUP
EXPERT_HIDDEN = 512
N_MOD = 6
MOD_ROWS = 8
LN_EPS = 1e-5
ALPHA = (2 * DEPTH) ** 0.25

TILE = 256
HALO = GRID_W
BLOCK_M = 256
ROUTE_W = 128
ROUTE_ROWS = 48
EXPERT_ROW0 = 8
COMBINE_ROWS = 32
RANK_TILES = 8
PAIRS_PER_GROUP = EXPERTS_PER_GROUP * (EXPERTS_PER_GROUP - 1) // 2
N_CLASSES = N_GROUPS * PAIRS_PER_GROUP
CLASS_LO = np.array([g * EXPERTS_PER_GROUP + lo for g in range(N_GROUPS) for lo in range(EXPERTS_PER_GROUP)
                     for hi in range(lo + 1, EXPERTS_PER_GROUP)] + [N_EXPERTS - 2] * (ROUTE_W - N_CLASSES), np.int32)
CLASS_HI = np.array([g * EXPERTS_PER_GROUP + hi for g in range(N_GROUPS) for lo in range(EXPERTS_PER_GROUP)
                     for hi in range(lo + 1, EXPERTS_PER_GROUP)] + [N_EXPERTS - 1] * (ROUTE_W - N_CLASSES), np.int32)
VMEM_LIMIT = 56 * 1024 * 1024


def _params(n_axes):
    return pltpu.CompilerParams(dimension_semantics=("arbitrary",) * n_axes, vmem_limit_bytes=VMEM_LIMIT)


def _standardize(v):
    mu = jnp.mean(v, axis=-1, keepdims=True)
    var = jnp.mean(jnp.square(v - mu), axis=-1, keepdims=True)
    return (v - mu) * lax.rsqrt(var + LN_EPS)


def _silu(v):
    return v * jax.nn.sigmoid(v)


def _ada_kernel(c_ref, w_ref, b_ref, o_ref):
    a = _silu(c_ref[...]).astype(BF16)
    o_ref[0] = jnp.dot(a, w_ref[0].astype(BF16), preferred_element_type=F32) + b_ref[0]


def _ada_call(cond, w_ada, b_ada):
    rows = cond.shape[0]
    cols = w_ada.shape[-1]
    tn = 1536
    return pl.pallas_call(
        _ada_kernel,
        grid=(DEPTH, cols // tn),
        in_specs=[
            pl.BlockSpec((rows, D_MODEL), lambda l, j: (0, 0)),
            pl.BlockSpec((1, D_MODEL, tn), lambda l, j: (l, 0, j)),
            pl.BlockSpec((1, 1, tn), lambda l, j: (l, 0, j)),
        ],
        out_specs=pl.BlockSpec((1, rows, tn), lambda l, j: (l, 0, j)),
        out_shape=jax.ShapeDtypeStruct((DEPTH, rows, cols), F32),
        compiler_params=_params(2),
    )(cond, w_ada, b_ada.reshape(DEPTH, 1, cols))


def _inproj_kernel(n_lat_tiles, x_ref, ctx_ref, mod_ref, w_ref, lgf_ref, lgb_ref, pc_ref, q_ref, k_ref, v_ref, g_ref,
                   su_ref, sv_ref, kv_ref):
    mod = mod_ref[0, 0, 0]
    xin = jnp.where(pl.program_id(1) >= n_lat_tiles, ctx_ref[0], x_ref[0])
    h = (xin * (1.0 + mod[1:2]) + mod[0:1]).astype(BF16)

    def proj(lo, hi):
        return jnp.dot(h, w_ref[0, :, lo:hi], preferred_element_type=F32)

    o = 3 * CONV_W
    pc_ref[0] = proj(0, o)
    q_ref[0] = proj(o, o + RET_W)
    k = proj(o + RET_W, o + 2 * RET_W) * (RET_DK ** -0.5)
    k_ref[0] = k
    v = proj(o + 2 * RET_W, o + 3 * RET_W)
    v_ref[0] = v
    g_ref[0] = proj(o + 3 * RET_W, o + 4 * RET_W)
    o += 4 * RET_W
    su_ref[0] = proj(o, o + SGU_W)
    sv_ref[0] = proj(o + SGU_W, o + 2 * SGU_W)

    pos = lax.broadcasted_iota(jnp.int32, (CHUNK, 1), 0).astype(F32)
    k_decay = (jnp.exp((CHUNK - 1.0 - pos) * lgf_ref[0]), jnp.exp(pos * lgb_ref[0]))
    contract_rows = (((0,), (0,)), ((), ()))
    for c in range(TILE // CHUNK):
        rows = slice(c * CHUNK, (c + 1) * CHUNK)
        vc = v[rows].astype(BF16)
        for d in range(2):
            kc = (k[rows] * k_decay[d]).astype(BF16)
            for p in range(RET_W // PAIR_W):
                cols = slice(p * PAIR_W, (p + 1) * PAIR_W)
                kv_ref[d, 0, c, cols, :] = lax.dot_general(kc[:, cols], vc[:, cols], contract_rows,
                                                           preferred_element_type=F32)


def _inproj_call(layer, x_lat, x_ctx, ctx_tile, mod, w_in_bf, lg_rows, n_lat_tiles):
    bsz = x_lat.shape[0]
    length = (n_lat_tiles + 1) * TILE
    widths = (3 * CONV_W, RET_W, RET_W, RET_W, RET_W, SGU_W, SGU_W)
    cpt = TILE // CHUNK
    return pl.pallas_call(
        functools.partial(_inproj_kernel, n_lat_tiles),
        grid=(bsz, length // TILE),
        in_specs=[
            pl.BlockSpec((1, TILE, D_MODEL), lambda b, i: (b, jnp.minimum(i, n_lat_tiles - 1), 0)),
            pl.BlockSpec((1, TILE, D_MODEL), lambda b, i: (b, ctx_tile, 0)),
            pl.BlockSpec((1, 1, 1, MOD_ROWS, D_MODEL), lambda b, i: (layer, b, jnp.minimum(i // n_lat_tiles, 1), 0, 0)),
            pl.BlockSpec((1, D_MODEL, IN_COLS), lambda b, i: (layer, 0, 0)),
            pl.BlockSpec((1, 1, RET_W), lambda b, i: (2 * layer, 0, 0)),
            pl.BlockSpec((1, 1, RET_W), lambda b, i: (2 * layer + 1, 0, 0)),
        ],
        out_specs=[pl.BlockSpec((1, TILE, w), lambda b, i: (b, i, 0)) for w in widths]
        + [pl.BlockSpec((2, 1, cpt, RET_W, PAIR_W), lambda b, i: (0, b, i, 0, 0))],
        out_shape=[jax.ShapeDtypeStruct((bsz, length, w), F32) for w in widths]
        + [jax.ShapeDtypeStruct((2, bsz, length // CHUNK, RET_W, PAIR_W), F32)],
        compiler_params=_params(2),
    )(x_lat, x_ctx, mod, w_in_bf, lg_rows, lg_rows)


def _scan_kernel(n_lat_chunks, kv_ref, lgc_ref, s_ref, state):
    nc = kv_ref.shape[2]
    backward = pl.program_id(1)
    decay = jnp.exp(CHUNK * lgc_ref[0])
    row_head = lax.broadcasted_iota(jnp.int32, (RET_W, PAIR_W), 0) % PAIR_W // RET_DK
    lane_head = lax.broadcasted_iota(jnp.int32, (RET_W, PAIR_W), 1) // RET_DK
    own_head = row_head == lane_head
    state[...] = jnp.zeros_like(state)

    def step(s, carry):
        c = jnp.where(backward == 1, nc - 1 - s, (s + n_lat_chunks) % nc)
        s_ref[0, 0, c] = jnp.where(own_head, state[...], 0.0).astype(BF16)
        state[...] = decay * state[...] + kv_ref[0, 0, c]
        return carry

    lax.fori_loop(0, nc, step, 0)


def _scan_call(layer, kv, lg_cols, n_lat_chunks):
    _, bsz, nc, _, _ = kv.shape
    blk = (1, 1, nc, RET_W, PAIR_W)
    return pl.pallas_call(
        functools.partial(_scan_kernel, n_lat_chunks),
        grid=(bsz, 2),
        in_specs=[pl.BlockSpec(blk, lambda b, d: (d, b, 0, 0, 0)),
                  pl.BlockSpec((1, RET_W, 1), lambda b, d: (2 * layer + d, 0, 0))],
        out_specs=pl.BlockSpec(blk, lambda b, d: (d, b, 0, 0, 0)),
        out_shape=jax.ShapeDtypeStruct(kv.shape, BF16),
        scratch_shapes=[pltpu.VMEM((RET_W, PAIR_W), F32)],
        compiler_params=_params(2),
    )(kv, lg_cols)


def _route_class(lt):
    assert EXPERTS_PER_GROUP == 8 and N_GROUPS <= 8
    sub = lax.broadcasted_iota(jnp.int32, (8, lt.shape[1]), 0)
    neg = jnp.float32(-jnp.inf)

    def top(vals):
        m = jnp.max(vals, axis=0, keepdims=True)
        return jnp.min(jnp.where(vals == m, sub, 8), axis=0, keepdims=True)

    gidx = top(jnp.where(sub < N_GROUPS, lt[0:8], neg))
    pair = jnp.zeros_like(gidx)
    for g in range(N_GROUPS):
        e = lt[EXPERT_ROW0 + 8 * g:EXPERT_ROW0 + 8 * (g + 1)]
        i1 = top(e)
        i2 = top(jnp.where(sub == i1, neg, e))
        e_lo = jnp.minimum(i1, i2)
        e_hi = jnp.maximum(i1, i2)
        pair_g = e_lo * (EXPERTS_PER_GROUP - 1) - ((e_lo * (e_lo - 1)) >> 1) + (e_hi - e_lo - 1)
        pair = jnp.where(gidx == g, pair_g, pair)
    return gidx * PAIRS_PER_GROUP + pair


def _mix_kernel(n_lat_tiles, lgf_ref, lgb_ref, pc_ref, hp_ref, hn_ref, q_ref, k_ref, v_ref, g_ref, su_ref, sv_ref,
                sf_ref, sb_ref, x_ref, ctx_ref, mod_ref, convw_ref, sguw_ref, sgub_ref, wout_ref, ln_ref, wr_ref, br_ref,
                lgfr_ref, lgbr_ref, x1_ref, hm_ref, route_ref, ycat, dec, qdec):
    i = pl.program_id(1)
    is_ctx = i >= n_lat_tiles
    row = lax.broadcasted_iota(jnp.int32, (TILE, 1), 0)

    @pl.when((pl.program_id(0) == 0) & (i == 0))
    def _():
        pos = lax.broadcasted_iota(jnp.int32, (CHUNK, 1), 0).astype(F32)
        rel = pos - lax.broadcasted_iota(jnp.int32, (1, CHUNK), 1).astype(F32)
        for h in range(RET_HEADS):
            dec[h // 2, (h % 2) * CHUNK:(h % 2 + 1) * CHUNK, :] = jnp.where(
                rel > 0, jnp.exp(lgf_ref[h] * jnp.maximum(rel, 0.0)),
                jnp.where(rel < 0, jnp.exp(lgb_ref[h] * jnp.maximum(-rel, 0.0)), 2.0))
        qdec[0] = jnp.exp(lgfr_ref[0] * (pos + 1.0))
        qdec[1] = jnp.exp(lgbr_ref[0] * (CHUNK - pos))

    low_head = lax.broadcasted_iota(jnp.int32, (1, PAIR_W), 1) < RET_DK

    def half_norm(o):
        def half_mean(t):
            lo = jnp.sum(jnp.where(low_head, t, 0.0), axis=-1, keepdims=True)
            hi = jnp.sum(jnp.where(low_head, 0.0, t), axis=-1, keepdims=True)
            return jnp.where(low_head, lo, hi) * (1.0 / RET_DK)

        centred = o - half_mean(o)
        return centred * lax.rsqrt(half_mean(jnp.square(centred)) + LN_EPS)

    tiles = [(slice(c * CHUNK, (c + 1) * CHUNK), slice(p * PAIR_W, (p + 1) * PAIR_W), c, p)
             for c in range(TILE // CHUNK) for p in range(RET_W // PAIR_W)]
    scores = []
    for rows, cols, c, p in tiles:
        qp = q_ref[0, rows, cols]
        q_split = jnp.concatenate([jnp.where(low_head, qp, 0.0), jnp.where(low_head, 0.0, qp)], axis=0)
        scores.append(lax.dot_general(q_split.astype(BF16), k_ref[0, rows, cols].astype(BF16),
                                      (((1,), (1,)), ((), ())), preferred_element_type=F32))

    pc = pc_ref[0]
    z = pc[:, CONV_W:2 * CONV_W] * pc[:, 2 * CONV_W:3 * CONV_W]
    line_mask = jnp.where(is_ctx, TILE - 1, GRID_W - 1)
    first = (row & line_mask) == 0
    last = (row & line_mask) == line_mask
    z_prev = jnp.where(first, 0.0, pltpu.roll(z, 1, 0))
    z_next = jnp.where(last, 0.0, pltpu.roll(z, TILE - 1, 0))
    hp = hp_ref[0]
    hn = hn_ref[0]
    z_top = jnp.where(i == 0, 0.0, hp[:, CONV_W:2 * CONV_W] * hp[:, 2 * CONV_W:3 * CONV_W])
    z_bot = jnp.where(i == n_lat_tiles - 1, 0.0, hn[:, CONV_W:2 * CONV_W] * hn[:, 2 * CONV_W:3 * CONV_W])
    zcat = jnp.concatenate([z_top, z, z_bot], axis=0)
    z_up = zcat[0:TILE]
    z_down = zcat[2 * HALO:2 * HALO + TILE]
    along_seq = lax.broadcasted_iota(jnp.int32, (1, CONV_W), 1) < jnp.where(is_ctx, CONV_W, CONV_W // 2)
    cw = convw_ref[0]
    conv = cw[0:1] * jnp.where(along_seq, z_prev, z_up) + cw[1:2] * z
    conv = conv + cw[2:3] * jnp.where(along_seq, z_next, z_down)
    ycat[:, 0:CONV_W] = (pc[:, 0:CONV_W] * conv).astype(BF16)

    outs = []
    for (rows, cols, c, p), sc in zip(tiles, scores):
        qp = q_ref[0, rows, cols]
        vp = v_ref[0, rows, cols]
        sc = sc * dec[p]
        lhs = jnp.concatenate([sc[0:CHUNK].astype(BF16), sc[CHUNK:].astype(BF16),
                               (qp * qdec[0, :, cols]).astype(BF16), (qp * qdec[1, :, cols]).astype(BF16)], axis=1)
        rhs = jnp.concatenate([jnp.where(low_head, vp, 0.0).astype(BF16), jnp.where(low_head, 0.0, vp).astype(BF16),
                               sf_ref[0, 0, c, cols, :], sb_ref[0, 0, c, cols, :]], axis=0)
        outs.append(jnp.dot(lhs, rhs, preferred_element_type=F32))

    vn = _standardize(sv_ref[0]).astype(BF16)
    group = lax.broadcasted_iota(jnp.int32, (1, SGU_W), 1) // (SGU_W // SGU_GROUPS)
    for c in range(TILE // CHUNK):
        rows = slice(c * CHUNK, (c + 1) * CHUNK)
        mixed = jnp.zeros((CHUNK, SGU_W), F32)
        for gi in range(SGU_GROUPS):
            m = jnp.dot(sguw_ref[0, gi], vn[rows], preferred_element_type=F32)
            mixed = jnp.where(group == gi, m, mixed)
        ycat[rows, CONV_W + RET_W:] = (su_ref[0, rows, :] * (mixed + sgub_ref[0])).astype(BF16)

    for (rows, cols, c, p), o in zip(tiles, outs):
        ycat[rows, CONV_W + p * PAIR_W:CONV_W + (p + 1) * PAIR_W] = (
            _silu(g_ref[0, rows, cols]) * half_norm(o)).astype(BF16)

    y = jnp.dot(ycat[...], wout_ref[0], preferred_element_type=F32)
    mod = mod_ref[0, 0, 0]
    ln = ln_ref[0]
    x1 = _standardize(ALPHA * jnp.where(is_ctx, ctx_ref[0], x_ref[0]) + mod[2:3] * y) * ln[0:1] + ln[1:2]
    x1_ref[0] = x1
    hm = x1 * (1.0 + mod[4:5]) + mod[3:4]
    hm_ref[0] = hm
    lt = lax.dot_general(wr_ref[0], hm.astype(BF16), (((1,), (1,)), ((), ())), preferred_element_type=F32) + br_ref[0]
    route_ref[0, 0] = jnp.concatenate([_route_class(lt).astype(F32), jnp.zeros((7, TILE), F32)], axis=0)


def _mix_call(layer, n_lat_tiles, n_proc, lgf, lgb, pc, q, k, v, g, su, sv, states, x_lat, x_ctx, ctx_tile, mod, conv_w,
              sgu_w_bf, sgu_bias, w_out_bf, ln1, w_route_t, b_route_t, lg_rows):
    bsz, length, _ = q.shape
    halos_per_tile = TILE // HALO
    n_halo = length // HALO
    cpt = TILE // CHUNK
    tok = lambda w: pl.BlockSpec((1, TILE, w), lambda b, i, *_: (b, i, 0))
    per_layer = lambda *shape: pl.BlockSpec((1,) + shape, lambda b, i, *_: (layer,) + (0,) * len(shape))
    grid_spec = pltpu.PrefetchScalarGridSpec(
        num_scalar_prefetch=2,
        grid=(bsz, n_proc),
        in_specs=[
            tok(3 * CONV_W),
            pl.BlockSpec((1, HALO, 3 * CONV_W), lambda b, i, *_: (b, jnp.maximum(i * halos_per_tile - 1, 0), 0)),
            pl.BlockSpec((1, HALO, 3 * CONV_W),
                         lambda b, i, *_: (b, jnp.minimum((i + 1) * halos_per_tile, n_halo - 1), 0)),
            tok(RET_W), tok(RET_W), tok(RET_W), tok(RET_W), tok(SGU_W), tok(SGU_W),
            pl.BlockSpec((1, 1, cpt, RET_W, PAIR_W), lambda b, i, *_: (0, b, i, 0, 0)),
            pl.BlockSpec((1, 1, cpt, RET_W, PAIR_W), lambda b, i, *_: (1, b, i, 0, 0)),
            pl.BlockSpec((1, TILE, D_MODEL), lambda b, i, *_: (b, jnp.minimum(i, n_lat_tiles - 1), 0)),
            pl.BlockSpec((1, TILE, D_MODEL), lambda b, i, *_: (b, ctx_tile, 0)),
            pl.BlockSpec((1, 1, 1, MOD_ROWS, D_MODEL),
                         lambda b, i, *_: (layer, b, jnp.minimum(i // n_lat_tiles, 1), 0, 0)),
            per_layer(8, CONV_W),
            per_layer(SGU_GROUPS, CHUNK, CHUNK),
            per_layer(CHUNK, SGU_W),
            per_layer(D_MODEL, D_MODEL),
            per_layer(8, D_MODEL),
            per_layer(ROUTE_ROWS, D_MODEL),
            per_layer(ROUTE_ROWS, 1),
            pl.BlockSpec((1, 1, RET_W), lambda b, i, *_: (2 * layer, 0, 0)),
            pl.BlockSpec((1, 1, RET_W), lambda b, i, *_: (2 * layer + 1, 0, 0)),
        ],
        out_specs=[tok(D_MODEL), tok(D_MODEL), pl.BlockSpec((1, 1, 8, TILE), lambda b, i, *_: (b, i, 0, 0))],
        scratch_shapes=[pltpu.VMEM((TILE, D_MODEL), BF16),
                        pltpu.VMEM((RET_W // PAIR_W, 2 * CHUNK, CHUNK), F32),
                        pltpu.VMEM((2, CHUNK, RET_W), F32)],
    )
    return pl.pallas_call(
        functools.partial(_mix_kernel, n_lat_tiles),
        grid_spec=grid_spec,
        out_shape=[jax.ShapeDtypeStruct((bsz, n_proc * TILE, D_MODEL), F32),
                   jax.ShapeDtypeStruct((bsz, n_proc * TILE, D_MODEL), F32),
                   jax.ShapeDtypeStruct((bsz, n_proc, 8, TILE), F32)],
        compiler_params=_params(2),
    )(lgf, lgb, pc, pc, pc, q, k, v, g, su, sv, states, states, x_lat, x_ctx, mod, conv_w, sgu_w_bf, sgu_bias, w_out_bf,
      ln1, w_route_t, b_route_t, lg_rows, lg_rows)


def _rank_kernel(route_ref, class_ref, rank_ref, counts_ref, running):
    @pl.when(pl.program_id(0) == 0)
    def _():
        running[...] = jnp.zeros_like(running)

    sub = lax.broadcasted_iota(jnp.int32, (ROUTE_W, TILE), 0)
    earlier = (lax.broadcasted_iota(jnp.int32, (TILE, TILE), 0)
               < lax.broadcasted_iota(jnp.int32, (TILE, TILE), 1)).astype(BF16)
    for t in range(route_ref.shape[0]):
        cls = route_ref[t, 0:1, :].astype(jnp.int32)
        class_ref[t] = cls
        onehot = sub == cls
        before = jnp.dot(onehot.astype(BF16), earlier, preferred_element_type=F32) + running[...]
        rank_ref[t] = jnp.sum(jnp.where(onehot, before, 0.0), axis=0, keepdims=True).astype(jnp.int32)
        running[...] += jnp.sum(onehot.astype(F32), axis=1, keepdims=True)
    counts_ref[...] = jnp.broadcast_to(running[...], counts_ref.shape)


def _rank_call(route):
    n_tiles = route.shape[0]
    per_step = math.gcd(n_tiles, RANK_TILES)
    per_tile = pl.BlockSpec((per_step, 1, TILE), lambda i: (i, 0, 0))
    return pl.pallas_call(
        _rank_kernel,
        grid=(n_tiles // per_step,),
        in_specs=[pl.BlockSpec((per_step, 8, TILE), lambda i: (i, 0, 0))],
        out_specs=[per_tile, per_tile, pl.BlockSpec((ROUTE_W, ROUTE_W), lambda i: (0, 0))],
        out_shape=[jax.ShapeDtypeStruct((n_tiles, 1, TILE), jnp.int32), jax.ShapeDtypeStruct((n_tiles, 1, TILE), jnp.int32),
                   jax.ShapeDtypeStruct((ROUTE_W, ROUTE_W), F32)],
        scratch_shapes=[pltpu.VMEM((ROUTE_W, 1), F32)],
        compiler_params=_params(1),
    )(route)


def _dispatch_kernel(dest_ref, pad_end_ref, pad_len_ref, n_used_ref, hm_ref, xs_ref, zeros, sem, pad_sem):
    step = pl.program_id(0)
    base = step * TILE
    half = BLOCK_M // 2

    def for_each_pad_copy(fn):
        def per_class(c, carry):
            off = pad_end_ref[c]
            n = pad_len_ref[c]
            for shift in range(BLOCK_M.bit_length() - 2, -1, -1):
                bit = 1 << shift
                off = off - (n & bit)

                @pl.when((n & bit) != 0)
                def _():
                    if bit >= 8:
                        fn(pltpu.make_async_copy(zeros.at[pl.ds(0, bit)], xs_ref.at[pl.ds(pl.multiple_of(off, 8), bit)],
                                                 pad_sem))
                    else:
                        for j in range(bit):
                            fn(pltpu.make_async_copy(zeros.at[pl.ds(0, 1)], xs_ref.at[pl.ds(off + j, 1)], pad_sem))

            return carry

        lax.fori_loop(0, N_CLASSES, per_class, 0)

        def per_half_block(j, carry):
            fn(pltpu.make_async_copy(zeros, xs_ref.at[pl.ds(pl.multiple_of(j * half, 8), half)], pad_sem))
            return carry

        lax.fori_loop(n_used_ref[0] * 2, xs_ref.shape[0] // half, per_half_block, 0)

    @pl.when(step == 0)
    def _():
        zeros[...] = jnp.zeros_like(zeros)
        for_each_pad_copy(lambda cp: cp.start())

    for r in range(TILE):
        pltpu.make_async_copy(hm_ref.at[pl.ds(r, 1)], xs_ref.at[pl.ds(dest_ref[base + r], 1)], sem).start()
    for r in range(TILE):
        pltpu.make_async_copy(hm_ref.at[pl.ds(0, 1)], xs_ref.at[pl.ds(0, 1)], sem).wait()

    @pl.when(step == pl.num_programs(0) - 1)
    def _():
        for_each_pad_copy(lambda cp: cp.wait())


def _dispatch_call(dest, pad_end, pad_len, n_used, hm_flat, n_slots):
    n_tok = hm_flat.shape[0]
    grid_spec = pltpu.PrefetchScalarGridSpec(
        num_scalar_prefetch=4,
        grid=(n_tok // TILE,),
        in_specs=[pl.BlockSpec((TILE, D_MODEL), lambda i, *_: (i, 0))],
        out_specs=pl.BlockSpec(memory_space=pl.ANY),
        scratch_shapes=[pltpu.VMEM((BLOCK_M // 2, D_MODEL), F32), pltpu.SemaphoreType.DMA, pltpu.SemaphoreType.DMA],
    )
    return pl.pallas_call(
        _dispatch_kernel,
        grid_spec=grid_spec,
        out_shape=jax.ShapeDtypeStruct((n_slots, D_MODEL), F32),
        compiler_params=_params(1),
    )(dest, pad_end, pad_len, n_used, hm_flat)


def _expert_kernel(blk_lo_ref, blk_hi_ref, n_used_ref, xs_ref, wr_ref, br_ref, wg_lo, wu_lo, wd_lo, wg_hi, wu_hi, wd_hi,
                   ys_ref):
    i = pl.program_id(0)
    used = i < n_used_ref[0]

    @pl.when(used)
    def _():
        xb = xs_ref[...].astype(BF16)

        logits = jnp.dot(xb, wr_ref[0], preferred_element_type=F32) + br_ref[0]
        lane = lax.broadcasted_iota(jnp.int32, logits.shape, 1)
        gl = jnp.where(lane < N_GROUPS, logits, -jnp.inf)
        g_prob = 1.0 / jnp.sum(jnp.exp(gl - jnp.max(gl, axis=-1, keepdims=True)), axis=-1, keepdims=True)
        l_lo = jnp.sum(jnp.where(lane == N_GROUPS + blk_lo_ref[i], logits, 0.0), axis=-1, keepdims=True)
        l_hi = jnp.sum(jnp.where(lane == N_GROUPS + blk_hi_ref[i], logits, 0.0), axis=-1, keepdims=True)
        m = jnp.maximum(l_lo, l_hi)
        p_lo = jnp.exp(l_lo - m)
        p_hi = jnp.exp(l_hi - m)

        h_lo = jnp.dot(xb, wg_lo[0, 0], preferred_element_type=F32)
        u_lo = jnp.dot(xb, wu_lo[0, 0], preferred_element_type=F32)
        h_hi = jnp.dot(xb, wg_hi[0, 0], preferred_element_type=F32)
        u_hi = jnp.dot(xb, wu_hi[0, 0], preferred_element_type=F32)
        a_lo = (_silu(h_lo) * u_lo).astype(BF16)
        a_hi = (_silu(h_hi) * u_hi).astype(BF16)
        y_lo = jnp.dot(a_lo, wd_lo[0, 0], preferred_element_type=F32)
        y_hi = jnp.dot(a_hi, wd_hi[0, 0], preferred_element_type=F32)
        ys_ref[...] = y_lo * (g_prob * (p_lo / (p_lo + p_hi))) + y_hi * (g_prob * (p_hi / (p_lo + p_hi)))

    @pl.when(jnp.logical_not(used))
    def _():
        ys_ref[...] = jnp.zeros_like(ys_ref)


def _expert_call(layer, blk_lo, blk_hi, n_used, xs, w_route_bf, b_route, w_gate_bf, w_up_bf, w_down_bf):
    n_blocks = xs.shape[0] // BLOCK_M
    up_spec = lambda which: pl.BlockSpec((1, 1, D_MODEL, EXPERT_HIDDEN),
                                         lambda i, lo, hi, nu: (layer, (lo, hi)[which][i], 0, 0))
    down_spec = lambda which: pl.BlockSpec((1, 1, EXPERT_HIDDEN, D_MODEL),
                                           lambda i, lo, hi, nu: (layer, (lo, hi)[which][i], 0, 0))
    grid_spec = pltpu.PrefetchScalarGridSpec(
        num_scalar_prefetch=3,
        grid=(n_blocks,),
        in_specs=[pl.BlockSpec((BLOCK_M, D_MODEL), lambda i, lo, hi, nu: (jnp.minimum(i, nu[0] - 1), 0)),
                  pl.BlockSpec((1, D_MODEL, ROUTE_W), lambda i, *_: (layer, 0, 0)),
                  pl.BlockSpec((1, 1, ROUTE_W), lambda i, *_: (layer, 0, 0)),
                  up_spec(0), up_spec(0), down_spec(0), up_spec(1), up_spec(1), down_spec(1)],
        out_specs=pl.BlockSpec((BLOCK_M, D_MODEL), lambda i, *_: (i, 0)),
    )
    return pl.pallas_call(
        _expert_kernel,
        grid_spec=grid_spec,
        out_shape=jax.ShapeDtypeStruct((xs.shape[0], D_MODEL), F32),
        compiler_params=_params(1),
    )(blk_lo, blk_hi, n_used, xs, w_route_bf, b_route, w_gate_bf, w_up_bf, w_down_bf, w_gate_bf, w_up_bf, w_down_bf)


def _combine_kernel(tiles_per_batch, dest_ref, ys_ref, x1_ref, mod_ref, ln_ref, out_ref, buf, sem):
    step = pl.program_id(0) * tiles_per_batch + pl.program_id(1)
    n_steps = pl.num_programs(0) * tiles_per_batch
    slot = step % 2
    nxt = jnp.minimum(step + 1, n_steps - 1)

    def gather(tile, to_slot, rows):
        for r in rows:
            pltpu.make_async_copy(ys_ref.at[pl.ds(dest_ref[tile * TILE + r], 1)], buf.at[to_slot, pl.ds(r, 1)],
                                  sem.at[to_slot]).start()

    def drain(from_slot):
        for r in range(TILE):
            pltpu.make_async_copy(ys_ref.at[pl.ds(0, 1)], buf.at[from_slot, pl.ds(0, 1)], sem.at[from_slot]).wait()

    @pl.when(step == 0)
    def _():
        gather(step, slot, range(TILE))

    drain(slot)
    mod = mod_ref[0, 0, 0]
    ln = ln_ref[0]
    for r0 in range(0, TILE, COMBINE_ROWS):
        rows = slice(r0, r0 + COMBINE_ROWS)
        out_ref[0, rows, :] = _standardize(ALPHA * x1_ref[0, rows, :] + mod[5:6] * buf[slot, rows, :]) * ln[0:1] + ln[1:2]
        gather(nxt, 1 - slot, range(r0, r0 + COMBINE_ROWS))

    @pl.when(step == n_steps - 1)
    def _():
        drain(1 - slot)


def _combine_call(layer, n_lat_tiles, dest, ys, x1, mod, ln2):
    bsz, length, _ = x1.shape
    tiles_per_batch = length // TILE
    tok = lambda w: pl.BlockSpec((1, TILE, w), lambda b, i, *_: (b, i, 0))
    grid_spec = pltpu.PrefetchScalarGridSpec(
        num_scalar_prefetch=1,
        grid=(bsz, tiles_per_batch),
        in_specs=[
            pl.BlockSpec(memory_space=pl.ANY),
            tok(D_MODEL),
            pl.BlockSpec((1, 1, 1, MOD_ROWS, D_MODEL),
                         lambda b, i, *_: (layer, b, jnp.minimum(i // n_lat_tiles, 1), 0, 0)),
            pl.BlockSpec((1, 8, D_MODEL), lambda b, i, *_: (layer, 0, 0)),
        ],
        out_specs=tok(D_MODEL),
        scratch_shapes=[pltpu.VMEM((2, TILE, D_MODEL), F32), pltpu.SemaphoreType.DMA((2,))],
    )
    return pl.pallas_call(
        functools.partial(_combine_kernel, tiles_per_batch),
        grid_spec=grid_spec,
        out_shape=jax.ShapeDtypeStruct(x1.shape, F32),
        compiler_params=_params(2),
    )(dest, ys, x1, mod, ln2)


def _dispatch_plan(class_of_token, rank, counts):
    n_tok = class_of_token.shape[0]
    counts = counts.astype(jnp.int32)
    pcounts = (counts + BLOCK_M - 1) // BLOCK_M * BLOCK_M
    pends = jnp.cumsum(pcounts)
    pstarts = pends - pcounts
    classes = jnp.arange(ROUTE_W, dtype=jnp.int32)
    dest = jnp.sum(jnp.where(class_of_token[:, None] == classes[None, :], pstarts[None, :], 0), axis=1) + rank
    n_blocks = n_tok // BLOCK_M + N_CLASSES
    blk_start = jnp.arange(n_blocks, dtype=jnp.int32) * BLOCK_M
    blk_class = jnp.minimum(jnp.sum((pends[None, :] <= blk_start[:, None]).astype(jnp.int32), axis=1), N_CLASSES - 1)
    n_used = pends[-1:] // BLOCK_M
    return (dest.astype(jnp.int32), pends, pcounts - counts, jnp.asarray(CLASS_LO)[blk_class],
            jnp.asarray(CLASS_HI)[blk_class], n_used.astype(jnp.int32), n_blocks * BLOCK_M)


def _pad_rows(a, rows):
    return jnp.pad(a, [(0, 0)] * (a.ndim - 2) + [(0, rows - a.shape[-2]), (0, 0)])


def kernel(x, c, ctx, c_ctx, w_ada, b_ada, w_in, conv_w, ret_decay_fwd, ret_decay_bwd, sgu_w, sgu_b, w_out, ln1_g, ln1_b,
           router_group_w, router_group_b, router_expert_w, router_expert_b, moe_w_gate, moe_w_up, moe_w_down, ln2_g,
           ln2_b):
    bsz, seq, d = x.shape
    ctx_len = ctx.shape[1]
    assert d == D_MODEL and ctx_len == TILE and seq % TILE == 0 and seq % GRID_W == 0
    n_lat_tiles = seq // TILE
    n_lat_chunks = seq // CHUNK

    cond = _pad_rows(jnp.concatenate([c, c_ctx[None, :]], axis=0), 16)
    ada = _ada_call(cond, w_ada, b_ada)
    mod_lat = ada[:, :bsz].reshape(DEPTH, bsz, N_MOD, d)
    mod_ctx = jnp.broadcast_to(ada[:, bsz].reshape(DEPTH, 1, N_MOD, d), mod_lat.shape)
    mod = _pad_rows(jnp.stack([mod_lat, mod_ctx], axis=2), MOD_ROWS)

    w_in_bf = w_in.astype(BF16)
    w_out_bf = w_out.astype(BF16)
    sgu_w_bf = sgu_w.astype(BF16)
    w_gate_bf = moe_w_gate.astype(BF16)
    w_up_bf = moe_w_up.astype(BF16)
    w_down_bf = moe_w_down.astype(BF16)
    sgu_bias = jnp.repeat(jnp.swapaxes(sgu_b, 1, 2), SGU_W // SGU_GROUPS, axis=2)
    conv_w8 = _pad_rows(conv_w, 8)
    ln1 = _pad_rows(jnp.stack([ln1_g, ln1_b], axis=1), 8)
    ln2 = _pad_rows(jnp.stack([ln2_g, ln2_b], axis=1), 8)
    w_route = jnp.concatenate([router_group_w, jnp.swapaxes(router_expert_w, 1, 2).reshape(DEPTH, d, N_EXPERTS)], axis=2)
    w_route_bf = jnp.pad(w_route, ((0, 0), (0, 0), (0, ROUTE_W - w_route.shape[2]))).astype(BF16)
    b_route = jnp.concatenate([router_group_b, router_expert_b.reshape(DEPTH, N_EXPERTS)], axis=1)
    b_route = jnp.pad(b_route, ((0, 0), (0, ROUTE_W - b_route.shape[1])))[:, None, :].astype(F32)
    w_route_t = jnp.concatenate([_pad_rows(jnp.swapaxes(w_route[:, :, :N_GROUPS], 1, 2), EXPERT_ROW0),
                                 jnp.swapaxes(w_route[:, :, N_GROUPS:], 1, 2)], axis=1)
    w_route_t = _pad_rows(w_route_t, ROUTE_ROWS).astype(BF16)
    b_route_t = jnp.concatenate([_pad_rows(b_route[:, 0, :N_GROUPS, None], EXPERT_ROW0),
                                 b_route[:, 0, N_GROUPS:N_GROUPS + N_EXPERTS, None]], axis=1)
    b_route_t = _pad_rows(b_route_t, ROUTE_ROWS)
    lg = jnp.stack([jax.nn.log_sigmoid(ret_decay_fwd.astype(F32)), jax.nn.log_sigmoid(ret_decay_bwd.astype(F32))],
                   axis=1)
    lg_lanes = jnp.repeat(lg, RET_DK, axis=2).reshape(DEPTH * 2, RET_W)
    lg_rows = lg_lanes[:, None, :]
    lg_cols = lg_lanes[:, :, None]

    x_lat, x_ctx, ctx_tile = x, ctx, 0
    for layer in range(DEPTH):
        last = layer == DEPTH - 1
        pc, q, k, v, g, su, sv, kv = _inproj_call(layer, x_lat, x_ctx, ctx_tile, mod, w_in_bf, lg_rows, n_lat_tiles)
        states = _scan_call(layer, kv, lg_cols, n_lat_chunks)
        n_proc = n_lat_tiles if last else n_lat_tiles + 1
        x1, hm, route = _mix_call(layer, n_lat_tiles, n_proc, lg[layer, 0], lg[layer, 1], pc, q, k, v, g, su, sv, states,
                                  x_lat, x_ctx, ctx_tile, mod, conv_w8, sgu_w_bf, sgu_bias, w_out_bf, ln1, w_route_t,
                                  b_route_t, lg_rows)
        class_of_token, rank, counts = _rank_call(route.reshape(-1, 8, TILE))
        dest, pad_end, pad_len, blk_lo, blk_hi, n_used, n_slots = _dispatch_plan(class_of_token.reshape(-1),
                                                                                  rank.reshape(-1), counts[:, 0])
        xs = _dispatch_call(dest, pad_end, pad_len, n_used, hm.reshape(-1, D_MODEL), n_slots)
        ys = _expert_call(layer, blk_lo, blk_hi, n_used, xs, w_route_bf, b_route, w_gate_bf, w_up_bf, w_down_bf)
        xa = _combine_call(layer, n_lat_tiles, dest, ys, x1, mod, ln2)
        x_lat, x_ctx, ctx_tile = xa, xa, n_lat_tiles
    return xa
```

```python
import functools
import math

import jax
import jax.numpy as jnp
import numpy as np
from jax import lax
from jax.experimental import pallas as pl
from jax.experimental.pallas import tpu as pltpu

F32 = jnp.float32
BF16 = jnp.bfloat16

D_MODEL = 1024
DEPTH = 2
GRID_W = 64
CONV_W = 256
RET_W = 512
RET_HEADS = 8
RET_DK = 64
PAIR_W = 2 * RET_DK
CHUNK = 128
SGU_W = 256
SGU_GROUPS = 4
IN_COLS = 3 * CONV_W + 4 * RET_W + 2 * SGU_W
N_GROUPS = 4
EXPERTS_PER_GROUP = 8
N_EXPERTS = N_GROUPS * EXPERTS_PER_GROUP
EXPERT_HIDDEN = 512
N_MOD = 6
MOD_ROWS = 8
LN_EPS = 1e-5
ALPHA = (2 * DEPTH) ** 0.25

TILE = 256
HALO = GRID_W
BLOCK_M = 256
ROUTE_W = 128
PACK_W = D_MODEL // 2
ROUTE_ROWS = 48
EXPERT_ROW0 = 8
RANK_TILES = 8
PAIRS_PER_GROUP = EXPERTS_PER_GROUP * (EXPERTS_PER_GROUP - 1) // 2
N_CLASSES = N_GROUPS * PAIRS_PER_GROUP
CLASS_LO = np.array([g * EXPERTS_PER_GROUP + lo for g in range(N_GROUPS) for lo in range(EXPERTS_PER_GROUP)
                     for hi in range(lo + 1, EXPERTS_PER_GROUP)] + [N_EXPERTS - 2] * (ROUTE_W - N_CLASSES), np.int32)
CLASS_HI = np.array([g * EXPERTS_PER_GROUP + hi for g in range(N_GROUPS) for lo in range(EXPERTS_PER_GROUP)
                     for hi in range(lo + 1, EXPERTS_PER_GROUP)] + [N_EXPERTS - 1] * (ROUTE_W - N_CLASSES), np.int32)
VMEM_LIMIT = 56 * 1024 * 1024


def _params(n_axes):
    return pltpu.CompilerParams(dimension_semantics=("arbitrary",) * n_axes, vmem_limit_bytes=VMEM_LIMIT)


def _standardize(v):
    mu = jnp.mean(v, axis=-1, keepdims=True)
    var = jnp.mean(jnp.square(v - mu), axis=-1, keepdims=True)
    return (v - mu) * lax.rsqrt(var + LN_EPS)


def _silu(v):
    return v * jax.nn.sigmoid(v)


def _pack_bf16_pairs(v):
    half = v.shape[1] // 2
    bits = lambda t: pltpu.bitcast(t.astype(BF16).astype(F32), jnp.uint32)
    return bits(v[:, :half]) | (bits(v[:, half:]) >> 16)


def _unpack_bf16_pairs(w):
    hi = pltpu.bitcast(w & jnp.uint32(0xFFFF0000), F32)
    lo = pltpu.bitcast(w << 16, F32)
    return jnp.concatenate([hi, lo], axis=1).astype(BF16)


def _ada_kernel(c_ref, w_ref, b_ref, o_ref):
    a = _silu(c_ref[...]).astype(BF16)
    o_ref[0] = jnp.dot(a, w_ref[0].astype(BF16), preferred_element_type=F32) + b_ref[0]


def _ada_call(cond, w_ada, b_ada):
    rows = cond.shape[0]
    cols = w_ada.shape[-1]
    tn = 1536
    return pl.pallas_call(
        _ada_kernel,
        grid=(DEPTH, cols // tn),
        in_specs=[
            pl.BlockSpec((rows, D_MODEL), lambda l, j: (0, 0)),
            pl.BlockSpec((1, D_MODEL, tn), lambda l, j: (l, 0, j)),
            pl.BlockSpec((1, 1, tn), lambda l, j: (l, 0, j)),
        ],
        out_specs=pl.BlockSpec((1, rows, tn), lambda l, j: (l, 0, j)),
        out_shape=jax.ShapeDtypeStruct((DEPTH, rows, cols), F32),
        compiler_params=_params(2),
    )(cond, w_ada, b_ada.reshape(DEPTH, 1, cols))


def _inproj_kernel(n_lat_tiles, x_ref, ctx_ref, mod_ref, w_ref, lgf_ref, lgb_ref, pc_ref, q_ref, k_ref, v_ref, g_ref,
                   su_ref, sv_ref, kv_ref):
    mod = mod_ref[0, 0, 0]
    xin = jnp.where(pl.program_id(1) >= n_lat_tiles, ctx_ref[0], x_ref[0])
    h = (xin * (1.0 + mod[1:2]) + mod[0:1]).astype(BF16)

    def proj(lo, hi):
        return jnp.dot(h, w_ref[0, :, lo:hi], preferred_element_type=F32)

    o = 3 * CONV_W
    pc_ref[0] = proj(0, o)
    q_ref[0] = proj(o, o + RET_W)
    k = proj(o + RET_W, o + 2 * RET_W) * (RET_DK ** -0.5)
    k_ref[0] = k
    v = proj(o + 2 * RET_W, o + 3 * RET_W)
    v_ref[0] = v
    g_ref[0] = proj(o + 3 * RET_W, o + 4 * RET_W)
    o += 4 * RET_W
    su_ref[0] = proj(o, o + SGU_W)
    sv_ref[0] = proj(o + SGU_W, o + 2 * SGU_W)

    pos = lax.broadcasted_iota(jnp.int32, (CHUNK, 1), 0).astype(F32)
    k_decay = (jnp.exp((CHUNK - 1.0 - pos) * lgf_ref[0]), jnp.exp(pos * lgb_ref[0]))
    contract_rows = (((0,), (0,)), ((), ()))
    for c in range(TILE // CHUNK):
        rows = slice(c * CHUNK, (c + 1) * CHUNK)
        vc = v[rows].astype(BF16)
        for d in range(2):
            kc = (k[rows] * k_decay[d]).astype(BF16)
            for p in range(RET_W // PAIR_W):
                cols = slice(p * PAIR_W, (p + 1) * PAIR_W)
                kv_ref[d, 0, c, cols, :] = lax.dot_general(kc[:, cols], vc[:, cols], contract_rows,
                                                           preferred_element_type=F32)


def _inproj_call(layer, x_lat, x_ctx, ctx_tile, mod, w_in_bf, lg_rows, n_lat_tiles):
    bsz = x_lat.shape[0]
    length = (n_lat_tiles + 1) * TILE
    widths = (3 * CONV_W, RET_W, RET_W, RET_W, RET_W, SGU_W, SGU_W)
    cpt = TILE // CHUNK
    return pl.pallas_call(
        functools.partial(_inproj_kernel, n_lat_tiles),
        grid=(bsz, length // TILE),
        in_specs=[
            pl.BlockSpec((1, TILE, D_MODEL), lambda b, i: (b, jnp.minimum(i, n_lat_tiles - 1), 0)),
            pl.BlockSpec((1, TILE, D_MODEL), lambda b, i: (b, ctx_tile, 0)),
            pl.BlockSpec((1, 1, 1, MOD_ROWS, D_MODEL), lambda b, i: (layer, b, jnp.minimum(i // n_lat_tiles, 1), 0, 0)),
            pl.BlockSpec((1, D_MODEL, IN_COLS), lambda b, i: (layer, 0, 0)),
            pl.BlockSpec((1, 1, RET_W), lambda b, i: (2 * layer, 0, 0)),
            pl.BlockSpec((1, 1, RET_W), lambda b, i: (2 * layer + 1, 0, 0)),
        ],
        out_specs=[pl.BlockSpec((1, TILE, w), lambda b, i: (b, i, 0)) for w in widths]
        + [pl.BlockSpec((2, 1, cpt, RET_W, PAIR_W), lambda b, i: (0, b, i, 0, 0))],
        out_shape=[jax.ShapeDtypeStruct((bsz, length, w), F32) for w in widths]
        + [jax.ShapeDtypeStruct((2, bsz, length // CHUNK, RET_W, PAIR_W), F32)],
        compiler_params=_params(2),
    )(x_lat, x_ctx, mod, w_in_bf, lg_rows, lg_rows)


def _scan_kernel(n_lat_chunks, kv_ref, lgc_ref, s_ref, state):
    nc = kv_ref.shape[2]
    backward = pl.program_id(1)
    decay = jnp.exp(CHUNK * lgc_ref[0])
    row_head = lax.broadcasted_iota(jnp.int32, (RET_W, PAIR_W), 0) % PAIR_W // RET_DK
    lane_head = lax.broadcasted_iota(jnp.int32, (RET_W, PAIR_W), 1) // RET_DK
    own_head = row_head == lane_head
    state[...] = jnp.zeros_like(state)

    def step(s, carry):
        c = jnp.where(backward == 1, nc - 1 - s, (s + n_lat_chunks) % nc)
        s_ref[0, 0, c] = jnp.where(own_head, state[...], 0.0).astype(BF16)
        state[...] = decay * state[...] + kv_ref[0, 0, c]
        return carry

    lax.fori_loop(0, nc, step, 0)


def _scan_call(layer, kv, lg_cols, n_lat_chunks):
    _, bsz, nc, _, _ = kv.shape
    blk = (1, 1, nc, RET_W, PAIR_W)
    return pl.pallas_call(
        functools.partial(_scan_kernel, n_lat_chunks),
        grid=(bsz, 2),
        in_specs=[pl.BlockSpec(blk, lambda b, d: (d, b, 0, 0, 0)),
                  pl.BlockSpec((1, RET_W, 1), lambda b, d: (2 * layer + d, 0, 0))],
        out_specs=pl.BlockSpec(blk, lambda b, d: (d, b, 0, 0, 0)),
        out_shape=jax.ShapeDtypeStruct(kv.shape, BF16),
        scratch_shapes=[pltpu.VMEM((RET_W, PAIR_W), F32)],
        compiler_params=_params(2),
    )(kv, lg_cols)


def _route_class(lt):
    assert EXPERTS_PER_GROUP == 8 and N_GROUPS <= 8
    sub = lax.broadcasted_iota(jnp.int32, (8, lt.shape[1]), 0)
    neg = jnp.float32(-jnp.inf)

    def top(vals):
        m = jnp.max(vals, axis=0, keepdims=True)
        return jnp.min(jnp.where(vals == m, sub, 8), axis=0, keepdims=True)

    gidx = top(jnp.where(sub < N_GROUPS, lt[0:8], neg))
    pair = jnp.zeros_like(gidx)
    for g in range(N_GROUPS):
        e = lt[EXPERT_ROW0 + 8 * g:EXPERT_ROW0 + 8 * (g + 1)]
        i1 = top(e)
        i2 = top(jnp.where(sub == i1, neg, e))
        e_lo = jnp.minimum(i1, i2)
        e_hi = jnp.maximum(i1, i2)
        pair_g = e_lo * (EXPERTS_PER_GROUP - 1) - ((e_lo * (e_lo - 1)) >> 1) + (e_hi - e_lo - 1)
        pair = jnp.where(gidx == g, pair_g, pair)
    return gidx * PAIRS_PER_GROUP + pair


def _mix_kernel(n_lat_tiles, lgf_ref, lgb_ref, pc_ref, hp_ref, hn_ref, q_ref, k_ref, v_ref, g_ref, su_ref, sv_ref,
                sf_ref, sb_ref, x_ref, ctx_ref, mod_ref, convw_ref, sguw_ref, sgub_ref, wout_ref, ln_ref, wr_ref, br_ref,
                lgfr_ref, lgbr_ref, x1_ref, hm_ref, route_ref, ycat, dec, qdec):
    i = pl.program_id(1)
    is_ctx = i >= n_lat_tiles
    row = lax.broadcasted_iota(jnp.int32, (TILE, 1), 0)

    @pl.when((pl.program_id(0) == 0) & (i == 0))
    def _():
        pos = lax.broadcasted_iota(jnp.int32, (CHUNK, 1), 0).astype(F32)
        rel = pos - lax.broadcasted_iota(jnp.int32, (1, CHUNK), 1).astype(F32)
        for h in range(RET_HEADS):
            dec[h // 2, (h % 2) * CHUNK:(h % 2 + 1) * CHUNK, :] = jnp.where(
                rel > 0, jnp.exp(lgf_ref[h] * jnp.maximum(rel, 0.0)),
                jnp.where(rel < 0, jnp.exp(lgb_ref[h] * jnp.maximum(-rel, 0.0)), 2.0))
        qdec[0] = jnp.exp(lgfr_ref[0] * (pos + 1.0))
        qdec[1] = jnp.exp(lgbr_ref[0] * (CHUNK - pos))

    low_head = lax.broadcasted_iota(jnp.int32, (1, PAIR_W), 1) < RET_DK

    def half_norm(o):
        def half_mean(t):
            lo = jnp.sum(jnp.where(low_head, t, 0.0), axis=-1, keepdims=True)
            hi = jnp.sum(jnp.where(low_head, 0.0, t), axis=-1, keepdims=True)
            return jnp.where(low_head, lo, hi) * (1.0 / RET_DK)

        centred = o - half_mean(o)
        return centred * lax.rsqrt(half_mean(jnp.square(centred)) + LN_EPS)

    tiles = [(slice(c * CHUNK, (c + 1) * CHUNK), slice(p * PAIR_W, (p + 1) * PAIR_W), c, p)
             for c in range(TILE // CHUNK) for p in range(RET_W // PAIR_W)]
    scores = []
    for rows, cols, c, p in tiles:
        qp = q_ref[0, rows, cols]
        q_split = jnp.concatenate([jnp.where(low_head, qp, 0.0), jnp.where(low_head, 0.0, qp)], axis=0)
        scores.append(lax.dot_general(q_split.astype(BF16), k_ref[0, rows, cols].astype(BF16),
                                      (((1,), (1,)), ((), ())), preferred_element_type=F32))

    pc = pc_ref[0]
    z = pc[:, CONV_W:2 * CONV_W] * pc[:, 2 * CONV_W:3 * CONV_W]
    line_mask = jnp.where(is_ctx, TILE - 1, GRID_W - 1)
    first = (row & line_mask) == 0
    last = (row & line_mask) == line_mask
    z_prev = jnp.where(first, 0.0, pltpu.roll(z, 1, 0))
    z_next = jnp.where(last, 0.0, pltpu.roll(z, TILE - 1, 0))
    hp = hp_ref[0]
    hn = hn_ref[0]
    z_top = jnp.where(i == 0, 0.0, hp[:, CONV_W:2 * CONV_W] * hp[:, 2 * CONV_W:3 * CONV_W])
    z_bot = jnp.where(i == n_lat_tiles - 1, 0.0, hn[:, CONV_W:2 * CONV_W] * hn[:, 2 * CONV_W:3 * CONV_W])
    zcat = jnp.concatenate([z_top, z, z_bot], axis=0)
    z_up = zcat[0:TILE]
    z_down = zcat[2 * HALO:2 * HALO + TILE]
    along_seq = lax.broadcasted_iota(jnp.int32, (1, CONV_W), 1) < jnp.where(is_ctx, CONV_W, CONV_W // 2)
    cw = convw_ref[0]
    conv = cw[0:1] * jnp.where(along_seq, z_prev, z_up) + cw[1:2] * z
    conv = conv + cw[2:3] * jnp.where(along_seq, z_next, z_down)
    ycat[:, 0:CONV_W] = (pc[:, 0:CONV_W] * conv).astype(BF16)

    outs = []
    for (rows, cols, c, p), sc in zip(tiles, scores):
        qp = q_ref[0, rows, cols]
        vp = v_ref[0, rows, cols]
        sc = sc * dec[p]
        lhs = jnp.concatenate([sc[0:CHUNK].astype(BF16), sc[CHUNK:].astype(BF16),
                               (qp * qdec[0, :, cols]).astype(BF16), (qp * qdec[1, :, cols]).astype(BF16)], axis=1)
        rhs = jnp.concatenate([jnp.where(low_head, vp, 0.0).astype(BF16), jnp.where(low_head, 0.0, vp).astype(BF16),
                               sf_ref[0, 0, c, cols, :], sb_ref[0, 0, c, cols, :]], axis=0)
        outs.append(jnp.dot(lhs, rhs, preferred_element_type=F32))

    vn = _standardize(sv_ref[0]).astype(BF16)
    group = lax.broadcasted_iota(jnp.int32, (1, SGU_W), 1) // (SGU_W // SGU_GROUPS)
    for c in range(TILE // CHUNK):
        rows = slice(c * CHUNK, (c + 1) * CHUNK)
        mixed = jnp.zeros((CHUNK, SGU_W), F32)
        for gi in range(SGU_GROUPS):
            m = jnp.dot(sguw_ref[0, gi], vn[rows], preferred_element_type=F32)
            mixed = jnp.where(group == gi, m, mixed)
        ycat[rows, CONV_W + RET_W:] = (su_ref[0, rows, :] * (mixed + sgub_ref[0])).astype(BF16)

    for (rows, cols, c, p), o in zip(tiles, outs):
        ycat[rows, CONV_W + p * PAIR_W:CONV_W + (p + 1) * PAIR_W] = (
            _silu(g_ref[0, rows, cols]) * half_norm(o)).astype(BF16)

    y = jnp.dot(ycat[...], wout_ref[0], preferred_element_type=F32)
    mod = mod_ref[0, 0, 0]
    ln = ln_ref[0]
    x1 = _standardize(ALPHA * jnp.where(is_ctx, ctx_ref[0], x_ref[0]) + mod[2:3] * y) * ln[0:1] + ln[1:2]
    x1_ref[0] = x1
    hm = x1 * (1.0 + mod[4:5]) + mod[3:4]
    hm_ref[0] = _pack_bf16_pairs(hm)
    lt = lax.dot_general(wr_ref[0], hm.astype(BF16), (((1,), (1,)), ((), ())), preferred_element_type=F32) + br_ref[0]
    route_ref[0, 0] = jnp.concatenate([_route_class(lt).astype(F32), jnp.zeros((7, TILE), F32)], axis=0)


def _mix_call(layer, n_lat_tiles, n_proc, lgf, lgb, pc, q, k, v, g, su, sv, states, x_lat, x_ctx, ctx_tile, mod, conv_w,
              sgu_w_bf, sgu_bias, w_out_bf, ln1, w_route_t, b_route_t, lg_rows):
    bsz, length, _ = q.shape
    halos_per_tile = TILE // HALO
    n_halo = length // HALO
    cpt = TILE // CHUNK
    tok = lambda w: pl.BlockSpec((1, TILE, w), lambda b, i, *_: (b, i, 0))
    per_layer = lambda *shape: pl.BlockSpec((1,) + shape, lambda b, i, *_: (layer,) + (0,) * len(shape))
    grid_spec = pltpu.PrefetchScalarGridSpec(
        num_scalar_prefetch=2,
        grid=(bsz, n_proc),
        in_specs=[
            tok(3 * CONV_W),
            pl.BlockSpec((1, HALO, 3 * CONV_W), lambda b, i, *_: (b, jnp.maximum(i * halos_per_tile - 1, 0), 0)),
            pl.BlockSpec((1, HALO, 3 * CONV_W),
                         lambda b, i, *_: (b, jnp.minimum((i + 1) * halos_per_tile, n_halo - 1), 0)),
            tok(RET_W), tok(RET_W), tok(RET_W), tok(RET_W), tok(SGU_W), tok(SGU_W),
            pl.BlockSpec((1, 1, cpt, RET_W, PAIR_W), lambda b, i, *_: (0, b, i, 0, 0)),
            pl.BlockSpec((1, 1, cpt, RET_W, PAIR_W), lambda b, i, *_: (1, b, i, 0, 0)),
            pl.BlockSpec((1, TILE, D_MODEL), lambda b, i, *_: (b, jnp.minimum(i, n_lat_tiles - 1), 0)),
            pl.BlockSpec((1, TILE, D_MODEL), lambda b, i, *_: (b, ctx_tile, 0)),
            pl.BlockSpec((1, 1, 1, MOD_ROWS, D_MODEL),
                         lambda b, i, *_: (layer, b, jnp.minimum(i // n_lat_tiles, 1), 0, 0)),
            per_layer(8, CONV_W),
            per_layer(SGU_GROUPS, CHUNK, CHUNK),
            per_layer(CHUNK, SGU_W),
            per_layer(D_MODEL, D_MODEL),
            per_layer(8, D_MODEL),
            per_layer(ROUTE_ROWS, D_MODEL),
            per_layer(ROUTE_ROWS, 1),
            pl.BlockSpec((1, 1, RET_W), lambda b, i, *_: (2 * layer, 0, 0)),
            pl.BlockSpec((1, 1, RET_W), lambda b, i, *_: (2 * layer + 1, 0, 0)),
        ],
        out_specs=[tok(D_MODEL), tok(PACK_W), pl.BlockSpec((1, 1, 8, TILE), lambda b, i, *_: (b, i, 0, 0))],
        scratch_shapes=[pltpu.VMEM((TILE, D_MODEL), BF16),
                        pltpu.VMEM((RET_W // PAIR_W, 2 * CHUNK, CHUNK), F32),
                        pltpu.VMEM((2, CHUNK, RET_W), F32)],
    )
    return pl.pallas_call(
        functools.partial(_mix_kernel, n_lat_tiles),
        grid_spec=grid_spec,
        out_shape=[jax.ShapeDtypeStruct((bsz, n_proc * TILE, D_MODEL), F32),
                   jax.ShapeDtypeStruct((bsz, n_proc * TILE, PACK_W), jnp.uint32),
                   jax.ShapeDtypeStruct((bsz, n_proc, 8, TILE), F32)],
        compiler_params=_params(2),
    )(lgf, lgb, pc, pc, pc, q, k, v, g, su, sv, states, states, x_lat, x_ctx, mod, conv_w, sgu_w_bf, sgu_bias, w_out_bf,
      ln1, w_route_t, b_route_t, lg_rows, lg_rows)


def _rank_kernel(route_ref, class_ref, rank_ref, counts_ref, running):
    @pl.when(pl.program_id(0) == 0)
    def _():
        running[...] = jnp.zeros_like(running)

    sub = lax.broadcasted_iota(jnp.int32, (ROUTE_W, TILE), 0)
    earlier = (lax.broadcasted_iota(jnp.int32, (TILE, TILE), 0)
               < lax.broadcasted_iota(jnp.int32, (TILE, TILE), 1)).astype(BF16)
    for t in range(route_ref.shape[0]):
        cls = route_ref[t, 0:1, :].astype(jnp.int32)
        class_ref[t] = cls
        onehot = sub == cls
        before = jnp.dot(onehot.astype(BF16), earlier, preferred_element_type=F32) + running[...]
        rank_ref[t] = jnp.sum(jnp.where(onehot, before, 0.0), axis=0, keepdims=True).astype(jnp.int32)
        running[...] += jnp.sum(onehot.astype(F32), axis=1, keepdims=True)
    counts_ref[...] = jnp.broadcast_to(running[...], counts_ref.shape)


def _rank_call(route):
    n_tiles = route.shape[0]
    per_step = math.gcd(n_tiles, RANK_TILES)
    per_tile = pl.BlockSpec((per_step, 1, TILE), lambda i: (i, 0, 0))
    return pl.pallas_call(
        _rank_kernel,
        grid=(n_tiles // per_step,),
        in_specs=[pl.BlockSpec((per_step, 8, TILE), lambda i: (i, 0, 0))],
        out_specs=[per_tile, per_tile, pl.BlockSpec((ROUTE_W, ROUTE_W), lambda i: (0, 0))],
        out_shape=[jax.ShapeDtypeStruct((n_tiles, 1, TILE), jnp.int32), jax.ShapeDtypeStruct((n_tiles, 1, TILE), jnp.int32),
                   jax.ShapeDtypeStruct((ROUTE_W, ROUTE_W), F32)],
        scratch_shapes=[pltpu.VMEM((ROUTE_W, 1), F32)],
        compiler_params=_params(1),
    )(route)


def _dispatch_kernel(dest_ref, pad_end_ref, pad_len_ref, n_used_ref, hm_ref, xs_ref, zeros, sem, pad_sem):
    step = pl.program_id(0)
    base = step * TILE
    half = BLOCK_M // 2

    def for_each_pad_copy(fn):
        def per_class(c, carry):
            off = pad_end_ref[c]
            n = pad_len_ref[c]
            for shift in range(BLOCK_M.bit_length() - 2, -1, -1):
                bit = 1 << shift
                off = off - (n & bit)

                @pl.when((n & bit) != 0)
                def _():
                    if bit >= 8:
                        fn(pltpu.make_async_copy(zeros.at[pl.ds(0, bit)], xs_ref.at[pl.ds(pl.multiple_of(off, 8), bit)],
                                                 pad_sem))
                    else:
                        for j in range(bit):
                            fn(pltpu.make_async_copy(zeros.at[pl.ds(0, 1)], xs_ref.at[pl.ds(off + j, 1)], pad_sem))

            return carry

        lax.fori_loop(0, N_CLASSES, per_class, 0)

        def per_half_block(j, carry):
            fn(pltpu.make_async_copy(zeros, xs_ref.at[pl.ds(pl.multiple_of(j * half, 8), half)], pad_sem))
            return carry

        lax.fori_loop(n_used_ref[0] * 2, xs_ref.shape[0] // half, per_half_block, 0)

    @pl.when(step == 0)
    def _():
        zeros[...] = jnp.zeros_like(zeros)
        for_each_pad_copy(lambda cp: cp.start())

    for r in range(TILE):
        pltpu.make_async_copy(hm_ref.at[pl.ds(r, 1)], xs_ref.at[pl.ds(dest_ref[base + r], 1)], sem).start()
    for r in range(TILE):
        pltpu.make_async_copy(hm_ref.at[pl.ds(0, 1)], xs_ref.at[pl.ds(0, 1)], sem).wait()

    @pl.when(step == pl.num_programs(0) - 1)
    def _():
        for_each_pad_copy(lambda cp: cp.wait())


def _dispatch_call(dest, pad_end, pad_len, n_used, hm_flat, n_slots):
    n_tok = hm_flat.shape[0]
    grid_spec = pltpu.PrefetchScalarGridSpec(
        num_scalar_prefetch=4,
        grid=(n_tok // TILE,),
        in_specs=[pl.BlockSpec((TILE, PACK_W), lambda i, *_: (i, 0))],
        out_specs=pl.BlockSpec(memory_space=pl.ANY),
        scratch_shapes=[pltpu.VMEM((BLOCK_M // 2, PACK_W), jnp.uint32), pltpu.SemaphoreType.DMA,
                        pltpu.SemaphoreType.DMA],
    )
    return pl.pallas_call(
        _dispatch_kernel,
        grid_spec=grid_spec,
        out_shape=jax.ShapeDtypeStruct((n_slots, PACK_W), jnp.uint32),
        compiler_params=_params(1),
    )(dest, pad_end, pad_len, n_used, hm_flat)


def _expert_kernel(blk_lo_ref, blk_hi_ref, n_used_ref, xs_ref, wr_ref, br_ref, wg_lo, wu_lo, wd_lo, wg_hi, wu_hi, wd_hi,
                   ys_ref):
    i = pl.program_id(0)
    used = i < n_used_ref[0]

    @pl.when(used)
    def _():
        xb = _unpack_bf16_pairs(xs_ref[...])

        logits = jnp.dot(xb, wr_ref[0], preferred_element_type=F32) + br_ref[0]
        lane = lax.broadcasted_iota(jnp.int32, logits.shape, 1)
        gl = jnp.where(lane < N_GROUPS, logits, -jnp.inf)
        g_prob = 1.0 / jnp.sum(jnp.exp(gl - jnp.max(gl, axis=-1, keepdims=True)), axis=-1, keepdims=True)
        l_lo = jnp.sum(jnp.where(lane == N_GROUPS + blk_lo_ref[i], logits, 0.0), axis=-1, keepdims=True)
        l_hi = jnp.sum(jnp.where(lane == N_GROUPS + blk_hi_ref[i], logits, 0.0), axis=-1, keepdims=True)
        m = jnp.maximum(l_lo, l_hi)
        p_lo = jnp.exp(l_lo - m)
        p_hi = jnp.exp(l_hi - m)

        h_lo = jnp.dot(xb, wg_lo[0, 0], preferred_element_type=F32)
        u_lo = jnp.dot(xb, wu_lo[0, 0], preferred_element_type=F32)
        h_hi = jnp.dot(xb, wg_hi[0, 0], preferred_element_type=F32)
        u_hi = jnp.dot(xb, wu_hi[0, 0], preferred_element_type=F32)
        a_lo = (_silu(h_lo) * u_lo).astype(BF16)
        a_hi = (_silu(h_hi) * u_hi).astype(BF16)
        y_lo = jnp.dot(a_lo, wd_lo[0, 0], preferred_element_type=F32)
        y_hi = jnp.dot(a_hi, wd_hi[0, 0], preferred_element_type=F32)
        ys_ref[...] = y_lo * (g_prob * (p_lo / (p_lo + p_hi))) + y_hi * (g_prob * (p_hi / (p_lo + p_hi)))

    @pl.when(jnp.logical_not(used))
    def _():
        ys_ref[...] = jnp.zeros_like(ys_ref)


def _expert_call(layer, blk_lo, blk_hi, n_used, xs, w_route_bf, b_route, w_gate_bf, w_up_bf, w_down_bf):
    n_blocks = xs.shape[0] // BLOCK_M
    up_spec = lambda which: pl.BlockSpec((1, 1, D_MODEL, EXPERT_HIDDEN),
                                         lambda i, lo, hi, nu: (layer, (lo, hi)[which][i], 0, 0))
    down_spec = lambda which: pl.BlockSpec((1, 1, EXPERT_HIDDEN, D_MODEL),
                                           lambda i, lo, hi, nu: (layer, (lo, hi)[which][i], 0, 0))
    grid_spec = pltpu.PrefetchScalarGridSpec(
        num_scalar_prefetch=3,
        grid=(n_blocks,),
        in_specs=[pl.BlockSpec((BLOCK_M, PACK_W), lambda i, lo, hi, nu: (jnp.minimum(i, nu[0] - 1), 0)),
                  pl.BlockSpec((1, D_MODEL, ROUTE_W), lambda i, *_: (layer, 0, 0)),
                  pl.BlockSpec((1, 1, ROUTE_W), lambda i, *_: (layer, 0, 0)),
                  up_spec(0), up_spec(0), down_spec(0), up_spec(1), up_spec(1), down_spec(1)],
        out_specs=pl.BlockSpec((BLOCK_M, D_MODEL), lambda i, *_: (i, 0)),
    )
    return pl.pallas_call(
        _expert_kernel,
        grid_spec=grid_spec,
        out_shape=jax.ShapeDtypeStruct((xs.shape[0], D_MODEL), F32),
        compiler_params=_params(1),
    )(blk_lo, blk_hi, n_used, xs, w_route_bf, b_route, w_gate_bf, w_up_bf, w_down_bf, w_gate_bf, w_up_bf, w_down_bf)


def _combine_kernel(tiles_per_batch, dest_ref, ys_ref, x1_ref, mod_ref, ln_ref, out_ref, buf, sem):
    step = pl.program_id(0) * tiles_per_batch + pl.program_id(1)
    slot = step % 2
    has_next = step + 1 < pl.num_programs(0) * tiles_per_batch

    def gather(tile, to_slot):
        base = tile * TILE

        for r in range(TILE):
            pltpu.make_async_copy(ys_ref.at[pl.ds(dest_ref[base + r], 1)], buf.at[to_slot, pl.ds(r, 1)],
                                  sem.at[to_slot]).start()

    @pl.when(step == 0)
    def _():
        gather(step, slot)

    @pl.when(has_next)
    def _():
        gather(step + 1, 1 - slot)

    for r in range(TILE):
        pltpu.make_async_copy(ys_ref.at[pl.ds(0, 1)], buf.at[slot, pl.ds(0, 1)], sem.at[slot]).wait()

    mod = mod_ref[0, 0, 0]
    ln = ln_ref[0]
    out_ref[0] = _standardize(ALPHA * x1_ref[0] + mod[5:6] * buf[slot]) * ln[0:1] + ln[1:2]


def _combine_call(layer, n_lat_tiles, dest, ys, x1, mod, ln2):
    bsz, length, _ = x1.shape
    tiles_per_batch = length // TILE
    tok = lambda w: pl.BlockSpec((1, TILE, w), lambda b, i, *_: (b, i, 0))
    grid_spec = pltpu.PrefetchScalarGridSpec(
        num_scalar_prefetch=1,
        grid=(bsz, tiles_per_batch),
        in_specs=[
            pl.BlockSpec(memory_space=pl.ANY),
            tok(D_MODEL),
            pl.BlockSpec((1, 1, 1, MOD_ROWS, D_MODEL),
                         lambda b, i, *_: (layer, b, jnp.minimum(i // n_lat_tiles, 1), 0, 0)),
            pl.BlockSpec((1, 8, D_MODEL), lambda b, i, *_: (layer, 0, 0)),
        ],
        out_specs=tok(D_MODEL),
        scratch_shapes=[pltpu.VMEM((2, TILE, D_MODEL), F32), pltpu.SemaphoreType.DMA((2,))],
    )
    return pl.pallas_call(
        functools.partial(_combine_kernel, tiles_per_batch),
        grid_spec=grid_spec,
        out_shape=jax.ShapeDtypeStruct(x1.shape, F32),
        compiler_params=_params(2),
    )(dest, ys, x1, mod, ln2)


def _dispatch_plan(class_of_token, rank, counts):
    n_tok = class_of_token.shape[0]
    counts = counts.astype(jnp.int32)
    pcounts = (counts + BLOCK_M - 1) // BLOCK_M * BLOCK_M
    pends = jnp.cumsum(pcounts)
    pstarts = pends - pcounts
    classes = jnp.arange(ROUTE_W, dtype=jnp.int32)
    dest = jnp.sum(jnp.where(class_of_token[:, None] == classes[None, :], pstarts[None, :], 0), axis=1) + rank
    n_blocks = n_tok // BLOCK_M + N_CLASSES
    blk_start = jnp.arange(n_blocks, dtype=jnp.int32) * BLOCK_M
    blk_class = jnp.minimum(jnp.sum((pends[None, :] <= blk_start[:, None]).astype(jnp.int32), axis=1), N_CLASSES - 1)
    n_used = pends[-1:] // BLOCK_M
    return (dest.astype(jnp.int32), pends, pcounts - counts, jnp.asarray(CLASS_LO)[blk_class],
            jnp.asarray(CLASS_HI)[blk_class], n_used.astype(jnp.int32), n_blocks * BLOCK_M)


def _pad_rows(a, rows):
    return jnp.pad(a, [(0, 0)] * (a.ndim - 2) + [(0, rows - a.shape[-2]), (0, 0)])


def kernel(x, c, ctx, c_ctx, w_ada, b_ada, w_in, conv_w, ret_decay_fwd, ret_decay_bwd, sgu_w, sgu_b, w_out, ln1_g, ln1_b,
           router_group_w, router_group_b, router_expert_w, router_expert_b, moe_w_gate, moe_w_up, moe_w_down, ln2_g,
           ln2_b):
    bsz, seq, d = x.shape
    ctx_len = ctx.shape[1]
    assert d == D_MODEL and ctx_len == TILE and seq % TILE == 0 and seq % GRID_W == 0
    n_lat_tiles = seq // TILE
    n_lat_chunks = seq // CHUNK

    cond = _pad_rows(jnp.concatenate([c, c_ctx[None, :]], axis=0), 16)
    ada = _ada_call(cond, w_ada, b_ada)
    mod_lat = ada[:, :bsz].reshape(DEPTH, bsz, N_MOD, d)
    mod_ctx = jnp.broadcast_to(ada[:, bsz].reshape(DEPTH, 1, N_MOD, d), mod_lat.shape)
    mod = _pad_rows(jnp.stack([mod_lat, mod_ctx], axis=2), MOD_ROWS)

    w_in_bf = w_in.astype(BF16)
    w_out_bf = w_out.astype(BF16)
    sgu_w_bf = sgu_w.astype(BF16)
    w_gate_bf = moe_w_gate.astype(BF16)
    w_up_bf = moe_w_up.astype(BF16)
    w_down_bf = moe_w_down.astype(BF16)
    sgu_bias = jnp.repeat(jnp.swapaxes(sgu_b, 1, 2), SGU_W // SGU_GROUPS, axis=2)
    conv_w8 = _pad_rows(conv_w, 8)
    ln1 = _pad_rows(jnp.stack([ln1_g, ln1_b], axis=1), 8)
    ln2 = _pad_rows(jnp.stack([ln2_g, ln2_b], axis=1), 8)
    w_route = jnp.concatenate([router_group_w, jnp.swapaxes(router_expert_w, 1, 2).reshape(DEPTH, d, N_EXPERTS)], axis=2)
    w_route_bf = jnp.pad(w_route, ((0, 0), (0, 0), (0, ROUTE_W - w_route.shape[2]))).astype(BF16)
    b_route = jnp.concatenate([router_group_b, router_expert_b.reshape(DEPTH, N_EXPERTS)], axis=1)
    b_route = jnp.pad(b_route, ((0, 0), (0, ROUTE_W - b_route.shape[1])))[:, None, :].astype(F32)
    w_route_t = jnp.concatenate([_pad_rows(jnp.swapaxes(w_route[:, :, :N_GROUPS], 1, 2), EXPERT_ROW0),
                                 jnp.swapaxes(w_route[:, :, N_GROUPS:], 1, 2)], axis=1)
    w_route_t = _pad_rows(w_route_t, ROUTE_ROWS).astype(BF16)
    b_route_t = jnp.concatenate([_pad_rows(b_route[:, 0, :N_GROUPS, None], EXPERT_ROW0),
                                 b_route[:, 0, N_GROUPS:N_GROUPS + N_EXPERTS, None]], axis=1)
    b_route_t = _pad_rows(b_route_t, ROUTE_ROWS)
    lg = jnp.stack([jax.nn.log_sigmoid(ret_decay_fwd.astype(F32)), jax.nn.log_sigmoid(ret_decay_bwd.astype(F32))],
                   axis=1)
    lg_lanes = jnp.repeat(lg, RET_DK, axis=2).reshape(DEPTH * 2, RET_W)
    lg_rows = lg_lanes[:, None, :]
    lg_cols = lg_lanes[:, :, None]

    x_lat, x_ctx, ctx_tile = x, ctx, 0
    for layer in range(DEPTH):
        last = layer == DEPTH - 1
        pc, q, k, v, g, su, sv, kv = _inproj_call(layer, x_lat, x_ctx, ctx_tile, mod, w_in_bf, lg_rows, n_lat_tiles)
        states = _scan_call(layer, kv, lg_cols, n_lat_chunks)
        n_proc = n_lat_tiles if last else n_lat_tiles + 1
        x1, hm, route = _mix_call(layer, n_lat_tiles, n_proc, lg[layer, 0], lg[layer, 1], pc, q, k, v, g, su, sv, states,
                                  x_lat, x_ctx, ctx_tile, mod, conv_w8, sgu_w_bf, sgu_bias, w_out_bf, ln1, w_route_t,
                                  b_route_t, lg_rows)
        class_of_token, rank, counts = _rank_call(route.reshape(-1, 8, TILE))
        dest, pad_end, pad_len, blk_lo, blk_hi, n_used, n_slots = _dispatch_plan(class_of_token.reshape(-1),
                                                                                  rank.reshape(-1), counts[:, 0])
        xs = _dispatch_call(dest, pad_end, pad_len, n_used, hm.reshape(-1, PACK_W), n_slots)
        ys = _expert_call(layer, blk_lo, blk_hi, n_used, xs, w_route_bf, b_route, w_gate_bf, w_up_bf, w_down_bf)
        xa = _combine_call(layer, n_lat_tiles, dest, ys, x1, mod, ln2)
        x_lat, x_ctx, ctx_tile = xa, xa, n_lat_tiles
    return xa
```

```python
import functools
import math

import jax
import jax.numpy as jnp
import numpy as np
from jax import lax
from jax.experimental import pallas as pl
from jax.experimental.pallas import tpu as pltpu

F32 = jnp.float32
BF16 = jnp.bfloat16

D_MODEL = 1024
DEPTH = 2
GRID_W = 64
CONV_W = 256
RET_W = 512
RET_HEADS = 8
RET_DK = 64
PAIR_W = 2 * RET_DK
CHUNK = 128
SGU_W = 256
SGU_GROUPS = 4
IN_COLS = 3 * CONV_W + 4 * RET_W + 2 * SGU_W
N_GROUPS = 4
EXPERTS_PER_GROUP = 8
N_EXPERTS = N_GROUPS * EXPERTS_PER_GROUP
EXPERT_HIDDEN = 512
N_MOD = 6
MOD_ROWS = 8
LN_EPS = 1e-5
ALPHA = (2 * DEPTH) ** 0.25

TILE = 256
HALO = GRID_W
BLOCK_M = 256
ROUTE_W = 128
PACK_W = D_MODEL // 2
ROUTE_ROWS = 48
EXPERT_ROW0 = 8
RANK_TILES = 8
PAIRS_PER_GROUP = EXPERTS_PER_GROUP * (EXPERTS_PER_GROUP - 1) // 2
N_CLASSES = N_GROUPS * PAIRS_PER_GROUP
CLASS_LO = np.array([g * EXPERTS_PER_GROUP + lo for g in range(N_GROUPS) for lo in range(EXPERTS_PER_GROUP)
                     for hi in range(lo + 1, EXPERTS_PER_GROUP)] + [N_EXPERTS - 2] * (ROUTE_W - N_CLASSES), np.int32)
CLASS_HI = np.array([g * EXPERTS_PER_GROUP + hi for g in range(N_GROUPS) for lo in range(EXPERTS_PER_GROUP)
                     for hi in range(lo + 1, EXPERTS_PER_GROUP)] + [N_EXPERTS - 1] * (ROUTE_W - N_CLASSES), np.int32)
VMEM_LIMIT = 56 * 1024 * 1024


def _params(n_axes):
    return pltpu.CompilerParams(dimension_semantics=("arbitrary",) * n_axes, vmem_limit_bytes=VMEM_LIMIT)


def _standardize(v):
    mu = jnp.mean(v, axis=-1, keepdims=True)
    var = jnp.mean(jnp.square(v - mu), axis=-1, keepdims=True)
    return (v - mu) * lax.rsqrt(var + LN_EPS)


def _silu(v):
    return v * jax.nn.sigmoid(v)


def _pack_bf16_pairs(v):
    half = v.shape[1] // 2
    bits = lambda t: pltpu.bitcast(t.astype(BF16).astype(F32), jnp.uint32)
    return bits(v[:, :half]) | (bits(v[:, half:]) >> 16)


def _unpack_bf16_pairs(w):
    hi = pltpu.bitcast(w & jnp.uint32(0xFFFF0000), F32)
    lo = pltpu.bitcast(w << 16, F32)
    return jnp.concatenate([hi, lo], axis=1).astype(BF16)


def _ada_kernel(c_ref, w_ref, b_ref, o_ref):
    a = _silu(c_ref[...]).astype(BF16)
    o_ref[0] = jnp.dot(a, w_ref[0].astype(BF16), preferred_element_type=F32) + b_ref[0]


def _ada_call(cond, w_ada, b_ada):
    rows = cond.shape[0]
    cols = w_ada.shape[-1]
    tn = 1536
    return pl.pallas_call(
        _ada_kernel,
        grid=(DEPTH, cols // tn),
        in_specs=[
            pl.BlockSpec((rows, D_MODEL), lambda l, j: (0, 0)),
            pl.BlockSpec((1, D_MODEL, tn), lambda l, j: (l, 0, j)),
            pl.BlockSpec((1, 1, tn), lambda l, j: (l, 0, j)),
        ],
        out_specs=pl.BlockSpec((1, rows, tn), lambda l, j: (l, 0, j)),
        out_shape=jax.ShapeDtypeStruct((DEPTH, rows, cols), F32),
        compiler_params=_params(2),
    )(cond, w_ada, b_ada.reshape(DEPTH, 1, cols))


def _inproj_kernel(n_lat_tiles, x_ref, ctx_ref, mod_ref, w_ref, lgf_ref, lgb_ref, pc_ref, q_ref, k_ref, v_ref, g_ref,
                   su_ref, sv_ref, kv_ref):
    mod = mod_ref[0, 0, 0]
    xin = jnp.where(pl.program_id(1) >= n_lat_tiles, ctx_ref[0], x_ref[0])
    h = (xin * (1.0 + mod[1:2]) + mod[0:1]).astype(BF16)

    def proj(lo, hi):
        return jnp.dot(h, w_ref[0, :, lo:hi], preferred_element_type=F32)

    o = 3 * CONV_W
    pc_ref[0] = proj(0, o)
    q_ref[0] = proj(o, o + RET_W)
    k = proj(o + RET_W, o + 2 * RET_W) * (RET_DK ** -0.5)
    k_ref[0] = k
    v = proj(o + 2 * RET_W, o + 3 * RET_W)
    v_ref[0] = v
    g_ref[0] = proj(o + 3 * RET_W, o + 4 * RET_W)
    o += 4 * RET_W
    su_ref[0] = proj(o, o + SGU_W)
    sv_ref[0] = proj(o + SGU_W, o + 2 * SGU_W)

    pos = lax.broadcasted_iota(jnp.int32, (CHUNK, 1), 0).astype(F32)
    k_decay = (jnp.exp((CHUNK - 1.0 - pos) * lgf_ref[0]), jnp.exp(pos * lgb_ref[0]))
    contract_rows = (((0,), (0,)), ((), ()))
    for c in range(TILE // CHUNK):
        rows = slice(c * CHUNK, (c + 1) * CHUNK)
        vc = v[rows].astype(BF16)
        for d in range(2):
            kc = (k[rows] * k_decay[d]).astype(BF16)
            for p in range(RET_W // PAIR_W):
                cols = slice(p * PAIR_W, (p + 1) * PAIR_W)
                kv_ref[d, 0, c, cols, :] = lax.dot_general(kc[:, cols], vc[:, cols], contract_rows,
                                                           preferred_element_type=F32)


def _inproj_call(layer, x_lat, x_ctx, ctx_tile, mod, w_in_bf, lg_rows, n_lat_tiles):
    bsz = x_lat.shape[0]
    length = (n_lat_tiles + 1) * TILE
    widths = (3 * CONV_W, RET_W, RET_W, RET_W, RET_W, SGU_W, SGU_W)
    cpt = TILE // CHUNK
    return pl.pallas_call(
        functools.partial(_inproj_kernel, n_lat_tiles),
        grid=(bsz, length // TILE),
        in_specs=[
            pl.BlockSpec((1, TILE, D_MODEL), lambda b, i: (b, jnp.minimum(i, n_lat_tiles - 1), 0)),
            pl.BlockSpec((1, TILE, D_MODEL), lambda b, i: (b, ctx_tile, 0)),
            pl.BlockSpec((1, 1, 1, MOD_ROWS, D_MODEL), lambda b, i: (layer, b, jnp.minimum(i // n_lat_tiles, 1), 0, 0)),
            pl.BlockSpec((1, D_MODEL, IN_COLS), lambda b, i: (layer, 0, 0)),
            pl.BlockSpec((1, 1, RET_W), lambda b, i: (2 * layer, 0, 0)),
            pl.BlockSpec((1, 1, RET_W), lambda b, i: (2 * layer + 1, 0, 0)),
        ],
        out_specs=[pl.BlockSpec((1, TILE, w), lambda b, i: (b, i, 0)) for w in widths]
        + [pl.BlockSpec((2, 1, cpt, RET_W, PAIR_W), lambda b, i: (0, b, i, 0, 0))],
        out_shape=[jax.ShapeDtypeStruct((bsz, length, w), F32) for w in widths]
        + [jax.ShapeDtypeStruct((2, bsz, length // CHUNK, RET_W, PAIR_W), F32)],
        compiler_params=_params(2),
    )(x_lat, x_ctx, mod, w_in_bf, lg_rows, lg_rows)


def _scan_kernel(n_lat_chunks, kv_ref, lgc_ref, s_ref, state):
    nc = kv_ref.shape[2]
    backward = pl.program_id(1)
    decay = jnp.exp(CHUNK * lgc_ref[0])
    row_head = lax.broadcasted_iota(jnp.int32, (RET_W, PAIR_W), 0) % PAIR_W // RET_DK
    lane_head = lax.broadcasted_iota(jnp.int32, (RET_W, PAIR_W), 1) // RET_DK
    own_head = row_head == lane_head
    state[...] = jnp.zeros_like(state)

    def step(s, carry):
        c = jnp.where(backward == 1, nc - 1 - s, (s + n_lat_chunks) % nc)
        s_ref[0, 0, c] = jnp.where(own_head, state[...], 0.0).astype(BF16)
        state[...] = decay * state[...] + kv_ref[0, 0, c]
        return carry

    lax.fori_loop(0, nc, step, 0)


def _scan_call(layer, kv, lg_cols, n_lat_chunks):
    _, bsz, nc, _, _ = kv.shape
    blk = (1, 1, nc, RET_W, PAIR_W)
    return pl.pallas_call(
        functools.partial(_scan_kernel, n_lat_chunks),
        grid=(bsz, 2),
        in_specs=[pl.BlockSpec(blk, lambda b, d: (d, b, 0, 0, 0)),
                  pl.BlockSpec((1, RET_W, 1), lambda b, d: (2 * layer + d, 0, 0))],
        out_specs=pl.BlockSpec(blk, lambda b, d: (d, b, 0, 0, 0)),
        out_shape=jax.ShapeDtypeStruct(kv.shape, BF16),
        scratch_shapes=[pltpu.VMEM((RET_W, PAIR_W), F32)],
        compiler_params=_params(2),
    )(kv, lg_cols)


def _route_class(lt):
    assert EXPERTS_PER_GROUP == 8 and N_GROUPS <= 8
    sub = lax.broadcasted_iota(jnp.int32, (8, lt.shape[1]), 0)
    neg = jnp.float32(-jnp.inf)

    def top(vals):
        m = jnp.max(vals, axis=0, keepdims=True)
        return jnp.min(jnp.where(vals == m, sub, 8), axis=0, keepdims=True)

    gidx = top(jnp.where(sub < N_GROUPS, lt[0:8], neg))
    pair = jnp.zeros_like(gidx)
    for g in range(N_GROUPS):
        e = lt[EXPERT_ROW0 + 8 * g:EXPERT_ROW0 + 8 * (g + 1)]
        i1 = top(e)
        i2 = top(jnp.where(sub == i1, neg, e))
        e_lo = jnp.minimum(i1, i2)
        e_hi = jnp.maximum(i1, i2)
        pair_g = e_lo * (EXPERTS_PER_GROUP - 1) - ((e_lo * (e_lo - 1)) >> 1) + (e_hi - e_lo - 1)
        pair = jnp.where(gidx == g, pair_g, pair)
    return gidx * PAIRS_PER_GROUP + pair


def _mix_kernel(n_lat_tiles, lgf_ref, lgb_ref, pc_ref, hp_ref, hn_ref, q_ref, k_ref, v_ref, g_ref, su_ref, sv_ref,
                sf_ref, sb_ref, x_ref, ctx_ref, mod_ref, convw_ref, sguw_ref, sgub_ref, wout_ref, ln_ref, wr_ref, br_ref,
                lgfr_ref, lgbr_ref, x1_ref, hm_ref, route_ref, ycat, dec, qdec):
    i = pl.program_id(1)
    is_ctx = i >= n_lat_tiles
    row = lax.broadcasted_iota(jnp.int32, (TILE, 1), 0)

    @pl.when((pl.program_id(0) == 0) & (i == 0))
    def _():
        pos = lax.broadcasted_iota(jnp.int32, (CHUNK, 1), 0).astype(F32)
        rel = pos - lax.broadcasted_iota(jnp.int32, (1, CHUNK), 1).astype(F32)
        for h in range(RET_HEADS):
            dec[h // 2, (h % 2) * CHUNK:(h % 2 + 1) * CHUNK, :] = jnp.where(
                rel > 0, jnp.exp(lgf_ref[h] * jnp.maximum(rel, 0.0)),
                jnp.where(rel < 0, jnp.exp(lgb_ref[h] * jnp.maximum(-rel, 0.0)), 2.0))
        qdec[0] = jnp.exp(lgfr_ref[0] * (pos + 1.0))
        qdec[1] = jnp.exp(lgbr_ref[0] * (CHUNK - pos))

    low_head = lax.broadcasted_iota(jnp.int32, (1, PAIR_W), 1) < RET_DK

    def half_norm(o):
        def half_mean(t):
            lo = jnp.sum(jnp.where(low_head, t, 0.0), axis=-1, keepdims=True)
            hi = jnp.sum(jnp.where(low_head, 0.0, t), axis=-1, keepdims=True)
            return jnp.where(low_head, lo, hi) * (1.0 / RET_DK)

        centred = o - half_mean(o)
        return centred * lax.rsqrt(half_mean(jnp.square(centred)) + LN_EPS)

    tiles = [(slice(c * CHUNK, (c + 1) * CHUNK), slice(p * PAIR_W, (p + 1) * PAIR_W), c, p)
             for c in range(TILE // CHUNK) for p in range(RET_W // PAIR_W)]
    scores = []
    for rows, cols, c, p in tiles:
        qp = q_ref[0, rows, cols]
        q_split = jnp.concatenate([jnp.where(low_head, qp, 0.0), jnp.where(low_head, 0.0, qp)], axis=0)
        scores.append(lax.dot_general(q_split.astype(BF16), k_ref[0, rows, cols].astype(BF16),
                                      (((1,), (1,)), ((), ())), preferred_element_type=F32))

    pc = pc_ref[0]
    z = pc[:, CONV_W:2 * CONV_W] * pc[:, 2 * CONV_W:3 * CONV_W]
    line_mask = jnp.where(is_ctx, TILE - 1, GRID_W - 1)
    first = (row & line_mask) == 0
    last = (row & line_mask) == line_mask
    z_prev = jnp.where(first, 0.0, pltpu.roll(z, 1, 0))
    z_next = jnp.where(last, 0.0, pltpu.roll(z, TILE - 1, 0))
    hp = hp_ref[0]
    hn = hn_ref[0]
    z_top = jnp.where(i == 0, 0.0, hp[:, CONV_W:2 * CONV_W] * hp[:, 2 * CONV_W:3 * CONV_W])
    z_bot = jnp.where(i == n_lat_tiles - 1, 0.0, hn[:, CONV_W:2 * CONV_W] * hn[:, 2 * CONV_W:3 * CONV_W])
    zcat = jnp.concatenate([z_top, z, z_bot], axis=0)
    z_up = zcat[0:TILE]
    z_down = zcat[2 * HALO:2 * HALO + TILE]
    along_seq = lax.broadcasted_iota(jnp.int32, (1, CONV_W), 1) < jnp.where(is_ctx, CONV_W, CONV_W // 2)
    cw = convw_ref[0]
    conv = cw[0:1] * jnp.where(along_seq, z_prev, z_up) + cw[1:2] * z
    conv = conv + cw[2:3] * jnp.where(along_seq, z_next, z_down)
    ycat[:, 0:CONV_W] = (pc[:, 0:CONV_W] * conv).astype(BF16)

    outs = []
    for (rows, cols, c, p), sc in zip(tiles, scores):
        qp = q_ref[0, rows, cols]
        vp = v_ref[0, rows, cols]
        sc = sc * dec[p]
        lhs = jnp.concatenate([sc[0:CHUNK].astype(BF16), sc[CHUNK:].astype(BF16),
                               (qp * qdec[0, :, cols]).astype(BF16), (qp * qdec[1, :, cols]).astype(BF16)], axis=1)
        rhs = jnp.concatenate([jnp.where(low_head, vp, 0.0).astype(BF16), jnp.where(low_head, 0.0, vp).astype(BF16),
                               sf_ref[0, 0, c, cols, :], sb_ref[0, 0, c, cols, :]], axis=0)
        outs.append(jnp.dot(lhs, rhs, preferred_element_type=F32))

    vn = _standardize(sv_ref[0]).astype(BF16)
    group = lax.broadcasted_iota(jnp.int32, (1, SGU_W), 1) // (SGU_W // SGU_GROUPS)
    for c in range(TILE // CHUNK):
        rows = slice(c * CHUNK, (c + 1) * CHUNK)
        mixed = jnp.zeros((CHUNK, SGU_W), F32)
        for gi in range(SGU_GROUPS):
            m = jnp.dot(sguw_ref[0, gi], vn[rows], preferred_element_type=F32)
            mixed = jnp.where(group == gi, m, mixed)
        ycat[rows, CONV_W + RET_W:] = (su_ref[0, rows, :] * (mixed + sgub_ref[0])).astype(BF16)

    for (rows, cols, c, p), o in zip(tiles, outs):
        ycat[rows, CONV_W + p * PAIR_W:CONV_W + (p + 1) * PAIR_W] = (
            _silu(g_ref[0, rows, cols]) * half_norm(o)).astype(BF16)

    y = jnp.dot(ycat[...], wout_ref[0], preferred_element_type=F32)
    mod = mod_ref[0, 0, 0]
    ln = ln_ref[0]
    x1 = _standardize(ALPHA * jnp.where(is_ctx, ctx_ref[0], x_ref[0]) + mod[2:3] * y) * ln[0:1] + ln[1:2]
    x1_ref[0] = x1
    hm = x1 * (1.0 + mod[4:5]) + mod[3:4]
    hm_ref[0] = _pack_bf16_pairs(hm)
    lt = lax.dot_general(wr_ref[0], hm.astype(BF16), (((1,), (1,)), ((), ())), preferred_element_type=F32) + br_ref[0]
    route_ref[0, 0] = jnp.concatenate([_route_class(lt).astype(F32), jnp.zeros((7, TILE), F32)], axis=0)


def _mix_call(layer, n_lat_tiles, n_proc, lgf, lgb, pc, q, k, v, g, su, sv, states, x_lat, x_ctx, ctx_tile, mod, conv_w,
              sgu_w_bf, sgu_bias, w_out_bf, ln1, w_route_t, b_route_t, lg_rows):
    bsz, length, _ = q.shape
    halos_per_tile = TILE // HALO
    n_halo = length // HALO
    cpt = TILE // CHUNK
    tok = lambda w: pl.BlockSpec((1, TILE, w), lambda b, i, *_: (b, i, 0))
    per_layer = lambda *shape: pl.BlockSpec((1,) + shape, lambda b, i, *_: (layer,) + (0,) * len(shape))
    grid_spec = pltpu.PrefetchScalarGridSpec(
        num_scalar_prefetch=2,
        grid=(bsz, n_proc),
        in_specs=[
            tok(3 * CONV_W),
            pl.BlockSpec((1, HALO, 3 * CONV_W), lambda b, i, *_: (b, jnp.maximum(i * halos_per_tile - 1, 0), 0)),
            pl.BlockSpec((1, HALO, 3 * CONV_W),
                         lambda b, i, *_: (b, jnp.minimum((i + 1) * halos_per_tile, n_halo - 1), 0)),
            tok(RET_W), tok(RET_W), tok(RET_W), tok(RET_W), tok(SGU_W), tok(SGU_W),
            pl.BlockSpec((1, 1, cpt, RET_W, PAIR_W), lambda b, i, *_: (0, b, i, 0, 0)),
            pl.BlockSpec((1, 1, cpt, RET_W, PAIR_W), lambda b, i, *_: (1, b, i, 0, 0)),
            pl.BlockSpec((1, TILE, D_MODEL), lambda b, i, *_: (b, jnp.minimum(i, n_lat_tiles - 1), 0)),
            pl.BlockSpec((1, TILE, D_MODEL), lambda b, i, *_: (b, ctx_tile, 0)),
            pl.BlockSpec((1, 1, 1, MOD_ROWS, D_MODEL),
                         lambda b, i, *_: (layer, b, jnp.minimum(i // n_lat_tiles, 1), 0, 0)),
            per_layer(8, CONV_W),
            per_layer(SGU_GROUPS, CHUNK, CHUNK),
            per_layer(CHUNK, SGU_W),
            per_layer(D_MODEL, D_MODEL),
            per_layer(8, D_MODEL),
            per_layer(ROUTE_ROWS, D_MODEL),
            per_layer(ROUTE_ROWS, 1),
            pl.BlockSpec((1, 1, RET_W), lambda b, i, *_: (2 * layer, 0, 0)),
            pl.BlockSpec((1, 1, RET_W), lambda b, i, *_: (2 * layer + 1, 0, 0)),
        ],
        out_specs=[tok(D_MODEL), tok(PACK_W), pl.BlockSpec((1, 1, 8, TILE), lambda b, i, *_: (b, i, 0, 0))],
        scratch_shapes=[pltpu.VMEM((TILE, D_MODEL), BF16),
                        pltpu.VMEM((RET_W // PAIR_W, 2 * CHUNK, CHUNK), F32),
                        pltpu.VMEM((2, CHUNK, RET_W), F32)],
    )
    return pl.pallas_call(
        functools.partial(_mix_kernel, n_lat_tiles),
        grid_spec=grid_spec,
        out_shape=[jax.ShapeDtypeStruct((bsz, n_proc * TILE, D_MODEL), F32),
                   jax.ShapeDtypeStruct((bsz, n_proc * TILE, PACK_W), jnp.uint32),
                   jax.ShapeDtypeStruct((bsz, n_proc, 8, TILE), F32)],
        compiler_params=_params(2),
    )(lgf, lgb, pc, pc, pc, q, k, v, g, su, sv, states, states, x_lat, x_ctx, mod, conv_w, sgu_w_bf, sgu_bias, w_out_bf,
      ln1, w_route_t, b_route_t, lg_rows, lg_rows)


def _rank_kernel(route_ref, class_ref, rank_ref, counts_ref, running):
    @pl.when(pl.program_id(0) == 0)
    def _():
        running[...] = jnp.zeros_like(running)

    sub = lax.broadcasted_iota(jnp.int32, (ROUTE_W, TILE), 0)
    earlier = (lax.broadcasted_iota(jnp.int32, (TILE, TILE), 0)
               < lax.broadcasted_iota(jnp.int32, (TILE, TILE), 1)).astype(BF16)
    for t in range(route_ref.shape[0]):
        cls = route_ref[t, 0:1, :].astype(jnp.int32)
        class_ref[t] = cls
        onehot = sub == cls
        before = jnp.dot(onehot.astype(BF16), earlier, preferred_element_type=F32) + running[...]
        rank_ref[t] = jnp.sum(jnp.where(onehot, before, 0.0), axis=0, keepdims=True).astype(jnp.int32)
        running[...] += jnp.sum(onehot.astype(F32), axis=1, keepdims=True)
    counts_ref[...] = jnp.broadcast_to(running[...], counts_ref.shape)


def _rank_call(route):
    n_tiles = route.shape[0]
    per_step = math.gcd(n_tiles, RANK_TILES)
    per_tile = pl.BlockSpec((per_step, 1, TILE), lambda i: (i, 0, 0))
    return pl.pallas_call(
        _rank_kernel,
        grid=(n_tiles // per_step,),
        in_specs=[pl.BlockSpec((per_step, 8, TILE), lambda i: (i, 0, 0))],
        out_specs=[per_tile, per_tile, pl.BlockSpec((ROUTE_W, ROUTE_W), lambda i: (0, 0))],
        out_shape=[jax.ShapeDtypeStruct((n_tiles, 1, TILE), jnp.int32), jax.ShapeDtypeStruct((n_tiles, 1, TILE), jnp.int32),
                   jax.ShapeDtypeStruct((ROUTE_W, ROUTE_W), F32)],
        scratch_shapes=[pltpu.VMEM((ROUTE_W, 1), F32)],
        compiler_params=_params(1),
    )(route)


def _dispatch_kernel(dest_ref, pad_end_ref, pad_len_ref, n_used_ref, hm_ref, xs_ref, zeros, sem, pad_sem):
    step = pl.program_id(0)
    base = step * TILE
    half = BLOCK_M // 2

    def for_each_pad_copy(fn):
        def per_class(c, carry):
            off = pad_end_ref[c]
            n = pad_len_ref[c]
            for shift in range(BLOCK_M.bit_length() - 2, -1, -1):
                bit = 1 << shift
                off = off - (n & bit)

                @pl.when((n & bit) != 0)
                def _():
                    if bit >= 8:
                        fn(pltpu.make_async_copy(zeros.at[pl.ds(0, bit)], xs_ref.at[pl.ds(pl.multiple_of(off, 8), bit)],
                                                 pad_sem))
                    else:
                        for j in range(bit):
                            fn(pltpu.make_async_copy(zeros.at[pl.ds(0, 1)], xs_ref.at[pl.ds(off + j, 1)], pad_sem))

            return carry

        lax.fori_loop(0, N_CLASSES, per_class, 0)

        def per_half_block(j, carry):
            fn(pltpu.make_async_copy(zeros, xs_ref.at[pl.ds(pl.multiple_of(j * half, 8), half)], pad_sem))
            return carry

        lax.fori_loop(n_used_ref[0] * 2, xs_ref.shape[0] // half, per_half_block, 0)

    @pl.when(step == 0)
    def _():
        zeros[...] = jnp.zeros_like(zeros)
        for_each_pad_copy(lambda cp: cp.start())

    for r in range(TILE):
        pltpu.make_async_copy(hm_ref.at[pl.ds(r, 1)], xs_ref.at[pl.ds(dest_ref[base + r], 1)],
                              sem).start(priority=r % 2)
    for r in range(TILE):
        pltpu.make_async_copy(hm_ref.at[pl.ds(0, 1)], xs_ref.at[pl.ds(0, 1)], sem).wait()

    @pl.when(step == pl.num_programs(0) - 1)
    def _():
        for_each_pad_copy(lambda cp: cp.wait())


def _dispatch_call(dest, pad_end, pad_len, n_used, hm_flat, n_slots):
    n_tok = hm_flat.shape[0]
    grid_spec = pltpu.PrefetchScalarGridSpec(
        num_scalar_prefetch=4,
        grid=(n_tok // TILE,),
        in_specs=[pl.BlockSpec((TILE, PACK_W), lambda i, *_: (i, 0))],
        out_specs=pl.BlockSpec(memory_space=pl.ANY),
        scratch_shapes=[pltpu.VMEM((BLOCK_M // 2, PACK_W), jnp.uint32), pltpu.SemaphoreType.DMA,
                        pltpu.SemaphoreType.DMA],
    )
    return pl.pallas_call(
        _dispatch_kernel,
        grid_spec=grid_spec,
        out_shape=jax.ShapeDtypeStruct((n_slots, PACK_W), jnp.uint32),
        compiler_params=_params(1),
    )(dest, pad_end, pad_len, n_used, hm_flat)


def _expert_kernel(blk_lo_ref, blk_hi_ref, n_used_ref, xs_ref, wr_ref, br_ref, wg_lo, wu_lo, wd_lo, wg_hi, wu_hi, wd_hi,
                   ys_ref):
    i = pl.program_id(0)
    used = i < n_used_ref[0]

    @pl.when(used)
    def _():
        xb = _unpack_bf16_pairs(xs_ref[...])

        logits = jnp.dot(xb, wr_ref[0], preferred_element_type=F32) + br_ref[0]
        lane = lax.broadcasted_iota(jnp.int32, logits.shape, 1)
        gl = jnp.where(lane < N_GROUPS, logits, -jnp.inf)
        g_prob = 1.0 / jnp.sum(jnp.exp(gl - jnp.max(gl, axis=-1, keepdims=True)), axis=-1, keepdims=True)
        l_lo = jnp.sum(jnp.where(lane == N_GROUPS + blk_lo_ref[i], logits, 0.0), axis=-1, keepdims=True)
        l_hi = jnp.sum(jnp.where(lane == N_GROUPS + blk_hi_ref[i], logits, 0.0), axis=-1, keepdims=True)
        m = jnp.maximum(l_lo, l_hi)
        p_lo = jnp.exp(l_lo - m)
        p_hi = jnp.exp(l_hi - m)

        h_lo = jnp.dot(xb, wg_lo[0, 0], preferred_element_type=F32)
        u_lo = jnp.dot(xb, wu_lo[0, 0], preferred_element_type=F32)
        h_hi = jnp.dot(xb, wg_hi[0, 0], preferred_element_type=F32)
        u_hi = jnp.dot(xb, wu_hi[0, 0], preferred_element_type=F32)
        a_lo = (_silu(h_lo) * u_lo).astype(BF16)
        a_hi = (_silu(h_hi) * u_hi).astype(BF16)
        y_lo = jnp.dot(a_lo, wd_lo[0, 0], preferred_element_type=F32)
        y_hi = jnp.dot(a_hi, wd_hi[0, 0], preferred_element_type=F32)
        ys_ref[...] = y_lo * (g_prob * (p_lo / (p_lo + p_hi))) + y_hi * (g_prob * (p_hi / (p_lo + p_hi)))

    @pl.when(jnp.logical_not(used))
    def _():
        ys_ref[...] = jnp.zeros_like(ys_ref)


def _expert_call(layer, blk_lo, blk_hi, n_used, xs, w_route_bf, b_route, w_gate_bf, w_up_bf, w_down_bf):
    n_blocks = xs.shape[0] // BLOCK_M
    up_spec = lambda which: pl.BlockSpec((1, 1, D_MODEL, EXPERT_HIDDEN),
                                         lambda i, lo, hi, nu: (layer, (lo, hi)[which][i], 0, 0))
    down_spec = lambda which: pl.BlockSpec((1, 1, EXPERT_HIDDEN, D_MODEL),
                                           lambda i, lo, hi, nu: (layer, (lo, hi)[which][i], 0, 0))
    grid_spec = pltpu.PrefetchScalarGridSpec(
        num_scalar_prefetch=3,
        grid=(n_blocks,),
        in_specs=[pl.BlockSpec((BLOCK_M, PACK_W), lambda i, lo, hi, nu: (jnp.minimum(i, nu[0] - 1), 0)),
                  pl.BlockSpec((1, D_MODEL, ROUTE_W), lambda i, *_: (layer, 0, 0)),
                  pl.BlockSpec((1, 1, ROUTE_W), lambda i, *_: (layer, 0, 0)),
                  up_spec(0), up_spec(0), down_spec(0), up_spec(1), up_spec(1), down_spec(1)],
        out_specs=pl.BlockSpec((BLOCK_M, D_MODEL), lambda i, *_: (i, 0)),
    )
    return pl.pallas_call(
        _expert_kernel,
        grid_spec=grid_spec,
        out_shape=jax.ShapeDtypeStruct((xs.shape[0], D_MODEL), F32),
        compiler_params=_params(1),
    )(blk_lo, blk_hi, n_used, xs, w_route_bf, b_route, w_gate_bf, w_up_bf, w_down_bf, w_gate_bf, w_up_bf, w_down_bf)


def _combine_kernel(tiles_per_batch, dest_ref, ys_ref, x1_ref, mod_ref, ln_ref, out_ref, buf, sem):
    step = pl.program_id(0) * tiles_per_batch + pl.program_id(1)
    slot = step % 2
    has_next = step + 1 < pl.num_programs(0) * tiles_per_batch

    def gather(tile, to_slot):
        base = tile * TILE

        for r in range(TILE):
            pltpu.make_async_copy(ys_ref.at[pl.ds(dest_ref[base + r], 1)], buf.at[to_slot, pl.ds(r, 1)],
                                  sem.at[to_slot]).start(priority=r % 2)

    @pl.when(step == 0)
    def _():
        gather(step, slot)

    @pl.when(has_next)
    def _():
        gather(step + 1, 1 - slot)

    for r in range(TILE):
        pltpu.make_async_copy(ys_ref.at[pl.ds(0, 1)], buf.at[slot, pl.ds(0, 1)], sem.at[slot]).wait()

    mod = mod_ref[0, 0, 0]
    ln = ln_ref[0]
    out_ref[0] = _standardize(ALPHA * x1_ref[0] + mod[5:6] * buf[slot]) * ln[0:1] + ln[1:2]


def _combine_call(layer, n_lat_tiles, dest, ys, x1, mod, ln2):
    bsz, length, _ = x1.shape
    tiles_per_batch = length // TILE
    tok = lambda w: pl.BlockSpec((1, TILE, w), lambda b, i, *_: (b, i, 0))
    grid_spec = pltpu.PrefetchScalarGridSpec(
        num_scalar_prefetch=1,
        grid=(bsz, tiles_per_batch),
        in_specs=[
            pl.BlockSpec(memory_space=pl.ANY),
            tok(D_MODEL),
            pl.BlockSpec((1, 1, 1, MOD_ROWS, D_MODEL),
                         lambda b, i, *_: (layer, b, jnp.minimum(i // n_lat_tiles, 1), 0, 0)),
            pl.BlockSpec((1, 8, D_MODEL), lambda b, i, *_: (layer, 0, 0)),
        ],
        out_specs=tok(D_MODEL),
        scratch_shapes=[pltpu.VMEM((2, TILE, D_MODEL), F32), pltpu.SemaphoreType.DMA((2,))],
    )
    return pl.pallas_call(
        functools.partial(_combine_kernel, tiles_per_batch),
        grid_spec=grid_spec,
        out_shape=jax.ShapeDtypeStruct(x1.shape, F32),
        compiler_params=_params(2),
    )(dest, ys, x1, mod, ln2)


def _dispatch_plan(class_of_token, rank, counts):
    n_tok = class_of_token.shape[0]
    counts = counts.astype(jnp.int32)
    pcounts = (counts + BLOCK_M - 1) // BLOCK_M * BLOCK_M
    pends = jnp.cumsum(pcounts)
    pstarts = pends - pcounts
    classes = jnp.arange(ROUTE_W, dtype=jnp.int32)
    dest = jnp.sum(jnp.where(class_of_token[:, None] == classes[None, :], pstarts[None, :], 0), axis=1) + rank
    n_blocks = n_tok // BLOCK_M + N_CLASSES
    blk_start = jnp.arange(n_blocks, dtype=jnp.int32) * BLOCK_M
    blk_class = jnp.minimum(jnp.sum((pends[None, :] <= blk_start[:, None]).astype(jnp.int32), axis=1), N_CLASSES - 1)
    n_used = pends[-1:] // BLOCK_M
    return (dest.astype(jnp.int32), pends, pcounts - counts, jnp.asarray(CLASS_LO)[blk_class],
            jnp.asarray(CLASS_HI)[blk_class], n_used.astype(jnp.int32), n_blocks * BLOCK_M)


def _pad_rows(a, rows):
    return jnp.pad(a, [(0, 0)] * (a.ndim - 2) + [(0, rows - a.shape[-2]), (0, 0)])


def kernel(x, c, ctx, c_ctx, w_ada, b_ada, w_in, conv_w, ret_decay_fwd, ret_decay_bwd, sgu_w, sgu_b, w_out, ln1_g, ln1_b,
           router_group_w, router_group_b, router_expert_w, router_expert_b, moe_w_gate, moe_w_up, moe_w_down, ln2_g,
           ln2_b):
    bsz, seq, d = x.shape
    ctx_len = ctx.shape[1]
    assert d == D_MODEL and ctx_len == TILE and seq % TILE == 0 and seq % GRID_W == 0
    n_lat_tiles = seq // TILE
    n_lat_chunks = seq // CHUNK

    cond = _pad_rows(jnp.concatenate([c, c_ctx[None, :]], axis=0), 16)
    ada = _ada_call(cond, w_ada, b_ada)
    mod_lat = ada[:, :bsz].reshape(DEPTH, bsz, N_MOD, d)
    mod_ctx = jnp.broadcast_to(ada[:, bsz].reshape(DEPTH, 1, N_MOD, d), mod_lat.shape)
    mod = _pad_rows(jnp.stack([mod_lat, mod_ctx], axis=2), MOD_ROWS)

    w_in_bf = w_in.astype(BF16)
    w_out_bf = w_out.astype(BF16)
    sgu_w_bf = sgu_w.astype(BF16)
    w_gate_bf = moe_w_gate.astype(BF16)
    w_up_bf = moe_w_up.astype(BF16)
    w_down_bf = moe_w_down.astype(BF16)
    sgu_bias = jnp.repeat(jnp.swapaxes(sgu_b, 1, 2), SGU_W // SGU_GROUPS, axis=2)
    conv_w8 = _pad_rows(conv_w, 8)
    ln1 = _pad_rows(jnp.stack([ln1_g, ln1_b], axis=1), 8)
    ln2 = _pad_rows(jnp.stack([ln2_g, ln2_b], axis=1), 8)
    w_route = jnp.concatenate([router_group_w, jnp.swapaxes(router_expert_w, 1, 2).reshape(DEPTH, d, N_EXPERTS)], axis=2)
    w_route_bf = jnp.pad(w_route, ((0, 0), (0, 0), (0, ROUTE_W - w_route.shape[2]))).astype(BF16)
    b_route = jnp.concatenate([router_group_b, router_expert_b.reshape(DEPTH, N_EXPERTS)], axis=1)
    b_route = jnp.pad(b_route, ((0, 0), (0, ROUTE_W - b_route.shape[1])))[:, None, :].astype(F32)
    w_route_t = jnp.concatenate([_pad_rows(jnp.swapaxes(w_route[:, :, :N_GROUPS], 1, 2), EXPERT_ROW0),
                                 jnp.swapaxes(w_route[:, :, N_GROUPS:], 1, 2)], axis=1)
    w_route_t = _pad_rows(w_route_t, ROUTE_ROWS).astype(BF16)
    b_route_t = jnp.concatenate([_pad_rows(b_route[:, 0, :N_GROUPS, None], EXPERT_ROW0),
                                 b_route[:, 0, N_GROUPS:N_GROUPS + N_EXPERTS, None]], axis=1)
    b_route_t = _pad_rows(b_route_t, ROUTE_ROWS)
    lg = jnp.stack([jax.nn.log_sigmoid(ret_decay_fwd.astype(F32)), jax.nn.log_sigmoid(ret_decay_bwd.astype(F32))],
                   axis=1)
    lg_lanes = jnp.repeat(lg, RET_DK, axis=2).reshape(DEPTH * 2, RET_W)
    lg_rows = lg_lanes[:, None, :]
    lg_cols = lg_lanes[:, :, None]

    x_lat, x_ctx, ctx_tile = x, ctx, 0
    for layer in range(DEPTH):
        last = layer == DEPTH - 1
        pc, q, k, v, g, su, sv, kv = _inproj_call(layer, x_lat, x_ctx, ctx_tile, mod, w_in_bf, lg_rows, n_lat_tiles)
        states = _scan_call(layer, kv, lg_cols, n_lat_chunks)
        n_proc = n_lat_tiles if last else n_lat_tiles + 1
        x1, hm, route = _mix_call(layer, n_lat_tiles, n_proc, lg[layer, 0], lg[layer, 1], pc, q, k, v, g, su, sv, states,
                                  x_lat, x_ctx, ctx_tile, mod, conv_w8, sgu_w_bf, sgu_bias, w_out_bf, ln1, w_route_t,
                                  b_route_t, lg_rows)
        class_of_token, rank, counts = _rank_call(route.reshape(-1, 8, TILE))
        dest, pad_end, pad_len, blk_lo, blk_hi, n_used, n_slots = _dispatch_plan(class_of_token.reshape(-1),
                                                                                  rank.reshape(-1), counts[:, 0])
        xs = _dispatch_call(dest, pad_end, pad_len, n_used, hm.reshape(-1, PACK_W), n_slots)
        ys = _expert_call(layer, blk_lo, blk_hi, n_used, xs, w_route_bf, b_route, w_gate_bf, w_up_bf, w_down_bf)
        xa = _combine_call(layer, n_lat_tiles, dest, ys, x1, mod, ln2)
        x_lat, x_ctx, ctx_tile = xa, xa, n_lat_tiles
    return xa
```

```python
import functools
import math

import jax
import jax.numpy as jnp
import numpy as np
from jax import lax
from jax.experimental import pallas as pl
from jax.experimental.pallas import tpu as pltpu

F32 = jnp.float32
BF16 = jnp.bfloat16

D_MODEL = 1024
DEPTH = 2
GRID_W = 64
CONV_W = 256
RET_W = 512
RET_HEADS = 8
RET_DK = 64
PAIR_W = 2 * RET_DK
CHUNK = 128
SGU_W = 256
SGU_GROUPS = 4
IN_COLS = 3 * CONV_W + 4 * RET_W + 2 * SGU_W
N_GROUPS = 4
EXPERTS_PER_GROUP = 8
N_EXPERTS = N_GROUPS * EXPERTS_PER_GROUP
EXPERT_HIDDEN = 512
N_MOD = 6
MOD_ROWS = 8
LN_EPS = 1e-5
ALPHA = (2 * DEPTH) ** 0.25

TILE = 256
HALO = GRID_W
BLOCK_M = 256
ROUTE_W = 128
CAST_STEPS = 128
PACK_W = D_MODEL // 2
ROUTE_ROWS = 48
EXPERT_ROW0 = 8
RANK_TILES = 8
PAIRS_PER_GROUP = EXPERTS_PER_GROUP * (EXPERTS_PER_GROUP - 1) // 2
N_CLASSES = N_GROUPS * PAIRS_PER_GROUP
CLASS_LO = np.array([g * EXPERTS_PER_GROUP + lo for g in range(N_GROUPS) for lo in range(EXPERTS_PER_GROUP)
                     for hi in range(lo + 1, EXPERTS_PER_GROUP)] + [N_EXPERTS - 2] * (ROUTE_W - N_CLASSES), np.int32)
CLASS_HI = np.array([g * EXPERTS_PER_GROUP + hi for g in range(N_GROUPS) for lo in range(EXPERTS_PER_GROUP)
                     for hi in range(lo + 1, EXPERTS_PER_GROUP)] + [N_EXPERTS - 1] * (ROUTE_W - N_CLASSES), np.int32)
VMEM_LIMIT = 56 * 1024 * 1024


def _params(n_axes):
    return pltpu.CompilerParams(dimension_semantics=("arbitrary",) * n_axes, vmem_limit_bytes=VMEM_LIMIT)


def _standardize(v):
    mu = jnp.mean(v, axis=-1, keepdims=True)
    var = jnp.mean(jnp.square(v - mu), axis=-1, keepdims=True)
    return (v - mu) * lax.rsqrt(var + LN_EPS)


def _silu(v):
    return v * jax.nn.sigmoid(v)


def _pack_bf16_pairs(v):
    half = v.shape[1] // 2
    bits = lambda t: pltpu.bitcast(t.astype(BF16).astype(F32), jnp.uint32)
    return bits(v[:, :half]) | (bits(v[:, half:]) >> 16)


def _unpack_bf16_pairs(w):
    hi = pltpu.bitcast(w & jnp.uint32(0xFFFF0000), F32)
    lo = pltpu.bitcast(w << 16, F32)
    return jnp.concatenate([hi, lo], axis=1).astype(BF16)


def _ada_kernel(c_ref, w_ref, b_ref, o_ref):
    a = _silu(c_ref[...]).astype(BF16)
    o_ref[0] = jnp.dot(a, w_ref[0].astype(BF16), preferred_element_type=F32) + b_ref[0]


def _ada_call(cond, w_ada, b_ada):
    rows = cond.shape[0]
    cols = w_ada.shape[-1]
    tn = 1536
    return pl.pallas_call(
        _ada_kernel,
        grid=(DEPTH, cols // tn),
        in_specs=[
            pl.BlockSpec((rows, D_MODEL), lambda l, j: (0, 0)),
            pl.BlockSpec((1, D_MODEL, tn), lambda l, j: (l, 0, j)),
            pl.BlockSpec((1, 1, tn), lambda l, j: (l, 0, j)),
        ],
        out_specs=pl.BlockSpec((1, rows, tn), lambda l, j: (l, 0, j)),
        out_shape=jax.ShapeDtypeStruct((DEPTH, rows, cols), F32),
        compiler_params=_params(2),
    )(cond, w_ada, b_ada.reshape(DEPTH, 1, cols))


def _inproj_kernel(n_lat_tiles, n_cast, x_ref, ctx_ref, mod_ref, w_ref, lgf_ref, lgb_ref, *refs):
    cast_in, (pc_ref, q_ref, k_ref, v_ref, g_ref, su_ref, sv_ref, kv_ref), cast_out = (
        refs[:n_cast], refs[n_cast:n_cast + 8], refs[n_cast + 8:])
    for src, dst in zip(cast_in, cast_out):
        dst[...] = src[...].astype(BF16)
    mod = mod_ref[0, 0, 0]
    xin = jnp.where(pl.program_id(1) >= n_lat_tiles, ctx_ref[0], x_ref[0])
    h = (xin * (1.0 + mod[1:2]) + mod[0:1]).astype(BF16)

    def proj(lo, hi):
        return jnp.dot(h, w_ref[0, :, lo:hi], preferred_element_type=F32)

    o = 3 * CONV_W
    pc_ref[0] = proj(0, o)
    q_ref[0] = proj(o, o + RET_W)
    k = proj(o + RET_W, o + 2 * RET_W) * (RET_DK ** -0.5)
    k_ref[0] = k
    v = proj(o + 2 * RET_W, o + 3 * RET_W)
    v_ref[0] = v
    g_ref[0] = proj(o + 3 * RET_W, o + 4 * RET_W)
    o += 4 * RET_W
    su_ref[0] = proj(o, o + SGU_W)
    sv_ref[0] = proj(o + SGU_W, o + 2 * SGU_W)

    pos = lax.broadcasted_iota(jnp.int32, (CHUNK, 1), 0).astype(F32)
    k_decay = (jnp.exp((CHUNK - 1.0 - pos) * lgf_ref[0]), jnp.exp(pos * lgb_ref[0]))
    contract_rows = (((0,), (0,)), ((), ()))
    for c in range(TILE // CHUNK):
        rows = slice(c * CHUNK, (c + 1) * CHUNK)
        vc = v[rows].astype(BF16)
        for d in range(2):
            kc = (k[rows] * k_decay[d]).astype(BF16)
            for p in range(RET_W // PAIR_W):
                cols = slice(p * PAIR_W, (p + 1) * PAIR_W)
                kv_ref[d, 0, c, cols, :] = lax.dot_general(kc[:, cols], vc[:, cols], contract_rows,
                                                           preferred_element_type=F32)


def _cast_specs(arrays, bsz, tiles_per_batch):
    steps = min(CAST_STEPS, 1 << ((bsz * tiles_per_batch).bit_length() - 1))
    slab = lambda b, i, *_: (jnp.minimum(b * tiles_per_batch + i, steps - 1), 0)
    return [pl.BlockSpec((a.shape[0] // steps, a.shape[1]), slab) for a in arrays]


def _inproj_call(layer, x_lat, x_ctx, ctx_tile, mod, w_in_bf, lg_rows, n_lat_tiles, to_cast=()):
    bsz = x_lat.shape[0]
    length = (n_lat_tiles + 1) * TILE
    widths = (3 * CONV_W, RET_W, RET_W, RET_W, RET_W, SGU_W, SGU_W)
    cpt = TILE // CHUNK
    cast_specs = _cast_specs(to_cast, bsz, n_lat_tiles + 1)
    return pl.pallas_call(
        functools.partial(_inproj_kernel, n_lat_tiles, len(to_cast)),
        grid=(bsz, length // TILE),
        in_specs=[
            pl.BlockSpec((1, TILE, D_MODEL), lambda b, i: (b, jnp.minimum(i, n_lat_tiles - 1), 0)),
            pl.BlockSpec((1, TILE, D_MODEL), lambda b, i: (b, ctx_tile, 0)),
            pl.BlockSpec((1, 1, 1, MOD_ROWS, D_MODEL), lambda b, i: (layer, b, jnp.minimum(i // n_lat_tiles, 1), 0, 0)),
            pl.BlockSpec((1, D_MODEL, IN_COLS), lambda b, i: (layer, 0, 0)),
            pl.BlockSpec((1, 1, RET_W), lambda b, i: (2 * layer, 0, 0)),
            pl.BlockSpec((1, 1, RET_W), lambda b, i: (2 * layer + 1, 0, 0)),
        ] + cast_specs,
        out_specs=[pl.BlockSpec((1, TILE, w), lambda b, i: (b, i, 0)) for w in widths]
        + [pl.BlockSpec((2, 1, cpt, RET_W, PAIR_W), lambda b, i: (0, b, i, 0, 0))] + cast_specs,
        out_shape=[jax.ShapeDtypeStruct((bsz, length, w), F32) for w in widths]
        + [jax.ShapeDtypeStruct((2, bsz, length // CHUNK, RET_W, PAIR_W), F32)]
        + [jax.ShapeDtypeStruct(a.shape, BF16) for a in to_cast],
        compiler_params=_params(2),
    )(x_lat, x_ctx, mod, w_in_bf, lg_rows, lg_rows, *to_cast)


def _scan_kernel(n_lat_chunks, kv_ref, lgc_ref, s_ref, state):
    nc = kv_ref.shape[2]
    backward = pl.program_id(1)
    decay = jnp.exp(CHUNK * lgc_ref[0])
    row_head = lax.broadcasted_iota(jnp.int32, (RET_W, PAIR_W), 0) % PAIR_W // RET_DK
    lane_head = lax.broadcasted_iota(jnp.int32, (RET_W, PAIR_W), 1) // RET_DK
    own_head = row_head == lane_head
    state[...] = jnp.zeros_like(state)

    def step(s, carry):
        c = jnp.where(backward == 1, nc - 1 - s, (s + n_lat_chunks) % nc)
        s_ref[0, 0, c] = jnp.where(own_head, state[...], 0.0).astype(BF16)
        state[...] = decay * state[...] + kv_ref[0, 0, c]
        return carry

    lax.fori_loop(0, nc, step, 0)


def _scan_call(layer, kv, lg_cols, n_lat_chunks):
    _, bsz, nc, _, _ = kv.shape
    blk = (1, 1, nc, RET_W, PAIR_W)
    return pl.pallas_call(
        functools.partial(_scan_kernel, n_lat_chunks),
        grid=(bsz, 2),
        in_specs=[pl.BlockSpec(blk, lambda b, d: (d, b, 0, 0, 0)),
                  pl.BlockSpec((1, RET_W, 1), lambda b, d: (2 * layer + d, 0, 0))],
        out_specs=pl.BlockSpec(blk, lambda b, d: (d, b, 0, 0, 0)),
        out_shape=jax.ShapeDtypeStruct(kv.shape, BF16),
        scratch_shapes=[pltpu.VMEM((RET_W, PAIR_W), F32)],
        compiler_params=_params(2),
    )(kv, lg_cols)


def _route_class(lt):
    assert EXPERTS_PER_GROUP == 8 and N_GROUPS <= 8
    sub = lax.broadcasted_iota(jnp.int32, (8, lt.shape[1]), 0)
    neg = jnp.float32(-jnp.inf)

    def top(vals):
        m = jnp.max(vals, axis=0, keepdims=True)
        return jnp.min(jnp.where(vals == m, sub, 8), axis=0, keepdims=True)

    gidx = top(jnp.where(sub < N_GROUPS, lt[0:8], neg))
    pair = jnp.zeros_like(gidx)
    for g in range(N_GROUPS):
        e = lt[EXPERT_ROW0 + 8 * g:EXPERT_ROW0 + 8 * (g + 1)]
        i1 = top(e)
        i2 = top(jnp.where(sub == i1, neg, e))
        e_lo = jnp.minimum(i1, i2)
        e_hi = jnp.maximum(i1, i2)
        pair_g = e_lo * (EXPERTS_PER_GROUP - 1) - ((e_lo * (e_lo - 1)) >> 1) + (e_hi - e_lo - 1)
        pair = jnp.where(gidx == g, pair_g, pair)
    return gidx * PAIRS_PER_GROUP + pair


def _mix_kernel(n_lat_tiles, n_cast, lgf_ref, lgb_ref, pc_ref, hp_ref, hn_ref, q_ref, k_ref, v_ref, g_ref, su_ref, sv_ref,
                sf_ref, sb_ref, x_ref, ctx_ref, mod_ref, convw_ref, sguw_ref, sgub_ref, wout_ref, ln_ref, wr_ref, br_ref,
                lgfr_ref, lgbr_ref, *refs):
    cast_in, (x1_ref, hm_ref, route_ref), cast_out = refs[:n_cast], refs[n_cast:n_cast + 3], refs[n_cast + 3:-3]
    ycat, dec, qdec = refs[-3:]
    for src, dst in zip(cast_in, cast_out):
        dst[...] = src[...].astype(BF16)
    i = pl.program_id(1)
    is_ctx = i >= n_lat_tiles
    row = lax.broadcasted_iota(jnp.int32, (TILE, 1), 0)

    @pl.when((pl.program_id(0) == 0) & (i == 0))
    def _():
        pos = lax.broadcasted_iota(jnp.int32, (CHUNK, 1), 0).astype(F32)
        rel = pos - lax.broadcasted_iota(jnp.int32, (1, CHUNK), 1).astype(F32)
        for h in range(RET_HEADS):
            dec[h // 2, (h % 2) * CHUNK:(h % 2 + 1) * CHUNK, :] = jnp.where(
                rel > 0, jnp.exp(lgf_ref[h] * jnp.maximum(rel, 0.0)),
                jnp.where(rel < 0, jnp.exp(lgb_ref[h] * jnp.maximum(-rel, 0.0)), 2.0))
        qdec[0] = jnp.exp(lgfr_ref[0] * (pos + 1.0))
        qdec[1] = jnp.exp(lgbr_ref[0] * (CHUNK - pos))

    low_head = lax.broadcasted_iota(jnp.int32, (1, PAIR_W), 1) < RET_DK

    def half_norm(o):
        def half_mean(t):
            lo = jnp.sum(jnp.where(low_head, t, 0.0), axis=-1, keepdims=True)
            hi = jnp.sum(jnp.where(low_head, 0.0, t), axis=-1, keepdims=True)
            return jnp.where(low_head, lo, hi) * (1.0 / RET_DK)

        centred = o - half_mean(o)
        return centred * lax.rsqrt(half_mean(jnp.square(centred)) + LN_EPS)

    tiles = [(slice(c * CHUNK, (c + 1) * CHUNK), slice(p * PAIR_W, (p + 1) * PAIR_W), c, p)
             for c in range(TILE // CHUNK) for p in range(RET_W // PAIR_W)]
    scores = []
    for rows, cols, c, p in tiles:
        qp = q_ref[0, rows, cols]
        q_split = jnp.concatenate([jnp.where(low_head, qp, 0.0), jnp.where(low_head, 0.0, qp)], axis=0)
        scores.append(lax.dot_general(q_split.astype(BF16), k_ref[0, rows, cols].astype(BF16),
                                      (((1,), (1,)), ((), ())), preferred_element_type=F32))

    pc = pc_ref[0]
    z = pc[:, CONV_W:2 * CONV_W] * pc[:, 2 * CONV_W:3 * CONV_W]
    line_mask = jnp.where(is_ctx, TILE - 1, GRID_W - 1)
    first = (row & line_mask) == 0
    last = (row & line_mask) == line_mask
    z_prev = jnp.where(first, 0.0, pltpu.roll(z, 1, 0))
    z_next = jnp.where(last, 0.0, pltpu.roll(z, TILE - 1, 0))
    hp = hp_ref[0]
    hn = hn_ref[0]
    z_top = jnp.where(i == 0, 0.0, hp[:, CONV_W:2 * CONV_W] * hp[:, 2 * CONV_W:3 * CONV_W])
    z_bot = jnp.where(i == n_lat_tiles - 1, 0.0, hn[:, CONV_W:2 * CONV_W] * hn[:, 2 * CONV_W:3 * CONV_W])
    zcat = jnp.concatenate([z_top, z, z_bot], axis=0)
    z_up = zcat[0:TILE]
    z_down = zcat[2 * HALO:2 * HALO + TILE]
    along_seq = lax.broadcasted_iota(jnp.int32, (1, CONV_W), 1) < jnp.where(is_ctx, CONV_W, CONV_W // 2)
    cw = convw_ref[0]
    conv = cw[0:1] * jnp.where(along_seq, z_prev, z_up) + cw[1:2] * z
    conv = conv + cw[2:3] * jnp.where(along_seq, z_next, z_down)
    ycat[:, 0:CONV_W] = (pc[:, 0:CONV_W] * conv).astype(BF16)

    outs = []
    for (rows, cols, c, p), sc in zip(tiles, scores):
        qp = q_ref[0, rows, cols]
        vp = v_ref[0, rows, cols]
        sc = sc * dec[p]
        lhs = jnp.concatenate([sc[0:CHUNK].astype(BF16), sc[CHUNK:].astype(BF16),
                               (qp * qdec[0, :, cols]).astype(BF16), (qp * qdec[1, :, cols]).astype(BF16)], axis=1)
        rhs = jnp.concatenate([jnp.where(low_head, vp, 0.0).astype(BF16), jnp.where(low_head, 0.0, vp).astype(BF16),
                               sf_ref[0, 0, c, cols, :], sb_ref[0, 0, c, cols, :]], axis=0)
        outs.append(jnp.dot(lhs, rhs, preferred_element_type=F32))

    vn = _standardize(sv_ref[0]).astype(BF16)
    group = lax.broadcasted_iota(jnp.int32, (1, SGU_W), 1) // (SGU_W // SGU_GROUPS)
    for c in range(TILE // CHUNK):
        rows = slice(c * CHUNK, (c + 1) * CHUNK)
        mixed = jnp.zeros((CHUNK, SGU_W), F32)
        for gi in range(SGU_GROUPS):
            m = jnp.dot(sguw_ref[0, gi], vn[rows], preferred_element_type=F32)
            mixed = jnp.where(group == gi, m, mixed)
        ycat[rows, CONV_W + RET_W:] = (su_ref[0, rows, :] * (mixed + sgub_ref[0])).astype(BF16)

    for (rows, cols, c, p), o in zip(tiles, outs):
        ycat[rows, CONV_W + p * PAIR_W:CONV_W + (p + 1) * PAIR_W] = (
            _silu(g_ref[0, rows, cols]) * half_norm(o)).astype(BF16)

    y = jnp.dot(ycat[...], wout_ref[0], preferred_element_type=F32)
    mod = mod_ref[0, 0, 0]
    ln = ln_ref[0]
    x1 = _standardize(ALPHA * jnp.where(is_ctx, ctx_ref[0], x_ref[0]) + mod[2:3] * y) * ln[0:1] + ln[1:2]
    x1_ref[0] = x1
    hm = x1 * (1.0 + mod[4:5]) + mod[3:4]
    hm_ref[0] = _pack_bf16_pairs(hm)
    lt = lax.dot_general(wr_ref[0], hm.astype(BF16), (((1,), (1,)), ((), ())), preferred_element_type=F32) + br_ref[0]
    route_ref[0, 0] = jnp.concatenate([_route_class(lt).astype(F32), jnp.zeros((7, TILE), F32)], axis=0)


def _mix_call(layer, n_lat_tiles, n_proc, lgf, lgb, pc, q, k, v, g, su, sv, states, x_lat, x_ctx, ctx_tile, mod, conv_w,
              sgu_w_bf, sgu_bias, w_out_bf, ln1, w_route_t, b_route_t, lg_rows, to_cast=()):
    bsz, length, _ = q.shape
    cast_specs = _cast_specs(to_cast, bsz, n_proc)
    halos_per_tile = TILE // HALO
    n_halo = length // HALO
    cpt = TILE // CHUNK
    tok = lambda w: pl.BlockSpec((1, TILE, w), lambda b, i, *_: (b, i, 0))
    per_layer = lambda *shape: pl.BlockSpec((1,) + shape, lambda b, i, *_: (layer,) + (0,) * len(shape))
    grid_spec = pltpu.PrefetchScalarGridSpec(
        num_scalar_prefetch=2,
        grid=(bsz, n_proc),
        in_specs=[
            tok(3 * CONV_W),
            pl.BlockSpec((1, HALO, 3 * CONV_W), lambda b, i, *_: (b, jnp.maximum(i * halos_per_tile - 1, 0), 0)),
            pl.BlockSpec((1, HALO, 3 * CONV_W),
                         lambda b, i, *_: (b, jnp.minimum((i + 1) * halos_per_tile, n_halo - 1), 0)),
            tok(RET_W), tok(RET_W), tok(RET_W), tok(RET_W), tok(SGU_W), tok(SGU_W),
            pl.BlockSpec((1, 1, cpt, RET_W, PAIR_W), lambda b, i, *_: (0, b, i, 0, 0)),
            pl.BlockSpec((1, 1, cpt, RET_W, PAIR_W), lambda b, i, *_: (1, b, i, 0, 0)),
            pl.BlockSpec((1, TILE, D_MODEL), lambda b, i, *_: (b, jnp.minimum(i, n_lat_tiles - 1), 0)),
            pl.BlockSpec((1, TILE, D_MODEL), lambda b, i, *_: (b, ctx_tile, 0)),
            pl.BlockSpec((1, 1, 1, MOD_ROWS, D_MODEL),
                         lambda b, i, *_: (layer, b, jnp.minimum(i // n_lat_tiles, 1), 0, 0)),
            per_layer(8, CONV_W),
            per_layer(SGU_GROUPS, CHUNK, CHUNK),
            per_layer(CHUNK, SGU_W),
            per_layer(D_MODEL, D_MODEL),
            per_layer(8, D_MODEL),
            per_layer(ROUTE_ROWS, D_MODEL),
            per_layer(ROUTE_ROWS, 1),
            pl.BlockSpec((1, 1, RET_W), lambda b, i, *_: (2 * layer, 0, 0)),
            pl.BlockSpec((1, 1, RET_W), lambda b, i, *_: (2 * layer + 1, 0, 0)),
        ] + cast_specs,
        out_specs=[tok(D_MODEL), tok(PACK_W), pl.BlockSpec((1, 1, 8, TILE), lambda b, i, *_: (b, i, 0, 0))] + cast_specs,
        scratch_shapes=[pltpu.VMEM((TILE, D_MODEL), BF16),
                        pltpu.VMEM((RET_W // PAIR_W, 2 * CHUNK, CHUNK), F32),
                        pltpu.VMEM((2, CHUNK, RET_W), F32)],
    )
    return pl.pallas_call(
        functools.partial(_mix_kernel, n_lat_tiles, len(to_cast)),
        grid_spec=grid_spec,
        out_shape=[jax.ShapeDtypeStruct((bsz, n_proc * TILE, D_MODEL), F32),
                   jax.ShapeDtypeStruct((bsz, n_proc * TILE, PACK_W), jnp.uint32),
                   jax.ShapeDtypeStruct((bsz, n_proc, 8, TILE), F32)]
        + [jax.ShapeDtypeStruct(a.shape, BF16) for a in to_cast],
        compiler_params=_params(2),
    )(lgf, lgb, pc, pc, pc, q, k, v, g, su, sv, states, states, x_lat, x_ctx, mod, conv_w, sgu_w_bf, sgu_bias, w_out_bf,
      ln1, w_route_t, b_route_t, lg_rows, lg_rows, *to_cast)


def _rank_kernel(route_ref, class_ref, rank_ref, counts_ref, running):
    @pl.when(pl.program_id(0) == 0)
    def _():
        running[...] = jnp.zeros_like(running)

    sub = lax.broadcasted_iota(jnp.int32, (ROUTE_W, TILE), 0)
    earlier = (lax.broadcasted_iota(jnp.int32, (TILE, TILE), 0)
               < lax.broadcasted_iota(jnp.int32, (TILE, TILE), 1)).astype(BF16)
    for t in range(route_ref.shape[0]):
        cls = route_ref[t, 0:1, :].astype(jnp.int32)
        class_ref[t] = cls
        onehot = sub == cls
        before = jnp.dot(onehot.astype(BF16), earlier, preferred_element_type=F32) + running[...]
        rank_ref[t] = jnp.sum(jnp.where(onehot, before, 0.0), axis=0, keepdims=True).astype(jnp.int32)
        running[...] += jnp.sum(onehot.astype(F32), axis=1, keepdims=True)
    counts_ref[...] = jnp.broadcast_to(running[...], counts_ref.shape)


def _rank_call(route):
    n_tiles = route.shape[0]
    per_step = math.gcd(n_tiles, RANK_TILES)
    per_tile = pl.BlockSpec((per_step, 1, TILE), lambda i: (i, 0, 0))
    return pl.pallas_call(
        _rank_kernel,
        grid=(n_tiles // per_step,),
        in_specs=[pl.BlockSpec((per_step, 8, TILE), lambda i: (i, 0, 0))],
        out_specs=[per_tile, per_tile, pl.BlockSpec((ROUTE_W, ROUTE_W), lambda i: (0, 0))],
        out_shape=[jax.ShapeDtypeStruct((n_tiles, 1, TILE), jnp.int32), jax.ShapeDtypeStruct((n_tiles, 1, TILE), jnp.int32),
                   jax.ShapeDtypeStruct((ROUTE_W, ROUTE_W), F32)],
        scratch_shapes=[pltpu.VMEM((ROUTE_W, 1), F32)],
        compiler_params=_params(1),
    )(route)


def _dispatch_kernel(dest_ref, pad_end_ref, pad_len_ref, n_used_ref, hm_ref, xs_ref, zeros, sem, pad_sem):
    step = pl.program_id(0)
    base = step * TILE
    half = BLOCK_M // 2

    def for_each_pad_copy(fn):
        def per_class(c, carry):
            off = pad_end_ref[c]
            n = pad_len_ref[c]
            for shift in range(BLOCK_M.bit_length() - 2, -1, -1):
                bit = 1 << shift
                off = off - (n & bit)

                @pl.when((n & bit) != 0)
                def _():
                    if bit >= 8:
                        fn(pltpu.make_async_copy(zeros.at[pl.ds(0, bit)], xs_ref.at[pl.ds(pl.multiple_of(off, 8), bit)],
                                                 pad_sem))
                    else:
                        for j in range(bit):
                            fn(pltpu.make_async_copy(zeros.at[pl.ds(0, 1)], xs_ref.at[pl.ds(off + j, 1)], pad_sem))

            return carry

        lax.fori_loop(0, N_CLASSES, per_class, 0)

        def per_half_block(j, carry):
            fn(pltpu.make_async_copy(zeros, xs_ref.at[pl.ds(pl.multiple_of(j * half, 8), half)], pad_sem))
            return carry

        lax.fori_loop(n_used_ref[0] * 2, xs_ref.shape[0] // half, per_half_block, 0)

    @pl.when(step == 0)
    def _():
        zeros[...] = jnp.zeros_like(zeros)
        for_each_pad_copy(lambda cp: cp.start())

    for r in range(TILE):
        pltpu.make_async_copy(hm_ref.at[pl.ds(r, 1)], xs_ref.at[pl.ds(dest_ref[base + r], 1)],
                              sem).start(priority=r % 2)
    for r in range(TILE):
        pltpu.make_async_copy(hm_ref.at[pl.ds(0, 1)], xs_ref.at[pl.ds(0, 1)], sem).wait()

    @pl.when(step == pl.num_programs(0) - 1)
    def _():
        for_each_pad_copy(lambda cp: cp.wait())


def _dispatch_call(dest, pad_end, pad_len, n_used, hm_flat, n_slots):
    n_tok = hm_flat.shape[0]
    grid_spec = pltpu.PrefetchScalarGridSpec(
        num_scalar_prefetch=4,
        grid=(n_tok // TILE,),
        in_specs=[pl.BlockSpec((TILE, PACK_W), lambda i, *_: (i, 0))],
        out_specs=pl.BlockSpec(memory_space=pl.ANY),
        scratch_shapes=[pltpu.VMEM((BLOCK_M // 2, PACK_W), jnp.uint32), pltpu.SemaphoreType.DMA,
                        pltpu.SemaphoreType.DMA],
    )
    return pl.pallas_call(
        _dispatch_kernel,
        grid_spec=grid_spec,
        out_shape=jax.ShapeDtypeStruct((n_slots, PACK_W), jnp.uint32),
        compiler_params=_params(1),
    )(dest, pad_end, pad_len, n_used, hm_flat)


def _expert_kernel(blk_lo_ref, blk_hi_ref, n_used_ref, xs_ref, wr_ref, br_ref, wg_lo, wu_lo, wd_lo, wg_hi, wu_hi, wd_hi,
                   ys_ref):
    i = pl.program_id(0)
    used = i < n_used_ref[0]

    @pl.when(used)
    def _():
        xb = _unpack_bf16_pairs(xs_ref[...])

        logits = jnp.dot(xb, wr_ref[0], preferred_element_type=F32) + br_ref[0]
        lane = lax.broadcasted_iota(jnp.int32, logits.shape, 1)
        gl = jnp.where(lane < N_GROUPS, logits, -jnp.inf)
        g_prob = 1.0 / jnp.sum(jnp.exp(gl - jnp.max(gl, axis=-1, keepdims=True)), axis=-1, keepdims=True)
        l_lo = jnp.sum(jnp.where(lane == N_GROUPS + blk_lo_ref[i], logits, 0.0), axis=-1, keepdims=True)
        l_hi = jnp.sum(jnp.where(lane == N_GROUPS + blk_hi_ref[i], logits, 0.0), axis=-1, keepdims=True)
        m = jnp.maximum(l_lo, l_hi)
        p_lo = jnp.exp(l_lo - m)
        p_hi = jnp.exp(l_hi - m)

        h_lo = jnp.dot(xb, wg_lo[0, 0], preferred_element_type=F32)
        u_lo = jnp.dot(xb, wu_lo[0, 0], preferred_element_type=F32)
        h_hi = jnp.dot(xb, wg_hi[0, 0], preferred_element_type=F32)
        u_hi = jnp.dot(xb, wu_hi[0, 0], preferred_element_type=F32)
        a_lo = (_silu(h_lo) * u_lo).astype(BF16)
        a_hi = (_silu(h_hi) * u_hi).astype(BF16)
        y_lo = jnp.dot(a_lo, wd_lo[0, 0], preferred_element_type=F32)
        y_hi = jnp.dot(a_hi, wd_hi[0, 0], preferred_element_type=F32)
        ys_ref[...] = y_lo * (g_prob * (p_lo / (p_lo + p_hi))) + y_hi * (g_prob * (p_hi / (p_lo + p_hi)))

    @pl.when(jnp.logical_not(used))
    def _():
        ys_ref[...] = jnp.zeros_like(ys_ref)


def _expert_call(layer, blk_lo, blk_hi, n_used, xs, w_route_bf, b_route, w_gate_bf, w_up_bf, w_down_bf):
    n_blocks = xs.shape[0] // BLOCK_M
    up_spec = lambda which: pl.BlockSpec((1, 1, D_MODEL, EXPERT_HIDDEN),
                                         lambda i, lo, hi, nu: (layer, (lo, hi)[which][i], 0, 0))
    down_spec = lambda which: pl.BlockSpec((1, 1, EXPERT_HIDDEN, D_MODEL),
                                           lambda i, lo, hi, nu: (layer, (lo, hi)[which][i], 0, 0))
    grid_spec = pltpu.PrefetchScalarGridSpec(
        num_scalar_prefetch=3,
        grid=(n_blocks,),
        in_specs=[pl.BlockSpec((BLOCK_M, PACK_W), lambda i, lo, hi, nu: (jnp.minimum(i, nu[0] - 1), 0)),
                  pl.BlockSpec((1, D_MODEL, ROUTE_W), lambda i, *_: (layer, 0, 0)),
                  pl.BlockSpec((1, 1, ROUTE_W), lambda i, *_: (layer, 0, 0)),
                  up_spec(0), up_spec(0), down_spec(0), up_spec(1), up_spec(1), down_spec(1)],
        out_specs=pl.BlockSpec((BLOCK_M, D_MODEL), lambda i, *_: (i, 0)),
    )
    return pl.pallas_call(
        _expert_kernel,
        grid_spec=grid_spec,
        out_shape=jax.ShapeDtypeStruct((xs.shape[0], D_MODEL), F32),
        compiler_params=_params(1),
    )(blk_lo, blk_hi, n_used, xs, w_route_bf, b_route, w_gate_bf, w_up_bf, w_down_bf, w_gate_bf, w_up_bf, w_down_bf)


def _combine_kernel(tiles_per_batch, dest_ref, ys_ref, x1_ref, mod_ref, ln_ref, out_ref, buf, sem):
    step = pl.program_id(0) * tiles_per_batch + pl.program_id(1)
    slot = step % 2
    has_next = step + 1 < pl.num_programs(0) * tiles_per_batch

    def gather(tile, to_slot):
        base = tile * TILE

        for r in range(TILE):
            pltpu.make_async_copy(ys_ref.at[pl.ds(dest_ref[base + r], 1)], buf.at[to_slot, pl.ds(r, 1)],
                                  sem.at[to_slot]).start(priority=r % 2)

    @pl.when(step == 0)
    def _():
        gather(step, slot)

    @pl.when(has_next)
    def _():
        gather(step + 1, 1 - slot)

    for r in range(TILE):
        pltpu.make_async_copy(ys_ref.at[pl.ds(0, 1)], buf.at[slot, pl.ds(0, 1)], sem.at[slot]).wait()

    mod = mod_ref[0, 0, 0]
    ln = ln_ref[0]
    out_ref[0] = _standardize(ALPHA * x1_ref[0] + mod[5:6] * buf[slot]) * ln[0:1] + ln[1:2]


def _combine_call(layer, n_lat_tiles, dest, ys, x1, mod, ln2):
    bsz, length, _ = x1.shape
    tiles_per_batch = length // TILE
    tok = lambda w: pl.BlockSpec((1, TILE, w), lambda b, i, *_: (b, i, 0))
    grid_spec = pltpu.PrefetchScalarGridSpec(
        num_scalar_prefetch=1,
        grid=(bsz, tiles_per_batch),
        in_specs=[
            pl.BlockSpec(memory_space=pl.ANY),
            tok(D_MODEL),
            pl.BlockSpec((1, 1, 1, MOD_ROWS, D_MODEL),
                         lambda b, i, *_: (layer, b, jnp.minimum(i // n_lat_tiles, 1), 0, 0)),
            pl.BlockSpec((1, 8, D_MODEL), lambda b, i, *_: (layer, 0, 0)),
        ],
        out_specs=tok(D_MODEL),
        scratch_shapes=[pltpu.VMEM((2, TILE, D_MODEL), F32), pltpu.SemaphoreType.DMA((2,))],
    )
    return pl.pallas_call(
        functools.partial(_combine_kernel, tiles_per_batch),
        grid_spec=grid_spec,
        out_shape=jax.ShapeDtypeStruct(x1.shape, F32),
        compiler_params=_params(2),
    )(dest, ys, x1, mod, ln2)


def _dispatch_plan(class_of_token, rank, counts):
    n_tok = class_of_token.shape[0]
    counts = counts.astype(jnp.int32)
    pcounts = (counts + BLOCK_M - 1) // BLOCK_M * BLOCK_M
    pends = jnp.cumsum(pcounts)
    pstarts = pends - pcounts
    classes = jnp.arange(ROUTE_W, dtype=jnp.int32)
    dest = jnp.sum(jnp.where(class_of_token[:, None] == classes[None, :], pstarts[None, :], 0), axis=1) + rank
    n_blocks = n_tok // BLOCK_M + N_CLASSES
    blk_start = jnp.arange(n_blocks, dtype=jnp.int32) * BLOCK_M
    blk_class = jnp.minimum(jnp.sum((pends[None, :] <= blk_start[:, None]).astype(jnp.int32), axis=1), N_CLASSES - 1)
    n_used = pends[-1:] // BLOCK_M
    return (dest.astype(jnp.int32), pends, pcounts - counts, jnp.asarray(CLASS_LO)[blk_class],
            jnp.asarray(CLASS_HI)[blk_class], n_used.astype(jnp.int32), n_blocks * BLOCK_M)


def _pad_rows(a, rows):
    return jnp.pad(a, [(0, 0)] * (a.ndim - 2) + [(0, rows - a.shape[-2]), (0, 0)])


def kernel(x, c, ctx, c_ctx, w_ada, b_ada, w_in, conv_w, ret_decay_fwd, ret_decay_bwd, sgu_w, sgu_b, w_out, ln1_g, ln1_b,
           router_group_w, router_group_b, router_expert_w, router_expert_b, moe_w_gate, moe_w_up, moe_w_down, ln2_g,
           ln2_b):
    bsz, seq, d = x.shape
    ctx_len = ctx.shape[1]
    assert d == D_MODEL and ctx_len == TILE and seq % TILE == 0 and seq % GRID_W == 0
    n_lat_tiles = seq // TILE
    n_lat_chunks = seq // CHUNK

    cond = _pad_rows(jnp.concatenate([c, c_ctx[None, :]], axis=0), 16)
    ada = _ada_call(cond, w_ada, b_ada)
    mod_lat = ada[:, :bsz].reshape(DEPTH, bsz, N_MOD, d)
    mod_ctx = jnp.broadcast_to(ada[:, bsz].reshape(DEPTH, 1, N_MOD, d), mod_lat.shape)
    mod = _pad_rows(jnp.stack([mod_lat, mod_ctx], axis=2), MOD_ROWS)

    w_in_bf = w_in.astype(BF16)
    w_out_bf = w_out.astype(BF16)
    sgu_w_bf = sgu_w.astype(BF16)
    sgu_bias = jnp.repeat(jnp.swapaxes(sgu_b, 1, 2), SGU_W // SGU_GROUPS, axis=2)
    conv_w8 = _pad_rows(conv_w, 8)
    ln1 = _pad_rows(jnp.stack([ln1_g, ln1_b], axis=1), 8)
    ln2 = _pad_rows(jnp.stack([ln2_g, ln2_b], axis=1), 8)
    w_route = jnp.concatenate([router_group_w, jnp.swapaxes(router_expert_w, 1, 2).reshape(DEPTH, d, N_EXPERTS)], axis=2)
    w_route_bf = jnp.pad(w_route, ((0, 0), (0, 0), (0, ROUTE_W - w_route.shape[2]))).astype(BF16)
    b_route = jnp.concatenate([router_group_b, router_expert_b.reshape(DEPTH, N_EXPERTS)], axis=1)
    b_route = jnp.pad(b_route, ((0, 0), (0, ROUTE_W - b_route.shape[1])))[:, None, :].astype(F32)
    w_route_t = jnp.concatenate([_pad_rows(jnp.swapaxes(w_route[:, :, :N_GROUPS], 1, 2), EXPERT_ROW0),
                                 jnp.swapaxes(w_route[:, :, N_GROUPS:], 1, 2)], axis=1)
    w_route_t = _pad_rows(w_route_t, ROUTE_ROWS).astype(BF16)
    b_route_t = jnp.concatenate([_pad_rows(b_route[:, 0, :N_GROUPS, None], EXPERT_ROW0),
                                 b_route[:, 0, N_GROUPS:N_GROUPS + N_EXPERTS, None]], axis=1)
    b_route_t = _pad_rows(b_route_t, ROUTE_ROWS)
    lg = jnp.stack([jax.nn.log_sigmoid(ret_decay_fwd.astype(F32)), jax.nn.log_sigmoid(ret_decay_bwd.astype(F32))],
                   axis=1)
    lg_lanes = jnp.repeat(lg, RET_DK, axis=2).reshape(DEPTH * 2, RET_W)
    lg_rows = lg_lanes[:, None, :]
    lg_cols = lg_lanes[:, :, None]

    x_lat, x_ctx, ctx_tile = x, ctx, 0
    for layer in range(DEPTH):
        last = layer == DEPTH - 1
        first = layer == 0
        pc, q, k, v, g, su, sv, kv, *cast = _inproj_call(
            layer, x_lat, x_ctx, ctx_tile, mod, w_in_bf, lg_rows, n_lat_tiles,
            to_cast=(moe_w_gate.reshape(-1, EXPERT_HIDDEN), moe_w_up.reshape(-1, EXPERT_HIDDEN)) if first else ())
        if first:
            w_gate_bf, w_up_bf = (a.reshape(moe_w_gate.shape) for a in cast)
        states = _scan_call(layer, kv, lg_cols, n_lat_chunks)
        n_proc = n_lat_tiles if last else n_lat_tiles + 1
        x1, hm, route, *cast = _mix_call(layer, n_lat_tiles, n_proc, lg[layer, 0], lg[layer, 1], pc, q, k, v, g, su, sv,
                                         states, x_lat, x_ctx, ctx_tile, mod, conv_w8, sgu_w_bf, sgu_bias, w_out_bf, ln1,
                                         w_route_t, b_route_t, lg_rows,
                                         to_cast=(moe_w_down.reshape(-1, D_MODEL),) if first else ())
        if first:
            w_down_bf = cast[0].reshape(moe_w_down.shape)
        class_of_token, rank, counts = _rank_call(route.reshape(-1, 8, TILE))
        dest, pad_end, pad_len, blk_lo, blk_hi, n_used, n_slots = _dispatch_plan(class_of_token.reshape(-1),
                                                                                  rank.reshape(-1), counts[:, 0])
        xs = _dispatch_call(dest, pad_end, pad_len, n_used, hm.reshape(-1, PACK_W), n_slots)
        ys = _expert_call(layer, blk_lo, blk_hi, n_used, xs, w_route_bf, b_route, w_gate_bf, w_up_bf, w_down_bf)
        xa = _combine_call(layer, n_lat_tiles, dest, ys, x1, mod, ln2)
        x_lat, x_ctx, ctx_tile = xa, xa, n_lat_tiles
    return xa
```

```python
import functools
import math

import jax
import jax.numpy as jnp
import numpy as np
from jax import lax
from jax.experimental import pallas as pl
from jax.experimental.pallas import tpu as pltpu

F32 = jnp.float32
BF16 = jnp.bfloat16

D_MODEL = 1024
DEPTH = 2
GRID_W = 64
CONV_W = 256
RET_W = 512
RET_HEADS = 8
RET_DK = 64
PAIR_W = 2 * RET_DK
CHUNK = 128
SGU_W = 256
SGU_GROUPS = 4
IN_COLS = 3 * CONV_W + 4 * RET_W + 2 * SGU_W
N_GROUPS = 4
EXPERTS_PER_GROUP = 8
N_EXPERTS = N_GROUPS * EXPERTS_PER_GROUP
EXPERT_HIDDEN = 512
N_MOD = 6
MOD_ROWS = 8
LN_EPS = 1e-5
ALPHA = (2 * DEPTH) ** 0.25

TILE = 256
HALO = GRID_W
BLOCK_M = 256
ROUTE_W = 128
CAST_STEPS = 128
PACK_W = D_MODEL // 2
ROUTE_ROWS = 48
EXPERT_ROW0 = 8
RANK_TILES = 8
PAIRS_PER_GROUP = EXPERTS_PER_GROUP * (EXPERTS_PER_GROUP - 1) // 2
N_CLASSES = N_GROUPS * PAIRS_PER_GROUP
CLASS_LO = np.array([g * EXPERTS_PER_GROUP + lo for g in range(N_GROUPS) for lo in range(EXPERTS_PER_GROUP)
                     for hi in range(lo + 1, EXPERTS_PER_GROUP)] + [N_EXPERTS - 2] * (ROUTE_W - N_CLASSES), np.int32)
CLASS_HI = np.array([g * EXPERTS_PER_GROUP + hi for g in range(N_GROUPS) for lo in range(EXPERTS_PER_GROUP)
                     for hi in range(lo + 1, EXPERTS_PER_GROUP)] + [N_EXPERTS - 1] * (ROUTE_W - N_CLASSES), np.int32)
VMEM_LIMIT = 56 * 1024 * 1024


def _params(n_axes):
    return pltpu.CompilerParams(dimension_semantics=("arbitrary",) * n_axes, vmem_limit_bytes=VMEM_LIMIT)


def _standardize(v):
    mu = jnp.mean(v, axis=-1, keepdims=True)
    var = jnp.mean(jnp.square(v - mu), axis=-1, keepdims=True)
    return (v - mu) * lax.rsqrt(var + LN_EPS)


def _silu(v):
    return v * jax.nn.sigmoid(v)


def _pack_bf16_pairs(v):
    half = v.shape[1] // 2
    bits = lambda t: pltpu.bitcast(t.astype(BF16).astype(F32), jnp.uint32)
    return bits(v[:, :half]) | (bits(v[:, half:]) >> 16)


def _unpack_bf16_pairs(w):
    hi = pltpu.bitcast(w & jnp.uint32(0xFFFF0000), F32)
    lo = pltpu.bitcast(w << 16, F32)
    return jnp.concatenate([hi, lo], axis=1).astype(BF16)


def _ada_kernel(c_ref, w_ref, b_ref, o_ref):
    a = _silu(c_ref[...]).astype(BF16)
    o_ref[0] = jnp.dot(a, w_ref[0].astype(BF16), preferred_element_type=F32) + b_ref[0]


def _ada_call(cond, w_ada, b_ada):
    rows = cond.shape[0]
    cols = w_ada.shape[-1]
    tn = 1536
    return pl.pallas_call(
        _ada_kernel,
        grid=(DEPTH, cols // tn),
        in_specs=[
            pl.BlockSpec((rows, D_MODEL), lambda l, j: (0, 0)),
            pl.BlockSpec((1, D_MODEL, tn), lambda l, j: (l, 0, j)),
            pl.BlockSpec((1, 1, tn), lambda l, j: (l, 0, j)),
        ],
        out_specs=pl.BlockSpec((1, rows, tn), lambda l, j: (l, 0, j)),
        out_shape=jax.ShapeDtypeStruct((DEPTH, rows, cols), F32),
        compiler_params=_params(2),
    )(cond, w_ada, b_ada.reshape(DEPTH, 1, cols))


def _inproj_kernel(n_lat_tiles, n_cast, x_ref, ctx_ref, mod_ref, w_ref, lgf_ref, lgb_ref, *refs):
    cast_in, (pc_ref, q_ref, k_ref, v_ref, g_ref, su_ref, sv_ref, kv_ref), cast_out = (
        refs[:n_cast], refs[n_cast:n_cast + 8], refs[n_cast + 8:])
    for src, dst in zip(cast_in, cast_out):
        dst[...] = src[...].astype(BF16)
    mod = mod_ref[0, 0, 0]
    xin = jnp.where(pl.program_id(1) >= n_lat_tiles, ctx_ref[0], x_ref[0])
    h = (xin * (1.0 + mod[1:2]) + mod[0:1]).astype(BF16)

    def proj(lo, hi):
        return jnp.dot(h, w_ref[0, :, lo:hi], preferred_element_type=F32)

    o = 3 * CONV_W
    pc_ref[0] = proj(0, o)
    q_ref[0] = proj(o, o + RET_W)
    k = proj(o + RET_W, o + 2 * RET_W) * (RET_DK ** -0.5)
    k_ref[0] = k
    v = proj(o + 2 * RET_W, o + 3 * RET_W)
    v_ref[0] = v
    g_ref[0] = proj(o + 3 * RET_W, o + 4 * RET_W)
    o += 4 * RET_W
    su_ref[0] = proj(o, o + SGU_W)
    sv_ref[0] = proj(o + SGU_W, o + 2 * SGU_W)

    pos = lax.broadcasted_iota(jnp.int32, (CHUNK, 1), 0).astype(F32)
    k_decay = (jnp.exp((CHUNK - 1.0 - pos) * lgf_ref[0]), jnp.exp(pos * lgb_ref[0]))
    contract_rows = (((0,), (0,)), ((), ()))
    for c in range(TILE // CHUNK):
        rows = slice(c * CHUNK, (c + 1) * CHUNK)
        vc = v[rows].astype(BF16)
        for d in range(2):
            kc = (k[rows] * k_decay[d]).astype(BF16)
            for p in range(RET_W // PAIR_W):
                cols = slice(p * PAIR_W, (p + 1) * PAIR_W)
                kv_ref[d, 0, c, cols, :] = lax.dot_general(kc[:, cols], vc[:, cols], contract_rows,
                                                           preferred_element_type=F32)


def _cast_specs(arrays, part, step_of, n_steps):
    steps = min(CAST_STEPS, 1 << (n_steps.bit_length() - 1))
    block = lambda a: (a.shape[0] // (DEPTH * steps), a.shape[1])
    slab = lambda *idx: jnp.minimum(step_of(*idx), steps - 1)
    return ([pl.BlockSpec(block(a), lambda *idx: (part * steps + slab(*idx), 0)) for a in arrays],
            [pl.BlockSpec(block(a), lambda *idx: (slab(*idx), 0)) for a in arrays],
            [jax.ShapeDtypeStruct((a.shape[0] // DEPTH, a.shape[1]), BF16) for a in arrays])


def _inproj_call(layer, x_lat, x_ctx, ctx_tile, mod, w_in_bf, lg_rows, n_lat_tiles, to_cast=()):
    bsz = x_lat.shape[0]
    length = (n_lat_tiles + 1) * TILE
    widths = (3 * CONV_W, RET_W, RET_W, RET_W, RET_W, SGU_W, SGU_W)
    cpt = TILE // CHUNK
    cast_in, cast_out, cast_shapes = _cast_specs(to_cast, layer, lambda b, i: b * (n_lat_tiles + 1) + i,
                                                  bsz * (n_lat_tiles + 1))
    return pl.pallas_call(
        functools.partial(_inproj_kernel, n_lat_tiles, len(to_cast)),
        grid=(bsz, length // TILE),
        in_specs=[
            pl.BlockSpec((1, TILE, D_MODEL), lambda b, i: (b, jnp.minimum(i, n_lat_tiles - 1), 0)),
            pl.BlockSpec((1, TILE, D_MODEL), lambda b, i: (b, ctx_tile, 0)),
            pl.BlockSpec((1, 1, 1, MOD_ROWS, D_MODEL), lambda b, i: (layer, b, jnp.minimum(i // n_lat_tiles, 1), 0, 0)),
            pl.BlockSpec((1, D_MODEL, IN_COLS), lambda b, i: (layer, 0, 0)),
            pl.BlockSpec((1, 1, RET_W), lambda b, i: (2 * layer, 0, 0)),
            pl.BlockSpec((1, 1, RET_W), lambda b, i: (2 * layer + 1, 0, 0)),
        ] + cast_in,
        out_specs=[pl.BlockSpec((1, TILE, w), lambda b, i: (b, i, 0)) for w in widths]
        + [pl.BlockSpec((2, 1, cpt, RET_W, PAIR_W), lambda b, i: (0, b, i, 0, 0))] + cast_out,
        out_shape=[jax.ShapeDtypeStruct((bsz, length, w), F32) for w in widths]
        + [jax.ShapeDtypeStruct((2, bsz, length // CHUNK, RET_W, PAIR_W), F32)] + cast_shapes,
        compiler_params=_params(2),
    )(x_lat, x_ctx, mod, w_in_bf, lg_rows, lg_rows, *to_cast)


def _scan_kernel(n_lat_chunks, kv_ref, lgc_ref, s_ref, state):
    nc = kv_ref.shape[2]
    backward = pl.program_id(1)
    decay = jnp.exp(CHUNK * lgc_ref[0])
    row_head = lax.broadcasted_iota(jnp.int32, (RET_W, PAIR_W), 0) % PAIR_W // RET_DK
    lane_head = lax.broadcasted_iota(jnp.int32, (RET_W, PAIR_W), 1) // RET_DK
    own_head = row_head == lane_head
    state[...] = jnp.zeros_like(state)

    def step(s, carry):
        c = jnp.where(backward == 1, nc - 1 - s, (s + n_lat_chunks) % nc)
        s_ref[0, 0, c] = jnp.where(own_head, state[...], 0.0).astype(BF16)
        state[...] = decay * state[...] + kv_ref[0, 0, c]
        return carry

    lax.fori_loop(0, nc, step, 0)


def _scan_call(layer, kv, lg_cols, n_lat_chunks):
    _, bsz, nc, _, _ = kv.shape
    blk = (1, 1, nc, RET_W, PAIR_W)
    return pl.pallas_call(
        functools.partial(_scan_kernel, n_lat_chunks),
        grid=(bsz, 2),
        in_specs=[pl.BlockSpec(blk, lambda b, d: (d, b, 0, 0, 0)),
                  pl.BlockSpec((1, RET_W, 1), lambda b, d: (2 * layer + d, 0, 0))],
        out_specs=pl.BlockSpec(blk, lambda b, d: (d, b, 0, 0, 0)),
        out_shape=jax.ShapeDtypeStruct(kv.shape, BF16),
        scratch_shapes=[pltpu.VMEM((RET_W, PAIR_W), F32)],
        compiler_params=_params(2),
    )(kv, lg_cols)


def _route_class(lt):
    assert EXPERTS_PER_GROUP == 8 and N_GROUPS <= 8
    sub = lax.broadcasted_iota(jnp.int32, (8, lt.shape[1]), 0)
    neg = jnp.float32(-jnp.inf)

    def top(vals):
        m = jnp.max(vals, axis=0, keepdims=True)
        return jnp.min(jnp.where(vals == m, sub, 8), axis=0, keepdims=True)

    gidx = top(jnp.where(sub < N_GROUPS, lt[0:8], neg))
    pair = jnp.zeros_like(gidx)
    for g in range(N_GROUPS):
        e = lt[EXPERT_ROW0 + 8 * g:EXPERT_ROW0 + 8 * (g + 1)]
        i1 = top(e)
        i2 = top(jnp.where(sub == i1, neg, e))
        e_lo = jnp.minimum(i1, i2)
        e_hi = jnp.maximum(i1, i2)
        pair_g = e_lo * (EXPERTS_PER_GROUP - 1) - ((e_lo * (e_lo - 1)) >> 1) + (e_hi - e_lo - 1)
        pair = jnp.where(gidx == g, pair_g, pair)
    return gidx * PAIRS_PER_GROUP + pair


def _mix_kernel(n_lat_tiles, n_cast, lgf_ref, lgb_ref, pc_ref, hp_ref, hn_ref, q_ref, k_ref, v_ref, g_ref, su_ref, sv_ref,
                sf_ref, sb_ref, x_ref, ctx_ref, mod_ref, convw_ref, sguw_ref, sgub_ref, wout_ref, ln_ref, wr_ref, br_ref,
                lgfr_ref, lgbr_ref, *refs):
    cast_in, (x1_ref, hm_ref, route_ref), cast_out = refs[:n_cast], refs[n_cast:n_cast + 3], refs[n_cast + 3:-3]
    ycat, dec, qdec = refs[-3:]
    for src, dst in zip(cast_in, cast_out):
        dst[...] = src[...].astype(BF16)
    i = pl.program_id(1)
    is_ctx = i >= n_lat_tiles
    row = lax.broadcasted_iota(jnp.int32, (TILE, 1), 0)

    @pl.when((pl.program_id(0) == 0) & (i == 0))
    def _():
        pos = lax.broadcasted_iota(jnp.int32, (CHUNK, 1), 0).astype(F32)
        rel = pos - lax.broadcasted_iota(jnp.int32, (1, CHUNK), 1).astype(F32)
        for h in range(RET_HEADS):
            dec[h // 2, (h % 2) * CHUNK:(h % 2 + 1) * CHUNK, :] = jnp.where(
                rel > 0, jnp.exp(lgf_ref[h] * jnp.maximum(rel, 0.0)),
                jnp.where(rel < 0, jnp.exp(lgb_ref[h] * jnp.maximum(-rel, 0.0)), 2.0))
        qdec[0] = jnp.exp(lgfr_ref[0] * (pos + 1.0))
        qdec[1] = jnp.exp(lgbr_ref[0] * (CHUNK - pos))

    low_head = lax.broadcasted_iota(jnp.int32, (1, PAIR_W), 1) < RET_DK

    def half_norm(o):
        def half_mean(t):
            lo = jnp.sum(jnp.where(low_head, t, 0.0), axis=-1, keepdims=True)
            hi = jnp.sum(jnp.where(low_head, 0.0, t), axis=-1, keepdims=True)
            return jnp.where(low_head, lo, hi) * (1.0 / RET_DK)

        centred = o - half_mean(o)
        return centred * lax.rsqrt(half_mean(jnp.square(centred)) + LN_EPS)

    tiles = [(slice(c * CHUNK, (c + 1) * CHUNK), slice(p * PAIR_W, (p + 1) * PAIR_W), c, p)
             for c in range(TILE // CHUNK) for p in range(RET_W // PAIR_W)]
    scores = []
    for rows, cols, c, p in tiles:
        qp = q_ref[0, rows, cols]
        q_split = jnp.concatenate([jnp.where(low_head, qp, 0.0), jnp.where(low_head, 0.0, qp)], axis=0)
        scores.append(lax.dot_general(q_split.astype(BF16), k_ref[0, rows, cols].astype(BF16),
                                      (((1,), (1,)), ((), ())), preferred_element_type=F32))

    pc = pc_ref[0]
    z = pc[:, CONV_W:2 * CONV_W] * pc[:, 2 * CONV_W:3 * CONV_W]
    line_mask = jnp.where(is_ctx, TILE - 1, GRID_W - 1)
    first = (row & line_mask) == 0
    last = (row & line_mask) == line_mask
    z_prev = jnp.where(first, 0.0, pltpu.roll(z, 1, 0))
    z_next = jnp.where(last, 0.0, pltpu.roll(z, TILE - 1, 0))
    hp = hp_ref[0]
    hn = hn_ref[0]
    z_top = jnp.where(i == 0, 0.0, hp[:, CONV_W:2 * CONV_W] * hp[:, 2 * CONV_W:3 * CONV_W])
    z_bot = jnp.where(i == n_lat_tiles - 1, 0.0, hn[:, CONV_W:2 * CONV_W] * hn[:, 2 * CONV_W:3 * CONV_W])
    zcat = jnp.concatenate([z_top, z, z_bot], axis=0)
    z_up = zcat[0:TILE]
    z_down = zcat[2 * HALO:2 * HALO + TILE]
    along_seq = lax.broadcasted_iota(jnp.int32, (1, CONV_W), 1) < jnp.where(is_ctx, CONV_W, CONV_W // 2)
    cw = convw_ref[0]
    conv = cw[0:1] * jnp.where(along_seq, z_prev, z_up) + cw[1:2] * z
    conv = conv + cw[2:3] * jnp.where(along_seq, z_next, z_down)
    ycat[:, 0:CONV_W] = (pc[:, 0:CONV_W] * conv).astype(BF16)

    outs = []
    for (rows, cols, c, p), sc in zip(tiles, scores):
        qp = q_ref[0, rows, cols]
        vp = v_ref[0, rows, cols]
        sc = sc * dec[p]
        lhs = jnp.concatenate([sc[0:CHUNK].astype(BF16), sc[CHUNK:].astype(BF16),
                               (qp * qdec[0, :, cols]).astype(BF16), (qp * qdec[1, :, cols]).astype(BF16)], axis=1)
        rhs = jnp.concatenate([jnp.where(low_head, vp, 0.0).astype(BF16), jnp.where(low_head, 0.0, vp).astype(BF16),
                               sf_ref[0, 0, c, cols, :], sb_ref[0, 0, c, cols, :]], axis=0)
        outs.append(jnp.dot(lhs, rhs, preferred_element_type=F32))

    vn = _standardize(sv_ref[0]).astype(BF16)
    group = lax.broadcasted_iota(jnp.int32, (1, SGU_W), 1) // (SGU_W // SGU_GROUPS)
    for c in range(TILE // CHUNK):
        rows = slice(c * CHUNK, (c + 1) * CHUNK)
        mixed = jnp.zeros((CHUNK, SGU_W), F32)
        for gi in range(SGU_GROUPS):
            m = jnp.dot(sguw_ref[0, gi], vn[rows], preferred_element_type=F32)
            mixed = jnp.where(group == gi, m, mixed)
        ycat[rows, CONV_W + RET_W:] = (su_ref[0, rows, :] * (mixed + sgub_ref[0])).astype(BF16)

    for (rows, cols, c, p), o in zip(tiles, outs):
        ycat[rows, CONV_W + p * PAIR_W:CONV_W + (p + 1) * PAIR_W] = (
            _silu(g_ref[0, rows, cols]) * half_norm(o)).astype(BF16)

    y = jnp.dot(ycat[...], wout_ref[0], preferred_element_type=F32)
    mod = mod_ref[0, 0, 0]
    ln = ln_ref[0]
    x1 = _standardize(ALPHA * jnp.where(is_ctx, ctx_ref[0], x_ref[0]) + mod[2:3] * y) * ln[0:1] + ln[1:2]
    x1_ref[0] = x1
    hm = x1 * (1.0 + mod[4:5]) + mod[3:4]
    hm_ref[0] = _pack_bf16_pairs(hm)
    lt = lax.dot_general(wr_ref[0], hm.astype(BF16), (((1,), (1,)), ((), ())), preferred_element_type=F32) + br_ref[0]
    route_ref[0, 0] = jnp.concatenate([_route_class(lt).astype(F32), jnp.zeros((7, TILE), F32)], axis=0)


def _mix_call(layer, n_lat_tiles, n_proc, lgf, lgb, pc, q, k, v, g, su, sv, states, x_lat, x_ctx, ctx_tile, mod, conv_w,
              sgu_w_bf, sgu_bias, w_out_bf, ln1, w_route_t, b_route_t, lg_rows, to_cast=()):
    bsz, length, _ = q.shape
    cast_in, cast_out, cast_shapes = _cast_specs(to_cast, layer, lambda b, i, *_: b * n_proc + i, bsz * n_proc)
    halos_per_tile = TILE // HALO
    n_halo = length // HALO
    cpt = TILE // CHUNK
    tok = lambda w: pl.BlockSpec((1, TILE, w), lambda b, i, *_: (b, i, 0))
    per_layer = lambda *shape: pl.BlockSpec((1,) + shape, lambda b, i, *_: (layer,) + (0,) * len(shape))
    grid_spec = pltpu.PrefetchScalarGridSpec(
        num_scalar_prefetch=2,
        grid=(bsz, n_proc),
        in_specs=[
            tok(3 * CONV_W),
            pl.BlockSpec((1, HALO, 3 * CONV_W), lambda b, i, *_: (b, jnp.maximum(i * halos_per_tile - 1, 0), 0)),
            pl.BlockSpec((1, HALO, 3 * CONV_W),
                         lambda b, i, *_: (b, jnp.minimum((i + 1) * halos_per_tile, n_halo - 1), 0)),
            tok(RET_W), tok(RET_W), tok(RET_W), tok(RET_W), tok(SGU_W), tok(SGU_W),
            pl.BlockSpec((1, 1, cpt, RET_W, PAIR_W), lambda b, i, *_: (0, b, i, 0, 0)),
            pl.BlockSpec((1, 1, cpt, RET_W, PAIR_W), lambda b, i, *_: (1, b, i, 0, 0)),
            pl.BlockSpec((1, TILE, D_MODEL), lambda b, i, *_: (b, jnp.minimum(i, n_lat_tiles - 1), 0)),
            pl.BlockSpec((1, TILE, D_MODEL), lambda b, i, *_: (b, ctx_tile, 0)),
            pl.BlockSpec((1, 1, 1, MOD_ROWS, D_MODEL),
                         lambda b, i, *_: (layer, b, jnp.minimum(i // n_lat_tiles, 1), 0, 0)),
            per_layer(8, CONV_W),
            per_layer(SGU_GROUPS, CHUNK, CHUNK),
            per_layer(CHUNK, SGU_W),
            per_layer(D_MODEL, D_MODEL),
            per_layer(8, D_MODEL),
            per_layer(ROUTE_ROWS, D_MODEL),
            per_layer(ROUTE_ROWS, 1),
            pl.BlockSpec((1, 1, RET_W), lambda b, i, *_: (2 * layer, 0, 0)),
            pl.BlockSpec((1, 1, RET_W), lambda b, i, *_: (2 * layer + 1, 0, 0)),
        ] + cast_in,
        out_specs=[tok(D_MODEL), tok(PACK_W), pl.BlockSpec((1, 1, 8, TILE), lambda b, i, *_: (b, i, 0, 0))] + cast_out,
        scratch_shapes=[pltpu.VMEM((TILE, D_MODEL), BF16),
                        pltpu.VMEM((RET_W // PAIR_W, 2 * CHUNK, CHUNK), F32),
                        pltpu.VMEM((2, CHUNK, RET_W), F32)],
    )
    return pl.pallas_call(
        functools.partial(_mix_kernel, n_lat_tiles, len(to_cast)),
        grid_spec=grid_spec,
        out_shape=[jax.ShapeDtypeStruct((bsz, n_proc * TILE, D_MODEL), F32),
                   jax.ShapeDtypeStruct((bsz, n_proc * TILE, PACK_W), jnp.uint32),
                   jax.ShapeDtypeStruct((bsz, n_proc, 8, TILE), F32)] + cast_shapes,
        compiler_params=_params(2),
    )(lgf, lgb, pc, pc, pc, q, k, v, g, su, sv, states, states, x_lat, x_ctx, mod, conv_w, sgu_w_bf, sgu_bias, w_out_bf,
      ln1, w_route_t, b_route_t, lg_rows, lg_rows, *to_cast)


def _rank_kernel(route_ref, class_ref, rank_ref, counts_ref, running):
    @pl.when(pl.program_id(0) == 0)
    def _():
        running[...] = jnp.zeros_like(running)

    sub = lax.broadcasted_iota(jnp.int32, (ROUTE_W, TILE), 0)
    earlier = (lax.broadcasted_iota(jnp.int32, (TILE, TILE), 0)
               < lax.broadcasted_iota(jnp.int32, (TILE, TILE), 1)).astype(BF16)
    for t in range(route_ref.shape[0]):
        cls = route_ref[t, 0:1, :].astype(jnp.int32)
        class_ref[t] = cls
        onehot = sub == cls
        before = jnp.dot(onehot.astype(BF16), earlier, preferred_element_type=F32) + running[...]
        rank_ref[t] = jnp.sum(jnp.where(onehot, before, 0.0), axis=0, keepdims=True).astype(jnp.int32)
        running[...] += jnp.sum(onehot.astype(F32), axis=1, keepdims=True)
    counts_ref[...] = jnp.broadcast_to(running[...], counts_ref.shape)


def _rank_call(route):
    n_tiles = route.shape[0]
    per_step = math.gcd(n_tiles, RANK_TILES)
    per_tile = pl.BlockSpec((per_step, 1, TILE), lambda i: (i, 0, 0))
    return pl.pallas_call(
        _rank_kernel,
        grid=(n_tiles // per_step,),
        in_specs=[pl.BlockSpec((per_step, 8, TILE), lambda i: (i, 0, 0))],
        out_specs=[per_tile, per_tile, pl.BlockSpec((ROUTE_W, ROUTE_W), lambda i: (0, 0))],
        out_shape=[jax.ShapeDtypeStruct((n_tiles, 1, TILE), jnp.int32), jax.ShapeDtypeStruct((n_tiles, 1, TILE), jnp.int32),
                   jax.ShapeDtypeStruct((ROUTE_W, ROUTE_W), F32)],
        scratch_shapes=[pltpu.VMEM((ROUTE_W, 1), F32)],
        compiler_params=_params(1),
    )(route)


def _dispatch_kernel(n_cast, dest_ref, pad_end_ref, pad_len_ref, n_used_ref, hm_ref, *refs):
    cast_in, xs_ref, cast_out = refs[:n_cast], refs[n_cast], refs[n_cast + 1:-3]
    zeros, sem, pad_sem = refs[-3:]
    for src, dst in zip(cast_in, cast_out):
        dst[...] = src[...].astype(BF16)
    step = pl.program_id(0)
    base = step * TILE
    half = BLOCK_M // 2

    def for_each_pad_copy(fn):
        def per_class(c, carry):
            off = pad_end_ref[c]
            n = pad_len_ref[c]
            for shift in range(BLOCK_M.bit_length() - 2, -1, -1):
                bit = 1 << shift
                off = off - (n & bit)

                @pl.when((n & bit) != 0)
                def _():
                    if bit >= 8:
                        fn(pltpu.make_async_copy(zeros.at[pl.ds(0, bit)], xs_ref.at[pl.ds(pl.multiple_of(off, 8), bit)],
                                                 pad_sem))
                    else:
                        for j in range(bit):
                            fn(pltpu.make_async_copy(zeros.at[pl.ds(0, 1)], xs_ref.at[pl.ds(off + j, 1)], pad_sem))

            return carry

        lax.fori_loop(0, N_CLASSES, per_class, 0)

        def per_half_block(j, carry):
            fn(pltpu.make_async_copy(zeros, xs_ref.at[pl.ds(pl.multiple_of(j * half, 8), half)], pad_sem))
            return carry

        lax.fori_loop(n_used_ref[0] * 2, xs_ref.shape[0] // half, per_half_block, 0)

    @pl.when(step == 0)
    def _():
        zeros[...] = jnp.zeros_like(zeros)
        for_each_pad_copy(lambda cp: cp.start())

    for r in range(TILE):
        pltpu.make_async_copy(hm_ref.at[pl.ds(r, 1)], xs_ref.at[pl.ds(dest_ref[base + r], 1)],
                              sem).start(priority=r % 2)
    for r in range(TILE):
        pltpu.make_async_copy(hm_ref.at[pl.ds(0, 1)], xs_ref.at[pl.ds(0, 1)], sem).wait()

    @pl.when(step == pl.num_programs(0) - 1)
    def _():
        for_each_pad_copy(lambda cp: cp.wait())


def _dispatch_call(layer, dest, pad_end, pad_len, n_used, hm_flat, n_slots, to_cast=()):
    n_tok = hm_flat.shape[0]
    cast_in, cast_out, cast_shapes = _cast_specs(to_cast, layer, lambda i, *_: i, n_tok // TILE)
    grid_spec = pltpu.PrefetchScalarGridSpec(
        num_scalar_prefetch=4,
        grid=(n_tok // TILE,),
        in_specs=[pl.BlockSpec((TILE, PACK_W), lambda i, *_: (i, 0))] + cast_in,
        out_specs=[pl.BlockSpec(memory_space=pl.ANY)] + cast_out,
        scratch_shapes=[pltpu.VMEM((BLOCK_M // 2, PACK_W), jnp.uint32), pltpu.SemaphoreType.DMA,
                        pltpu.SemaphoreType.DMA],
    )
    return pl.pallas_call(
        functools.partial(_dispatch_kernel, len(to_cast)),
        grid_spec=grid_spec,
        out_shape=[jax.ShapeDtypeStruct((n_slots, PACK_W), jnp.uint32)] + cast_shapes,
        compiler_params=_params(1),
    )(dest, pad_end, pad_len, n_used, hm_flat, *to_cast)


def _expert_kernel(blk_lo_ref, blk_hi_ref, blk_live_ref, n_used_ref, xs_ref, wr_ref, br_ref, wg_lo, wu_lo, wd_lo,
                   wg_hi, wu_hi, wd_hi, ys_ref):
    del n_used_ref
    i = pl.program_id(0)
    live = blk_live_ref[i]

    def experts_on(rows):
        xb = _unpack_bf16_pairs(xs_ref[0:rows, :])

        logits = jnp.dot(xb, wr_ref[0], preferred_element_type=F32) + br_ref[0]
        lane = lax.broadcasted_iota(jnp.int32, logits.shape, 1)
        gl = jnp.where(lane < N_GROUPS, logits, -jnp.inf)
        g_prob = 1.0 / jnp.sum(jnp.exp(gl - jnp.max(gl, axis=-1, keepdims=True)), axis=-1, keepdims=True)
        l_lo = jnp.sum(jnp.where(lane == N_GROUPS + blk_lo_ref[i], logits, 0.0), axis=-1, keepdims=True)
        l_hi = jnp.sum(jnp.where(lane == N_GROUPS + blk_hi_ref[i], logits, 0.0), axis=-1, keepdims=True)
        m = jnp.maximum(l_lo, l_hi)
        p_lo = jnp.exp(l_lo - m)
        p_hi = jnp.exp(l_hi - m)

        h_lo = jnp.dot(xb, wg_lo[0], preferred_element_type=F32)
        u_lo = jnp.dot(xb, wu_lo[0], preferred_element_type=F32)
        h_hi = jnp.dot(xb, wg_hi[0], preferred_element_type=F32)
        u_hi = jnp.dot(xb, wu_hi[0], preferred_element_type=F32)
        a_lo = (_silu(h_lo) * u_lo).astype(BF16)
        a_hi = (_silu(h_hi) * u_hi).astype(BF16)
        y_lo = jnp.dot(a_lo, wd_lo[0], preferred_element_type=F32)
        y_hi = jnp.dot(a_hi, wd_hi[0], preferred_element_type=F32)
        ys_ref[0:rows, :] = y_lo * (g_prob * (p_lo / (p_lo + p_hi))) + y_hi * (g_prob * (p_hi / (p_lo + p_hi)))
        if rows < BLOCK_M:
            ys_ref[rows:, :] = jnp.zeros((BLOCK_M - rows, D_MODEL), F32)

    half = BLOCK_M // 2

    @pl.when(live > half)
    def _():
        experts_on(BLOCK_M)

    @pl.when((live > 0) & (live <= half))
    def _():
        experts_on(half)

    @pl.when(live == 0)
    def _():
        ys_ref[...] = jnp.zeros_like(ys_ref)


def _expert_call(layer, blk_lo, blk_hi, blk_live, n_used, xs, w_route_bf, b_route, w_gate_bf, w_up_bf, w_down_bf):
    n_blocks = xs.shape[0] // BLOCK_M
    up_spec = lambda which: pl.BlockSpec((1, D_MODEL, EXPERT_HIDDEN), lambda i, lo, hi, lv, nu: ((lo, hi)[which][i], 0, 0))
    down_spec = lambda which: pl.BlockSpec((1, EXPERT_HIDDEN, D_MODEL), lambda i, lo, hi, lv, nu: ((lo, hi)[which][i], 0, 0))
    grid_spec = pltpu.PrefetchScalarGridSpec(
        num_scalar_prefetch=4,
        grid=(n_blocks,),
        in_specs=[pl.BlockSpec((BLOCK_M, PACK_W), lambda i, lo, hi, lv, nu: (jnp.minimum(i, nu[0] - 1), 0)),
                  pl.BlockSpec((1, D_MODEL, ROUTE_W), lambda i, *_: (layer, 0, 0)),
                  pl.BlockSpec((1, 1, ROUTE_W), lambda i, *_: (layer, 0, 0)),
                  up_spec(0), up_spec(0), down_spec(0), up_spec(1), up_spec(1), down_spec(1)],
        out_specs=pl.BlockSpec((BLOCK_M, D_MODEL), lambda i, *_: (i, 0)),
    )
    return pl.pallas_call(
        _expert_kernel,
        grid_spec=grid_spec,
        out_shape=jax.ShapeDtypeStruct((xs.shape[0], D_MODEL), F32),
        compiler_params=_params(1),
    )(blk_lo, blk_hi, blk_live, n_used, xs, w_route_bf, b_route, w_gate_bf, w_up_bf, w_down_bf, w_gate_bf, w_up_bf,
      w_down_bf)


def _combine_kernel(tiles_per_batch, dest_ref, ys_ref, x1_ref, mod_ref, ln_ref, out_ref, buf, sem):
    step = pl.program_id(0) * tiles_per_batch + pl.program_id(1)
    slot = step % 2
    has_next = step + 1 < pl.num_programs(0) * tiles_per_batch

    def gather(tile, to_slot):
        base = tile * TILE

        for r in range(TILE):
            pltpu.make_async_copy(ys_ref.at[pl.ds(dest_ref[base + r], 1)], buf.at[to_slot, pl.ds(r, 1)],
                                  sem.at[to_slot]).start(priority=r % 2)

    @pl.when(step == 0)
    def _():
        gather(step, slot)

    @pl.when(has_next)
    def _():
        gather(step + 1, 1 - slot)

    for r in range(TILE):
        pltpu.make_async_copy(ys_ref.at[pl.ds(0, 1)], buf.at[slot, pl.ds(0, 1)], sem.at[slot]).wait()

    mod = mod_ref[0, 0, 0]
    ln = ln_ref[0]
    out_ref[0] = _standardize(ALPHA * x1_ref[0] + mod[5:6] * buf[slot]) * ln[0:1] + ln[1:2]


def _combine_call(layer, n_lat_tiles, dest, ys, x1, mod, ln2):
    bsz, length, _ = x1.shape
    tiles_per_batch = length // TILE
    tok = lambda w: pl.BlockSpec((1, TILE, w), lambda b, i, *_: (b, i, 0))
    grid_spec = pltpu.PrefetchScalarGridSpec(
        num_scalar_prefetch=1,
        grid=(bsz, tiles_per_batch),
        in_specs=[
            pl.BlockSpec(memory_space=pl.ANY),
            tok(D_MODEL),
            pl.BlockSpec((1, 1, 1, MOD_ROWS, D_MODEL),
                         lambda b, i, *_: (layer, b, jnp.minimum(i // n_lat_tiles, 1), 0, 0)),
            pl.BlockSpec((1, 8, D_MODEL), lambda b, i, *_: (layer, 0, 0)),
        ],
        out_specs=tok(D_MODEL),
        scratch_shapes=[pltpu.VMEM((2, TILE, D_MODEL), F32), pltpu.SemaphoreType.DMA((2,))],
    )
    return pl.pallas_call(
        functools.partial(_combine_kernel, tiles_per_batch),
        grid_spec=grid_spec,
        out_shape=jax.ShapeDtypeStruct(x1.shape, F32),
        compiler_params=_params(2),
    )(dest, ys, x1, mod, ln2)


def _dispatch_plan(class_of_token, rank, counts):
    n_tok = class_of_token.shape[0]
    counts = counts.astype(jnp.int32)
    pcounts = (counts + BLOCK_M - 1) // BLOCK_M * BLOCK_M
    pends = jnp.cumsum(pcounts)
    pstarts = pends - pcounts
    classes = jnp.arange(ROUTE_W, dtype=jnp.int32)
    dest = jnp.sum(jnp.where(class_of_token[:, None] == classes[None, :], pstarts[None, :], 0), axis=1) + rank
    n_blocks = n_tok // BLOCK_M + N_CLASSES
    blk_start = jnp.arange(n_blocks, dtype=jnp.int32) * BLOCK_M
    blk_class = jnp.minimum(jnp.sum((pends[None, :] <= blk_start[:, None]).astype(jnp.int32), axis=1), N_CLASSES - 1)
    blk_live = jnp.clip(counts[blk_class] - (blk_start - pstarts[blk_class]), 0, BLOCK_M)
    n_used = pends[-1:] // BLOCK_M
    return (dest.astype(jnp.int32), pends, pcounts - counts, jnp.asarray(CLASS_LO)[blk_class],
            jnp.asarray(CLASS_HI)[blk_class], blk_live.astype(jnp.int32), n_used.astype(jnp.int32), n_blocks * BLOCK_M)


def _pad_rows(a, rows):
    return jnp.pad(a, [(0, 0)] * (a.ndim - 2) + [(0, rows - a.shape[-2]), (0, 0)])


def kernel(x, c, ctx, c_ctx, w_ada, b_ada, w_in, conv_w, ret_decay_fwd, ret_decay_bwd, sgu_w, sgu_b, w_out, ln1_g, ln1_b,
           router_group_w, router_group_b, router_expert_w, router_expert_b, moe_w_gate, moe_w_up, moe_w_down, ln2_g,
           ln2_b):
    bsz, seq, d = x.shape
    ctx_len = ctx.shape[1]
    assert d == D_MODEL and ctx_len == TILE and seq % TILE == 0 and seq % GRID_W == 0
    n_lat_tiles = seq // TILE
    n_lat_chunks = seq // CHUNK

    cond = _pad_rows(jnp.concatenate([c, c_ctx[None, :]], axis=0), 16)
    ada = _ada_call(cond, w_ada, b_ada)
    mod_lat = ada[:, :bsz].reshape(DEPTH, bsz, N_MOD, d)
    mod_ctx = jnp.broadcast_to(ada[:, bsz].reshape(DEPTH, 1, N_MOD, d), mod_lat.shape)
    mod = _pad_rows(jnp.stack([mod_lat, mod_ctx], axis=2), MOD_ROWS)

    w_in_bf = w_in.astype(BF16)
    w_out_bf = w_out.astype(BF16)
    sgu_w_bf = sgu_w.astype(BF16)
    sgu_bias = jnp.repeat(jnp.swapaxes(sgu_b, 1, 2), SGU_W // SGU_GROUPS, axis=2)
    conv_w8 = _pad_rows(conv_w, 8)
    ln1 = _pad_rows(jnp.stack([ln1_g, ln1_b], axis=1), 8)
    ln2 = _pad_rows(jnp.stack([ln2_g, ln2_b], axis=1), 8)
    w_route = jnp.concatenate([router_group_w, jnp.swapaxes(router_expert_w, 1, 2).reshape(DEPTH, d, N_EXPERTS)], axis=2)
    w_route_bf = jnp.pad(w_route, ((0, 0), (0, 0), (0, ROUTE_W - w_route.shape[2]))).astype(BF16)
    b_route = jnp.concatenate([router_group_b, router_expert_b.reshape(DEPTH, N_EXPERTS)], axis=1)
    b_route = jnp.pad(b_route, ((0, 0), (0, ROUTE_W - b_route.shape[1])))[:, None, :].astype(F32)
    w_route_t = jnp.concatenate([_pad_rows(jnp.swapaxes(w_route[:, :, :N_GROUPS], 1, 2), EXPERT_ROW0),
                                 jnp.swapaxes(w_route[:, :, N_GROUPS:], 1, 2)], axis=1)
    w_route_t = _pad_rows(w_route_t, ROUTE_ROWS).astype(BF16)
    b_route_t = jnp.concatenate([_pad_rows(b_route[:, 0, :N_GROUPS, None], EXPERT_ROW0),
                                 b_route[:, 0, N_GROUPS:N_GROUPS + N_EXPERTS, None]], axis=1)
    b_route_t = _pad_rows(b_route_t, ROUTE_ROWS)
    lg = jnp.stack([jax.nn.log_sigmoid(ret_decay_fwd.astype(F32)), jax.nn.log_sigmoid(ret_decay_bwd.astype(F32))],
                   axis=1)
    lg_lanes = jnp.repeat(lg, RET_DK, axis=2).reshape(DEPTH * 2, RET_W)
    lg_rows = lg_lanes[:, None, :]
    lg_cols = lg_lanes[:, :, None]

    x_lat, x_ctx, ctx_tile = x, ctx, 0
    for layer in range(DEPTH):
        last = layer == DEPTH - 1
        pc, q, k, v, g, su, sv, kv, w_gate_bf = _inproj_call(layer, x_lat, x_ctx, ctx_tile, mod, w_in_bf, lg_rows, n_lat_tiles,
                                                             to_cast=(moe_w_gate.reshape(-1, EXPERT_HIDDEN),))
        states = _scan_call(layer, kv, lg_cols, n_lat_chunks)
        n_proc = n_lat_tiles if last else n_lat_tiles + 1
        x1, hm, route, w_up_bf = _mix_call(layer, n_lat_tiles, n_proc, lg[layer, 0], lg[layer, 1], pc, q, k, v, g, su, sv,
                                           states, x_lat, x_ctx, ctx_tile, mod, conv_w8, sgu_w_bf, sgu_bias, w_out_bf, ln1,
                                           w_route_t, b_route_t, lg_rows, to_cast=(moe_w_up.reshape(-1, EXPERT_HIDDEN),))
        class_of_token, rank, counts = _rank_call(route.reshape(-1, 8, TILE))
        dest, pad_end, pad_len, blk_lo, blk_hi, blk_live, n_used, n_slots = _dispatch_plan(class_of_token.reshape(-1),
                                                                                  rank.reshape(-1), counts[:, 0])
        xs, w_down_bf = _dispatch_call(layer, dest, pad_end, pad_len, n_used, hm.reshape(-1, PACK_W), n_slots,
                                       to_cast=(moe_w_down.reshape(-1, D_MODEL),))
        ys = _expert_call(layer, blk_lo, blk_hi, blk_live, n_used, xs, w_route_bf, b_route,
                          w_gate_bf.reshape(N_EXPERTS, D_MODEL, EXPERT_HIDDEN), w_up_bf.reshape(N_EXPERTS, D_MODEL, EXPERT_HIDDEN),
                          w_down_bf.reshape(N_EXPERTS, EXPERT_HIDDEN, D_MODEL))
        xa = _combine_call(layer, n_lat_tiles, dest, ys, x1, mod, ln2)
        x_lat, x_ctx, ctx_tile = xa, xa, n_lat_tiles
    return xa
```

```python
import functools
import math

import jax
import jax.numpy as jnp
import numpy as np
from jax import lax
from jax.experimental import pallas as pl
from jax.experimental.pallas import tpu as pltpu

F32 = jnp.float32
BF16 = jnp.bfloat16

D_MODEL = 1024
DEPTH = 2
GRID_W = 64
CONV_W = 256
RET_W = 512
RET_HEADS = 8
RET_DK = 64
PAIR_W = 2 * RET_DK
CHUNK = 128
SGU_W = 256
SGU_GROUPS = 4
IN_COLS = 3 * CONV_W + 4 * RET_W + 2 * SGU_W
N_GROUPS = 4
EXPERTS_PER_GROUP = 8
N_EXPERTS = N_GROUPS * EXPERTS_PER_GROUP
EXPERT_HIDDEN = 512
N_MOD = 6
MOD_ROWS = 8
LN_EPS = 1e-5
ALPHA = (2 * DEPTH) ** 0.25

TILE = 256
HALO = GRID_W
BLOCK_M = 256
ROUTE_W = 128
CAST_STEPS = 128
PACK_W = D_MODEL // 2
ROUTE_ROWS = 48
EXPERT_ROW0 = 8
RANK_TILES = 8
PAIRS_PER_GROUP = EXPERTS_PER_GROUP * (EXPERTS_PER_GROUP - 1) // 2
N_CLASSES = N_GROUPS * PAIRS_PER_GROUP
CLASS_LO = np.array([g * EXPERTS_PER_GROUP + lo for g in range(N_GROUPS) for lo in range(EXPERTS_PER_GROUP)
                     for hi in range(lo + 1, EXPERTS_PER_GROUP)] + [N_EXPERTS - 2] * (ROUTE_W - N_CLASSES), np.int32)
CLASS_HI = np.array([g * EXPERTS_PER_GROUP + hi for g in range(N_GROUPS) for lo in range(EXPERTS_PER_GROUP)
                     for hi in range(lo + 1, EXPERTS_PER_GROUP)] + [N_EXPERTS - 1] * (ROUTE_W - N_CLASSES), np.int32)
VMEM_LIMIT = 56 * 1024 * 1024


def _params(n_axes):
    return pltpu.CompilerParams(dimension_semantics=("arbitrary",) * n_axes, vmem_limit_bytes=VMEM_LIMIT)


def _standardize(v):
    mu = jnp.mean(v, axis=-1, keepdims=True)
    var = jnp.mean(jnp.square(v - mu), axis=-1, keepdims=True)
    return (v - mu) * lax.rsqrt(var + LN_EPS)


def _silu(v):
    return v * jax.nn.sigmoid(v)


def _pack_bf16_pairs(v):
    half = v.shape[1] // 2
    bits = lambda t: pltpu.bitcast(t.astype(BF16).astype(F32), jnp.uint32)
    return bits(v[:, :half]) | (bits(v[:, half:]) >> 16)


def _unpack_bf16_pairs(w):
    hi = pltpu.bitcast(w & jnp.uint32(0xFFFF0000), F32)
    lo = pltpu.bitcast(w << 16, F32)
    return jnp.concatenate([hi, lo], axis=1).astype(BF16)


def _ada_kernel(c_ref, w_ref, b_ref, o_ref):
    a = _silu(c_ref[...]).astype(BF16)
    o_ref[0] = jnp.dot(a, w_ref[0].astype(BF16), preferred_element_type=F32) + b_ref[0]


def _ada_call(cond, w_ada, b_ada):
    rows = cond.shape[0]
    cols = w_ada.shape[-1]
    tn = 1536
    return pl.pallas_call(
        _ada_kernel,
        grid=(DEPTH, cols // tn),
        in_specs=[
            pl.BlockSpec((rows, D_MODEL), lambda l, j: (0, 0)),
            pl.BlockSpec((1, D_MODEL, tn), lambda l, j: (l, 0, j)),
            pl.BlockSpec((1, 1, tn), lambda l, j: (l, 0, j)),
        ],
        out_specs=pl.BlockSpec((1, rows, tn), lambda l, j: (l, 0, j)),
        out_shape=jax.ShapeDtypeStruct((DEPTH, rows, cols), F32),
        compiler_params=_params(2),
    )(cond, w_ada, b_ada.reshape(DEPTH, 1, cols))


def _inproj_kernel(n_lat_tiles, n_cast, x_ref, ctx_ref, mod_ref, w_ref, lgf_ref, lgb_ref, *refs):
    cast_in, (pc_ref, q_ref, k_ref, v_ref, g_ref, su_ref, sv_ref, kv_ref), cast_out = (
        refs[:n_cast], refs[n_cast:n_cast + 8], refs[n_cast + 8:])
    for src, dst in zip(cast_in, cast_out):
        dst[...] = src[...].astype(BF16)
    mod = mod_ref[0, 0, 0]
    xin = jnp.where(pl.program_id(1) >= n_lat_tiles, ctx_ref[0], x_ref[0])
    h = (xin * (1.0 + mod[1:2]) + mod[0:1]).astype(BF16)

    def proj(lo, hi):
        return jnp.dot(h, w_ref[0, :, lo:hi], preferred_element_type=F32)

    o = 3 * CONV_W
    pc_ref[0] = proj(0, o)
    q_ref[0] = proj(o, o + RET_W).astype(BF16)
    k = proj(o + RET_W, o + 2 * RET_W) * (RET_DK ** -0.5)
    k_ref[0] = k.astype(BF16)
    v = proj(o + 2 * RET_W, o + 3 * RET_W)
    v_ref[0] = v.astype(BF16)
    g_ref[0] = proj(o + 3 * RET_W, o + 4 * RET_W)
    o += 4 * RET_W
    su_ref[0] = proj(o, o + SGU_W)
    sv_ref[0] = proj(o + SGU_W, o + 2 * SGU_W)

    pos = lax.broadcasted_iota(jnp.int32, (CHUNK, 1), 0).astype(F32)
    k_decay = (jnp.exp((CHUNK - 1.0 - pos) * lgf_ref[0]), jnp.exp(pos * lgb_ref[0]))
    contract_rows = (((0,), (0,)), ((), ()))
    for c in range(TILE // CHUNK):
        rows = slice(c * CHUNK, (c + 1) * CHUNK)
        vc = v[rows].astype(BF16)
        for d in range(2):
            kc = (k[rows] * k_decay[d]).astype(BF16)
            for p in range(RET_W // PAIR_W):
                cols = slice(p * PAIR_W, (p + 1) * PAIR_W)
                kv_ref[d, 0, c, cols, :] = lax.dot_general(kc[:, cols], vc[:, cols], contract_rows,
                                                           preferred_element_type=F32)


def _cast_specs(arrays, part, step_of, n_steps):
    steps = min(CAST_STEPS, 1 << (n_steps.bit_length() - 1))
    block = lambda a: (a.shape[0] // (DEPTH * steps), a.shape[1])
    slab = lambda *idx: jnp.minimum(step_of(*idx), steps - 1)
    return ([pl.BlockSpec(block(a), lambda *idx: (part * steps + slab(*idx), 0)) for a in arrays],
            [pl.BlockSpec(block(a), lambda *idx: (slab(*idx), 0)) for a in arrays],
            [jax.ShapeDtypeStruct((a.shape[0] // DEPTH, a.shape[1]), BF16) for a in arrays])


def _inproj_call(layer, x_lat, x_ctx, ctx_tile, mod, w_in_bf, lg_rows, n_lat_tiles, to_cast=()):
    bsz = x_lat.shape[0]
    length = (n_lat_tiles + 1) * TILE
    widths = (3 * CONV_W, RET_W, RET_W, RET_W, RET_W, SGU_W, SGU_W)
    cpt = TILE // CHUNK
    cast_in, cast_out, cast_shapes = _cast_specs(to_cast, layer, lambda b, i: b * (n_lat_tiles + 1) + i,
                                                  bsz * (n_lat_tiles + 1))
    return pl.pallas_call(
        functools.partial(_inproj_kernel, n_lat_tiles, len(to_cast)),
        grid=(bsz, length // TILE),
        in_specs=[
            pl.BlockSpec((1, TILE, D_MODEL), lambda b, i: (b, jnp.minimum(i, n_lat_tiles - 1), 0)),
            pl.BlockSpec((1, TILE, D_MODEL), lambda b, i: (b, ctx_tile, 0)),
            pl.BlockSpec((1, 1, 1, MOD_ROWS, D_MODEL), lambda b, i: (layer, b, jnp.minimum(i // n_lat_tiles, 1), 0, 0)),
            pl.BlockSpec((1, D_MODEL, IN_COLS), lambda b, i: (layer, 0, 0)),
            pl.BlockSpec((1, 1, RET_W), lambda b, i: (2 * layer, 0, 0)),
            pl.BlockSpec((1, 1, RET_W), lambda b, i: (2 * layer + 1, 0, 0)),
        ] + cast_in,
        out_specs=[pl.BlockSpec((1, TILE, w), lambda b, i: (b, i, 0)) for w in widths]
        + [pl.BlockSpec((2, 1, cpt, RET_W, PAIR_W), lambda b, i: (0, b, i, 0, 0))] + cast_out,
        out_shape=[jax.ShapeDtypeStruct((bsz, length, w), BF16 if 1 <= j <= 3 else F32) for j, w in enumerate(widths)]
        + [jax.ShapeDtypeStruct((2, bsz, length // CHUNK, RET_W, PAIR_W), F32)] + cast_shapes,
        compiler_params=_params(2),
    )(x_lat, x_ctx, mod, w_in_bf, lg_rows, lg_rows, *to_cast)


def _scan_kernel(n_lat_chunks, kv_ref, lgc_ref, s_ref, state):
    nc = kv_ref.shape[2]
    backward = pl.program_id(1)
    decay = jnp.exp(CHUNK * lgc_ref[0])
    row_head = lax.broadcasted_iota(jnp.int32, (RET_W, PAIR_W), 0) % PAIR_W // RET_DK
    lane_head = lax.broadcasted_iota(jnp.int32, (RET_W, PAIR_W), 1) // RET_DK
    own_head = row_head == lane_head
    state[...] = jnp.zeros_like(state)

    def step(s, carry):
        c = jnp.where(backward == 1, nc - 1 - s, (s + n_lat_chunks) % nc)
        s_ref[0, 0, c] = jnp.where(own_head, state[...], 0.0).astype(BF16)
        state[...] = decay * state[...] + kv_ref[0, 0, c]
        return carry

    lax.fori_loop(0, nc, step, 0)


def _scan_call(layer, kv, lg_cols, n_lat_chunks):
    _, bsz, nc, _, _ = kv.shape
    blk = (1, 1, nc, RET_W, PAIR_W)
    return pl.pallas_call(
        functools.partial(_scan_kernel, n_lat_chunks),
        grid=(bsz, 2),
        in_specs=[pl.BlockSpec(blk, lambda b, d: (d, b, 0, 0, 0)),
                  pl.BlockSpec((1, RET_W, 1), lambda b, d: (2 * layer + d, 0, 0))],
        out_specs=pl.BlockSpec(blk, lambda b, d: (d, b, 0, 0, 0)),
        out_shape=jax.ShapeDtypeStruct(kv.shape, BF16),
        scratch_shapes=[pltpu.VMEM((RET_W, PAIR_W), F32)],
        compiler_params=_params(2),
    )(kv, lg_cols)


def _route_class(lt):
    assert EXPERTS_PER_GROUP == 8 and N_GROUPS <= 8
    sub = lax.broadcasted_iota(jnp.int32, (8, lt.shape[1]), 0)
    neg = jnp.float32(-jnp.inf)

    def top(vals):
        m = jnp.max(vals, axis=0, keepdims=True)
        return jnp.min(jnp.where(vals == m, sub, 8), axis=0, keepdims=True)

    gidx = top(jnp.where(sub < N_GROUPS, lt[0:8], neg))
    pair = jnp.zeros_like(gidx)
    for g in range(N_GROUPS):
        e = lt[EXPERT_ROW0 + 8 * g:EXPERT_ROW0 + 8 * (g + 1)]
        i1 = top(e)
        i2 = top(jnp.where(sub == i1, neg, e))
        e_lo = jnp.minimum(i1, i2)
        e_hi = jnp.maximum(i1, i2)
        pair_g = e_lo * (EXPERTS_PER_GROUP - 1) - ((e_lo * (e_lo - 1)) >> 1) + (e_hi - e_lo - 1)
        pair = jnp.where(gidx == g, pair_g, pair)
    return gidx * PAIRS_PER_GROUP + pair


def _mix_kernel(n_lat_tiles, n_cast, lgf_ref, lgb_ref, pc_ref, hp_ref, hn_ref, q_ref, k_ref, v_ref, g_ref, su_ref, sv_ref,
                sf_ref, sb_ref, x_ref, ctx_ref, mod_ref, convw_ref, sguw_ref, sgub_ref, wout_ref, ln_ref, wr_ref, br_ref,
                lgfr_ref, lgbr_ref, *refs):
    cast_in, (x1_ref, hm_ref, route_ref), cast_out = refs[:n_cast], refs[n_cast:n_cast + 3], refs[n_cast + 3:-3]
    ycat, dec, qdec = refs[-3:]
    for src, dst in zip(cast_in, cast_out):
        dst[...] = src[...].astype(BF16)
    i = pl.program_id(1)
    is_ctx = i >= n_lat_tiles
    row = lax.broadcasted_iota(jnp.int32, (TILE, 1), 0)

    @pl.when((pl.program_id(0) == 0) & (i == 0))
    def _():
        pos = lax.broadcasted_iota(jnp.int32, (CHUNK, 1), 0).astype(F32)
        rel = pos - lax.broadcasted_iota(jnp.int32, (1, CHUNK), 1).astype(F32)
        for h in range(RET_HEADS):
            dec[h // 2, (h % 2) * CHUNK:(h % 2 + 1) * CHUNK, :] = jnp.where(
                rel > 0, jnp.exp(lgf_ref[h] * jnp.maximum(rel, 0.0)),
                jnp.where(rel < 0, jnp.exp(lgb_ref[h] * jnp.maximum(-rel, 0.0)), 2.0))
        qdec[0] = jnp.exp(lgfr_ref[0] * (pos + 1.0))
        qdec[1] = jnp.exp(lgbr_ref[0] * (CHUNK - pos))

    low_head = lax.broadcasted_iota(jnp.int32, (1, PAIR_W), 1) < RET_DK

    def half_norm(o):
        def half_mean(t):
            lo = jnp.sum(jnp.where(low_head, t, 0.0), axis=-1, keepdims=True)
            hi = jnp.sum(jnp.where(low_head, 0.0, t), axis=-1, keepdims=True)
            return jnp.where(low_head, lo, hi) * (1.0 / RET_DK)

        centred = o - half_mean(o)
        return centred * lax.rsqrt(half_mean(jnp.square(centred)) + LN_EPS)

    tiles = [(slice(c * CHUNK, (c + 1) * CHUNK), slice(p * PAIR_W, (p + 1) * PAIR_W), c, p)
             for c in range(TILE // CHUNK) for p in range(RET_W // PAIR_W)]
    scores = []
    for rows, cols, c, p in tiles:
        qp = q_ref[0, rows, cols]
        zero = jnp.zeros_like(qp)
        q_split = jnp.concatenate([jnp.where(low_head, qp, zero), jnp.where(low_head, zero, qp)], axis=0)
        scores.append(lax.dot_general(q_split, k_ref[0, rows, cols],
                                      (((1,), (1,)), ((), ())), preferred_element_type=F32))

    pc = pc_ref[0]
    z = pc[:, CONV_W:2 * CONV_W] * pc[:, 2 * CONV_W:3 * CONV_W]
    line_mask = jnp.where(is_ctx, TILE - 1, GRID_W - 1)
    first = (row & line_mask) == 0
    last = (row & line_mask) == line_mask
    z_prev = jnp.where(first, 0.0, pltpu.roll(z, 1, 0))
    z_next = jnp.where(last, 0.0, pltpu.roll(z, TILE - 1, 0))
    hp = hp_ref[0]
    hn = hn_ref[0]
    z_top = jnp.where(i == 0, 0.0, hp[:, CONV_W:2 * CONV_W] * hp[:, 2 * CONV_W:3 * CONV_W])
    z_bot = jnp.where(i == n_lat_tiles - 1, 0.0, hn[:, CONV_W:2 * CONV_W] * hn[:, 2 * CONV_W:3 * CONV_W])
    zcat = jnp.concatenate([z_top, z, z_bot], axis=0)
    z_up = zcat[0:TILE]
    z_down = zcat[2 * HALO:2 * HALO + TILE]
    along_seq = lax.broadcasted_iota(jnp.int32, (1, CONV_W), 1) < jnp.where(is_ctx, CONV_W, CONV_W // 2)
    cw = convw_ref[0]
    conv = cw[0:1] * jnp.where(along_seq, z_prev, z_up) + cw[1:2] * z
    conv = conv + cw[2:3] * jnp.where(along_seq, z_next, z_down)
    ycat[:, 0:CONV_W] = (pc[:, 0:CONV_W] * conv).astype(BF16)

    outs = []
    for (rows, cols, c, p), sc in zip(tiles, scores):
        qp = q_ref[0, rows, cols].astype(F32)
        vp = v_ref[0, rows, cols]
        zero = jnp.zeros_like(vp)
        sc = sc * dec[p]
        lhs = jnp.concatenate([sc[0:CHUNK].astype(BF16), sc[CHUNK:].astype(BF16),
                               (qp * qdec[0, :, cols]).astype(BF16), (qp * qdec[1, :, cols]).astype(BF16)], axis=1)
        rhs = jnp.concatenate([jnp.where(low_head, vp, zero), jnp.where(low_head, zero, vp),
                               sf_ref[0, 0, c, cols, :], sb_ref[0, 0, c, cols, :]], axis=0)
        outs.append(jnp.dot(lhs, rhs, preferred_element_type=F32))

    vn = _standardize(sv_ref[0]).astype(BF16)
    group = lax.broadcasted_iota(jnp.int32, (1, SGU_W), 1) // (SGU_W // SGU_GROUPS)
    for c in range(TILE // CHUNK):
        rows = slice(c * CHUNK, (c + 1) * CHUNK)
        mixed = jnp.zeros((CHUNK, SGU_W), F32)
        for gi in range(SGU_GROUPS):
            m = jnp.dot(sguw_ref[0, gi], vn[rows], preferred_element_type=F32)
            mixed = jnp.where(group == gi, m, mixed)
        ycat[rows, CONV_W + RET_W:] = (su_ref[0, rows, :] * (mixed + sgub_ref[0])).astype(BF16)

    for (rows, cols, c, p), o in zip(tiles, outs):
        ycat[rows, CONV_W + p * PAIR_W:CONV_W + (p + 1) * PAIR_W] = (
            _silu(g_ref[0, rows, cols]) * half_norm(o)).astype(BF16)

    y = jnp.dot(ycat[...], wout_ref[0], preferred_element_type=F32)
    mod = mod_ref[0, 0, 0]
    ln = ln_ref[0]
    x1 = _standardize(ALPHA * jnp.where(is_ctx, ctx_ref[0], x_ref[0]) + mod[2:3] * y) * ln[0:1] + ln[1:2]
    x1_ref[0] = x1
    hm = x1 * (1.0 + mod[4:5]) + mod[3:4]
    hm_ref[0] = _pack_bf16_pairs(hm)
    lt = lax.dot_general(wr_ref[0], hm.astype(BF16), (((1,), (1,)), ((), ())), preferred_element_type=F32) + br_ref[0]
    route_ref[0, 0] = jnp.concatenate([_route_class(lt).astype(F32), jnp.zeros((7, TILE), F32)], axis=0)


def _mix_call(layer, n_lat_tiles, n_proc, lgf, lgb, pc, q, k, v, g, su, sv, states, x_lat, x_ctx, ctx_tile, mod, conv_w,
              sgu_w_bf, sgu_bias, w_out_bf, ln1, w_route_t, b_route_t, lg_rows, to_cast=()):
    bsz, length, _ = q.shape
    cast_in, cast_out, cast_shapes = _cast_specs(to_cast, layer, lambda b, i, *_: b * n_proc + i, bsz * n_proc)
    halos_per_tile = TILE // HALO
    n_halo = length // HALO
    cpt = TILE // CHUNK
    tok = lambda w: pl.BlockSpec((1, TILE, w), lambda b, i, *_: (b, i, 0))
    per_layer = lambda *shape: pl.BlockSpec((1,) + shape, lambda b, i, *_: (layer,) + (0,) * len(shape))
    grid_spec = pltpu.PrefetchScalarGridSpec(
        num_scalar_prefetch=2,
        grid=(bsz, n_proc),
        in_specs=[
            tok(3 * CONV_W),
            pl.BlockSpec((1, HALO, 3 * CONV_W), lambda b, i, *_: (b, jnp.maximum(i * halos_per_tile - 1, 0), 0)),
            pl.BlockSpec((1, HALO, 3 * CONV_W),
                         lambda b, i, *_: (b, jnp.minimum((i + 1) * halos_per_tile, n_halo - 1), 0)),
            tok(RET_W), tok(RET_W), tok(RET_W), tok(RET_W), tok(SGU_W), tok(SGU_W),
            pl.BlockSpec((1, 1, cpt, RET_W, PAIR_W), lambda b, i, *_: (0, b, i, 0, 0)),
            pl.BlockSpec((1, 1, cpt, RET_W, PAIR_W), lambda b, i, *_: (1, b, i, 0, 0)),
            pl.BlockSpec((1, TILE, D_MODEL), lambda b, i, *_: (b, jnp.minimum(i, n_lat_tiles - 1), 0)),
            pl.BlockSpec((1, TILE, D_MODEL), lambda b, i, *_: (b, ctx_tile, 0)),
            pl.BlockSpec((1, 1, 1, MOD_ROWS, D_MODEL),
                         lambda b, i, *_: (layer, b, jnp.minimum(i // n_lat_tiles, 1), 0, 0)),
            per_layer(8, CONV_W),
            per_layer(SGU_GROUPS, CHUNK, CHUNK),
            per_layer(CHUNK, SGU_W),
            per_layer(D_MODEL, D_MODEL),
            per_layer(8, D_MODEL),
            per_layer(ROUTE_ROWS, D_MODEL),
            per_layer(ROUTE_ROWS, 1),
            pl.BlockSpec((1, 1, RET_W), lambda b, i, *_: (2 * layer, 0, 0)),
            pl.BlockSpec((1, 1, RET_W), lambda b, i, *_: (2 * layer + 1, 0, 0)),
        ] + cast_in,
        out_specs=[tok(D_MODEL), tok(PACK_W), pl.BlockSpec((1, 1, 8, TILE), lambda b, i, *_: (b, i, 0, 0))] + cast_out,
        scratch_shapes=[pltpu.VMEM((TILE, D_MODEL), BF16),
                        pltpu.VMEM((RET_W // PAIR_W, 2 * CHUNK, CHUNK), F32),
                        pltpu.VMEM((2, CHUNK, RET_W), F32)],
    )
    return pl.pallas_call(
        functools.partial(_mix_kernel, n_lat_tiles, len(to_cast)),
        grid_spec=grid_spec,
        out_shape=[jax.ShapeDtypeStruct((bsz, n_proc * TILE, D_MODEL), F32),
                   jax.ShapeDtypeStruct((bsz, n_proc * TILE, PACK_W), jnp.uint32),
                   jax.ShapeDtypeStruct((bsz, n_proc, 8, TILE), F32)] + cast_shapes,
        compiler_params=_params(2),
    )(lgf, lgb, pc, pc, pc, q, k, v, g, su, sv, states, states, x_lat, x_ctx, mod, conv_w, sgu_w_bf, sgu_bias, w_out_bf,
      ln1, w_route_t, b_route_t, lg_rows, lg_rows, *to_cast)


def _rank_kernel(route_ref, class_ref, rank_ref, counts_ref, running):
    @pl.when(pl.program_id(0) == 0)
    def _():
        running[...] = jnp.zeros_like(running)

    sub = lax.broadcasted_iota(jnp.int32, (ROUTE_W, TILE), 0)
    earlier = (lax.broadcasted_iota(jnp.int32, (TILE, TILE), 0)
               < lax.broadcasted_iota(jnp.int32, (TILE, TILE), 1)).astype(BF16)
    for t in range(route_ref.shape[0]):
        cls = route_ref[t, 0:1, :].astype(jnp.int32)
        class_ref[t] = cls
        onehot = sub == cls
        before = jnp.dot(onehot.astype(BF16), earlier, preferred_element_type=F32) + running[...]
        rank_ref[t] = jnp.sum(jnp.where(onehot, before, 0.0), axis=0, keepdims=True).astype(jnp.int32)
        running[...] += jnp.sum(onehot.astype(F32), axis=1, keepdims=True)
    counts_ref[...] = jnp.broadcast_to(running[...], counts_ref.shape)


def _rank_call(route):
    n_tiles = route.shape[0]
    per_step = math.gcd(n_tiles, RANK_TILES)
    per_tile = pl.BlockSpec((per_step, 1, TILE), lambda i: (i, 0, 0))
    return pl.pallas_call(
        _rank_kernel,
        grid=(n_tiles // per_step,),
        in_specs=[pl.BlockSpec((per_step, 8, TILE), lambda i: (i, 0, 0))],
        out_specs=[per_tile, per_tile, pl.BlockSpec((ROUTE_W, ROUTE_W), lambda i: (0, 0))],
        out_shape=[jax.ShapeDtypeStruct((n_tiles, 1, TILE), jnp.int32), jax.ShapeDtypeStruct((n_tiles, 1, TILE), jnp.int32),
                   jax.ShapeDtypeStruct((ROUTE_W, ROUTE_W), F32)],
        scratch_shapes=[pltpu.VMEM((ROUTE_W, 1), F32)],
        compiler_params=_params(1),
    )(route)


def _dispatch_kernel(n_cast, dest_ref, pad_end_ref, pad_len_ref, n_used_ref, hm_ref, *refs):
    cast_in, xs_ref, cast_out = refs[:n_cast], refs[n_cast], refs[n_cast + 1:-3]
    zeros, sem, pad_sem = refs[-3:]
    for src, dst in zip(cast_in, cast_out):
        dst[...] = src[...].astype(BF16)
    step = pl.program_id(0)
    base = step * TILE
    half = BLOCK_M // 2

    def for_each_pad_copy(fn):
        def per_class(c, carry):
            off = pad_end_ref[c]
            n = pad_len_ref[c]
            for shift in range(BLOCK_M.bit_length() - 2, -1, -1):
                bit = 1 << shift
                off = off - (n & bit)

                @pl.when((n & bit) != 0)
                def _():
                    if bit >= 8:
                        fn(pltpu.make_async_copy(zeros.at[pl.ds(0, bit)], xs_ref.at[pl.ds(pl.multiple_of(off, 8), bit)],
                                                 pad_sem))
                    else:
                        for j in range(bit):
                            fn(pltpu.make_async_copy(zeros.at[pl.ds(0, 1)], xs_ref.at[pl.ds(off + j, 1)], pad_sem))

            return carry

        lax.fori_loop(0, N_CLASSES, per_class, 0)

        def per_half_block(j, carry):
            fn(pltpu.make_async_copy(zeros, xs_ref.at[pl.ds(pl.multiple_of(j * half, 8), half)], pad_sem))
            return carry

        lax.fori_loop(n_used_ref[0] * 2, xs_ref.shape[0] // half, per_half_block, 0)

    @pl.when(step == 0)
    def _():
        zeros[...] = jnp.zeros_like(zeros)
        for_each_pad_copy(lambda cp: cp.start())

    for r in range(TILE):
        pltpu.make_async_copy(hm_ref.at[pl.ds(r, 1)], xs_ref.at[pl.ds(dest_ref[base + r], 1)],
                              sem).start(priority=r % 2)
    for r in range(TILE):
        pltpu.make_async_copy(hm_ref.at[pl.ds(0, 1)], xs_ref.at[pl.ds(0, 1)], sem).wait()

    @pl.when(step == pl.num_programs(0) - 1)
    def _():
        for_each_pad_copy(lambda cp: cp.wait())


def _dispatch_call(layer, dest, pad_end, pad_len, n_used, hm_flat, n_slots, to_cast=()):
    n_tok = hm_flat.shape[0]
    cast_in, cast_out, cast_shapes = _cast_specs(to_cast, layer, lambda i, *_: i, n_tok // TILE)
    grid_spec = pltpu.PrefetchScalarGridSpec(
        num_scalar_prefetch=4,
        grid=(n_tok // TILE,),
        in_specs=[pl.BlockSpec((TILE, PACK_W), lambda i, *_: (i, 0))] + cast_in,
        out_specs=[pl.BlockSpec(memory_space=pl.ANY)] + cast_out,
        scratch_shapes=[pltpu.VMEM((BLOCK_M // 2, PACK_W), jnp.uint32), pltpu.SemaphoreType.DMA,
                        pltpu.SemaphoreType.DMA],
    )
    return pl.pallas_call(
        functools.partial(_dispatch_kernel, len(to_cast)),
        grid_spec=grid_spec,
        out_shape=[jax.ShapeDtypeStruct((n_slots, PACK_W), jnp.uint32)] + cast_shapes,
        compiler_params=_params(1),
    )(dest, pad_end, pad_len, n_used, hm_flat, *to_cast)


def _expert_kernel(blk_lo_ref, blk_hi_ref, blk_live_ref, n_used_ref, xs_ref, wr_ref, br_ref, wg_lo, wu_lo, wd_lo,
                   wg_hi, wu_hi, wd_hi, ys_ref):
    del n_used_ref
    i = pl.program_id(0)
    live = blk_live_ref[i]

    def experts_on(rows):
        xb = _unpack_bf16_pairs(xs_ref[0:rows, :])

        logits = jnp.dot(xb, wr_ref[0], preferred_element_type=F32) + br_ref[0]
        lane = lax.broadcasted_iota(jnp.int32, logits.shape, 1)
        gl = jnp.where(lane < N_GROUPS, logits, -jnp.inf)
        g_prob = 1.0 / jnp.sum(jnp.exp(gl - jnp.max(gl, axis=-1, keepdims=True)), axis=-1, keepdims=True)
        l_lo = jnp.sum(jnp.where(lane == N_GROUPS + blk_lo_ref[i], logits, 0.0), axis=-1, keepdims=True)
        l_hi = jnp.sum(jnp.where(lane == N_GROUPS + blk_hi_ref[i], logits, 0.0), axis=-1, keepdims=True)
        m = jnp.maximum(l_lo, l_hi)
        p_lo = jnp.exp(l_lo - m)
        p_hi = jnp.exp(l_hi - m)

        h_lo = jnp.dot(xb, wg_lo[0], preferred_element_type=F32)
        u_lo = jnp.dot(xb, wu_lo[0], preferred_element_type=F32)
        h_hi = jnp.dot(xb, wg_hi[0], preferred_element_type=F32)
        u_hi = jnp.dot(xb, wu_hi[0], preferred_element_type=F32)
        a_lo = (_silu(h_lo) * u_lo).astype(BF16)
        a_hi = (_silu(h_hi) * u_hi).astype(BF16)
        y_lo = jnp.dot(a_lo, wd_lo[0], preferred_element_type=F32)
        y_hi = jnp.dot(a_hi, wd_hi[0], preferred_element_type=F32)
        ys_ref[0:rows, :] = y_lo * (g_prob * (p_lo / (p_lo + p_hi))) + y_hi * (g_prob * (p_hi / (p_lo + p_hi)))
        if rows < BLOCK_M:
            ys_ref[rows:, :] = jnp.zeros((BLOCK_M - rows, D_MODEL), F32)

    half = BLOCK_M // 2

    @pl.when(live > half)
    def _():
        experts_on(BLOCK_M)

    @pl.when((live > 0) & (live <= half))
    def _():
        experts_on(half)

    @pl.when(live == 0)
    def _():
        ys_ref[...] = jnp.zeros_like(ys_ref)


def _expert_call(layer, blk_lo, blk_hi, blk_live, n_used, xs, w_route_bf, b_route, w_gate_bf, w_up_bf, w_down_bf):
    n_blocks = xs.shape[0] // BLOCK_M
    up_spec = lambda which: pl.BlockSpec((1, D_MODEL, EXPERT_HIDDEN), lambda i, lo, hi, lv, nu: ((lo, hi)[which][i], 0, 0))
    down_spec = lambda which: pl.BlockSpec((1, EXPERT_HIDDEN, D_MODEL), lambda i, lo, hi, lv, nu: ((lo, hi)[which][i], 0, 0))
    grid_spec = pltpu.PrefetchScalarGridSpec(
        num_scalar_prefetch=4,
        grid=(n_blocks,),
        in_specs=[pl.BlockSpec((BLOCK_M, PACK_W), lambda i, lo, hi, lv, nu: (jnp.minimum(i, nu[0] - 1), 0)),
                  pl.BlockSpec((1, D_MODEL, ROUTE_W), lambda i, *_: (layer, 0, 0)),
                  pl.BlockSpec((1, 1, ROUTE_W), lambda i, *_: (layer, 0, 0)),
                  up_spec(0), up_spec(0), down_spec(0), up_spec(1), up_spec(1), down_spec(1)],
        out_specs=pl.BlockSpec((BLOCK_M, D_MODEL), lambda i, *_: (i, 0)),
    )
    return pl.pallas_call(
        _expert_kernel,
        grid_spec=grid_spec,
        out_shape=jax.ShapeDtypeStruct((xs.shape[0], D_MODEL), F32),
        compiler_params=_params(1),
    )(blk_lo, blk_hi, blk_live, n_used, xs, w_route_bf, b_route, w_gate_bf, w_up_bf, w_down_bf, w_gate_bf, w_up_bf,
      w_down_bf)


def _combine_kernel(tiles_per_batch, dest_ref, ys_ref, x1_ref, mod_ref, ln_ref, out_ref, buf, sem):
    step = pl.program_id(0) * tiles_per_batch + pl.program_id(1)
    slot = step % 2
    has_next = step + 1 < pl.num_programs(0) * tiles_per_batch

    def gather(tile, to_slot):
        base = tile * TILE

        for r in range(TILE):
            pltpu.make_async_copy(ys_ref.at[pl.ds(dest_ref[base + r], 1)], buf.at[to_slot, pl.ds(r, 1)],
                                  sem.at[to_slot]).start(priority=r % 2)

    @pl.when(step == 0)
    def _():
        gather(step, slot)

    @pl.when(has_next)
    def _():
        gather(step + 1, 1 - slot)

    for r in range(TILE):
        pltpu.make_async_copy(ys_ref.at[pl.ds(0, 1)], buf.at[slot, pl.ds(0, 1)], sem.at[slot]).wait()

    mod = mod_ref[0, 0, 0]
    ln = ln_ref[0]
    out_ref[0] = _standardize(ALPHA * x1_ref[0] + mod[5:6] * buf[slot]) * ln[0:1] + ln[1:2]


def _combine_call(layer, n_lat_tiles, dest, ys, x1, mod, ln2):
    bsz, length, _ = x1.shape
    tiles_per_batch = length // TILE
    tok = lambda w: pl.BlockSpec((1, TILE, w), lambda b, i, *_: (b, i, 0))
    grid_spec = pltpu.PrefetchScalarGridSpec(
        num_scalar_prefetch=1,
        grid=(bsz, tiles_per_batch),
        in_specs=[
            pl.BlockSpec(memory_space=pl.ANY),
            tok(D_MODEL),
            pl.BlockSpec((1, 1, 1, MOD_ROWS, D_MODEL),
                         lambda b, i, *_: (layer, b, jnp.minimum(i // n_lat_tiles, 1), 0, 0)),
            pl.BlockSpec((1, 8, D_MODEL), lambda b, i, *_: (layer, 0, 0)),
        ],
        out_specs=tok(D_MODEL),
        scratch_shapes=[pltpu.VMEM((2, TILE, D_MODEL), F32), pltpu.SemaphoreType.DMA((2,))],
    )
    return pl.pallas_call(
        functools.partial(_combine_kernel, tiles_per_batch),
        grid_spec=grid_spec,
        out_shape=jax.ShapeDtypeStruct(x1.shape, F32),
        compiler_params=_params(2),
    )(dest, ys, x1, mod, ln2)


def _dispatch_plan(class_of_token, rank, counts):
    n_tok = class_of_token.shape[0]
    counts = counts.astype(jnp.int32)
    pcounts = (counts + BLOCK_M - 1) // BLOCK_M * BLOCK_M
    pends = jnp.cumsum(pcounts)
    pstarts = pends - pcounts
    classes = jnp.arange(ROUTE_W, dtype=jnp.int32)
    dest = jnp.sum(jnp.where(class_of_token[:, None] == classes[None, :], pstarts[None, :], 0), axis=1) + rank
    n_blocks = n_tok // BLOCK_M + N_CLASSES
    blk_start = jnp.arange(n_blocks, dtype=jnp.int32) * BLOCK_M
    blk_class = jnp.minimum(jnp.sum((pends[None, :] <= blk_start[:, None]).astype(jnp.int32), axis=1), N_CLASSES - 1)
    blk_live = jnp.clip(counts[blk_class] - (blk_start - pstarts[blk_class]), 0, BLOCK_M)
    n_used = pends[-1:] // BLOCK_M
    return (dest.astype(jnp.int32), pends, pcounts - counts, jnp.asarray(CLASS_LO)[blk_class],
            jnp.asarray(CLASS_HI)[blk_class], blk_live.astype(jnp.int32), n_used.astype(jnp.int32), n_blocks * BLOCK_M)


def _pad_rows(a, rows):
    return jnp.pad(a, [(0, 0)] * (a.ndim - 2) + [(0, rows - a.shape[-2]), (0, 0)])


def kernel(x, c, ctx, c_ctx, w_ada, b_ada, w_in, conv_w, ret_decay_fwd, ret_decay_bwd, sgu_w, sgu_b, w_out, ln1_g, ln1_b,
           router_group_w, router_group_b, router_expert_w, router_expert_b, moe_w_gate, moe_w_up, moe_w_down, ln2_g,
           ln2_b):
    bsz, seq, d = x.shape
    ctx_len = ctx.shape[1]
    assert d == D_MODEL and ctx_len == TILE and seq % TILE == 0 and seq % GRID_W == 0
    n_lat_tiles = seq // TILE
    n_lat_chunks = seq // CHUNK

    cond = _pad_rows(jnp.concatenate([c, c_ctx[None, :]], axis=0), 16)
    ada = _ada_call(cond, w_ada, b_ada)
    mod_lat = ada[:, :bsz].reshape(DEPTH, bsz, N_MOD, d)
    mod_ctx = jnp.broadcast_to(ada[:, bsz].reshape(DEPTH, 1, N_MOD, d), mod_lat.shape)
    mod = _pad_rows(jnp.stack([mod_lat, mod_ctx], axis=2), MOD_ROWS)

    w_in_bf = w_in.astype(BF16)
    w_out_bf = w_out.astype(BF16)
    sgu_w_bf = sgu_w.astype(BF16)
    sgu_bias = jnp.repeat(jnp.swapaxes(sgu_b, 1, 2), SGU_W // SGU_GROUPS, axis=2)
    conv_w8 = _pad_rows(conv_w, 8)
    ln1 = _pad_rows(jnp.stack([ln1_g, ln1_b], axis=1), 8)
    ln2 = _pad_rows(jnp.stack([ln2_g, ln2_b], axis=1), 8)
    w_route = jnp.concatenate([router_group_w, jnp.swapaxes(router_expert_w, 1, 2).reshape(DEPTH, d, N_EXPERTS)], axis=2)
    w_route_bf = jnp.pad(w_route, ((0, 0), (0, 0), (0, ROUTE_W - w_route.shape[2]))).astype(BF16)
    b_route = jnp.concatenate([router_group_b, router_expert_b.reshape(DEPTH, N_EXPERTS)], axis=1)
    b_route = jnp.pad(b_route, ((0, 0), (0, ROUTE_W - b_route.shape[1])))[:, None, :].astype(F32)
    w_route_t = jnp.concatenate([_pad_rows(jnp.swapaxes(w_route[:, :, :N_GROUPS], 1, 2), EXPERT_ROW0),
                                 jnp.swapaxes(w_route[:, :, N_GROUPS:], 1, 2)], axis=1)
    w_route_t = _pad_rows(w_route_t, ROUTE_ROWS).astype(BF16)
    b_route_t = jnp.concatenate([_pad_rows(b_route[:, 0, :N_GROUPS, None], EXPERT_ROW0),
                                 b_route[:, 0, N_GROUPS:N_GROUPS + N_EXPERTS, None]], axis=1)
    b_route_t = _pad_rows(b_route_t, ROUTE_ROWS)
    lg = jnp.stack([jax.nn.log_sigmoid(ret_decay_fwd.astype(F32)), jax.nn.log_sigmoid(ret_decay_bwd.astype(F32))],
                   axis=1)
    lg_lanes = jnp.repeat(lg, RET_DK, axis=2).reshape(DEPTH * 2, RET_W)
    lg_rows = lg_lanes[:, None, :]
    lg_cols = lg_lanes[:, :, None]

    x_lat, x_ctx, ctx_tile = x, ctx, 0
    for layer in range(DEPTH):
        last = layer == DEPTH - 1
        pc, q, k, v, g, su, sv, kv, w_gate_bf = _inproj_call(layer, x_lat, x_ctx, ctx_tile, mod, w_in_bf, lg_rows, n_lat_tiles,
                                                             to_cast=(moe_w_gate.reshape(-1, EXPERT_HIDDEN),))
        states = _scan_call(layer, kv, lg_cols, n_lat_chunks)
        n_proc = n_lat_tiles if last else n_lat_tiles + 1
        x1, hm, route, w_up_bf = _mix_call(layer, n_lat_tiles, n_proc, lg[layer, 0], lg[layer, 1], pc, q, k, v, g, su, sv,
                                           states, x_lat, x_ctx, ctx_tile, mod, conv_w8, sgu_w_bf, sgu_bias, w_out_bf, ln1,
                                           w_route_t, b_route_t, lg_rows, to_cast=(moe_w_up.reshape(-1, EXPERT_HIDDEN),))
        class_of_token, rank, counts = _rank_call(route.reshape(-1, 8, TILE))
        dest, pad_end, pad_len, blk_lo, blk_hi, blk_live, n_used, n_slots = _dispatch_plan(class_of_token.reshape(-1),
                                                                                  rank.reshape(-1), counts[:, 0])
        xs, w_down_bf = _dispatch_call(layer, dest, pad_end, pad_len, n_used, hm.reshape(-1, PACK_W), n_slots,
                                       to_cast=(moe_w_down.reshape(-1, D_MODEL),))
        ys = _expert_call(layer, blk_lo, blk_hi, blk_live, n_used, xs, w_route_bf, b_route,
                          w_gate_bf.reshape(N_EXPERTS, D_MODEL, EXPERT_HIDDEN), w_up_bf.reshape(N_EXPERTS, D_MODEL, EXPERT_HIDDEN),
                          w_down_bf.reshape(N_EXPERTS, EXPERT_HIDDEN, D_MODEL))
        xa = _combine_call(layer, n_lat_tiles, dest, ys, x1, mod, ln2)
        x_lat, x_ctx, ctx_tile = xa, xa, n_lat_tiles
    return xa
```

```python
import functools
import math

import jax
import jax.numpy as jnp
import numpy as np
from jax import lax
from jax.experimental import pallas as pl
from jax.experimental.pallas import tpu as pltpu

F32 = jnp.float32
BF16 = jnp.bfloat16

D_MODEL = 1024
DEPTH = 2
GRID_W = 64
CONV_W = 256
RET_W = 512
RET_HEADS = 8
RET_DK = 64
PAIR_W = 2 * RET_DK
CHUNK = 128
SGU_W = 256
SGU_GROUPS = 4
IN_COLS = 3 * CONV_W + 4 * RET_W + 2 * SGU_W
N_GROUPS = 4
EXPERTS_PER_GROUP = 8
N_EXPERTS = N_GROUPS * EXPERTS_PER_GROUP
EXPERT_HIDDEN = 512
N_MOD = 6
MOD_ROWS = 8
LN_EPS = 1e-5
ALPHA = (2 * DEPTH) ** 0.25

TILE = 256
BATCH_PER_STEP = 2
HALO = GRID_W
BLOCK_M = 256
ROUTE_W = 128
CAST_STEPS = 128
PACK_W = D_MODEL // 2
ROUTE_ROWS = 48
EXPERT_ROW0 = 8
RANK_TILES = 8
PAIRS_PER_GROUP = EXPERTS_PER_GROUP * (EXPERTS_PER_GROUP - 1) // 2
N_CLASSES = N_GROUPS * PAIRS_PER_GROUP
CLASS_LO = np.array([g * EXPERTS_PER_GROUP + lo for g in range(N_GROUPS) for lo in range(EXPERTS_PER_GROUP)
                     for hi in range(lo + 1, EXPERTS_PER_GROUP)] + [N_EXPERTS - 2] * (ROUTE_W - N_CLASSES), np.int32)
CLASS_HI = np.array([g * EXPERTS_PER_GROUP + hi for g in range(N_GROUPS) for lo in range(EXPERTS_PER_GROUP)
                     for hi in range(lo + 1, EXPERTS_PER_GROUP)] + [N_EXPERTS - 1] * (ROUTE_W - N_CLASSES), np.int32)
VMEM_LIMIT = 56 * 1024 * 1024


def _params(n_axes):
    return pltpu.CompilerParams(dimension_semantics=("arbitrary",) * n_axes, vmem_limit_bytes=VMEM_LIMIT)


def _standardize(v):
    mu = jnp.mean(v, axis=-1, keepdims=True)
    var = jnp.mean(jnp.square(v - mu), axis=-1, keepdims=True)
    return (v - mu) * lax.rsqrt(var + LN_EPS)


def _silu(v):
    return v * jax.nn.sigmoid(v)


def _pack_bf16_pairs(v):
    half = v.shape[1] // 2
    bits = lambda t: pltpu.bitcast(t.astype(BF16).astype(F32), jnp.uint32)
    return bits(v[:, :half]) | (bits(v[:, half:]) >> 16)


def _unpack_bf16_pairs(w):
    hi = pltpu.bitcast(w & jnp.uint32(0xFFFF0000), F32)
    lo = pltpu.bitcast(w << 16, F32)
    return jnp.concatenate([hi, lo], axis=1).astype(BF16)


def _ada_kernel(c_ref, w_ref, b_ref, o_ref):
    a = _silu(c_ref[...]).astype(BF16)
    o_ref[0] = jnp.dot(a, w_ref[0].astype(BF16), preferred_element_type=F32) + b_ref[0]


def _ada_call(cond, w_ada, b_ada):
    rows = cond.shape[0]
    cols = w_ada.shape[-1]
    tn = 1536
    return pl.pallas_call(
        _ada_kernel,
        grid=(DEPTH, cols // tn),
        in_specs=[
            pl.BlockSpec((rows, D_MODEL), lambda l, j: (0, 0)),
            pl.BlockSpec((1, D_MODEL, tn), lambda l, j: (l, 0, j)),
            pl.BlockSpec((1, 1, tn), lambda l, j: (l, 0, j)),
        ],
        out_specs=pl.BlockSpec((1, rows, tn), lambda l, j: (l, 0, j)),
        out_shape=jax.ShapeDtypeStruct((DEPTH, rows, cols), F32),
        compiler_params=_params(2),
    )(cond, w_ada, b_ada.reshape(DEPTH, 1, cols))


def _inproj_kernel(n_lat_tiles, n_cast, x_ref, ctx_ref, mod_ref, w_ref, lgf_ref, lgb_ref, *refs):
    cast_in, (pc_ref, q_ref, k_ref, v_ref, g_ref, su_ref, sv_ref, kv_ref), cast_out = (
        refs[:n_cast], refs[n_cast:n_cast + 8], refs[n_cast + 8:])
    for src, dst in zip(cast_in, cast_out):
        dst[...] = src[...].astype(BF16)
    is_ctx = pl.program_id(1) >= n_lat_tiles
    h = jnp.concatenate([
        (jnp.where(is_ctx, ctx_ref[j], x_ref[j]) * (1.0 + mod_ref[0, j, 0, 1:2]) + mod_ref[0, j, 0, 0:1]).astype(BF16)
        for j in range(BATCH_PER_STEP)], axis=0)

    def proj(lo, hi):
        return jnp.dot(h, w_ref[0, :, lo:hi], preferred_element_type=F32)

    def put(ref, val):
        for j in range(BATCH_PER_STEP):
            ref[j] = val[j * TILE:(j + 1) * TILE].astype(ref.dtype)

    o = 3 * CONV_W
    put(pc_ref, proj(0, o))
    put(q_ref, proj(o, o + RET_W))
    k = proj(o + RET_W, o + 2 * RET_W) * (RET_DK ** -0.5)
    put(k_ref, k)
    v = proj(o + 2 * RET_W, o + 3 * RET_W)
    put(v_ref, v)
    put(g_ref, proj(o + 3 * RET_W, o + 4 * RET_W))
    o += 4 * RET_W
    put(su_ref, proj(o, o + SGU_W))
    put(sv_ref, proj(o + SGU_W, o + 2 * SGU_W))

    pos = lax.broadcasted_iota(jnp.int32, (CHUNK, 1), 0).astype(F32)
    k_decay = (jnp.exp((CHUNK - 1.0 - pos) * lgf_ref[0]), jnp.exp(pos * lgb_ref[0]))
    contract_rows = (((0,), (0,)), ((), ()))
    for j in range(BATCH_PER_STEP):
        for c in range(TILE // CHUNK):
            rows = slice(j * TILE + c * CHUNK, j * TILE + (c + 1) * CHUNK)
            vc = v[rows].astype(BF16)
            for d in range(2):
                kc = (k[rows] * k_decay[d]).astype(BF16)
                for p in range(RET_W // PAIR_W):
                    cols = slice(p * PAIR_W, (p + 1) * PAIR_W)
                    kv_ref[d, j, c, cols, :] = lax.dot_general(kc[:, cols], vc[:, cols], contract_rows,
                                                               preferred_element_type=F32)


def _cast_specs(arrays, part, step_of, n_steps):
    steps = min(CAST_STEPS, 1 << (n_steps.bit_length() - 1))
    block = lambda a: (a.shape[0] // (DEPTH * steps), a.shape[1])
    slab = lambda *idx: jnp.minimum(step_of(*idx), steps - 1)
    return ([pl.BlockSpec(block(a), lambda *idx: (part * steps + slab(*idx), 0)) for a in arrays],
            [pl.BlockSpec(block(a), lambda *idx: (slab(*idx), 0)) for a in arrays],
            [jax.ShapeDtypeStruct((a.shape[0] // DEPTH, a.shape[1]), BF16) for a in arrays])


def _inproj_call(layer, x_lat, x_ctx, ctx_tile, mod, w_in_bf, lg_rows, n_lat_tiles, to_cast=()):
    bsz = x_lat.shape[0]
    assert bsz % BATCH_PER_STEP == 0
    nb = BATCH_PER_STEP
    length = (n_lat_tiles + 1) * TILE
    widths = (3 * CONV_W, RET_W, RET_W, RET_W, RET_W, SGU_W, SGU_W)
    cpt = TILE // CHUNK
    cast_in, cast_out, cast_shapes = _cast_specs(to_cast, layer, lambda b, i: b * (n_lat_tiles + 1) + i,
                                                  bsz // nb * (n_lat_tiles + 1))
    return pl.pallas_call(
        functools.partial(_inproj_kernel, n_lat_tiles, len(to_cast)),
        grid=(bsz // nb, length // TILE),
        in_specs=[
            pl.BlockSpec((nb, TILE, D_MODEL), lambda b, i: (b, jnp.minimum(i, n_lat_tiles - 1), 0)),
            pl.BlockSpec((nb, TILE, D_MODEL), lambda b, i: (b, ctx_tile, 0)),
            pl.BlockSpec((1, nb, 1, MOD_ROWS, D_MODEL), lambda b, i: (layer, b, jnp.minimum(i // n_lat_tiles, 1), 0, 0)),
            pl.BlockSpec((1, D_MODEL, IN_COLS), lambda b, i: (layer, 0, 0)),
            pl.BlockSpec((1, 1, RET_W), lambda b, i: (2 * layer, 0, 0)),
            pl.BlockSpec((1, 1, RET_W), lambda b, i: (2 * layer + 1, 0, 0)),
        ] + cast_in,
        out_specs=[pl.BlockSpec((nb, TILE, w), lambda b, i: (b, i, 0)) for w in widths]
        + [pl.BlockSpec((2, nb, cpt, RET_W, PAIR_W), lambda b, i: (0, b, i, 0, 0))] + cast_out,
        out_shape=[jax.ShapeDtypeStruct((bsz, length, w), BF16 if 1 <= j <= 3 else F32) for j, w in enumerate(widths)]
        + [jax.ShapeDtypeStruct((2, bsz, length // CHUNK, RET_W, PAIR_W), F32)] + cast_shapes,
        compiler_params=_params(2),
    )(x_lat, x_ctx, mod, w_in_bf, lg_rows, lg_rows, *to_cast)


def _scan_kernel(n_lat_chunks, kv_ref, lgc_ref, s_ref, state):
    nc = kv_ref.shape[2]
    backward = pl.program_id(1)
    decay = jnp.exp(CHUNK * lgc_ref[0])
    row_head = lax.broadcasted_iota(jnp.int32, (RET_W, PAIR_W), 0) % PAIR_W // RET_DK
    lane_head = lax.broadcasted_iota(jnp.int32, (RET_W, PAIR_W), 1) // RET_DK
    own_head = row_head == lane_head
    state[...] = jnp.zeros_like(state)

    def step(s, carry):
        c = jnp.where(backward == 1, nc - 1 - s, (s + n_lat_chunks) % nc)
        s_ref[0, 0, c] = jnp.where(own_head, state[...], 0.0).astype(BF16)
        state[...] = decay * state[...] + kv_ref[0, 0, c]
        return carry

    lax.fori_loop(0, nc, step, 0)


def _scan_call(layer, kv, lg_cols, n_lat_chunks):
    _, bsz, nc, _, _ = kv.shape
    blk = (1, 1, nc, RET_W, PAIR_W)
    return pl.pallas_call(
        functools.partial(_scan_kernel, n_lat_chunks),
        grid=(bsz, 2),
        in_specs=[pl.BlockSpec(blk, lambda b, d: (d, b, 0, 0, 0)),
                  pl.BlockSpec((1, RET_W, 1), lambda b, d: (2 * layer + d, 0, 0))],
        out_specs=pl.BlockSpec(blk, lambda b, d: (d, b, 0, 0, 0)),
        out_shape=jax.ShapeDtypeStruct(kv.shape, BF16),
        scratch_shapes=[pltpu.VMEM((RET_W, PAIR_W), F32)],
        compiler_params=_params(2),
    )(kv, lg_cols)


def _route_class(lt):
    assert EXPERTS_PER_GROUP == 8 and N_GROUPS <= 8
    sub = lax.broadcasted_iota(jnp.int32, (8, lt.shape[1]), 0)
    neg = jnp.float32(-jnp.inf)

    def top(vals):
        m = jnp.max(vals, axis=0, keepdims=True)
        return jnp.min(jnp.where(vals == m, sub, 8), axis=0, keepdims=True)

    gidx = top(jnp.where(sub < N_GROUPS, lt[0:8], neg))
    pair = jnp.zeros_like(gidx)
    for g in range(N_GROUPS):
        e = lt[EXPERT_ROW0 + 8 * g:EXPERT_ROW0 + 8 * (g + 1)]
        i1 = top(e)
        i2 = top(jnp.where(sub == i1, neg, e))
        e_lo = jnp.minimum(i1, i2)
        e_hi = jnp.maximum(i1, i2)
        pair_g = e_lo * (EXPERTS_PER_GROUP - 1) - ((e_lo * (e_lo - 1)) >> 1) + (e_hi - e_lo - 1)
        pair = jnp.where(gidx == g, pair_g, pair)
    return gidx * PAIRS_PER_GROUP + pair


def _mix_kernel(n_lat_tiles, n_cast, lgf_ref, lgb_ref, pc_ref, hp_ref, hn_ref, q_ref, k_ref, v_ref, g_ref, su_ref, sv_ref,
                sf_ref, sb_ref, x_ref, ctx_ref, mod_ref, convw_ref, sguw_ref, sgub_ref, wout_ref, ln_ref, wr_ref, br_ref,
                lgfr_ref, lgbr_ref, *refs):
    cast_in, (x1_ref, hm_ref, route_ref), cast_out = refs[:n_cast], refs[n_cast:n_cast + 3], refs[n_cast + 3:-3]
    ycat, dec, qdec = refs[-3:]
    for src, dst in zip(cast_in, cast_out):
        dst[...] = src[...].astype(BF16)
    i = pl.program_id(1)
    is_ctx = i >= n_lat_tiles
    row = lax.broadcasted_iota(jnp.int32, (TILE, 1), 0)

    @pl.when((pl.program_id(0) == 0) & (i == 0))
    def _():
        pos = lax.broadcasted_iota(jnp.int32, (CHUNK, 1), 0).astype(F32)
        rel = pos - lax.broadcasted_iota(jnp.int32, (1, CHUNK), 1).astype(F32)
        for h in range(RET_HEADS):
            dec[h // 2, (h % 2) * CHUNK:(h % 2 + 1) * CHUNK, :] = jnp.where(
                rel > 0, jnp.exp(lgf_ref[h] * jnp.maximum(rel, 0.0)),
                jnp.where(rel < 0, jnp.exp(lgb_ref[h] * jnp.maximum(-rel, 0.0)), 2.0))
        qdec[0] = jnp.exp(lgfr_ref[0] * (pos + 1.0))
        qdec[1] = jnp.exp(lgbr_ref[0] * (CHUNK - pos))

    low_head = lax.broadcasted_iota(jnp.int32, (1, PAIR_W), 1) < RET_DK

    def half_norm(o):
        def half_mean(t):
            lo = jnp.sum(jnp.where(low_head, t, 0.0), axis=-1, keepdims=True)
            hi = jnp.sum(jnp.where(low_head, 0.0, t), axis=-1, keepdims=True)
            return jnp.where(low_head, lo, hi) * (1.0 / RET_DK)

        centred = o - half_mean(o)
        return centred * lax.rsqrt(half_mean(jnp.square(centred)) + LN_EPS)

    nb = BATCH_PER_STEP
    tiles = [(j, slice(c * CHUNK, (c + 1) * CHUNK), slice(p * PAIR_W, (p + 1) * PAIR_W), c, p)
             for j in range(nb) for c in range(TILE // CHUNK) for p in range(RET_W // PAIR_W)]
    out_rows = lambda j, rows: slice(j * TILE + rows.start, j * TILE + rows.stop)
    scores = []
    for j, rows, cols, c, p in tiles:
        qp = q_ref[j, rows, cols]
        zero = jnp.zeros_like(qp)
        q_split = jnp.concatenate([jnp.where(low_head, qp, zero), jnp.where(low_head, zero, qp)], axis=0)
        scores.append(lax.dot_general(q_split, k_ref[j, rows, cols],
                                      (((1,), (1,)), ((), ())), preferred_element_type=F32))

    line_mask = jnp.where(is_ctx, TILE - 1, GRID_W - 1)
    first = (row & line_mask) == 0
    last = (row & line_mask) == line_mask
    along_seq = lax.broadcasted_iota(jnp.int32, (1, CONV_W), 1) < jnp.where(is_ctx, CONV_W, CONV_W // 2)
    cw = convw_ref[0]
    for j in range(nb):
        pc = pc_ref[j]
        z = pc[:, CONV_W:2 * CONV_W] * pc[:, 2 * CONV_W:3 * CONV_W]
        z_prev = jnp.where(first, 0.0, pltpu.roll(z, 1, 0))
        z_next = jnp.where(last, 0.0, pltpu.roll(z, TILE - 1, 0))
        hp = hp_ref[j]
        hn = hn_ref[j]
        z_top = jnp.where(i == 0, 0.0, hp[:, CONV_W:2 * CONV_W] * hp[:, 2 * CONV_W:3 * CONV_W])
        z_bot = jnp.where(i == n_lat_tiles - 1, 0.0, hn[:, CONV_W:2 * CONV_W] * hn[:, 2 * CONV_W:3 * CONV_W])
        zcat = jnp.concatenate([z_top, z, z_bot], axis=0)
        z_up = zcat[0:TILE]
        z_down = zcat[2 * HALO:2 * HALO + TILE]
        conv = cw[0:1] * jnp.where(along_seq, z_prev, z_up) + cw[1:2] * z
        conv = conv + cw[2:3] * jnp.where(along_seq, z_next, z_down)
        ycat[j * TILE:(j + 1) * TILE, 0:CONV_W] = (pc[:, 0:CONV_W] * conv).astype(BF16)

    outs = []
    for (j, rows, cols, c, p), sc in zip(tiles, scores):
        qp = q_ref[j, rows, cols].astype(F32)
        vp = v_ref[j, rows, cols]
        zero = jnp.zeros_like(vp)
        sc = sc * dec[p]
        lhs = jnp.concatenate([sc[0:CHUNK].astype(BF16), sc[CHUNK:].astype(BF16),
                               (qp * qdec[0, :, cols]).astype(BF16), (qp * qdec[1, :, cols]).astype(BF16)], axis=1)
        rhs = jnp.concatenate([jnp.where(low_head, vp, zero), jnp.where(low_head, zero, vp),
                               sf_ref[0, j, c, cols, :], sb_ref[0, j, c, cols, :]], axis=0)
        outs.append(jnp.dot(lhs, rhs, preferred_element_type=F32))

    group = lax.broadcasted_iota(jnp.int32, (1, SGU_W), 1) // (SGU_W // SGU_GROUPS)
    for j in range(nb):
        vn = _standardize(sv_ref[j]).astype(BF16)
        for c in range(TILE // CHUNK):
            rows = slice(c * CHUNK, (c + 1) * CHUNK)
            mixed = jnp.zeros((CHUNK, SGU_W), F32)
            for gi in range(SGU_GROUPS):
                m = jnp.dot(sguw_ref[0, gi], vn[rows], preferred_element_type=F32)
                mixed = jnp.where(group == gi, m, mixed)
            ycat[out_rows(j, rows), CONV_W + RET_W:] = (su_ref[j, rows, :] * (mixed + sgub_ref[0])).astype(BF16)

    for (j, rows, cols, c, p), o in zip(tiles, outs):
        ycat[out_rows(j, rows), CONV_W + p * PAIR_W:CONV_W + (p + 1) * PAIR_W] = (
            _silu(g_ref[j, rows, cols]) * half_norm(o)).astype(BF16)

    y = jnp.dot(ycat[...], wout_ref[0], preferred_element_type=F32)
    ln = ln_ref[0]
    hms = []
    for j in range(nb):
        mod = mod_ref[0, j, 0]
        x1 = _standardize(ALPHA * jnp.where(is_ctx, ctx_ref[j], x_ref[j]) + mod[2:3] * y[j * TILE:(j + 1) * TILE])
        x1 = x1 * ln[0:1] + ln[1:2]
        x1_ref[j] = x1
        hm = x1 * (1.0 + mod[4:5]) + mod[3:4]
        hm_ref[j] = _pack_bf16_pairs(hm)
        hms.append(hm.astype(BF16))
    lt = lax.dot_general(wr_ref[0], jnp.concatenate(hms, axis=0), (((1,), (1,)), ((), ())),
                         preferred_element_type=F32) + br_ref[0]
    cls = _route_class(lt).astype(F32)
    for j in range(nb):
        route_ref[j, 0] = jnp.concatenate([cls[:, j * TILE:(j + 1) * TILE], jnp.zeros((7, TILE), F32)], axis=0)


def _mix_call(layer, n_lat_tiles, n_proc, lgf, lgb, pc, q, k, v, g, su, sv, states, x_lat, x_ctx, ctx_tile, mod, conv_w,
              sgu_w_bf, sgu_bias, w_out_bf, ln1, w_route_t, b_route_t, lg_rows, to_cast=()):
    bsz, length, _ = q.shape
    nb = BATCH_PER_STEP
    cast_in, cast_out, cast_shapes = _cast_specs(to_cast, layer, lambda b, i, *_: b * n_proc + i, bsz // nb * n_proc)
    halos_per_tile = TILE // HALO
    n_halo = length // HALO
    cpt = TILE // CHUNK
    tok = lambda w: pl.BlockSpec((nb, TILE, w), lambda b, i, *_: (b, i, 0))
    per_layer = lambda *shape: pl.BlockSpec((1,) + shape, lambda b, i, *_: (layer,) + (0,) * len(shape))
    grid_spec = pltpu.PrefetchScalarGridSpec(
        num_scalar_prefetch=2,
        grid=(bsz // nb, n_proc),
        in_specs=[
            tok(3 * CONV_W),
            pl.BlockSpec((nb, HALO, 3 * CONV_W), lambda b, i, *_: (b, jnp.maximum(i * halos_per_tile - 1, 0), 0)),
            pl.BlockSpec((nb, HALO, 3 * CONV_W),
                         lambda b, i, *_: (b, jnp.minimum((i + 1) * halos_per_tile, n_halo - 1), 0)),
            tok(RET_W), tok(RET_W), tok(RET_W), tok(RET_W), tok(SGU_W), tok(SGU_W),
            pl.BlockSpec((1, nb, cpt, RET_W, PAIR_W), lambda b, i, *_: (0, b, i, 0, 0)),
            pl.BlockSpec((1, nb, cpt, RET_W, PAIR_W), lambda b, i, *_: (1, b, i, 0, 0)),
            pl.BlockSpec((nb, TILE, D_MODEL), lambda b, i, *_: (b, jnp.minimum(i, n_lat_tiles - 1), 0)),
            pl.BlockSpec((nb, TILE, D_MODEL), lambda b, i, *_: (b, ctx_tile, 0)),
            pl.BlockSpec((1, nb, 1, MOD_ROWS, D_MODEL),
                         lambda b, i, *_: (layer, b, jnp.minimum(i // n_lat_tiles, 1), 0, 0)),
            per_layer(8, CONV_W),
            per_layer(SGU_GROUPS, CHUNK, CHUNK),
            per_layer(CHUNK, SGU_W),
            per_layer(D_MODEL, D_MODEL),
            per_layer(8, D_MODEL),
            per_layer(ROUTE_ROWS, D_MODEL),
            per_layer(ROUTE_ROWS, 1),
            pl.BlockSpec((1, 1, RET_W), lambda b, i, *_: (2 * layer, 0, 0)),
            pl.BlockSpec((1, 1, RET_W), lambda b, i, *_: (2 * layer + 1, 0, 0)),
        ] + cast_in,
        out_specs=[tok(D_MODEL), tok(PACK_W), pl.BlockSpec((nb, 1, 8, TILE), lambda b, i, *_: (b, i, 0, 0))] + cast_out,
        scratch_shapes=[pltpu.VMEM((nb * TILE, D_MODEL), BF16),
                        pltpu.VMEM((RET_W // PAIR_W, 2 * CHUNK, CHUNK), F32),
                        pltpu.VMEM((2, CHUNK, RET_W), F32)],
    )
    return pl.pallas_call(
        functools.partial(_mix_kernel, n_lat_tiles, len(to_cast)),
        grid_spec=grid_spec,
        out_shape=[jax.ShapeDtypeStruct((bsz, n_proc * TILE, D_MODEL), F32),
                   jax.ShapeDtypeStruct((bsz, n_proc * TILE, PACK_W), jnp.uint32),
                   jax.ShapeDtypeStruct((bsz, n_proc, 8, TILE), F32)] + cast_shapes,
        compiler_params=_params(2),
    )(lgf, lgb, pc, pc, pc, q, k, v, g, su, sv, states, states, x_lat, x_ctx, mod, conv_w, sgu_w_bf, sgu_bias, w_out_bf,
      ln1, w_route_t, b_route_t, lg_rows, lg_rows, *to_cast)


def _rank_kernel(route_ref, class_ref, rank_ref, counts_ref, running):
    @pl.when(pl.program_id(0) == 0)
    def _():
        running[...] = jnp.zeros_like(running)

    sub = lax.broadcasted_iota(jnp.int32, (ROUTE_W, TILE), 0)
    earlier = (lax.broadcasted_iota(jnp.int32, (TILE, TILE), 0)
               < lax.broadcasted_iota(jnp.int32, (TILE, TILE), 1)).astype(BF16)
    for t in range(route_ref.shape[0]):
        cls = route_ref[t, 0:1, :].astype(jnp.int32)
        class_ref[t] = cls
        onehot = sub == cls
        before = jnp.dot(onehot.astype(BF16), earlier, preferred_element_type=F32) + running[...]
        rank_ref[t] = jnp.sum(jnp.where(onehot, before, 0.0), axis=0, keepdims=True).astype(jnp.int32)
        running[...] += jnp.sum(onehot.astype(F32), axis=1, keepdims=True)
    counts_ref[...] = jnp.broadcast_to(running[...], counts_ref.shape)


def _rank_call(route):
    n_tiles = route.shape[0]
    per_step = math.gcd(n_tiles, RANK_TILES)
    per_tile = pl.BlockSpec((per_step, 1, TILE), lambda i: (i, 0, 0))
    return pl.pallas_call(
        _rank_kernel,
        grid=(n_tiles // per_step,),
        in_specs=[pl.BlockSpec((per_step, 8, TILE), lambda i: (i, 0, 0))],
        out_specs=[per_tile, per_tile, pl.BlockSpec((ROUTE_W, ROUTE_W), lambda i: (0, 0))],
        out_shape=[jax.ShapeDtypeStruct((n_tiles, 1, TILE), jnp.int32), jax.ShapeDtypeStruct((n_tiles, 1, TILE), jnp.int32),
                   jax.ShapeDtypeStruct((ROUTE_W, ROUTE_W), F32)],
        scratch_shapes=[pltpu.VMEM((ROUTE_W, 1), F32)],
        compiler_params=_params(1),
    )(route)


def _dispatch_kernel(n_cast, dest_ref, pad_end_ref, pad_len_ref, n_used_ref, hm_ref, *refs):
    cast_in, xs_ref, cast_out = refs[:n_cast], refs[n_cast], refs[n_cast + 1:-3]
    zeros, sem, pad_sem = refs[-3:]
    for src, dst in zip(cast_in, cast_out):
        dst[...] = src[...].astype(BF16)
    step = pl.program_id(0)
    base = step * TILE
    half = BLOCK_M // 2

    def for_each_pad_copy(fn):
        def per_class(c, carry):
            off = pad_end_ref[c]
            n = pad_len_ref[c]
            for shift in range(BLOCK_M.bit_length() - 2, -1, -1):
                bit = 1 << shift
                off = off - (n & bit)

                @pl.when((n & bit) != 0)
                def _():
                    if bit >= 8:
                        fn(pltpu.make_async_copy(zeros.at[pl.ds(0, bit)], xs_ref.at[pl.ds(pl.multiple_of(off, 8), bit)],
                                                 pad_sem))
                    else:
                        for j in range(bit):
                            fn(pltpu.make_async_copy(zeros.at[pl.ds(0, 1)], xs_ref.at[pl.ds(off + j, 1)], pad_sem))

            return carry

        lax.fori_loop(0, N_CLASSES, per_class, 0)

        def per_half_block(j, carry):
            fn(pltpu.make_async_copy(zeros, xs_ref.at[pl.ds(pl.multiple_of(j * half, 8), half)], pad_sem))
            return carry

        lax.fori_loop(n_used_ref[0] * 2, xs_ref.shape[0] // half, per_half_block, 0)

    @pl.when(step == 0)
    def _():
        zeros[...] = jnp.zeros_like(zeros)
        for_each_pad_copy(lambda cp: cp.start())

    for r in range(TILE):
        pltpu.make_async_copy(hm_ref.at[pl.ds(r, 1)], xs_ref.at[pl.ds(dest_ref[base + r], 1)],
                              sem).start(priority=r % 2)
    for r in range(TILE):
        pltpu.make_async_copy(hm_ref.at[pl.ds(0, 1)], xs_ref.at[pl.ds(0, 1)], sem).wait()

    @pl.when(step == pl.num_programs(0) - 1)
    def _():
        for_each_pad_copy(lambda cp: cp.wait())


def _dispatch_call(layer, dest, pad_end, pad_len, n_used, hm_flat, n_slots, to_cast=()):
    n_tok = hm_flat.shape[0]
    cast_in, cast_out, cast_shapes = _cast_specs(to_cast, layer, lambda i, *_: i, n_tok // TILE)
    grid_spec = pltpu.PrefetchScalarGridSpec(
        num_scalar_prefetch=4,
        grid=(n_tok // TILE,),
        in_specs=[pl.BlockSpec((TILE, PACK_W), lambda i, *_: (i, 0))] + cast_in,
        out_specs=[pl.BlockSpec(memory_space=pl.ANY)] + cast_out,
        scratch_shapes=[pltpu.VMEM((BLOCK_M // 2, PACK_W), jnp.uint32), pltpu.SemaphoreType.DMA,
                        pltpu.SemaphoreType.DMA],
    )
    return pl.pallas_call(
        functools.partial(_dispatch_kernel, len(to_cast)),
        grid_spec=grid_spec,
        out_shape=[jax.ShapeDtypeStruct((n_slots, PACK_W), jnp.uint32)] + cast_shapes,
        compiler_params=_params(1),
    )(dest, pad_end, pad_len, n_used, hm_flat, *to_cast)


def _expert_kernel(blk_lo_ref, blk_hi_ref, blk_live_ref, n_used_ref, xs_ref, wr_ref, br_ref, wg_lo, wu_lo, wd_lo,
                   wg_hi, wu_hi, wd_hi, ys_ref):
    del n_used_ref
    i = pl.program_id(0)
    live = blk_live_ref[i]

    def experts_on(rows):
        xb = _unpack_bf16_pairs(xs_ref[0:rows, :])

        logits = jnp.dot(xb, wr_ref[0], preferred_element_type=F32) + br_ref[0]
        lane = lax.broadcasted_iota(jnp.int32, logits.shape, 1)
        gl = jnp.where(lane < N_GROUPS, logits, -jnp.inf)
        g_prob = 1.0 / jnp.sum(jnp.exp(gl - jnp.max(gl, axis=-1, keepdims=True)), axis=-1, keepdims=True)
        l_lo = jnp.sum(jnp.where(lane == N_GROUPS + blk_lo_ref[i], logits, 0.0), axis=-1, keepdims=True)
        l_hi = jnp.sum(jnp.where(lane == N_GROUPS + blk_hi_ref[i], logits, 0.0), axis=-1, keepdims=True)
        m = jnp.maximum(l_lo, l_hi)
        p_lo = jnp.exp(l_lo - m)
        p_hi = jnp.exp(l_hi - m)

        h_lo = jnp.dot(xb, wg_lo[0], preferred_element_type=F32)
        u_lo = jnp.dot(xb, wu_lo[0], preferred_element_type=F32)
        h_hi = jnp.dot(xb, wg_hi[0], preferred_element_type=F32)
        u_hi = jnp.dot(xb, wu_hi[0], preferred_element_type=F32)
        a_lo = (_silu(h_lo) * u_lo).astype(BF16)
        a_hi = (_silu(h_hi) * u_hi).astype(BF16)
        y_lo = jnp.dot(a_lo, wd_lo[0], preferred_element_type=F32)
        y_hi = jnp.dot(a_hi, wd_hi[0], preferred_element_type=F32)
        ys_ref[0:rows, :] = y_lo * (g_prob * (p_lo / (p_lo + p_hi))) + y_hi * (g_prob * (p_hi / (p_lo + p_hi)))
        if rows < BLOCK_M:
            ys_ref[rows:, :] = jnp.zeros((BLOCK_M - rows, D_MODEL), F32)

    half = BLOCK_M // 2

    @pl.when(live > half)
    def _():
        experts_on(BLOCK_M)

    @pl.when((live > 0) & (live <= half))
    def _():
        experts_on(half)

    @pl.when(live == 0)
    def _():
        ys_ref[...] = jnp.zeros_like(ys_ref)


def _expert_call(layer, blk_lo, blk_hi, blk_live, n_used, xs, w_route_bf, b_route, w_gate_bf, w_up_bf, w_down_bf):
    n_blocks = xs.shape[0] // BLOCK_M
    up_spec = lambda which: pl.BlockSpec((1, D_MODEL, EXPERT_HIDDEN), lambda i, lo, hi, lv, nu: ((lo, hi)[which][i], 0, 0))
    down_spec = lambda which: pl.BlockSpec((1, EXPERT_HIDDEN, D_MODEL), lambda i, lo, hi, lv, nu: ((lo, hi)[which][i], 0, 0))
    grid_spec = pltpu.PrefetchScalarGridSpec(
        num_scalar_prefetch=4,
        grid=(n_blocks,),
        in_specs=[pl.BlockSpec((BLOCK_M, PACK_W), lambda i, lo, hi, lv, nu: (jnp.minimum(i, nu[0] - 1), 0)),
                  pl.BlockSpec((1, D_MODEL, ROUTE_W), lambda i, *_: (layer, 0, 0)),
                  pl.BlockSpec((1, 1, ROUTE_W), lambda i, *_: (layer, 0, 0)),
                  up_spec(0), up_spec(0), down_spec(0), up_spec(1), up_spec(1), down_spec(1)],
        out_specs=pl.BlockSpec((BLOCK_M, D_MODEL), lambda i, *_: (i, 0)),
    )
    return pl.pallas_call(
        _expert_kernel,
        grid_spec=grid_spec,
        out_shape=jax.ShapeDtypeStruct((xs.shape[0], D_MODEL), F32),
        compiler_params=_params(1),
    )(blk_lo, blk_hi, blk_live, n_used, xs, w_route_bf, b_route, w_gate_bf, w_up_bf, w_down_bf, w_gate_bf, w_up_bf,
      w_down_bf)


def _combine_kernel(tiles_per_batch, dest_ref, ys_ref, x1_ref, mod_ref, ln_ref, out_ref, buf, sem):
    step = pl.program_id(0) * tiles_per_batch + pl.program_id(1)
    slot = step % 2
    has_next = step + 1 < pl.num_programs(0) * tiles_per_batch

    def gather(tile, to_slot):
        base = tile * TILE

        for r in range(TILE):
            pltpu.make_async_copy(ys_ref.at[pl.ds(dest_ref[base + r], 1)], buf.at[to_slot, pl.ds(r, 1)],
                                  sem.at[to_slot]).start(priority=r % 2)

    @pl.when(step == 0)
    def _():
        gather(step, slot)

    @pl.when(has_next)
    def _():
        gather(step + 1, 1 - slot)

    for r in range(TILE):
        pltpu.make_async_copy(ys_ref.at[pl.ds(0, 1)], buf.at[slot, pl.ds(0, 1)], sem.at[slot]).wait()

    mod = mod_ref[0, 0, 0]
    ln = ln_ref[0]
    out_ref[0] = _standardize(ALPHA * x1_ref[0] + mod[5:6] * buf[slot]) * ln[0:1] + ln[1:2]


def _combine_call(layer, n_lat_tiles, dest, ys, x1, mod, ln2):
    bsz, length, _ = x1.shape
    tiles_per_batch = length // TILE
    tok = lambda w: pl.BlockSpec((1, TILE, w), lambda b, i, *_: (b, i, 0))
    grid_spec = pltpu.PrefetchScalarGridSpec(
        num_scalar_prefetch=1,
        grid=(bsz, tiles_per_batch),
        in_specs=[
            pl.BlockSpec(memory_space=pl.ANY),
            tok(D_MODEL),
            pl.BlockSpec((1, 1, 1, MOD_ROWS, D_MODEL),
                         lambda b, i, *_: (layer, b, jnp.minimum(i // n_lat_tiles, 1), 0, 0)),
            pl.BlockSpec((1, 8, D_MODEL), lambda b, i, *_: (layer, 0, 0)),
        ],
        out_specs=tok(D_MODEL),
        scratch_shapes=[pltpu.VMEM((2, TILE, D_MODEL), F32), pltpu.SemaphoreType.DMA((2,))],
    )
    return pl.pallas_call(
        functools.partial(_combine_kernel, tiles_per_batch),
        grid_spec=grid_spec,
        out_shape=jax.ShapeDtypeStruct(x1.shape, F32),
        compiler_params=_params(2),
    )(dest, ys, x1, mod, ln2)


def _dispatch_plan(class_of_token, rank, counts):
    n_tok = class_of_token.shape[0]
    counts = counts.astype(jnp.int32)
    pcounts = (counts + BLOCK_M - 1) // BLOCK_M * BLOCK_M
    pends = jnp.cumsum(pcounts)
    pstarts = pends - pcounts
    classes = jnp.arange(ROUTE_W, dtype=jnp.int32)
    dest = jnp.sum(jnp.where(class_of_token[:, None] == classes[None, :], pstarts[None, :], 0), axis=1) + rank
    n_blocks = n_tok // BLOCK_M + N_CLASSES
    blk_start = jnp.arange(n_blocks, dtype=jnp.int32) * BLOCK_M
    blk_class = jnp.minimum(jnp.sum((pends[None, :] <= blk_start[:, None]).astype(jnp.int32), axis=1), N_CLASSES - 1)
    blk_live = jnp.clip(counts[blk_class] - (blk_start - pstarts[blk_class]), 0, BLOCK_M)
    n_used = pends[-1:] // BLOCK_M
    return (dest.astype(jnp.int32), pends, pcounts - counts, jnp.asarray(CLASS_LO)[blk_class],
            jnp.asarray(CLASS_HI)[blk_class], blk_live.astype(jnp.int32), n_used.astype(jnp.int32), n_blocks * BLOCK_M)


def _pad_rows(a, rows):
    return jnp.pad(a, [(0, 0)] * (a.ndim - 2) + [(0, rows - a.shape[-2]), (0, 0)])


def kernel(x, c, ctx, c_ctx, w_ada, b_ada, w_in, conv_w, ret_decay_fwd, ret_decay_bwd, sgu_w, sgu_b, w_out, ln1_g, ln1_b,
           router_group_w, router_group_b, router_expert_w, router_expert_b, moe_w_gate, moe_w_up, moe_w_down, ln2_g,
           ln2_b):
    bsz, seq, d = x.shape
    ctx_len = ctx.shape[1]
    assert d == D_MODEL and ctx_len == TILE and seq % TILE == 0 and seq % GRID_W == 0
    n_lat_tiles = seq // TILE
    n_lat_chunks = seq // CHUNK

    cond = _pad_rows(jnp.concatenate([c, c_ctx[None, :]], axis=0), 16)
    ada = _ada_call(cond, w_ada, b_ada)
    mod_lat = ada[:, :bsz].reshape(DEPTH, bsz, N_MOD, d)
    mod_ctx = jnp.broadcast_to(ada[:, bsz].reshape(DEPTH, 1, N_MOD, d), mod_lat.shape)
    mod = _pad_rows(jnp.stack([mod_lat, mod_ctx], axis=2), MOD_ROWS)

    w_in_bf = w_in.astype(BF16)
    w_out_bf = w_out.astype(BF16)
    sgu_w_bf = sgu_w.astype(BF16)
    sgu_bias = jnp.repeat(jnp.swapaxes(sgu_b, 1, 2), SGU_W // SGU_GROUPS, axis=2)
    conv_w8 = _pad_rows(conv_w, 8)
    ln1 = _pad_rows(jnp.stack([ln1_g, ln1_b], axis=1), 8)
    ln2 = _pad_rows(jnp.stack([ln2_g, ln2_b], axis=1), 8)
    w_route = jnp.concatenate([router_group_w, jnp.swapaxes(router_expert_w, 1, 2).reshape(DEPTH, d, N_EXPERTS)], axis=2)
    w_route_bf = jnp.pad(w_route, ((0, 0), (0, 0), (0, ROUTE_W - w_route.shape[2]))).astype(BF16)
    b_route = jnp.concatenate([router_group_b, router_expert_b.reshape(DEPTH, N_EXPERTS)], axis=1)
    b_route = jnp.pad(b_route, ((0, 0), (0, ROUTE_W - b_route.shape[1])))[:, None, :].astype(F32)
    w_route_t = jnp.concatenate([_pad_rows(jnp.swapaxes(w_route[:, :, :N_GROUPS], 1, 2), EXPERT_ROW0),
                                 jnp.swapaxes(w_route[:, :, N_GROUPS:], 1, 2)], axis=1)
    w_route_t = _pad_rows(w_route_t, ROUTE_ROWS).astype(BF16)
    b_route_t = jnp.concatenate([_pad_rows(b_route[:, 0, :N_GROUPS, None], EXPERT_ROW0),
                                 b_route[:, 0, N_GROUPS:N_GROUPS + N_EXPERTS, None]], axis=1)
    b_route_t = _pad_rows(b_route_t, ROUTE_ROWS)
    lg = jnp.stack([jax.nn.log_sigmoid(ret_decay_fwd.astype(F32)), jax.nn.log_sigmoid(ret_decay_bwd.astype(F32))],
                   axis=1)
    lg_lanes = jnp.repeat(lg, RET_DK, axis=2).reshape(DEPTH * 2, RET_W)
    lg_rows = lg_lanes[:, None, :]
    lg_cols = lg_lanes[:, :, None]

    x_lat, x_ctx, ctx_tile = x, ctx, 0
    for layer in range(DEPTH):
        last = layer == DEPTH - 1
        pc, q, k, v, g, su, sv, kv, w_gate_bf = _inproj_call(layer, x_lat, x_ctx, ctx_tile, mod, w_in_bf, lg_rows, n_lat_tiles,
                                                             to_cast=(moe_w_gate.reshape(-1, EXPERT_HIDDEN),))
        states = _scan_call(layer, kv, lg_cols, n_lat_chunks)
        n_proc = n_lat_tiles if last else n_lat_tiles + 1
        x1, hm, route, w_up_bf = _mix_call(layer, n_lat_tiles, n_proc, lg[layer, 0], lg[layer, 1], pc, q, k, v, g, su, sv,
                                           states, x_lat, x_ctx, ctx_tile, mod, conv_w8, sgu_w_bf, sgu_bias, w_out_bf, ln1,
                                           w_route_t, b_route_t, lg_rows, to_cast=(moe_w_up.reshape(-1, EXPERT_HIDDEN),))
        class_of_token, rank, counts = _rank_call(route.reshape(-1, 8, TILE))
        dest, pad_end, pad_len, blk_lo, blk_hi, blk_live, n_used, n_slots = _dispatch_plan(class_of_token.reshape(-1),
                                                                                  rank.reshape(-1), counts[:, 0])
        xs, w_down_bf = _dispatch_call(layer, dest, pad_end, pad_len, n_used, hm.reshape(-1, PACK_W), n_slots,
                                       to_cast=(moe_w_down.reshape(-1, D_MODEL),))
        ys = _expert_call(layer, blk_lo, blk_hi, blk_live, n_used, xs, w_route_bf, b_route,
                          w_gate_bf.reshape(N_EXPERTS, D_MODEL, EXPERT_HIDDEN), w_up_bf.reshape(N_EXPERTS, D_MODEL, EXPERT_HIDDEN),
                          w_down_bf.reshape(N_EXPERTS, EXPERT_HIDDEN, D_MODEL))
        xa = _combine_call(layer, n_lat_tiles, dest, ys, x1, mod, ln2)
        x_lat, x_ctx, ctx_tile = xa, xa, n_lat_tiles
    return xa
```

```python
import functools
import math

import jax
import jax.numpy as jnp
import numpy as np
from jax import lax
from jax.experimental import pallas as pl
from jax.experimental.pallas import tpu as pltpu

F32 = jnp.float32
BF16 = jnp.bfloat16

D_MODEL = 1024
DEPTH = 2
GRID_W = 64
CONV_W = 256
RET_W = 512
RET_HEADS = 8
RET_DK = 64
PAIR_W = 2 * RET_DK
CHUNK = 128
SGU_W = 256
SGU_GROUPS = 4
IN_COLS = 3 * CONV_W + 4 * RET_W + 2 * SGU_W
N_GROUPS = 4
EXPERTS_PER_GROUP = 8
N_EXPERTS = N_GROUPS * EXPERTS_PER_GROUP
EXPERT_HIDDEN = 512
N_MOD = 6
MOD_ROWS = 8
LN_EPS = 1e-5
ALPHA = (2 * DEPTH) ** 0.25

TILE = 256
BATCH_PER_STEP = 2
HALO = GRID_W
BLOCK_M = 256
ROUTE_W = 128
CAST_STEPS = 128
PACK_W = D_MODEL // 2
ROUTE_ROWS = 48
EXPERT_ROW0 = 8
RANK_TILES = 8
PAIRS_PER_GROUP = EXPERTS_PER_GROUP * (EXPERTS_PER_GROUP - 1) // 2
N_CLASSES = N_GROUPS * PAIRS_PER_GROUP
CLASS_LO = np.array([g * EXPERTS_PER_GROUP + lo for g in range(N_GROUPS) for lo in range(EXPERTS_PER_GROUP)
                     for hi in range(lo + 1, EXPERTS_PER_GROUP)] + [N_EXPERTS - 2] * (ROUTE_W - N_CLASSES), np.int32)
CLASS_HI = np.array([g * EXPERTS_PER_GROUP + hi for g in range(N_GROUPS) for lo in range(EXPERTS_PER_GROUP)
                     for hi in range(lo + 1, EXPERTS_PER_GROUP)] + [N_EXPERTS - 1] * (ROUTE_W - N_CLASSES), np.int32)
VMEM_LIMIT = 56 * 1024 * 1024


def _params(n_axes):
    return pltpu.CompilerParams(dimension_semantics=("arbitrary",) * n_axes, vmem_limit_bytes=VMEM_LIMIT)


def _standardize(v):
    mu = jnp.mean(v, axis=-1, keepdims=True)
    var = jnp.mean(jnp.square(v - mu), axis=-1, keepdims=True)
    return (v - mu) * lax.rsqrt(var + LN_EPS)


def _silu(v):
    return v * jax.nn.sigmoid(v)


def _pack_bf16_pairs(v):
    half = v.shape[1] // 2
    bits = lambda t: pltpu.bitcast(t.astype(BF16).astype(F32), jnp.uint32)
    return bits(v[:, :half]) | (bits(v[:, half:]) >> 16)


def _unpack_bf16_pairs(w):
    hi = pltpu.bitcast(w & jnp.uint32(0xFFFF0000), F32)
    lo = pltpu.bitcast(w << 16, F32)
    return jnp.concatenate([hi, lo], axis=1).astype(BF16)


def _ada_kernel(c_ref, w_ref, b_ref, o_ref):
    a = _silu(c_ref[...]).astype(BF16)
    o_ref[0] = jnp.dot(a, w_ref[0].astype(BF16), preferred_element_type=F32) + b_ref[0]


def _ada_call(cond, w_ada, b_ada):
    rows = cond.shape[0]
    cols = w_ada.shape[-1]
    tn = 1536
    return pl.pallas_call(
        _ada_kernel,
        grid=(DEPTH, cols // tn),
        in_specs=[
            pl.BlockSpec((rows, D_MODEL), lambda l, j: (0, 0)),
            pl.BlockSpec((1, D_MODEL, tn), lambda l, j: (l, 0, j)),
            pl.BlockSpec((1, 1, tn), lambda l, j: (l, 0, j)),
        ],
        out_specs=pl.BlockSpec((1, rows, tn), lambda l, j: (l, 0, j)),
        out_shape=jax.ShapeDtypeStruct((DEPTH, rows, cols), F32),
        compiler_params=_params(2),
    )(cond, w_ada, b_ada.reshape(DEPTH, 1, cols))


def _own_head_mask():
    row_head = lax.broadcasted_iota(jnp.int32, (RET_W, PAIR_W), 0) % PAIR_W // RET_DK
    lane_head = lax.broadcasted_iota(jnp.int32, (RET_W, PAIR_W), 1) // RET_DK
    return row_head == lane_head


def _pair_kv(kc, vc):
    contract_rows = (((0,), (0,)), ((), ()))
    return jnp.concatenate([
        lax.dot_general(kc[:, p * PAIR_W:(p + 1) * PAIR_W], vc[:, p * PAIR_W:(p + 1) * PAIR_W], contract_rows,
                        preferred_element_type=F32) for p in range(RET_W // PAIR_W)], axis=0)


def _inproj_kernel(n_lat_tiles, n_cast, x_ref, ctx_ref, mod_ref, w_ref, lgf_ref, lgfc_ref, *refs):
    cast_in, (pc_ref, q_ref, k_ref, v_ref, g_ref, su_ref, sv_ref, sf_ref), cast_out = (
        refs[:n_cast], refs[n_cast:n_cast + 8], refs[n_cast + 8:-1])
    state = refs[-1]
    for src, dst in zip(cast_in, cast_out):
        dst[...] = src[...].astype(BF16)
    step = pl.program_id(1)
    is_ctx = step == 0
    h = jnp.concatenate([
        (jnp.where(is_ctx, ctx_ref[j], x_ref[j]) * (1.0 + mod_ref[0, j, 0, 1:2]) + mod_ref[0, j, 0, 0:1]).astype(BF16)
        for j in range(BATCH_PER_STEP)], axis=0)

    def proj(lo, hi):
        return jnp.dot(h, w_ref[0, :, lo:hi], preferred_element_type=F32)

    def put(ref, val):
        for j in range(BATCH_PER_STEP):
            ref[j] = val[j * TILE:(j + 1) * TILE].astype(ref.dtype)

    o = 3 * CONV_W
    put(pc_ref, proj(0, o))
    put(q_ref, proj(o, o + RET_W))
    k = proj(o + RET_W, o + 2 * RET_W) * (RET_DK ** -0.5)
    put(k_ref, k)
    v = proj(o + 2 * RET_W, o + 3 * RET_W)
    put(v_ref, v)
    put(g_ref, proj(o + 3 * RET_W, o + 4 * RET_W))
    o += 4 * RET_W
    put(su_ref, proj(o, o + SGU_W))
    put(sv_ref, proj(o + SGU_W, o + 2 * SGU_W))

    @pl.when(step == 0)
    def _():
        state[...] = jnp.zeros_like(state)

    pos = lax.broadcasted_iota(jnp.int32, (CHUNK, 1), 0).astype(F32)
    k_decay = jnp.exp((CHUNK - 1.0 - pos) * lgf_ref[0])
    chunk_decay = jnp.exp(CHUNK * lgfc_ref[0])
    own_head = _own_head_mask()
    for j in range(BATCH_PER_STEP):
        kvs = [_pair_kv((k[rows] * k_decay).astype(BF16), v[rows].astype(BF16))
               for rows in (slice(j * TILE + c * CHUNK, j * TILE + (c + 1) * CHUNK) for c in range(TILE // CHUNK))]
        st = state[j]
        for c, kv in enumerate(kvs):
            sf_ref[j, c] = jnp.where(own_head, st, 0.0).astype(BF16)
            st = chunk_decay * st + kv
        state[j] = st


def _cast_specs(arrays, part, step_of, n_steps):
    steps = min(CAST_STEPS, 1 << (n_steps.bit_length() - 1))
    block = lambda a: (a.shape[0] // (DEPTH * steps), a.shape[1])
    slab = lambda *idx: jnp.minimum(step_of(*idx), steps - 1)
    return ([pl.BlockSpec(block(a), lambda *idx: (part * steps + slab(*idx), 0)) for a in arrays],
            [pl.BlockSpec(block(a), lambda *idx: (slab(*idx), 0)) for a in arrays],
            [jax.ShapeDtypeStruct((a.shape[0] // DEPTH, a.shape[1]), BF16) for a in arrays])


def _inproj_call(layer, x_lat, x_ctx, ctx_tile, mod, w_in_bf, lg_rows, lg_cols, n_lat_tiles, to_cast=()):
    bsz = x_lat.shape[0]
    assert bsz % BATCH_PER_STEP == 0
    nb = BATCH_PER_STEP
    length = (n_lat_tiles + 1) * TILE
    widths = (3 * CONV_W, RET_W, RET_W, RET_W, RET_W, SGU_W, SGU_W)
    cpt = TILE // CHUNK
    n_tiles = n_lat_tiles + 1
    tile_of = lambda s: (s + n_lat_tiles) % n_tiles
    cast_in, cast_out, cast_shapes = _cast_specs(to_cast, layer, lambda b, s: b * n_tiles + s, bsz // nb * n_tiles)
    return pl.pallas_call(
        functools.partial(_inproj_kernel, n_lat_tiles, len(to_cast)),
        grid=(bsz // nb, n_tiles),
        in_specs=[
            pl.BlockSpec((nb, TILE, D_MODEL), lambda b, s: (b, jnp.minimum(tile_of(s), n_lat_tiles - 1), 0)),
            pl.BlockSpec((nb, TILE, D_MODEL), lambda b, s: (b, ctx_tile, 0)),
            pl.BlockSpec((1, nb, 1, MOD_ROWS, D_MODEL), lambda b, s: (layer, b, tile_of(s) // n_lat_tiles, 0, 0)),
            pl.BlockSpec((1, D_MODEL, IN_COLS), lambda b, s: (layer, 0, 0)),
            pl.BlockSpec((1, 1, RET_W), lambda b, s: (2 * layer, 0, 0)),
            pl.BlockSpec((1, RET_W, 1), lambda b, s: (2 * layer, 0, 0)),
        ] + cast_in,
        out_specs=[pl.BlockSpec((nb, TILE, w), lambda b, s: (b, tile_of(s), 0)) for w in widths]
        + [pl.BlockSpec((nb, cpt, RET_W, PAIR_W), lambda b, s: (b, tile_of(s), 0, 0))] + cast_out,
        out_shape=[jax.ShapeDtypeStruct((bsz, length, w), BF16 if 1 <= j <= 3 else F32) for j, w in enumerate(widths)]
        + [jax.ShapeDtypeStruct((bsz, length // CHUNK, RET_W, PAIR_W), BF16)] + cast_shapes,
        scratch_shapes=[pltpu.VMEM((nb, RET_W, PAIR_W), F32)],
        compiler_params=_params(2),
    )(x_lat, x_ctx, mod, w_in_bf, lg_rows, lg_cols, *to_cast)


def _route_class(lt):
    assert EXPERTS_PER_GROUP == 8 and N_GROUPS <= 8
    sub = lax.broadcasted_iota(jnp.int32, (8, lt.shape[1]), 0)
    neg = jnp.float32(-jnp.inf)

    def top(vals):
        m = jnp.max(vals, axis=0, keepdims=True)
        return jnp.min(jnp.where(vals == m, sub, 8), axis=0, keepdims=True)

    gidx = top(jnp.where(sub < N_GROUPS, lt[0:8], neg))
    pair = jnp.zeros_like(gidx)
    for g in range(N_GROUPS):
        e = lt[EXPERT_ROW0 + 8 * g:EXPERT_ROW0 + 8 * (g + 1)]
        i1 = top(e)
        i2 = top(jnp.where(sub == i1, neg, e))
        e_lo = jnp.minimum(i1, i2)
        e_hi = jnp.maximum(i1, i2)
        pair_g = e_lo * (EXPERTS_PER_GROUP - 1) - ((e_lo * (e_lo - 1)) >> 1) + (e_hi - e_lo - 1)
        pair = jnp.where(gidx == g, pair_g, pair)
    return gidx * PAIRS_PER_GROUP + pair


def _mix_kernel(n_lat_tiles, n_cast, lgf_ref, lgb_ref, pc_ref, hp_ref, hn_ref, q_ref, k_ref, v_ref, g_ref, su_ref, sv_ref,
                sf_ref, x_ref, ctx_ref, mod_ref, convw_ref, sguw_ref, sgub_ref, wout_ref, ln_ref, wr_ref, br_ref,
                lgfr_ref, lgbr_ref, lgbc_ref, *refs):
    cast_in, (x1_ref, hm_ref, route_ref), cast_out = refs[:n_cast], refs[n_cast:n_cast + 3], refs[n_cast + 3:-4]
    ycat, dec, qdec, state = refs[-4:]
    for src, dst in zip(cast_in, cast_out):
        dst[...] = src[...].astype(BF16)
    step = pl.program_id(1)
    i = n_lat_tiles - step
    is_ctx = step == 0
    row = lax.broadcasted_iota(jnp.int32, (TILE, 1), 0)
    nb = BATCH_PER_STEP

    @pl.when(step == 0)
    def _():
        state[...] = jnp.zeros_like(state)

    pos_b = lax.broadcasted_iota(jnp.int32, (CHUNK, 1), 0).astype(F32)
    k_decay_b = jnp.exp(pos_b * lgbr_ref[0])
    chunk_decay_b = jnp.exp(CHUNK * lgbc_ref[0])
    own_head = _own_head_mask()
    sb = {}
    for j in range(nb):
        kvs = {c: _pair_kv((k_ref[j, c * CHUNK:(c + 1) * CHUNK, :].astype(F32) * k_decay_b).astype(BF16),
                           v_ref[j, c * CHUNK:(c + 1) * CHUNK, :]) for c in range(TILE // CHUNK)}
        st = state[j]
        for c in reversed(range(TILE // CHUNK)):
            sb[j, c] = jnp.where(own_head, st, 0.0).astype(BF16)
            st = chunk_decay_b * st + kvs[c]
        state[j] = st

    @pl.when((pl.program_id(0) == 0) & (step == 0))
    def _():
        pos = lax.broadcasted_iota(jnp.int32, (CHUNK, 1), 0).astype(F32)
        rel = pos - lax.broadcasted_iota(jnp.int32, (1, CHUNK), 1).astype(F32)
        for h in range(RET_HEADS):
            dec[h // 2, (h % 2) * CHUNK:(h % 2 + 1) * CHUNK, :] = jnp.where(
                rel > 0, jnp.exp(lgf_ref[h] * jnp.maximum(rel, 0.0)),
                jnp.where(rel < 0, jnp.exp(lgb_ref[h] * jnp.maximum(-rel, 0.0)), 2.0))
        qdec[0] = jnp.exp(lgfr_ref[0] * (pos + 1.0))
        qdec[1] = jnp.exp(lgbr_ref[0] * (CHUNK - pos))

    low_head = lax.broadcasted_iota(jnp.int32, (1, PAIR_W), 1) < RET_DK

    def half_norm(o):
        def half_mean(t):
            lo = jnp.sum(jnp.where(low_head, t, 0.0), axis=-1, keepdims=True)
            hi = jnp.sum(jnp.where(low_head, 0.0, t), axis=-1, keepdims=True)
            return jnp.where(low_head, lo, hi) * (1.0 / RET_DK)

        centred = o - half_mean(o)
        return centred * lax.rsqrt(half_mean(jnp.square(centred)) + LN_EPS)

    tiles = [(j, slice(c * CHUNK, (c + 1) * CHUNK), slice(p * PAIR_W, (p + 1) * PAIR_W), c, p)
             for j in range(nb) for c in range(TILE // CHUNK) for p in range(RET_W // PAIR_W)]
    out_rows = lambda j, rows: slice(j * TILE + rows.start, j * TILE + rows.stop)
    scores = []
    for j, rows, cols, c, p in tiles:
        qp = q_ref[j, rows, cols]
        zero = jnp.zeros_like(qp)
        q_split = jnp.concatenate([jnp.where(low_head, qp, zero), jnp.where(low_head, zero, qp)], axis=0)
        scores.append(lax.dot_general(q_split, k_ref[j, rows, cols],
                                      (((1,), (1,)), ((), ())), preferred_element_type=F32))

    line_mask = jnp.where(is_ctx, TILE - 1, GRID_W - 1)
    first = (row & line_mask) == 0
    last = (row & line_mask) == line_mask
    along_seq = lax.broadcasted_iota(jnp.int32, (1, CONV_W), 1) < jnp.where(is_ctx, CONV_W, CONV_W // 2)
    cw = convw_ref[0]
    for j in range(nb):
        pc = pc_ref[j]
        z = pc[:, CONV_W:2 * CONV_W] * pc[:, 2 * CONV_W:3 * CONV_W]
        z_prev = jnp.where(first, 0.0, pltpu.roll(z, 1, 0))
        z_next = jnp.where(last, 0.0, pltpu.roll(z, TILE - 1, 0))
        hp = hp_ref[j]
        hn = hn_ref[j]
        z_top = jnp.where(i == 0, 0.0, hp[:, CONV_W:2 * CONV_W] * hp[:, 2 * CONV_W:3 * CONV_W])
        z_bot = jnp.where(i == n_lat_tiles - 1, 0.0, hn[:, CONV_W:2 * CONV_W] * hn[:, 2 * CONV_W:3 * CONV_W])
        zcat = jnp.concatenate([z_top, z, z_bot], axis=0)
        z_up = zcat[0:TILE]
        z_down = zcat[2 * HALO:2 * HALO + TILE]
        conv = cw[0:1] * jnp.where(along_seq, z_prev, z_up) + cw[1:2] * z
        conv = conv + cw[2:3] * jnp.where(along_seq, z_next, z_down)
        ycat[j * TILE:(j + 1) * TILE, 0:CONV_W] = (pc[:, 0:CONV_W] * conv).astype(BF16)

    outs = []
    for (j, rows, cols, c, p), sc in zip(tiles, scores):
        qp = q_ref[j, rows, cols].astype(F32)
        vp = v_ref[j, rows, cols]
        zero = jnp.zeros_like(vp)
        sc = sc * dec[p]
        lhs = jnp.concatenate([sc[0:CHUNK].astype(BF16), sc[CHUNK:].astype(BF16),
                               (qp * qdec[0, :, cols]).astype(BF16), (qp * qdec[1, :, cols]).astype(BF16)], axis=1)
        rhs = jnp.concatenate([jnp.where(low_head, vp, zero), jnp.where(low_head, zero, vp),
                               sf_ref[j, c, cols, :], sb[j, c][cols, :]], axis=0)
        outs.append(jnp.dot(lhs, rhs, preferred_element_type=F32))

    group = lax.broadcasted_iota(jnp.int32, (1, SGU_W), 1) // (SGU_W // SGU_GROUPS)
    for j in range(nb):
        vn = _standardize(sv_ref[j]).astype(BF16)
        for c in range(TILE // CHUNK):
            rows = slice(c * CHUNK, (c + 1) * CHUNK)
            mixed = jnp.zeros((CHUNK, SGU_W), F32)
            for gi in range(SGU_GROUPS):
                m = jnp.dot(sguw_ref[0, gi], vn[rows], preferred_element_type=F32)
                mixed = jnp.where(group == gi, m, mixed)
            ycat[out_rows(j, rows), CONV_W + RET_W:] = (su_ref[j, rows, :] * (mixed + sgub_ref[0])).astype(BF16)

    for (j, rows, cols, c, p), o in zip(tiles, outs):
        ycat[out_rows(j, rows), CONV_W + p * PAIR_W:CONV_W + (p + 1) * PAIR_W] = (
            _silu(g_ref[j, rows, cols]) * half_norm(o)).astype(BF16)

    y = jnp.dot(ycat[...], wout_ref[0], preferred_element_type=F32)
    ln = ln_ref[0]
    hms = []
    for j in range(nb):
        mod = mod_ref[0, j, 0]
        x1 = _standardize(ALPHA * jnp.where(is_ctx, ctx_ref[j], x_ref[j]) + mod[2:3] * y[j * TILE:(j + 1) * TILE])
        x1 = x1 * ln[0:1] + ln[1:2]
        x1_ref[j] = x1
        hm = x1 * (1.0 + mod[4:5]) + mod[3:4]
        hm_ref[j] = _pack_bf16_pairs(hm)
        hms.append(hm.astype(BF16))
    lt = lax.dot_general(wr_ref[0], jnp.concatenate(hms, axis=0), (((1,), (1,)), ((), ())),
                         preferred_element_type=F32) + br_ref[0]
    cls = _route_class(lt).astype(F32)
    for j in range(nb):
        route_ref[j, 0] = jnp.concatenate([cls[:, j * TILE:(j + 1) * TILE], jnp.zeros((7, TILE), F32)], axis=0)


def _mix_call(layer, n_lat_tiles, n_proc, lgf, lgb, pc, q, k, v, g, su, sv, sf, x_lat, x_ctx, ctx_tile, mod, conv_w,
              sgu_w_bf, sgu_bias, w_out_bf, ln1, w_route_t, b_route_t, lg_rows, lg_cols, to_cast=()):
    bsz, length, _ = q.shape
    nb = BATCH_PER_STEP
    n_tiles = n_lat_tiles + 1
    tile_of = lambda s: n_lat_tiles - s
    cast_in, cast_out, cast_shapes = _cast_specs(to_cast, layer, lambda b, s, *_: b * n_tiles + s, bsz // nb * n_tiles)
    halos_per_tile = TILE // HALO
    n_halo = length // HALO
    cpt = TILE // CHUNK
    tok = lambda w: pl.BlockSpec((nb, TILE, w), lambda b, s, *_: (b, tile_of(s), 0))
    tok_out = lambda w: pl.BlockSpec((nb, TILE, w), lambda b, s, *_: (b, jnp.minimum(tile_of(s), n_proc - 1), 0))
    per_layer = lambda *shape: pl.BlockSpec((1,) + shape, lambda b, s, *_: (layer,) + (0,) * len(shape))
    grid_spec = pltpu.PrefetchScalarGridSpec(
        num_scalar_prefetch=2,
        grid=(bsz // nb, n_tiles),
        in_specs=[
            tok(3 * CONV_W),
            pl.BlockSpec((nb, HALO, 3 * CONV_W),
                         lambda b, s, *_: (b, jnp.maximum(tile_of(s) * halos_per_tile - 1, 0), 0)),
            pl.BlockSpec((nb, HALO, 3 * CONV_W),
                         lambda b, s, *_: (b, jnp.minimum((tile_of(s) + 1) * halos_per_tile, n_halo - 1), 0)),
            tok(RET_W), tok(RET_W), tok(RET_W), tok(RET_W), tok(SGU_W), tok(SGU_W),
            pl.BlockSpec((nb, cpt, RET_W, PAIR_W), lambda b, s, *_: (b, tile_of(s), 0, 0)),
            pl.BlockSpec((nb, TILE, D_MODEL), lambda b, s, *_: (b, jnp.minimum(tile_of(s), n_lat_tiles - 1), 0)),
            pl.BlockSpec((nb, TILE, D_MODEL), lambda b, s, *_: (b, ctx_tile, 0)),
            pl.BlockSpec((1, nb, 1, MOD_ROWS, D_MODEL),
                         lambda b, s, *_: (layer, b, tile_of(s) // n_lat_tiles, 0, 0)),
            per_layer(8, CONV_W),
            per_layer(SGU_GROUPS, CHUNK, CHUNK),
            per_layer(CHUNK, SGU_W),
            per_layer(D_MODEL, D_MODEL),
            per_layer(8, D_MODEL),
            per_layer(ROUTE_ROWS, D_MODEL),
            per_layer(ROUTE_ROWS, 1),
            pl.BlockSpec((1, 1, RET_W), lambda b, s, *_: (2 * layer, 0, 0)),
            pl.BlockSpec((1, 1, RET_W), lambda b, s, *_: (2 * layer + 1, 0, 0)),
            pl.BlockSpec((1, RET_W, 1), lambda b, s, *_: (2 * layer + 1, 0, 0)),
        ] + cast_in,
        out_specs=[tok_out(D_MODEL), tok_out(PACK_W),
                   pl.BlockSpec((nb, 1, 8, TILE), lambda b, s, *_: (b, jnp.minimum(tile_of(s), n_proc - 1), 0, 0))]
        + cast_out,
        scratch_shapes=[pltpu.VMEM((nb * TILE, D_MODEL), BF16),
                        pltpu.VMEM((RET_W // PAIR_W, 2 * CHUNK, CHUNK), F32),
                        pltpu.VMEM((2, CHUNK, RET_W), F32),
                        pltpu.VMEM((nb, RET_W, PAIR_W), F32)],
    )
    return pl.pallas_call(
        functools.partial(_mix_kernel, n_lat_tiles, len(to_cast)),
        grid_spec=grid_spec,
        out_shape=[jax.ShapeDtypeStruct((bsz, n_proc * TILE, D_MODEL), F32),
                   jax.ShapeDtypeStruct((bsz, n_proc * TILE, PACK_W), jnp.uint32),
                   jax.ShapeDtypeStruct((bsz, n_proc, 8, TILE), F32)] + cast_shapes,
        compiler_params=_params(2),
    )(lgf, lgb, pc, pc, pc, q, k, v, g, su, sv, sf, x_lat, x_ctx, mod, conv_w, sgu_w_bf, sgu_bias, w_out_bf,
      ln1, w_route_t, b_route_t, lg_rows, lg_rows, lg_cols, *to_cast)


def _rank_kernel(route_ref, class_ref, rank_ref, counts_ref, running):
    @pl.when(pl.program_id(0) == 0)
    def _():
        running[...] = jnp.zeros_like(running)

    sub = lax.broadcasted_iota(jnp.int32, (ROUTE_W, TILE), 0)
    earlier = (lax.broadcasted_iota(jnp.int32, (TILE, TILE), 0)
               < lax.broadcasted_iota(jnp.int32, (TILE, TILE), 1)).astype(BF16)
    for t in range(route_ref.shape[0]):
        cls = route_ref[t, 0:1, :].astype(jnp.int32)
        class_ref[t] = cls
        onehot = sub == cls
        before = jnp.dot(onehot.astype(BF16), earlier, preferred_element_type=F32) + running[...]
        rank_ref[t] = jnp.sum(jnp.where(onehot, before, 0.0), axis=0, keepdims=True).astype(jnp.int32)
        running[...] += jnp.sum(onehot.astype(F32), axis=1, keepdims=True)
    counts_ref[...] = jnp.broadcast_to(running[...], counts_ref.shape)


def _rank_call(route):
    n_tiles = route.shape[0]
    per_step = math.gcd(n_tiles, RANK_TILES)
    per_tile = pl.BlockSpec((per_step, 1, TILE), lambda i: (i, 0, 0))
    return pl.pallas_call(
        _rank_kernel,
        grid=(n_tiles // per_step,),
        in_specs=[pl.BlockSpec((per_step, 8, TILE), lambda i: (i, 0, 0))],
        out_specs=[per_tile, per_tile, pl.BlockSpec((ROUTE_W, ROUTE_W), lambda i: (0, 0))],
        out_shape=[jax.ShapeDtypeStruct((n_tiles, 1, TILE), jnp.int32), jax.ShapeDtypeStruct((n_tiles, 1, TILE), jnp.int32),
                   jax.ShapeDtypeStruct((ROUTE_W, ROUTE_W), F32)],
        scratch_shapes=[pltpu.VMEM((ROUTE_W, 1), F32)],
        compiler_params=_params(1),
    )(route)


def _dispatch_kernel(n_cast, dest_ref, pad_end_ref, pad_len_ref, n_used_ref, hm_ref, *refs):
    cast_in, xs_ref, cast_out = refs[:n_cast], refs[n_cast], refs[n_cast + 1:-3]
    zeros, sem, pad_sem = refs[-3:]
    for src, dst in zip(cast_in, cast_out):
        dst[...] = src[...].astype(BF16)
    step = pl.program_id(0)
    base = step * TILE
    half = BLOCK_M // 2

    def for_each_pad_copy(fn):
        def per_class(c, carry):
            off = pad_end_ref[c]
            n = pad_len_ref[c]
            for shift in range(BLOCK_M.bit_length() - 2, -1, -1):
                bit = 1 << shift
                off = off - (n & bit)

                @pl.when((n & bit) != 0)
                def _():
                    if bit >= 8:
                        fn(pltpu.make_async_copy(zeros.at[pl.ds(0, bit)], xs_ref.at[pl.ds(pl.multiple_of(off, 8), bit)],
                                                 pad_sem))
                    else:
                        for j in range(bit):
                            fn(pltpu.make_async_copy(zeros.at[pl.ds(0, 1)], xs_ref.at[pl.ds(off + j, 1)], pad_sem))

            return carry

        lax.fori_loop(0, N_CLASSES, per_class, 0)

        def per_half_block(j, carry):
            fn(pltpu.make_async_copy(zeros, xs_ref.at[pl.ds(pl.multiple_of(j * half, 8), half)], pad_sem))
            return carry

        lax.fori_loop(n_used_ref[0] * 2, xs_ref.shape[0] // half, per_half_block, 0)

    @pl.when(step == 0)
    def _():
        zeros[...] = jnp.zeros_like(zeros)
        for_each_pad_copy(lambda cp: cp.start())

    for r in range(TILE):
        pltpu.make_async_copy(hm_ref.at[pl.ds(r, 1)], xs_ref.at[pl.ds(dest_ref[base + r], 1)],
                              sem).start(priority=r % 2)
    for r in range(TILE):
        pltpu.make_async_copy(hm_ref.at[pl.ds(0, 1)], xs_ref.at[pl.ds(0, 1)], sem).wait()

    @pl.when(step == pl.num_programs(0) - 1)
    def _():
        for_each_pad_copy(lambda cp: cp.wait())


def _dispatch_call(layer, dest, pad_end, pad_len, n_used, hm_flat, n_slots, to_cast=()):
    n_tok = hm_flat.shape[0]
    cast_in, cast_out, cast_shapes = _cast_specs(to_cast, layer, lambda i, *_: i, n_tok // TILE)
    grid_spec = pltpu.PrefetchScalarGridSpec(
        num_scalar_prefetch=4,
        grid=(n_tok // TILE,),
        in_specs=[pl.BlockSpec((TILE, PACK_W), lambda i, *_: (i, 0))] + cast_in,
        out_specs=[pl.BlockSpec(memory_space=pl.ANY)] + cast_out,
        scratch_shapes=[pltpu.VMEM((BLOCK_M // 2, PACK_W), jnp.uint32), pltpu.SemaphoreType.DMA,
                        pltpu.SemaphoreType.DMA],
    )
    return pl.pallas_call(
        functools.partial(_dispatch_kernel, len(to_cast)),
        grid_spec=grid_spec,
        out_shape=[jax.ShapeDtypeStruct((n_slots, PACK_W), jnp.uint32)] + cast_shapes,
        compiler_params=_params(1),
    )(dest, pad_end, pad_len, n_used, hm_flat, *to_cast)


def _expert_kernel(blk_lo_ref, blk_hi_ref, blk_live_ref, n_used_ref, xs_ref, wr_ref, br_ref, wg_lo, wu_lo, wd_lo,
                   wg_hi, wu_hi, wd_hi, ys_ref):
    del n_used_ref
    i = pl.program_id(0)
    live = blk_live_ref[i]

    def experts_on(rows):
        xb = _unpack_bf16_pairs(xs_ref[0:rows, :])

        logits = jnp.dot(xb, wr_ref[0], preferred_element_type=F32) + br_ref[0]
        lane = lax.broadcasted_iota(jnp.int32, logits.shape, 1)
        gl = jnp.where(lane < N_GROUPS, logits, -jnp.inf)
        g_prob = 1.0 / jnp.sum(jnp.exp(gl - jnp.max(gl, axis=-1, keepdims=True)), axis=-1, keepdims=True)
        l_lo = jnp.sum(jnp.where(lane == N_GROUPS + blk_lo_ref[i], logits, 0.0), axis=-1, keepdims=True)
        l_hi = jnp.sum(jnp.where(lane == N_GROUPS + blk_hi_ref[i], logits, 0.0), axis=-1, keepdims=True)
        m = jnp.maximum(l_lo, l_hi)
        p_lo = jnp.exp(l_lo - m)
        p_hi = jnp.exp(l_hi - m)

        h_lo = jnp.dot(xb, wg_lo[0], preferred_element_type=F32)
        u_lo = jnp.dot(xb, wu_lo[0], preferred_element_type=F32)
        h_hi = jnp.dot(xb, wg_hi[0], preferred_element_type=F32)
        u_hi = jnp.dot(xb, wu_hi[0], preferred_element_type=F32)
        a_lo = (_silu(h_lo) * u_lo).astype(BF16)
        a_hi = (_silu(h_hi) * u_hi).astype(BF16)
        y_lo = jnp.dot(a_lo, wd_lo[0], preferred_element_type=F32)
        y_hi = jnp.dot(a_hi, wd_hi[0], preferred_element_type=F32)
        ys_ref[0:rows, :] = y_lo * (g_prob * (p_lo / (p_lo + p_hi))) + y_hi * (g_prob * (p_hi / (p_lo + p_hi)))
        if rows < BLOCK_M:
            ys_ref[rows:, :] = jnp.zeros((BLOCK_M - rows, D_MODEL), F32)

    half = BLOCK_M // 2

    @pl.when(live > half)
    def _():
        experts_on(BLOCK_M)

    @pl.when((live > 0) & (live <= half))
    def _():
        experts_on(half)

    @pl.when(live == 0)
    def _():
        ys_ref[...] = jnp.zeros_like(ys_ref)


def _expert_call(layer, blk_lo, blk_hi, blk_live, n_used, xs, w_route_bf, b_route, w_gate_bf, w_up_bf, w_down_bf):
    n_blocks = xs.shape[0] // BLOCK_M
    up_spec = lambda which: pl.BlockSpec((1, D_MODEL, EXPERT_HIDDEN), lambda i, lo, hi, lv, nu: ((lo, hi)[which][i], 0, 0))
    down_spec = lambda which: pl.BlockSpec((1, EXPERT_HIDDEN, D_MODEL), lambda i, lo, hi, lv, nu: ((lo, hi)[which][i], 0, 0))
    grid_spec = pltpu.PrefetchScalarGridSpec(
        num_scalar_prefetch=4,
        grid=(n_blocks,),
        in_specs=[pl.BlockSpec((BLOCK_M, PACK_W), lambda i, lo, hi, lv, nu: (jnp.minimum(i, nu[0] - 1), 0)),
                  pl.BlockSpec((1, D_MODEL, ROUTE_W), lambda i, *_: (layer, 0, 0)),
                  pl.BlockSpec((1, 1, ROUTE_W), lambda i, *_: (layer, 0, 0)),
                  up_spec(0), up_spec(0), down_spec(0), up_spec(1), up_spec(1), down_spec(1)],
        out_specs=pl.BlockSpec((BLOCK_M, D_MODEL), lambda i, *_: (i, 0)),
    )
    return pl.pallas_call(
        _expert_kernel,
        grid_spec=grid_spec,
        out_shape=jax.ShapeDtypeStruct((xs.shape[0], D_MODEL), F32),
        compiler_params=_params(1),
    )(blk_lo, blk_hi, blk_live, n_used, xs, w_route_bf, b_route, w_gate_bf, w_up_bf, w_down_bf, w_gate_bf, w_up_bf,
      w_down_bf)


def _combine_kernel(tiles_per_batch, dest_ref, ys_ref, x1_ref, mod_ref, ln_ref, out_ref, buf, sem):
    step = pl.program_id(0) * tiles_per_batch + pl.program_id(1)
    slot = step % 2
    has_next = step + 1 < pl.num_programs(0) * tiles_per_batch

    def gather(tile, to_slot):
        base = tile * TILE

        for r in range(TILE):
            pltpu.make_async_copy(ys_ref.at[pl.ds(dest_ref[base + r], 1)], buf.at[to_slot, pl.ds(r, 1)],
                                  sem.at[to_slot]).start(priority=r % 2)

    @pl.when(step == 0)
    def _():
        gather(step, slot)

    @pl.when(has_next)
    def _():
        gather(step + 1, 1 - slot)

    for r in range(TILE):
        pltpu.make_async_copy(ys_ref.at[pl.ds(0, 1)], buf.at[slot, pl.ds(0, 1)], sem.at[slot]).wait()

    mod = mod_ref[0, 0, 0]
    ln = ln_ref[0]
    out_ref[0] = _standardize(ALPHA * x1_ref[0] + mod[5:6] * buf[slot]) * ln[0:1] + ln[1:2]


def _combine_call(layer, n_lat_tiles, dest, ys, x1, mod, ln2):
    bsz, length, _ = x1.shape
    tiles_per_batch = length // TILE
    tok = lambda w: pl.BlockSpec((1, TILE, w), lambda b, i, *_: (b, i, 0))
    grid_spec = pltpu.PrefetchScalarGridSpec(
        num_scalar_prefetch=1,
        grid=(bsz, tiles_per_batch),
        in_specs=[
            pl.BlockSpec(memory_space=pl.ANY),
            tok(D_MODEL),
            pl.BlockSpec((1, 1, 1, MOD_ROWS, D_MODEL),
                         lambda b, i, *_: (layer, b, jnp.minimum(i // n_lat_tiles, 1), 0, 0)),
            pl.BlockSpec((1, 8, D_MODEL), lambda b, i, *_: (layer, 0, 0)),
        ],
        out_specs=tok(D_MODEL),
        scratch_shapes=[pltpu.VMEM((2, TILE, D_MODEL), F32), pltpu.SemaphoreType.DMA((2,))],
    )
    return pl.pallas_call(
        functools.partial(_combine_kernel, tiles_per_batch),
        grid_spec=grid_spec,
        out_shape=jax.ShapeDtypeStruct(x1.shape, F32),
        compiler_params=_params(2),
    )(dest, ys, x1, mod, ln2)


def _dispatch_plan(class_of_token, rank, counts):
    n_tok = class_of_token.shape[0]
    counts = counts.astype(jnp.int32)
    pcounts = (counts + BLOCK_M - 1) // BLOCK_M * BLOCK_M
    pends = jnp.cumsum(pcounts)
    pstarts = pends - pcounts
    classes = jnp.arange(ROUTE_W, dtype=jnp.int32)
    dest = jnp.sum(jnp.where(class_of_token[:, None] == classes[None, :], pstarts[None, :], 0), axis=1) + rank
    n_blocks = n_tok // BLOCK_M + N_CLASSES
    blk_start = jnp.arange(n_blocks, dtype=jnp.int32) * BLOCK_M
    blk_class = jnp.minimum(jnp.sum((pends[None, :] <= blk_start[:, None]).astype(jnp.int32), axis=1), N_CLASSES - 1)
    blk_live = jnp.clip(counts[blk_class] - (blk_start - pstarts[blk_class]), 0, BLOCK_M)
    n_used = pends[-1:] // BLOCK_M
    return (dest.astype(jnp.int32), pends, pcounts - counts, jnp.asarray(CLASS_LO)[blk_class],
            jnp.asarray(CLASS_HI)[blk_class], blk_live.astype(jnp.int32), n_used.astype(jnp.int32), n_blocks * BLOCK_M)


def _pad_rows(a, rows):
    return jnp.pad(a, [(0, 0)] * (a.ndim - 2) + [(0, rows - a.shape[-2]), (0, 0)])


def kernel(x, c, ctx, c_ctx, w_ada, b_ada, w_in, conv_w, ret_decay_fwd, ret_decay_bwd, sgu_w, sgu_b, w_out, ln1_g, ln1_b,
           router_group_w, router_group_b, router_expert_w, router_expert_b, moe_w_gate, moe_w_up, moe_w_down, ln2_g,
           ln2_b):
    bsz, seq, d = x.shape
    ctx_len = ctx.shape[1]
    assert d == D_MODEL and ctx_len == TILE and seq % TILE == 0 and seq % GRID_W == 0
    n_lat_tiles = seq // TILE

    cond = _pad_rows(jnp.concatenate([c, c_ctx[None, :]], axis=0), 16)
    ada = _ada_call(cond, w_ada, b_ada)
    mod_lat = ada[:, :bsz].reshape(DEPTH, bsz, N_MOD, d)
    mod_ctx = jnp.broadcast_to(ada[:, bsz].reshape(DEPTH, 1, N_MOD, d), mod_lat.shape)
    mod = _pad_rows(jnp.stack([mod_lat, mod_ctx], axis=2), MOD_ROWS)

    w_in_bf = w_in.astype(BF16)
    w_out_bf = w_out.astype(BF16)
    sgu_w_bf = sgu_w.astype(BF16)
    sgu_bias = jnp.repeat(jnp.swapaxes(sgu_b, 1, 2), SGU_W // SGU_GROUPS, axis=2)
    conv_w8 = _pad_rows(conv_w, 8)
    ln1 = _pad_rows(jnp.stack([ln1_g, ln1_b], axis=1), 8)
    ln2 = _pad_rows(jnp.stack([ln2_g, ln2_b], axis=1), 8)
    w_route = jnp.concatenate([router_group_w, jnp.swapaxes(router_expert_w, 1, 2).reshape(DEPTH, d, N_EXPERTS)], axis=2)
    w_route_bf = jnp.pad(w_route, ((0, 0), (0, 0), (0, ROUTE_W - w_route.shape[2]))).astype(BF16)
    b_route = jnp.concatenate([router_group_b, router_expert_b.reshape(DEPTH, N_EXPERTS)], axis=1)
    b_route = jnp.pad(b_route, ((0, 0), (0, ROUTE_W - b_route.shape[1])))[:, None, :].astype(F32)
    w_route_t = jnp.concatenate([_pad_rows(jnp.swapaxes(w_route[:, :, :N_GROUPS], 1, 2), EXPERT_ROW0),
                                 jnp.swapaxes(w_route[:, :, N_GROUPS:], 1, 2)], axis=1)
    w_route_t = _pad_rows(w_route_t, ROUTE_ROWS).astype(BF16)
    b_route_t = jnp.concatenate([_pad_rows(b_route[:, 0, :N_GROUPS, None], EXPERT_ROW0),
                                 b_route[:, 0, N_GROUPS:N_GROUPS + N_EXPERTS, None]], axis=1)
    b_route_t = _pad_rows(b_route_t, ROUTE_ROWS)
    lg = jnp.stack([jax.nn.log_sigmoid(ret_decay_fwd.astype(F32)), jax.nn.log_sigmoid(ret_decay_bwd.astype(F32))],
                   axis=1)
    lg_lanes = jnp.repeat(lg, RET_DK, axis=2).reshape(DEPTH * 2, RET_W)
    lg_rows = lg_lanes[:, None, :]
    lg_cols = lg_lanes[:, :, None]

    x_lat, x_ctx, ctx_tile = x, ctx, 0
    for layer in range(DEPTH):
        last = layer == DEPTH - 1
        pc, q, k, v, g, su, sv, sf, w_gate_bf = _inproj_call(layer, x_lat, x_ctx, ctx_tile, mod, w_in_bf, lg_rows, lg_cols,
                                                             n_lat_tiles, to_cast=(moe_w_gate.reshape(-1, EXPERT_HIDDEN),))
        n_proc = n_lat_tiles if last else n_lat_tiles + 1
        x1, hm, route, w_up_bf = _mix_call(layer, n_lat_tiles, n_proc, lg[layer, 0], lg[layer, 1], pc, q, k, v, g, su, sv,
                                           sf, x_lat, x_ctx, ctx_tile, mod, conv_w8, sgu_w_bf, sgu_bias, w_out_bf, ln1,
                                           w_route_t, b_route_t, lg_rows, lg_cols,
                                           to_cast=(moe_w_up.reshape(-1, EXPERT_HIDDEN),))
        class_of_token, rank, counts = _rank_call(route.reshape(-1, 8, TILE))
        dest, pad_end, pad_len, blk_lo, blk_hi, blk_live, n_used, n_slots = _dispatch_plan(class_of_token.reshape(-1),
                                                                                  rank.reshape(-1), counts[:, 0])
        xs, w_down_bf = _dispatch_call(layer, dest, pad_end, pad_len, n_used, hm.reshape(-1, PACK_W), n_slots,
                                       to_cast=(moe_w_down.reshape(-1, D_MODEL),))
        ys = _expert_call(layer, blk_lo, blk_hi, blk_live, n_used, xs, w_route_bf, b_route,
                          w_gate_bf.reshape(N_EXPERTS, D_MODEL, EXPERT_HIDDEN), w_up_bf.reshape(N_EXPERTS, D_MODEL, EXPERT_HIDDEN),
                          w_down_bf.reshape(N_EXPERTS, EXPERT_HIDDEN, D_MODEL))
        xa = _combine_call(layer, n_lat_tiles, dest, ys, x1, mod, ln2)
        x_lat, x_ctx, ctx_tile = xa, xa, n_lat_tiles
    return xa
```

```python
import functools
import math

import jax
import jax.numpy as jnp
import numpy as np
from jax import lax
from jax.experimental import pallas as pl
from jax.experimental.pallas import tpu as pltpu

F32 = jnp.float32
BF16 = jnp.bfloat16

D_MODEL = 1024
DEPTH = 2
GRID_W = 64
CONV_W = 256
RET_W = 512
RET_HEADS = 8
RET_DK = 64
PAIR_W = 2 * RET_DK
CHUNK = 128
SGU_W = 256
SGU_GROUPS = 4
IN_COLS = 3 * CONV_W + 4 * RET_W + 2 * SGU_W
N_GROUPS = 4
EXPERTS_PER_GROUP = 8
N_EXPERTS = N_GROUPS * EXPERTS_PER_GROUP
EXPERT_HIDDEN = 512
N_MOD = 6
MOD_ROWS = 8
LN_EPS = 1e-5
ALPHA = (2 * DEPTH) ** 0.25

TILE = 256
BATCH_PER_STEP = 2
DISPATCH_TILE = 2 * TILE
HALO = GRID_W
BLOCK_M = 256
ROUTE_W = 128
CAST_STEPS = 128
PACK_W = D_MODEL // 2
ROUTE_ROWS = 48
EXPERT_ROW0 = 8
RANK_TILES = 8
PAIRS_PER_GROUP = EXPERTS_PER_GROUP * (EXPERTS_PER_GROUP - 1) // 2
N_CLASSES = N_GROUPS * PAIRS_PER_GROUP
CLASS_LO = np.array([g * EXPERTS_PER_GROUP + lo for g in range(N_GROUPS) for lo in range(EXPERTS_PER_GROUP)
                     for hi in range(lo + 1, EXPERTS_PER_GROUP)] + [N_EXPERTS - 2] * (ROUTE_W - N_CLASSES), np.int32)
CLASS_HI = np.array([g * EXPERTS_PER_GROUP + hi for g in range(N_GROUPS) for lo in range(EXPERTS_PER_GROUP)
                     for hi in range(lo + 1, EXPERTS_PER_GROUP)] + [N_EXPERTS - 1] * (ROUTE_W - N_CLASSES), np.int32)
VMEM_LIMIT = 56 * 1024 * 1024


def _params(n_axes):
    return pltpu.CompilerParams(dimension_semantics=("arbitrary",) * n_axes, vmem_limit_bytes=VMEM_LIMIT)


def _standardize(v):
    mu = jnp.mean(v, axis=-1, keepdims=True)
    var = jnp.mean(jnp.square(v - mu), axis=-1, keepdims=True)
    return (v - mu) * lax.rsqrt(var + LN_EPS)


def _silu(v):
    return v * jax.nn.sigmoid(v)


def _pack_bf16_pairs(v):
    half = v.shape[1] // 2
    bits = lambda t: pltpu.bitcast(t.astype(BF16).astype(F32), jnp.uint32)
    return bits(v[:, :half]) | (bits(v[:, half:]) >> 16)


def _unpack_bf16_pairs(w):
    hi = pltpu.bitcast(w & jnp.uint32(0xFFFF0000), F32)
    lo = pltpu.bitcast(w << 16, F32)
    return jnp.concatenate([hi, lo], axis=1).astype(BF16)


def _ada_kernel(c_ref, w_ref, b_ref, o_ref):
    a = _silu(c_ref[...]).astype(BF16)
    o_ref[0] = jnp.dot(a, w_ref[0].astype(BF16), preferred_element_type=F32) + b_ref[0]


def _ada_call(cond, w_ada, b_ada):
    rows = cond.shape[0]
    cols = w_ada.shape[-1]
    tn = 1536
    return pl.pallas_call(
        _ada_kernel,
        grid=(DEPTH, cols // tn),
        in_specs=[
            pl.BlockSpec((rows, D_MODEL), lambda l, j: (0, 0)),
            pl.BlockSpec((1, D_MODEL, tn), lambda l, j: (l, 0, j)),
            pl.BlockSpec((1, 1, tn), lambda l, j: (l, 0, j)),
        ],
        out_specs=pl.BlockSpec((1, rows, tn), lambda l, j: (l, 0, j)),
        out_shape=jax.ShapeDtypeStruct((DEPTH, rows, cols), F32),
        compiler_params=_params(2),
    )(cond, w_ada, b_ada.reshape(DEPTH, 1, cols))


def _own_head_mask():
    row_head = lax.broadcasted_iota(jnp.int32, (RET_W, PAIR_W), 0) % PAIR_W // RET_DK
    lane_head = lax.broadcasted_iota(jnp.int32, (RET_W, PAIR_W), 1) // RET_DK
    return row_head == lane_head


def _pair_kv(kc, vc):
    contract_rows = (((0,), (0,)), ((), ()))
    return jnp.concatenate([
        lax.dot_general(kc[:, p * PAIR_W:(p + 1) * PAIR_W], vc[:, p * PAIR_W:(p + 1) * PAIR_W], contract_rows,
                        preferred_element_type=F32) for p in range(RET_W // PAIR_W)], axis=0)


def _inproj_kernel(n_lat_tiles, n_cast, x_ref, ctx_ref, mod_ref, w_ref, lgf_ref, lgfc_ref, *refs):
    cast_in, (pc_ref, q_ref, k_ref, v_ref, g_ref, su_ref, sv_ref, sf_ref), cast_out = (
        refs[:n_cast], refs[n_cast:n_cast + 8], refs[n_cast + 8:-1])
    state = refs[-1]
    for src, dst in zip(cast_in, cast_out):
        dst[...] = src[...].astype(BF16)
    step = pl.program_id(1)
    is_ctx = step == 0
    h = jnp.concatenate([
        (jnp.where(is_ctx, ctx_ref[j], x_ref[j]) * (1.0 + mod_ref[0, j, 0, 1:2]) + mod_ref[0, j, 0, 0:1]).astype(BF16)
        for j in range(BATCH_PER_STEP)], axis=0)

    def proj(lo, hi):
        return jnp.dot(h, w_ref[0, :, lo:hi], preferred_element_type=F32)

    def put(ref, val):
        for j in range(BATCH_PER_STEP):
            ref[j] = val[j * TILE:(j + 1) * TILE].astype(ref.dtype)

    o = 3 * CONV_W
    put(pc_ref, proj(0, o))
    put(q_ref, proj(o, o + RET_W))
    k = proj(o + RET_W, o + 2 * RET_W) * (RET_DK ** -0.5)
    put(k_ref, k)
    v = proj(o + 2 * RET_W, o + 3 * RET_W)
    put(v_ref, v)
    put(g_ref, proj(o + 3 * RET_W, o + 4 * RET_W))
    o += 4 * RET_W
    put(su_ref, proj(o, o + SGU_W))
    put(sv_ref, proj(o + SGU_W, o + 2 * SGU_W))

    @pl.when(step == 0)
    def _():
        state[...] = jnp.zeros_like(state)

    pos = lax.broadcasted_iota(jnp.int32, (CHUNK, 1), 0).astype(F32)
    k_decay = jnp.exp((CHUNK - 1.0 - pos) * lgf_ref[0])
    chunk_decay = jnp.exp(CHUNK * lgfc_ref[0])
    own_head = _own_head_mask()
    for j in range(BATCH_PER_STEP):
        kvs = [_pair_kv((k[rows] * k_decay).astype(BF16), v[rows].astype(BF16))
               for rows in (slice(j * TILE + c * CHUNK, j * TILE + (c + 1) * CHUNK) for c in range(TILE // CHUNK))]
        st = state[j]
        for c, kv in enumerate(kvs):
            sf_ref[j, c] = jnp.where(own_head, st, 0.0).astype(BF16)
            st = chunk_decay * st + kv
        state[j] = st


def _cast_specs(arrays, part, step_of, n_steps):
    steps = min(CAST_STEPS, 1 << (n_steps.bit_length() - 1))
    block = lambda a: (a.shape[0] // (DEPTH * steps), a.shape[1])
    slab = lambda *idx: jnp.minimum(step_of(*idx), steps - 1)
    return ([pl.BlockSpec(block(a), lambda *idx: (part * steps + slab(*idx), 0)) for a in arrays],
            [pl.BlockSpec(block(a), lambda *idx: (slab(*idx), 0)) for a in arrays],
            [jax.ShapeDtypeStruct((a.shape[0] // DEPTH, a.shape[1]), BF16) for a in arrays])


def _inproj_call(layer, x_lat, x_ctx, ctx_tile, mod, w_in_bf, lg_rows, lg_cols, n_lat_tiles, to_cast=()):
    bsz = x_lat.shape[0]
    assert bsz % BATCH_PER_STEP == 0
    nb = BATCH_PER_STEP
    length = (n_lat_tiles + 1) * TILE
    widths = (3 * CONV_W, RET_W, RET_W, RET_W, RET_W, SGU_W, SGU_W)
    cpt = TILE // CHUNK
    n_tiles = n_lat_tiles + 1
    tile_of = lambda s: (s + n_lat_tiles) % n_tiles
    cast_in, cast_out, cast_shapes = _cast_specs(to_cast, layer, lambda b, s: b * n_tiles + s, bsz // nb * n_tiles)
    return pl.pallas_call(
        functools.partial(_inproj_kernel, n_lat_tiles, len(to_cast)),
        grid=(bsz // nb, n_tiles),
        in_specs=[
            pl.BlockSpec((nb, TILE, D_MODEL), lambda b, s: (b, jnp.minimum(tile_of(s), n_lat_tiles - 1), 0)),
            pl.BlockSpec((nb, TILE, D_MODEL), lambda b, s: (b, ctx_tile, 0)),
            pl.BlockSpec((1, nb, 1, MOD_ROWS, D_MODEL), lambda b, s: (layer, b, tile_of(s) // n_lat_tiles, 0, 0)),
            pl.BlockSpec((1, D_MODEL, IN_COLS), lambda b, s: (layer, 0, 0)),
            pl.BlockSpec((1, 1, RET_W), lambda b, s: (2 * layer, 0, 0)),
            pl.BlockSpec((1, RET_W, 1), lambda b, s: (2 * layer, 0, 0)),
        ] + cast_in,
        out_specs=[pl.BlockSpec((nb, TILE, w), lambda b, s: (b, tile_of(s), 0)) for w in widths]
        + [pl.BlockSpec((nb, cpt, RET_W, PAIR_W), lambda b, s: (b, tile_of(s), 0, 0))] + cast_out,
        out_shape=[jax.ShapeDtypeStruct((bsz, length, w), BF16 if 1 <= j <= 3 else F32) for j, w in enumerate(widths)]
        + [jax.ShapeDtypeStruct((bsz, length // CHUNK, RET_W, PAIR_W), BF16)] + cast_shapes,
        scratch_shapes=[pltpu.VMEM((nb, RET_W, PAIR_W), F32)],
        compiler_params=_params(2),
    )(x_lat, x_ctx, mod, w_in_bf, lg_rows, lg_cols, *to_cast)


def _route_class(lt):
    assert EXPERTS_PER_GROUP == 8 and N_GROUPS <= 8
    sub = lax.broadcasted_iota(jnp.int32, (8, lt.shape[1]), 0)
    neg = jnp.float32(-jnp.inf)

    def top(vals):
        m = jnp.max(vals, axis=0, keepdims=True)
        return jnp.min(jnp.where(vals == m, sub, 8), axis=0, keepdims=True)

    gidx = top(jnp.where(sub < N_GROUPS, lt[0:8], neg))
    pair = jnp.zeros_like(gidx)
    for g in range(N_GROUPS):
        e = lt[EXPERT_ROW0 + 8 * g:EXPERT_ROW0 + 8 * (g + 1)]
        i1 = top(e)
        i2 = top(jnp.where(sub == i1, neg, e))
        e_lo = jnp.minimum(i1, i2)
        e_hi = jnp.maximum(i1, i2)
        pair_g = e_lo * (EXPERTS_PER_GROUP - 1) - ((e_lo * (e_lo - 1)) >> 1) + (e_hi - e_lo - 1)
        pair = jnp.where(gidx == g, pair_g, pair)
    return gidx * PAIRS_PER_GROUP + pair


def _mix_kernel(n_lat_tiles, n_cast, lgf_ref, lgb_ref, pc_ref, hp_ref, hn_ref, q_ref, k_ref, v_ref, g_ref, su_ref, sv_ref,
                sf_ref, x_ref, ctx_ref, mod_ref, convw_ref, sguw_ref, sgub_ref, wout_ref, ln_ref, wr_ref, br_ref,
                lgfr_ref, lgbr_ref, lgbc_ref, *refs):
    cast_in, (x1_ref, hm_ref, route_ref), cast_out = refs[:n_cast], refs[n_cast:n_cast + 3], refs[n_cast + 3:-4]
    ycat, dec, qdec, state = refs[-4:]
    for src, dst in zip(cast_in, cast_out):
        dst[...] = src[...].astype(BF16)
    step = pl.program_id(1)
    i = n_lat_tiles - step
    is_ctx = step == 0
    row = lax.broadcasted_iota(jnp.int32, (TILE, 1), 0)
    nb = BATCH_PER_STEP

    @pl.when(step == 0)
    def _():
        state[...] = jnp.zeros_like(state)

    pos_b = lax.broadcasted_iota(jnp.int32, (CHUNK, 1), 0).astype(F32)
    k_decay_b = jnp.exp(pos_b * lgbr_ref[0])
    chunk_decay_b = jnp.exp(CHUNK * lgbc_ref[0])
    own_head = _own_head_mask()
    sb = {}
    for j in range(nb):
        kvs = {c: _pair_kv((k_ref[j, c * CHUNK:(c + 1) * CHUNK, :].astype(F32) * k_decay_b).astype(BF16),
                           v_ref[j, c * CHUNK:(c + 1) * CHUNK, :]) for c in range(TILE // CHUNK)}
        st = state[j]
        for c in reversed(range(TILE // CHUNK)):
            sb[j, c] = jnp.where(own_head, st, 0.0).astype(BF16)
            st = chunk_decay_b * st + kvs[c]
        state[j] = st

    @pl.when((pl.program_id(0) == 0) & (step == 0))
    def _():
        pos = lax.broadcasted_iota(jnp.int32, (CHUNK, 1), 0).astype(F32)
        rel = pos - lax.broadcasted_iota(jnp.int32, (1, CHUNK), 1).astype(F32)
        for h in range(RET_HEADS):
            dec[h // 2, (h % 2) * CHUNK:(h % 2 + 1) * CHUNK, :] = jnp.where(
                rel > 0, jnp.exp(lgf_ref[h] * jnp.maximum(rel, 0.0)),
                jnp.where(rel < 0, jnp.exp(lgb_ref[h] * jnp.maximum(-rel, 0.0)), 2.0))
        qdec[0] = jnp.exp(lgfr_ref[0] * (pos + 1.0))
        qdec[1] = jnp.exp(lgbr_ref[0] * (CHUNK - pos))

    low_head = lax.broadcasted_iota(jnp.int32, (1, PAIR_W), 1) < RET_DK

    def half_norm(o):
        def half_mean(t):
            lo = jnp.sum(jnp.where(low_head, t, 0.0), axis=-1, keepdims=True)
            hi = jnp.sum(jnp.where(low_head, 0.0, t), axis=-1, keepdims=True)
            return jnp.where(low_head, lo, hi) * (1.0 / RET_DK)

        centred = o - half_mean(o)
        return centred * lax.rsqrt(half_mean(jnp.square(centred)) + LN_EPS)

    tiles = [(j, slice(c * CHUNK, (c + 1) * CHUNK), slice(p * PAIR_W, (p + 1) * PAIR_W), c, p)
             for j in range(nb) for c in range(TILE // CHUNK) for p in range(RET_W // PAIR_W)]
    out_rows = lambda j, rows: slice(j * TILE + rows.start, j * TILE + rows.stop)
    scores = []
    for j, rows, cols, c, p in tiles:
        qp = q_ref[j, rows, cols]
        zero = jnp.zeros_like(qp)
        q_split = jnp.concatenate([jnp.where(low_head, qp, zero), jnp.where(low_head, zero, qp)], axis=0)
        scores.append(lax.dot_general(q_split, k_ref[j, rows, cols],
                                      (((1,), (1,)), ((), ())), preferred_element_type=F32))

    line_mask = jnp.where(is_ctx, TILE - 1, GRID_W - 1)
    first = (row & line_mask) == 0
    last = (row & line_mask) == line_mask
    along_seq = lax.broadcasted_iota(jnp.int32, (1, CONV_W), 1) < jnp.where(is_ctx, CONV_W, CONV_W // 2)
    cw = convw_ref[0]
    for j in range(nb):
        pc = pc_ref[j]
        z = pc[:, CONV_W:2 * CONV_W] * pc[:, 2 * CONV_W:3 * CONV_W]
        z_prev = jnp.where(first, 0.0, pltpu.roll(z, 1, 0))
        z_next = jnp.where(last, 0.0, pltpu.roll(z, TILE - 1, 0))
        hp = hp_ref[j]
        hn = hn_ref[j]
        z_top = jnp.where(i == 0, 0.0, hp[:, CONV_W:2 * CONV_W] * hp[:, 2 * CONV_W:3 * CONV_W])
        z_bot = jnp.where(i == n_lat_tiles - 1, 0.0, hn[:, CONV_W:2 * CONV_W] * hn[:, 2 * CONV_W:3 * CONV_W])
        zcat = jnp.concatenate([z_top, z, z_bot], axis=0)
        z_up = zcat[0:TILE]
        z_down = zcat[2 * HALO:2 * HALO + TILE]
        conv = cw[0:1] * jnp.where(along_seq, z_prev, z_up) + cw[1:2] * z
        conv = conv + cw[2:3] * jnp.where(along_seq, z_next, z_down)
        ycat[j * TILE:(j + 1) * TILE, 0:CONV_W] = (pc[:, 0:CONV_W] * conv).astype(BF16)

    outs = []
    for (j, rows, cols, c, p), sc in zip(tiles, scores):
        qp = q_ref[j, rows, cols].astype(F32)
        vp = v_ref[j, rows, cols]
        zero = jnp.zeros_like(vp)
        sc = sc * dec[p]
        lhs = jnp.concatenate([sc[0:CHUNK].astype(BF16), sc[CHUNK:].astype(BF16),
                               (qp * qdec[0, :, cols]).astype(BF16), (qp * qdec[1, :, cols]).astype(BF16)], axis=1)
        rhs = jnp.concatenate([jnp.where(low_head, vp, zero), jnp.where(low_head, zero, vp),
                               sf_ref[j, c, cols, :], sb[j, c][cols, :]], axis=0)
        outs.append(jnp.dot(lhs, rhs, preferred_element_type=F32))

    group = lax.broadcasted_iota(jnp.int32, (1, SGU_W), 1) // (SGU_W // SGU_GROUPS)
    for j in range(nb):
        vn = _standardize(sv_ref[j]).astype(BF16)
        for c in range(TILE // CHUNK):
            rows = slice(c * CHUNK, (c + 1) * CHUNK)
            mixed = jnp.zeros((CHUNK, SGU_W), F32)
            for gi in range(SGU_GROUPS):
                m = jnp.dot(sguw_ref[0, gi], vn[rows], preferred_element_type=F32)
                mixed = jnp.where(group == gi, m, mixed)
            ycat[out_rows(j, rows), CONV_W + RET_W:] = (su_ref[j, rows, :] * (mixed + sgub_ref[0])).astype(BF16)

    for (j, rows, cols, c, p), o in zip(tiles, outs):
        ycat[out_rows(j, rows), CONV_W + p * PAIR_W:CONV_W + (p + 1) * PAIR_W] = (
            _silu(g_ref[j, rows, cols]) * half_norm(o)).astype(BF16)

    y = jnp.dot(ycat[...], wout_ref[0], preferred_element_type=F32)
    ln = ln_ref[0]
    hms = []
    for j in range(nb):
        mod = mod_ref[0, j, 0]
        x1 = _standardize(ALPHA * jnp.where(is_ctx, ctx_ref[j], x_ref[j]) + mod[2:3] * y[j * TILE:(j + 1) * TILE])
        x1 = x1 * ln[0:1] + ln[1:2]
        x1_ref[j] = x1
        hm = x1 * (1.0 + mod[4:5]) + mod[3:4]
        hm_ref[j] = _pack_bf16_pairs(hm)
        hms.append(hm.astype(BF16))
    lt = lax.dot_general(wr_ref[0], jnp.concatenate(hms, axis=0), (((1,), (1,)), ((), ())),
                         preferred_element_type=F32) + br_ref[0]
    cls = _route_class(lt).astype(F32)
    for j in range(nb):
        route_ref[j, 0] = jnp.concatenate([cls[:, j * TILE:(j + 1) * TILE], jnp.zeros((7, TILE), F32)], axis=0)


def _mix_call(layer, n_lat_tiles, n_proc, lgf, lgb, pc, q, k, v, g, su, sv, sf, x_lat, x_ctx, ctx_tile, mod, conv_w,
              sgu_w_bf, sgu_bias, w_out_bf, ln1, w_route_t, b_route_t, lg_rows, lg_cols, to_cast=()):
    bsz, length, _ = q.shape
    nb = BATCH_PER_STEP
    n_tiles = n_lat_tiles + 1
    tile_of = lambda s: n_lat_tiles - s
    cast_in, cast_out, cast_shapes = _cast_specs(to_cast, layer, lambda b, s, *_: b * n_tiles + s, bsz // nb * n_tiles)
    halos_per_tile = TILE // HALO
    n_halo = length // HALO
    cpt = TILE // CHUNK
    tok = lambda w: pl.BlockSpec((nb, TILE, w), lambda b, s, *_: (b, tile_of(s), 0))
    tok_out = lambda w: pl.BlockSpec((nb, TILE, w), lambda b, s, *_: (b, jnp.minimum(tile_of(s), n_proc - 1), 0))
    per_layer = lambda *shape: pl.BlockSpec((1,) + shape, lambda b, s, *_: (layer,) + (0,) * len(shape))
    grid_spec = pltpu.PrefetchScalarGridSpec(
        num_scalar_prefetch=2,
        grid=(bsz // nb, n_tiles),
        in_specs=[
            tok(3 * CONV_W),
            pl.BlockSpec((nb, HALO, 3 * CONV_W),
                         lambda b, s, *_: (b, jnp.maximum(tile_of(s) * halos_per_tile - 1, 0), 0)),
            pl.BlockSpec((nb, HALO, 3 * CONV_W),
                         lambda b, s, *_: (b, jnp.minimum((tile_of(s) + 1) * halos_per_tile, n_halo - 1), 0)),
            tok(RET_W), tok(RET_W), tok(RET_W), tok(RET_W), tok(SGU_W), tok(SGU_W),
            pl.BlockSpec((nb, cpt, RET_W, PAIR_W), lambda b, s, *_: (b, tile_of(s), 0, 0)),
            pl.BlockSpec((nb, TILE, D_MODEL), lambda b, s, *_: (b, jnp.minimum(tile_of(s), n_lat_tiles - 1), 0)),
            pl.BlockSpec((nb, TILE, D_MODEL), lambda b, s, *_: (b, ctx_tile, 0)),
            pl.BlockSpec((1, nb, 1, MOD_ROWS, D_MODEL),
                         lambda b, s, *_: (layer, b, tile_of(s) // n_lat_tiles, 0, 0)),
            per_layer(8, CONV_W),
            per_layer(SGU_GROUPS, CHUNK, CHUNK),
            per_layer(CHUNK, SGU_W),
            per_layer(D_MODEL, D_MODEL),
            per_layer(8, D_MODEL),
            per_layer(ROUTE_ROWS, D_MODEL),
            per_layer(ROUTE_ROWS, 1),
            pl.BlockSpec((1, 1, RET_W), lambda b, s, *_: (2 * layer, 0, 0)),
            pl.BlockSpec((1, 1, RET_W), lambda b, s, *_: (2 * layer + 1, 0, 0)),
            pl.BlockSpec((1, RET_W, 1), lambda b, s, *_: (2 * layer + 1, 0, 0)),
        ] + cast_in,
        out_specs=[tok_out(D_MODEL), tok_out(PACK_W),
                   pl.BlockSpec((nb, 1, 8, TILE), lambda b, s, *_: (b, jnp.minimum(tile_of(s), n_proc - 1), 0, 0))]
        + cast_out,
        scratch_shapes=[pltpu.VMEM((nb * TILE, D_MODEL), BF16),
                        pltpu.VMEM((RET_W // PAIR_W, 2 * CHUNK, CHUNK), F32),
                        pltpu.VMEM((2, CHUNK, RET_W), F32),
                        pltpu.VMEM((nb, RET_W, PAIR_W), F32)],
    )
    return pl.pallas_call(
        functools.partial(_mix_kernel, n_lat_tiles, len(to_cast)),
        grid_spec=grid_spec,
        out_shape=[jax.ShapeDtypeStruct((bsz, n_proc * TILE, D_MODEL), F32),
                   jax.ShapeDtypeStruct((bsz, n_proc * TILE, PACK_W), jnp.uint32),
                   jax.ShapeDtypeStruct((bsz, n_proc, 8, TILE), F32)] + cast_shapes,
        compiler_params=_params(2),
    )(lgf, lgb, pc, pc, pc, q, k, v, g, su, sv, sf, x_lat, x_ctx, mod, conv_w, sgu_w_bf, sgu_bias, w_out_bf,
      ln1, w_route_t, b_route_t, lg_rows, lg_rows, lg_cols, *to_cast)


def _rank_kernel(route_ref, class_ref, rank_ref, counts_ref, running):
    @pl.when(pl.program_id(0) == 0)
    def _():
        running[...] = jnp.zeros_like(running)

    sub = lax.broadcasted_iota(jnp.int32, (ROUTE_W, TILE), 0)
    earlier = (lax.broadcasted_iota(jnp.int32, (TILE, TILE), 0)
               < lax.broadcasted_iota(jnp.int32, (TILE, TILE), 1)).astype(BF16)
    for t in range(route_ref.shape[0]):
        cls = route_ref[t, 0:1, :].astype(jnp.int32)
        class_ref[t] = cls
        onehot = sub == cls
        before = jnp.dot(onehot.astype(BF16), earlier, preferred_element_type=F32) + running[...]
        rank_ref[t] = jnp.sum(jnp.where(onehot, before, 0.0), axis=0, keepdims=True).astype(jnp.int32)
        running[...] += jnp.sum(onehot.astype(F32), axis=1, keepdims=True)
    counts_ref[...] = jnp.broadcast_to(running[...], counts_ref.shape)


def _rank_call(route):
    n_tiles = route.shape[0]
    per_step = math.gcd(n_tiles, RANK_TILES)
    per_tile = pl.BlockSpec((per_step, 1, TILE), lambda i: (i, 0, 0))
    return pl.pallas_call(
        _rank_kernel,
        grid=(n_tiles // per_step,),
        in_specs=[pl.BlockSpec((per_step, 8, TILE), lambda i: (i, 0, 0))],
        out_specs=[per_tile, per_tile, pl.BlockSpec((ROUTE_W, ROUTE_W), lambda i: (0, 0))],
        out_shape=[jax.ShapeDtypeStruct((n_tiles, 1, TILE), jnp.int32), jax.ShapeDtypeStruct((n_tiles, 1, TILE), jnp.int32),
                   jax.ShapeDtypeStruct((ROUTE_W, ROUTE_W), F32)],
        scratch_shapes=[pltpu.VMEM((ROUTE_W, 1), F32)],
        compiler_params=_params(1),
    )(route)


def _dispatch_kernel(n_cast, dest_ref, pad_end_ref, pad_len_ref, n_used_ref, hm_ref, *refs):
    cast_in, xs_ref, cast_out = refs[:n_cast], refs[n_cast], refs[n_cast + 1:-3]
    zeros, sem, pad_sem = refs[-3:]
    for src, dst in zip(cast_in, cast_out):
        dst[...] = src[...].astype(BF16)
    step = pl.program_id(0)
    base = step * DISPATCH_TILE
    half = BLOCK_M // 2

    def for_each_pad_copy(fn):
        def per_class(c, carry):
            off = pad_end_ref[c]
            n = pad_len_ref[c]
            for shift in range(BLOCK_M.bit_length() - 2, -1, -1):
                bit = 1 << shift
                off = off - (n & bit)

                @pl.when((n & bit) != 0)
                def _():
                    if bit >= 8:
                        fn(pltpu.make_async_copy(zeros.at[pl.ds(0, bit)], xs_ref.at[pl.ds(pl.multiple_of(off, 8), bit)],
                                                 pad_sem))
                    else:
                        for j in range(bit):
                            fn(pltpu.make_async_copy(zeros.at[pl.ds(0, 1)], xs_ref.at[pl.ds(off + j, 1)], pad_sem))

            return carry

        lax.fori_loop(0, N_CLASSES, per_class, 0)

        def per_half_block(j, carry):
            fn(pltpu.make_async_copy(zeros, xs_ref.at[pl.ds(pl.multiple_of(j * half, 8), half)], pad_sem))
            return carry

        lax.fori_loop(n_used_ref[0] * 2, xs_ref.shape[0] // half, per_half_block, 0)

    @pl.when(step == 0)
    def _():
        zeros[...] = jnp.zeros_like(zeros)
        for_each_pad_copy(lambda cp: cp.start())

    for r in range(DISPATCH_TILE):
        pltpu.make_async_copy(hm_ref.at[pl.ds(r, 1)], xs_ref.at[pl.ds(dest_ref[base + r], 1)],
                              sem).start(priority=r % 2)
    for r in range(DISPATCH_TILE):
        pltpu.make_async_copy(hm_ref.at[pl.ds(0, 1)], xs_ref.at[pl.ds(0, 1)], sem).wait()

    @pl.when(step == pl.num_programs(0) - 1)
    def _():
        for_each_pad_copy(lambda cp: cp.wait())


def _dispatch_call(layer, dest, pad_end, pad_len, n_used, hm_flat, n_slots, to_cast=()):
    n_tok = hm_flat.shape[0]
    assert n_tok % DISPATCH_TILE == 0
    cast_in, cast_out, cast_shapes = _cast_specs(to_cast, layer, lambda i, *_: i, n_tok // DISPATCH_TILE)
    grid_spec = pltpu.PrefetchScalarGridSpec(
        num_scalar_prefetch=4,
        grid=(n_tok // DISPATCH_TILE,),
        in_specs=[pl.BlockSpec((DISPATCH_TILE, PACK_W), lambda i, *_: (i, 0))] + cast_in,
        out_specs=[pl.BlockSpec(memory_space=pl.ANY)] + cast_out,
        scratch_shapes=[pltpu.VMEM((BLOCK_M // 2, PACK_W), jnp.uint32), pltpu.SemaphoreType.DMA,
                        pltpu.SemaphoreType.DMA],
    )
    return pl.pallas_call(
        functools.partial(_dispatch_kernel, len(to_cast)),
        grid_spec=grid_spec,
        out_shape=[jax.ShapeDtypeStruct((n_slots, PACK_W), jnp.uint32)] + cast_shapes,
        compiler_params=_params(1),
    )(dest, pad_end, pad_len, n_used, hm_flat, *to_cast)


def _expert_kernel(blk_lo_ref, blk_hi_ref, blk_live_ref, n_used_ref, xs_ref, wr_ref, br_ref, wg_lo, wu_lo, wd_lo,
                   wg_hi, wu_hi, wd_hi, ys_ref):
    del n_used_ref
    i = pl.program_id(0)
    live = blk_live_ref[i]

    def experts_on(rows):
        xb = _unpack_bf16_pairs(xs_ref[0:rows, :])

        logits = jnp.dot(xb, wr_ref[0], preferred_element_type=F32) + br_ref[0]
        lane = lax.broadcasted_iota(jnp.int32, logits.shape, 1)
        gl = jnp.where(lane < N_GROUPS, logits, -jnp.inf)
        g_prob = 1.0 / jnp.sum(jnp.exp(gl - jnp.max(gl, axis=-1, keepdims=True)), axis=-1, keepdims=True)
        l_lo = jnp.sum(jnp.where(lane == N_GROUPS + blk_lo_ref[i], logits, 0.0), axis=-1, keepdims=True)
        l_hi = jnp.sum(jnp.where(lane == N_GROUPS + blk_hi_ref[i], logits, 0.0), axis=-1, keepdims=True)
        m = jnp.maximum(l_lo, l_hi)
        p_lo = jnp.exp(l_lo - m)
        p_hi = jnp.exp(l_hi - m)

        h_lo = jnp.dot(xb, wg_lo[0], preferred_element_type=F32)
        u_lo = jnp.dot(xb, wu_lo[0], preferred_element_type=F32)
        h_hi = jnp.dot(xb, wg_hi[0], preferred_element_type=F32)
        u_hi = jnp.dot(xb, wu_hi[0], preferred_element_type=F32)
        a_lo = (_silu(h_lo) * u_lo).astype(BF16)
        a_hi = (_silu(h_hi) * u_hi).astype(BF16)
        y_lo = jnp.dot(a_lo, wd_lo[0], preferred_element_type=F32)
        y_hi = jnp.dot(a_hi, wd_hi[0], preferred_element_type=F32)
        ys_ref[0:rows, :] = y_lo * (g_prob * (p_lo / (p_lo + p_hi))) + y_hi * (g_prob * (p_hi / (p_lo + p_hi)))
        if rows < BLOCK_M:
            ys_ref[rows:, :] = jnp.zeros((BLOCK_M - rows, D_MODEL), F32)

    half = BLOCK_M // 2

    @pl.when(live > half)
    def _():
        experts_on(BLOCK_M)

    @pl.when((live > 0) & (live <= half))
    def _():
        experts_on(half)

    @pl.when(live == 0)
    def _():
        ys_ref[...] = jnp.zeros_like(ys_ref)


def _expert_call(layer, blk_lo, blk_hi, blk_live, n_used, xs, w_route_bf, b_route, w_gate_bf, w_up_bf, w_down_bf):
    n_blocks = xs.shape[0] // BLOCK_M
    up_spec = lambda which: pl.BlockSpec((1, D_MODEL, EXPERT_HIDDEN), lambda i, lo, hi, lv, nu: ((lo, hi)[which][i], 0, 0))
    down_spec = lambda which: pl.BlockSpec((1, EXPERT_HIDDEN, D_MODEL), lambda i, lo, hi, lv, nu: ((lo, hi)[which][i], 0, 0))
    grid_spec = pltpu.PrefetchScalarGridSpec(
        num_scalar_prefetch=4,
        grid=(n_blocks,),
        in_specs=[pl.BlockSpec((BLOCK_M, PACK_W), lambda i, lo, hi, lv, nu: (jnp.minimum(i, nu[0] - 1), 0)),
                  pl.BlockSpec((1, D_MODEL, ROUTE_W), lambda i, *_: (layer, 0, 0)),
                  pl.BlockSpec((1, 1, ROUTE_W), lambda i, *_: (layer, 0, 0)),
                  up_spec(0), up_spec(0), down_spec(0), up_spec(1), up_spec(1), down_spec(1)],
        out_specs=pl.BlockSpec((BLOCK_M, D_MODEL), lambda i, *_: (i, 0)),
    )
    return pl.pallas_call(
        _expert_kernel,
        grid_spec=grid_spec,
        out_shape=jax.ShapeDtypeStruct((xs.shape[0], D_MODEL), F32),
        compiler_params=_params(1),
    )(blk_lo, blk_hi, blk_live, n_used, xs, w_route_bf, b_route, w_gate_bf, w_up_bf, w_down_bf, w_gate_bf, w_up_bf,
      w_down_bf)


def _combine_kernel(tiles_per_batch, dest_ref, ys_ref, x1_ref, mod_ref, ln_ref, out_ref, buf, sem):
    nb = BATCH_PER_STEP
    group, i = pl.program_id(0), pl.program_id(1)
    step = group * tiles_per_batch + i
    slot = step % 2
    has_next = step + 1 < pl.num_programs(0) * tiles_per_batch
    wrap = i + 1 == tiles_per_batch

    def gather(grp, tile, to_slot):
        for j in range(nb):
            base = ((grp * nb + j) * tiles_per_batch + tile) * TILE
            for r in range(TILE):
                pltpu.make_async_copy(ys_ref.at[pl.ds(dest_ref[base + r], 1)], buf.at[to_slot, j, pl.ds(r, 1)],
                                      sem.at[to_slot]).start(priority=r % 2)

    @pl.when(step == 0)
    def _():
        gather(group, i, slot)

    @pl.when(has_next)
    def _():
        gather(jnp.where(wrap, group + 1, group), jnp.where(wrap, 0, i + 1), 1 - slot)

    for r in range(nb * TILE):
        pltpu.make_async_copy(ys_ref.at[pl.ds(0, 1)], buf.at[slot, 0, pl.ds(0, 1)], sem.at[slot]).wait()

    ln = ln_ref[0]
    for j in range(nb):
        out_ref[j] = _standardize(ALPHA * x1_ref[j] + mod_ref[0, j, 0, 5:6] * buf[slot, j]) * ln[0:1] + ln[1:2]


def _combine_call(layer, n_lat_tiles, dest, ys, x1, mod, ln2):
    bsz, length, _ = x1.shape
    nb = BATCH_PER_STEP
    tiles_per_batch = length // TILE
    tok = lambda w: pl.BlockSpec((nb, TILE, w), lambda b, i, *_: (b, i, 0))
    grid_spec = pltpu.PrefetchScalarGridSpec(
        num_scalar_prefetch=1,
        grid=(bsz // nb, tiles_per_batch),
        in_specs=[
            pl.BlockSpec(memory_space=pl.ANY),
            tok(D_MODEL),
            pl.BlockSpec((1, nb, 1, MOD_ROWS, D_MODEL),
                         lambda b, i, *_: (layer, b, jnp.minimum(i // n_lat_tiles, 1), 0, 0)),
            pl.BlockSpec((1, 8, D_MODEL), lambda b, i, *_: (layer, 0, 0)),
        ],
        out_specs=tok(D_MODEL),
        scratch_shapes=[pltpu.VMEM((2, nb, TILE, D_MODEL), F32), pltpu.SemaphoreType.DMA((2,))],
    )
    return pl.pallas_call(
        functools.partial(_combine_kernel, tiles_per_batch),
        grid_spec=grid_spec,
        out_shape=jax.ShapeDtypeStruct(x1.shape, F32),
        compiler_params=_params(2),
    )(dest, ys, x1, mod, ln2)


def _dispatch_plan(class_of_token, rank, counts):
    n_tok = class_of_token.shape[0]
    counts = counts.astype(jnp.int32)
    pcounts = (counts + BLOCK_M - 1) // BLOCK_M * BLOCK_M
    pends = jnp.cumsum(pcounts)
    pstarts = pends - pcounts
    classes = jnp.arange(ROUTE_W, dtype=jnp.int32)
    dest = jnp.sum(jnp.where(class_of_token[:, None] == classes[None, :], pstarts[None, :], 0), axis=1) + rank
    n_blocks = n_tok // BLOCK_M + N_CLASSES
    blk_start = jnp.arange(n_blocks, dtype=jnp.int32) * BLOCK_M
    blk_class = jnp.minimum(jnp.sum((pends[None, :] <= blk_start[:, None]).astype(jnp.int32), axis=1), N_CLASSES - 1)
    blk_live = jnp.clip(counts[blk_class] - (blk_start - pstarts[blk_class]), 0, BLOCK_M)
    n_used = pends[-1:] // BLOCK_M
    return (dest.astype(jnp.int32), pends, pcounts - counts, jnp.asarray(CLASS_LO)[blk_class],
            jnp.asarray(CLASS_HI)[blk_class], blk_live.astype(jnp.int32), n_used.astype(jnp.int32), n_blocks * BLOCK_M)


def _pad_rows(a, rows):
    return jnp.pad(a, [(0, 0)] * (a.ndim - 2) + [(0, rows - a.shape[-2]), (0, 0)])


def kernel(x, c, ctx, c_ctx, w_ada, b_ada, w_in, conv_w, ret_decay_fwd, ret_decay_bwd, sgu_w, sgu_b, w_out, ln1_g, ln1_b,
           router_group_w, router_group_b, router_expert_w, router_expert_b, moe_w_gate, moe_w_up, moe_w_down, ln2_g,
           ln2_b):
    bsz, seq, d = x.shape
    ctx_len = ctx.shape[1]
    assert d == D_MODEL and ctx_len == TILE and seq % TILE == 0 and seq % GRID_W == 0
    n_lat_tiles = seq // TILE

    cond = _pad_rows(jnp.concatenate([c, c_ctx[None, :]], axis=0), 16)
    ada = _ada_call(cond, w_ada, b_ada)
    mod_lat = ada[:, :bsz].reshape(DEPTH, bsz, N_MOD, d)
    mod_ctx = jnp.broadcast_to(ada[:, bsz].reshape(DEPTH, 1, N_MOD, d), mod_lat.shape)
    mod = _pad_rows(jnp.stack([mod_lat, mod_ctx], axis=2), MOD_ROWS)

    w_in_bf = w_in.astype(BF16)
    w_out_bf = w_out.astype(BF16)
    sgu_w_bf = sgu_w.astype(BF16)
    sgu_bias = jnp.repeat(jnp.swapaxes(sgu_b, 1, 2), SGU_W // SGU_GROUPS, axis=2)
    conv_w8 = _pad_rows(conv_w, 8)
    ln1 = _pad_rows(jnp.stack([ln1_g, ln1_b], axis=1), 8)
    ln2 = _pad_rows(jnp.stack([ln2_g, ln2_b], axis=1), 8)
    w_route = jnp.concatenate([router_group_w, jnp.swapaxes(router_expert_w, 1, 2).reshape(DEPTH, d, N_EXPERTS)], axis=2)
    w_route_bf = jnp.pad(w_route, ((0, 0), (0, 0), (0, ROUTE_W - w_route.shape[2]))).astype(BF16)
    b_route = jnp.concatenate([router_group_b, router_expert_b.reshape(DEPTH, N_EXPERTS)], axis=1)
    b_route = jnp.pad(b_route, ((0, 0), (0, ROUTE_W - b_route.shape[1])))[:, None, :].astype(F32)
    w_route_t = jnp.concatenate([_pad_rows(jnp.swapaxes(w_route[:, :, :N_GROUPS], 1, 2), EXPERT_ROW0),
                                 jnp.swapaxes(w_route[:, :, N_GROUPS:], 1, 2)], axis=1)
    w_route_t = _pad_rows(w_route_t, ROUTE_ROWS).astype(BF16)
    b_route_t = jnp.concatenate([_pad_rows(b_route[:, 0, :N_GROUPS, None], EXPERT_ROW0),
                                 b_route[:, 0, N_GROUPS:N_GROUPS + N_EXPERTS, None]], axis=1)
    b_route_t = _pad_rows(b_route_t, ROUTE_ROWS)
    lg = jnp.stack([jax.nn.log_sigmoid(ret_decay_fwd.astype(F32)), jax.nn.log_sigmoid(ret_decay_bwd.astype(F32))],
                   axis=1)
    lg_lanes = jnp.repeat(lg, RET_DK, axis=2).reshape(DEPTH * 2, RET_W)
    lg_rows = lg_lanes[:, None, :]
    lg_cols = lg_lanes[:, :, None]

    x_lat, x_ctx, ctx_tile = x, ctx, 0
    for layer in range(DEPTH):
        last = layer == DEPTH - 1
        pc, q, k, v, g, su, sv, sf, w_gate_bf = _inproj_call(layer, x_lat, x_ctx, ctx_tile, mod, w_in_bf, lg_rows, lg_cols,
                                                             n_lat_tiles, to_cast=(moe_w_gate.reshape(-1, EXPERT_HIDDEN),))
        n_proc = n_lat_tiles if last else n_lat_tiles + 1
        x1, hm, route, w_up_bf = _mix_call(layer, n_lat_tiles, n_proc, lg[layer, 0], lg[layer, 1], pc, q, k, v, g, su, sv,
                                           sf, x_lat, x_ctx, ctx_tile, mod, conv_w8, sgu_w_bf, sgu_bias, w_out_bf, ln1,
                                           w_route_t, b_route_t, lg_rows, lg_cols,
                                           to_cast=(moe_w_up.reshape(-1, EXPERT_HIDDEN),))
        class_of_token, rank, counts = _rank_call(route.reshape(-1, 8, TILE))
        dest, pad_end, pad_len, blk_lo, blk_hi, blk_live, n_used, n_slots = _dispatch_plan(class_of_token.reshape(-1),
                                                                                  rank.reshape(-1), counts[:, 0])
        xs, w_down_bf = _dispatch_call(layer, dest, pad_end, pad_len, n_used, hm.reshape(-1, PACK_W), n_slots,
                                       to_cast=(moe_w_down.reshape(-1, D_MODEL),))
        ys = _expert_call(layer, blk_lo, blk_hi, blk_live, n_used, xs, w_route_bf, b_route,
                          w_gate_bf.reshape(N_EXPERTS, D_MODEL, EXPERT_HIDDEN), w_up_bf.reshape(N_EXPERTS, D_MODEL, EXPERT_HIDDEN),
                          w_down_bf.reshape(N_EXPERTS, EXPERT_HIDDEN, D_MODEL))
        xa = _combine_call(layer, n_lat_tiles, dest, ys, x1, mod, ln2)
        x_lat, x_ctx, ctx_tile = xa, xa, n_lat_tiles
    return xa
```

```python
import functools
import math

import jax
import jax.numpy as jnp
import numpy as np
from jax import lax
from jax.experimental import pallas as pl
from jax.experimental.pallas import tpu as pltpu

F32 = jnp.float32
BF16 = jnp.bfloat16

D_MODEL = 1024
DEPTH = 2
GRID_W = 64
CONV_W = 256
RET_W = 512
RET_HEADS = 8
RET_DK = 64
PAIR_W = 2 * RET_DK
CHUNK = 128
SGU_W = 256
SGU_GROUPS = 4
IN_COLS = 3 * CONV_W + 4 * RET_W + 2 * SGU_W
N_GROUPS = 4
EXPERTS_PER_GROUP = 8
N_EXPERTS = N_GROUPS * EXPERTS_PER_GROUP
EXPERT_HIDDEN = 512
N_MOD = 6
MOD_ROWS = 8
LN_EPS = 1e-5
ALPHA = (2 * DEPTH) ** 0.25

TILE = 256
BATCH_PER_STEP = 2
DISPATCH_TILE = 4 * TILE
HALO = GRID_W
BLOCK_M = 256
ROUTE_W = 128
CAST_STEPS = 128
PACK_W = D_MODEL // 2
ROUTE_ROWS = 48
EXPERT_ROW0 = 8
RANK_TILES = 8
PAIRS_PER_GROUP = EXPERTS_PER_GROUP * (EXPERTS_PER_GROUP - 1) // 2
N_CLASSES = N_GROUPS * PAIRS_PER_GROUP
CLASS_LO = np.array([g * EXPERTS_PER_GROUP + lo for g in range(N_GROUPS) for lo in range(EXPERTS_PER_GROUP)
                     for hi in range(lo + 1, EXPERTS_PER_GROUP)] + [N_EXPERTS - 2] * (ROUTE_W - N_CLASSES), np.int32)
CLASS_HI = np.array([g * EXPERTS_PER_GROUP + hi for g in range(N_GROUPS) for lo in range(EXPERTS_PER_GROUP)
                     for hi in range(lo + 1, EXPERTS_PER_GROUP)] + [N_EXPERTS - 1] * (ROUTE_W - N_CLASSES), np.int32)
VMEM_LIMIT = 56 * 1024 * 1024


def _params(n_axes):
    return pltpu.CompilerParams(dimension_semantics=("arbitrary",) * n_axes, vmem_limit_bytes=VMEM_LIMIT)


def _standardize(v):
    mu = jnp.mean(v, axis=-1, keepdims=True)
    var = jnp.mean(jnp.square(v - mu), axis=-1, keepdims=True)
    return (v - mu) * lax.rsqrt(var + LN_EPS)


def _silu(v):
    return v * jax.nn.sigmoid(v)


def _pack_bf16_pairs(v):
    half = v.shape[1] // 2
    bits = lambda t: pltpu.bitcast(t.astype(BF16).astype(F32), jnp.uint32)
    return bits(v[:, :half]) | (bits(v[:, half:]) >> 16)


def _unpack_bf16_pairs(w):
    hi = pltpu.bitcast(w & jnp.uint32(0xFFFF0000), F32)
    lo = pltpu.bitcast(w << 16, F32)
    return jnp.concatenate([hi, lo], axis=1).astype(BF16)


def _ada_kernel(c_ref, w_ref, b_ref, o_ref):
    a = _silu(c_ref[...]).astype(BF16)
    o_ref[0] = jnp.dot(a, w_ref[0].astype(BF16), preferred_element_type=F32) + b_ref[0]


def _ada_call(cond, w_ada, b_ada):
    rows = cond.shape[0]
    cols = w_ada.shape[-1]
    tn = 1536
    return pl.pallas_call(
        _ada_kernel,
        grid=(DEPTH, cols // tn),
        in_specs=[
            pl.BlockSpec((rows, D_MODEL), lambda l, j: (0, 0)),
            pl.BlockSpec((1, D_MODEL, tn), lambda l, j: (l, 0, j)),
            pl.BlockSpec((1, 1, tn), lambda l, j: (l, 0, j)),
        ],
        out_specs=pl.BlockSpec((1, rows, tn), lambda l, j: (l, 0, j)),
        out_shape=jax.ShapeDtypeStruct((DEPTH, rows, cols), F32),
        compiler_params=_params(2),
    )(cond, w_ada, b_ada.reshape(DEPTH, 1, cols))


def _own_head_mask():
    row_head = lax.broadcasted_iota(jnp.int32, (RET_W, PAIR_W), 0) % PAIR_W // RET_DK
    lane_head = lax.broadcasted_iota(jnp.int32, (RET_W, PAIR_W), 1) // RET_DK
    return row_head == lane_head


def _pair_kv(kc, vc):
    contract_rows = (((0,), (0,)), ((), ()))
    return jnp.concatenate([
        lax.dot_general(kc[:, p * PAIR_W:(p + 1) * PAIR_W], vc[:, p * PAIR_W:(p + 1) * PAIR_W], contract_rows,
                        preferred_element_type=F32) for p in range(RET_W // PAIR_W)], axis=0)


def _inproj_kernel(n_lat_tiles, n_cast, x_ref, ctx_ref, mod_ref, w_ref, lgf_ref, lgfc_ref, *refs):
    cast_in, (pc_ref, q_ref, k_ref, v_ref, g_ref, su_ref, sv_ref, sf_ref), cast_out = (
        refs[:n_cast], refs[n_cast:n_cast + 8], refs[n_cast + 8:-1])
    state = refs[-1]
    for src, dst in zip(cast_in, cast_out):
        dst[...] = src[...].astype(BF16)
    step = pl.program_id(1)
    is_ctx = step == 0
    h = jnp.concatenate([
        (jnp.where(is_ctx, ctx_ref[j], x_ref[j]) * (1.0 + mod_ref[0, j, 0, 1:2]) + mod_ref[0, j, 0, 0:1]).astype(BF16)
        for j in range(BATCH_PER_STEP)], axis=0)

    def proj(lo, hi):
        return jnp.dot(h, w_ref[0, :, lo:hi], preferred_element_type=F32)

    def put(ref, val):
        for j in range(BATCH_PER_STEP):
            ref[j] = val[j * TILE:(j + 1) * TILE].astype(ref.dtype)

    o = 3 * CONV_W
    put(pc_ref, proj(0, o))
    put(q_ref, proj(o, o + RET_W))
    k = proj(o + RET_W, o + 2 * RET_W) * (RET_DK ** -0.5)
    put(k_ref, k)
    v = proj(o + 2 * RET_W, o + 3 * RET_W)
    put(v_ref, v)
    put(g_ref, proj(o + 3 * RET_W, o + 4 * RET_W))
    o += 4 * RET_W
    put(su_ref, proj(o, o + SGU_W))
    put(sv_ref, proj(o + SGU_W, o + 2 * SGU_W))

    @pl.when(step == 0)
    def _():
        state[...] = jnp.zeros_like(state)

    pos = lax.broadcasted_iota(jnp.int32, (CHUNK, 1), 0).astype(F32)
    k_decay = jnp.exp((CHUNK - 1.0 - pos) * lgf_ref[0])
    chunk_decay = jnp.exp(CHUNK * lgfc_ref[0])
    own_head = _own_head_mask()
    for j in range(BATCH_PER_STEP):
        kvs = [_pair_kv((k[rows] * k_decay).astype(BF16), v[rows].astype(BF16))
               for rows in (slice(j * TILE + c * CHUNK, j * TILE + (c + 1) * CHUNK) for c in range(TILE // CHUNK))]
        st = state[j]
        for c, kv in enumerate(kvs):
            sf_ref[j, c] = jnp.where(own_head, st, 0.0).astype(BF16)
            st = chunk_decay * st + kv
        state[j] = st


def _cast_specs(arrays, part, step_of, n_steps):
    steps = min(CAST_STEPS, 1 << (n_steps.bit_length() - 1))
    block = lambda a: (a.shape[0] // (DEPTH * steps), a.shape[1])
    slab = lambda *idx: jnp.minimum(step_of(*idx), steps - 1)
    return ([pl.BlockSpec(block(a), lambda *idx: (part * steps + slab(*idx), 0)) for a in arrays],
            [pl.BlockSpec(block(a), lambda *idx: (slab(*idx), 0)) for a in arrays],
            [jax.ShapeDtypeStruct((a.shape[0] // DEPTH, a.shape[1]), BF16) for a in arrays])


def _inproj_call(layer, x_lat, x_ctx, ctx_tile, mod, w_in_bf, lg_rows, lg_cols, n_lat_tiles, to_cast=()):
    bsz = x_lat.shape[0]
    assert bsz % BATCH_PER_STEP == 0
    nb = BATCH_PER_STEP
    length = (n_lat_tiles + 1) * TILE
    widths = (3 * CONV_W, RET_W, RET_W, RET_W, RET_W, SGU_W, SGU_W)
    cpt = TILE // CHUNK
    n_tiles = n_lat_tiles + 1
    tile_of = lambda s: (s + n_lat_tiles) % n_tiles
    cast_in, cast_out, cast_shapes = _cast_specs(to_cast, layer, lambda b, s: b * n_tiles + s, bsz // nb * n_tiles)
    return pl.pallas_call(
        functools.partial(_inproj_kernel, n_lat_tiles, len(to_cast)),
        grid=(bsz // nb, n_tiles),
        in_specs=[
            pl.BlockSpec((nb, TILE, D_MODEL), lambda b, s: (b, jnp.minimum(tile_of(s), n_lat_tiles - 1), 0)),
            pl.BlockSpec((nb, TILE, D_MODEL), lambda b, s: (b, ctx_tile, 0)),
            pl.BlockSpec((1, nb, 1, MOD_ROWS, D_MODEL), lambda b, s: (layer, b, tile_of(s) // n_lat_tiles, 0, 0)),
            pl.BlockSpec((1, D_MODEL, IN_COLS), lambda b, s: (layer, 0, 0)),
            pl.BlockSpec((1, 1, RET_W), lambda b, s: (2 * layer, 0, 0)),
            pl.BlockSpec((1, RET_W, 1), lambda b, s: (2 * layer, 0, 0)),
        ] + cast_in,
        out_specs=[pl.BlockSpec((nb, TILE, w), lambda b, s: (b, tile_of(s), 0)) for w in widths]
        + [pl.BlockSpec((nb, cpt, RET_W, PAIR_W), lambda b, s: (b, tile_of(s), 0, 0))] + cast_out,
        out_shape=[jax.ShapeDtypeStruct((bsz, length, w), BF16 if 1 <= j <= 3 else F32) for j, w in enumerate(widths)]
        + [jax.ShapeDtypeStruct((bsz, length // CHUNK, RET_W, PAIR_W), BF16)] + cast_shapes,
        scratch_shapes=[pltpu.VMEM((nb, RET_W, PAIR_W), F32)],
        compiler_params=_params(2),
    )(x_lat, x_ctx, mod, w_in_bf, lg_rows, lg_cols, *to_cast)


def _route_class(lt):
    assert EXPERTS_PER_GROUP == 8 and N_GROUPS <= 8
    sub = lax.broadcasted_iota(jnp.int32, (8, lt.shape[1]), 0)
    neg = jnp.float32(-jnp.inf)

    def top(vals):
        m = jnp.max(vals, axis=0, keepdims=True)
        return jnp.min(jnp.where(vals == m, sub, 8), axis=0, keepdims=True)

    gidx = top(jnp.where(sub < N_GROUPS, lt[0:8], neg))
    pair = jnp.zeros_like(gidx)
    for g in range(N_GROUPS):
        e = lt[EXPERT_ROW0 + 8 * g:EXPERT_ROW0 + 8 * (g + 1)]
        i1 = top(e)
        i2 = top(jnp.where(sub == i1, neg, e))
        e_lo = jnp.minimum(i1, i2)
        e_hi = jnp.maximum(i1, i2)
        pair_g = e_lo * (EXPERTS_PER_GROUP - 1) - ((e_lo * (e_lo - 1)) >> 1) + (e_hi - e_lo - 1)
        pair = jnp.where(gidx == g, pair_g, pair)
    return gidx * PAIRS_PER_GROUP + pair


def _mix_kernel(n_lat_tiles, n_cast, lgf_ref, lgb_ref, pc_ref, hp_ref, hn_ref, q_ref, k_ref, v_ref, g_ref, su_ref, sv_ref,
                sf_ref, x_ref, ctx_ref, mod_ref, convw_ref, sguw_ref, sgub_ref, wout_ref, ln_ref, wr_ref, br_ref,
                lgfr_ref, lgbr_ref, lgbc_ref, *refs):
    cast_in, (x1_ref, hm_ref, route_ref), cast_out = refs[:n_cast], refs[n_cast:n_cast + 3], refs[n_cast + 3:-4]
    ycat, dec, qdec, state = refs[-4:]
    for src, dst in zip(cast_in, cast_out):
        dst[...] = src[...].astype(BF16)
    step = pl.program_id(1)
    i = n_lat_tiles - step
    is_ctx = step == 0
    row = lax.broadcasted_iota(jnp.int32, (TILE, 1), 0)
    nb = BATCH_PER_STEP

    @pl.when(step == 0)
    def _():
        state[...] = jnp.zeros_like(state)

    pos_b = lax.broadcasted_iota(jnp.int32, (CHUNK, 1), 0).astype(F32)
    k_decay_b = jnp.exp(pos_b * lgbr_ref[0])
    chunk_decay_b = jnp.exp(CHUNK * lgbc_ref[0])
    own_head = _own_head_mask()
    sb = {}
    for j in range(nb):
        kvs = {c: _pair_kv((k_ref[j, c * CHUNK:(c + 1) * CHUNK, :].astype(F32) * k_decay_b).astype(BF16),
                           v_ref[j, c * CHUNK:(c + 1) * CHUNK, :]) for c in range(TILE // CHUNK)}
        st = state[j]
        for c in reversed(range(TILE // CHUNK)):
            sb[j, c] = jnp.where(own_head, st, 0.0).astype(BF16)
            st = chunk_decay_b * st + kvs[c]
        state[j] = st

    @pl.when((pl.program_id(0) == 0) & (step == 0))
    def _():
        pos = lax.broadcasted_iota(jnp.int32, (CHUNK, 1), 0).astype(F32)
        rel = pos - lax.broadcasted_iota(jnp.int32, (1, CHUNK), 1).astype(F32)
        for h in range(RET_HEADS):
            dec[h // 2, (h % 2) * CHUNK:(h % 2 + 1) * CHUNK, :] = jnp.where(
                rel > 0, jnp.exp(lgf_ref[h] * jnp.maximum(rel, 0.0)),
                jnp.where(rel < 0, jnp.exp(lgb_ref[h] * jnp.maximum(-rel, 0.0)), 2.0))
        qdec[0] = jnp.exp(lgfr_ref[0] * (pos + 1.0))
        qdec[1] = jnp.exp(lgbr_ref[0] * (CHUNK - pos))

    low_head = lax.broadcasted_iota(jnp.int32, (1, PAIR_W), 1) < RET_DK

    def half_norm(o):
        def half_mean(t):
            lo = jnp.sum(jnp.where(low_head, t, 0.0), axis=-1, keepdims=True)
            hi = jnp.sum(jnp.where(low_head, 0.0, t), axis=-1, keepdims=True)
            return jnp.where(low_head, lo, hi) * (1.0 / RET_DK)

        centred = o - half_mean(o)
        return centred * lax.rsqrt(half_mean(jnp.square(centred)) + LN_EPS)

    tiles = [(j, slice(c * CHUNK, (c + 1) * CHUNK), slice(p * PAIR_W, (p + 1) * PAIR_W), c, p)
             for j in range(nb) for c in range(TILE // CHUNK) for p in range(RET_W // PAIR_W)]
    out_rows = lambda j, rows: slice(j * TILE + rows.start, j * TILE + rows.stop)
    scores = []
    for j, rows, cols, c, p in tiles:
        qp = q_ref[j, rows, cols]
        zero = jnp.zeros_like(qp)
        q_split = jnp.concatenate([jnp.where(low_head, qp, zero), jnp.where(low_head, zero, qp)], axis=0)
        scores.append(lax.dot_general(q_split, k_ref[j, rows, cols],
                                      (((1,), (1,)), ((), ())), preferred_element_type=F32))

    line_mask = jnp.where(is_ctx, TILE - 1, GRID_W - 1)
    first = (row & line_mask) == 0
    last = (row & line_mask) == line_mask
    along_seq = lax.broadcasted_iota(jnp.int32, (1, CONV_W), 1) < jnp.where(is_ctx, CONV_W, CONV_W // 2)
    cw = convw_ref[0]
    for j in range(nb):
        pc = pc_ref[j]
        z = pc[:, CONV_W:2 * CONV_W] * pc[:, 2 * CONV_W:3 * CONV_W]
        z_prev = jnp.where(first, 0.0, pltpu.roll(z, 1, 0))
        z_next = jnp.where(last, 0.0, pltpu.roll(z, TILE - 1, 0))
        hp = hp_ref[j]
        hn = hn_ref[j]
        z_top = jnp.where(i == 0, 0.0, hp[:, CONV_W:2 * CONV_W] * hp[:, 2 * CONV_W:3 * CONV_W])
        z_bot = jnp.where(i == n_lat_tiles - 1, 0.0, hn[:, CONV_W:2 * CONV_W] * hn[:, 2 * CONV_W:3 * CONV_W])
        zcat = jnp.concatenate([z_top, z, z_bot], axis=0)
        z_up = zcat[0:TILE]
        z_down = zcat[2 * HALO:2 * HALO + TILE]
        conv = cw[0:1] * jnp.where(along_seq, z_prev, z_up) + cw[1:2] * z
        conv = conv + cw[2:3] * jnp.where(along_seq, z_next, z_down)
        ycat[j * TILE:(j + 1) * TILE, 0:CONV_W] = (pc[:, 0:CONV_W] * conv).astype(BF16)

    outs = []
    for (j, rows, cols, c, p), sc in zip(tiles, scores):
        qp = q_ref[j, rows, cols].astype(F32)
        vp = v_ref[j, rows, cols]
        zero = jnp.zeros_like(vp)
        sc = sc * dec[p]
        lhs = jnp.concatenate([sc[0:CHUNK].astype(BF16), sc[CHUNK:].astype(BF16),
                               (qp * qdec[0, :, cols]).astype(BF16), (qp * qdec[1, :, cols]).astype(BF16)], axis=1)
        rhs = jnp.concatenate([jnp.where(low_head, vp, zero), jnp.where(low_head, zero, vp),
                               sf_ref[j, c, cols, :], sb[j, c][cols, :]], axis=0)
        outs.append(jnp.dot(lhs, rhs, preferred_element_type=F32))

    group = lax.broadcasted_iota(jnp.int32, (1, SGU_W), 1) // (SGU_W // SGU_GROUPS)
    for j in range(nb):
        vn = _standardize(sv_ref[j]).astype(BF16)
        for c in range(TILE // CHUNK):
            rows = slice(c * CHUNK, (c + 1) * CHUNK)
            mixed = jnp.zeros((CHUNK, SGU_W), F32)
            for gi in range(SGU_GROUPS):
                m = jnp.dot(sguw_ref[0, gi], vn[rows], preferred_element_type=F32)
                mixed = jnp.where(group == gi, m, mixed)
            ycat[out_rows(j, rows), CONV_W + RET_W:] = (su_ref[j, rows, :] * (mixed + sgub_ref[0])).astype(BF16)

    for (j, rows, cols, c, p), o in zip(tiles, outs):
        ycat[out_rows(j, rows), CONV_W + p * PAIR_W:CONV_W + (p + 1) * PAIR_W] = (
            _silu(g_ref[j, rows, cols]) * half_norm(o)).astype(BF16)

    y = jnp.dot(ycat[...], wout_ref[0], preferred_element_type=F32)
    ln = ln_ref[0]
    hms = []
    for j in range(nb):
        mod = mod_ref[0, j, 0]
        x1 = _standardize(ALPHA * jnp.where(is_ctx, ctx_ref[j], x_ref[j]) + mod[2:3] * y[j * TILE:(j + 1) * TILE])
        x1 = x1 * ln[0:1] + ln[1:2]
        x1_ref[j] = x1
        hm = x1 * (1.0 + mod[4:5]) + mod[3:4]
        hm_ref[j] = _pack_bf16_pairs(hm)
        hms.append(hm.astype(BF16))
    lt = lax.dot_general(wr_ref[0], jnp.concatenate(hms, axis=0), (((1,), (1,)), ((), ())),
                         preferred_element_type=F32) + br_ref[0]
    cls = _route_class(lt).astype(F32)
    for j in range(nb):
        route_ref[j, 0] = jnp.concatenate([cls[:, j * TILE:(j + 1) * TILE], jnp.zeros((7, TILE), F32)], axis=0)


def _mix_call(layer, n_lat_tiles, n_proc, lgf, lgb, pc, q, k, v, g, su, sv, sf, x_lat, x_ctx, ctx_tile, mod, conv_w,
              sgu_w_bf, sgu_bias, w_out_bf, ln1, w_route_t, b_route_t, lg_rows, lg_cols, to_cast=()):
    bsz, length, _ = q.shape
    nb = BATCH_PER_STEP
    n_tiles = n_lat_tiles + 1
    tile_of = lambda s: n_lat_tiles - s
    cast_in, cast_out, cast_shapes = _cast_specs(to_cast, layer, lambda b, s, *_: b * n_tiles + s, bsz // nb * n_tiles)
    halos_per_tile = TILE // HALO
    n_halo = length // HALO
    cpt = TILE // CHUNK
    tok = lambda w: pl.BlockSpec((nb, TILE, w), lambda b, s, *_: (b, tile_of(s), 0))
    tok_out = lambda w: pl.BlockSpec((nb, TILE, w), lambda b, s, *_: (b, jnp.minimum(tile_of(s), n_proc - 1), 0))
    per_layer = lambda *shape: pl.BlockSpec((1,) + shape, lambda b, s, *_: (layer,) + (0,) * len(shape))
    grid_spec = pltpu.PrefetchScalarGridSpec(
        num_scalar_prefetch=2,
        grid=(bsz // nb, n_tiles),
        in_specs=[
            tok(3 * CONV_W),
            pl.BlockSpec((nb, HALO, 3 * CONV_W),
                         lambda b, s, *_: (b, jnp.maximum(tile_of(s) * halos_per_tile - 1, 0), 0)),
            pl.BlockSpec((nb, HALO, 3 * CONV_W),
                         lambda b, s, *_: (b, jnp.minimum((tile_of(s) + 1) * halos_per_tile, n_halo - 1), 0)),
            tok(RET_W), tok(RET_W), tok(RET_W), tok(RET_W), tok(SGU_W), tok(SGU_W),
            pl.BlockSpec((nb, cpt, RET_W, PAIR_W), lambda b, s, *_: (b, tile_of(s), 0, 0)),
            pl.BlockSpec((nb, TILE, D_MODEL), lambda b, s, *_: (b, jnp.minimum(tile_of(s), n_lat_tiles - 1), 0)),
            pl.BlockSpec((nb, TILE, D_MODEL), lambda b, s, *_: (b, ctx_tile, 0)),
            pl.BlockSpec((1, nb, 1, MOD_ROWS, D_MODEL),
                         lambda b, s, *_: (layer, b, tile_of(s) // n_lat_tiles, 0, 0)),
            per_layer(8, CONV_W),
            per_layer(SGU_GROUPS, CHUNK, CHUNK),
            per_layer(CHUNK, SGU_W),
            per_layer(D_MODEL, D_MODEL),
            per_layer(8, D_MODEL),
            per_layer(ROUTE_ROWS, D_MODEL),
            per_layer(ROUTE_ROWS, 1),
            pl.BlockSpec((1, 1, RET_W), lambda b, s, *_: (2 * layer, 0, 0)),
            pl.BlockSpec((1, 1, RET_W), lambda b, s, *_: (2 * layer + 1, 0, 0)),
            pl.BlockSpec((1, RET_W, 1), lambda b, s, *_: (2 * layer + 1, 0, 0)),
        ] + cast_in,
        out_specs=[tok_out(D_MODEL), tok_out(PACK_W),
                   pl.BlockSpec((nb, 1, 8, TILE), lambda b, s, *_: (b, jnp.minimum(tile_of(s), n_proc - 1), 0, 0))]
        + cast_out,
        scratch_shapes=[pltpu.VMEM((nb * TILE, D_MODEL), BF16),
                        pltpu.VMEM((RET_W // PAIR_W, 2 * CHUNK, CHUNK), F32),
                        pltpu.VMEM((2, CHUNK, RET_W), F32),
                        pltpu.VMEM((nb, RET_W, PAIR_W), F32)],
    )
    return pl.pallas_call(
        functools.partial(_mix_kernel, n_lat_tiles, len(to_cast)),
        grid_spec=grid_spec,
        out_shape=[jax.ShapeDtypeStruct((bsz, n_proc * TILE, D_MODEL), F32),
                   jax.ShapeDtypeStruct((bsz, n_proc * TILE, PACK_W), jnp.uint32),
                   jax.ShapeDtypeStruct((bsz, n_proc, 8, TILE), F32)] + cast_shapes,
        compiler_params=_params(2),
    )(lgf, lgb, pc, pc, pc, q, k, v, g, su, sv, sf, x_lat, x_ctx, mod, conv_w, sgu_w_bf, sgu_bias, w_out_bf,
      ln1, w_route_t, b_route_t, lg_rows, lg_rows, lg_cols, *to_cast)


def _rank_kernel(route_ref, class_ref, rank_ref, counts_ref, running):
    @pl.when(pl.program_id(0) == 0)
    def _():
        running[...] = jnp.zeros_like(running)

    sub = lax.broadcasted_iota(jnp.int32, (ROUTE_W, TILE), 0)
    earlier = (lax.broadcasted_iota(jnp.int32, (TILE, TILE), 0)
               < lax.broadcasted_iota(jnp.int32, (TILE, TILE), 1)).astype(BF16)
    for t in range(route_ref.shape[0]):
        cls = route_ref[t, 0:1, :].astype(jnp.int32)
        class_ref[t] = cls
        onehot = sub == cls
        before = jnp.dot(onehot.astype(BF16), earlier, preferred_element_type=F32) + running[...]
        rank_ref[t] = jnp.sum(jnp.where(onehot, before, 0.0), axis=0, keepdims=True).astype(jnp.int32)
        running[...] += jnp.sum(onehot.astype(F32), axis=1, keepdims=True)
    counts_ref[...] = jnp.broadcast_to(running[...], counts_ref.shape)


def _rank_call(route):
    n_tiles = route.shape[0]
    per_step = math.gcd(n_tiles, RANK_TILES)
    per_tile = pl.BlockSpec((per_step, 1, TILE), lambda i: (i, 0, 0))
    return pl.pallas_call(
        _rank_kernel,
        grid=(n_tiles // per_step,),
        in_specs=[pl.BlockSpec((per_step, 8, TILE), lambda i: (i, 0, 0))],
        out_specs=[per_tile, per_tile, pl.BlockSpec((ROUTE_W, ROUTE_W), lambda i: (0, 0))],
        out_shape=[jax.ShapeDtypeStruct((n_tiles, 1, TILE), jnp.int32), jax.ShapeDtypeStruct((n_tiles, 1, TILE), jnp.int32),
                   jax.ShapeDtypeStruct((ROUTE_W, ROUTE_W), F32)],
        scratch_shapes=[pltpu.VMEM((ROUTE_W, 1), F32)],
        compiler_params=_params(1),
    )(route)


def _dispatch_kernel(n_cast, dest_ref, pad_end_ref, pad_len_ref, n_used_ref, hm_ref, *refs):
    cast_in, xs_ref, cast_out = refs[:n_cast], refs[n_cast], refs[n_cast + 1:-3]
    zeros, sem, pad_sem = refs[-3:]
    for src, dst in zip(cast_in, cast_out):
        dst[...] = src[...].astype(BF16)
    step = pl.program_id(0)
    base = step * DISPATCH_TILE
    half = BLOCK_M // 2

    def for_each_pad_copy(fn):
        def per_class(c, carry):
            off = pad_end_ref[c]
            n = pad_len_ref[c]
            for shift in range(BLOCK_M.bit_length() - 2, -1, -1):
                bit = 1 << shift
                off = off - (n & bit)

                @pl.when((n & bit) != 0)
                def _():
                    if bit >= 8:
                        fn(pltpu.make_async_copy(zeros.at[pl.ds(0, bit)], xs_ref.at[pl.ds(pl.multiple_of(off, 8), bit)],
                                                 pad_sem))
                    else:
                        for j in range(bit):
                            fn(pltpu.make_async_copy(zeros.at[pl.ds(0, 1)], xs_ref.at[pl.ds(off + j, 1)], pad_sem))

            return carry

        lax.fori_loop(0, N_CLASSES, per_class, 0)

        def per_half_block(j, carry):
            fn(pltpu.make_async_copy(zeros, xs_ref.at[pl.ds(pl.multiple_of(j * half, 8), half)], pad_sem))
            return carry

        lax.fori_loop(n_used_ref[0] * 2, xs_ref.shape[0] // half, per_half_block, 0)

    @pl.when(step == 0)
    def _():
        zeros[...] = jnp.zeros_like(zeros)
        for_each_pad_copy(lambda cp: cp.start())

    for r in range(DISPATCH_TILE):
        pltpu.make_async_copy(hm_ref.at[pl.ds(r, 1)], xs_ref.at[pl.ds(dest_ref[base + r], 1)],
                              sem).start(priority=r % 2)
    for r in range(DISPATCH_TILE):
        pltpu.make_async_copy(hm_ref.at[pl.ds(0, 1)], xs_ref.at[pl.ds(0, 1)], sem).wait()

    @pl.when(step == pl.num_programs(0) - 1)
    def _():
        for_each_pad_copy(lambda cp: cp.wait())


def _dispatch_call(layer, dest, pad_end, pad_len, n_used, hm_flat, n_slots, to_cast=()):
    n_tok = hm_flat.shape[0]
    assert n_tok % DISPATCH_TILE == 0
    cast_in, cast_out, cast_shapes = _cast_specs(to_cast, layer, lambda i, *_: i, n_tok // DISPATCH_TILE)
    grid_spec = pltpu.PrefetchScalarGridSpec(
        num_scalar_prefetch=4,
        grid=(n_tok // DISPATCH_TILE,),
        in_specs=[pl.BlockSpec((DISPATCH_TILE, PACK_W), lambda i, *_: (i, 0))] + cast_in,
        out_specs=[pl.BlockSpec(memory_space=pl.ANY)] + cast_out,
        scratch_shapes=[pltpu.VMEM((BLOCK_M // 2, PACK_W), jnp.uint32), pltpu.SemaphoreType.DMA,
                        pltpu.SemaphoreType.DMA],
    )
    return pl.pallas_call(
        functools.partial(_dispatch_kernel, len(to_cast)),
        grid_spec=grid_spec,
        out_shape=[jax.ShapeDtypeStruct((n_slots, PACK_W), jnp.uint32)] + cast_shapes,
        compiler_params=_params(1),
    )(dest, pad_end, pad_len, n_used, hm_flat, *to_cast)


def _expert_kernel(first_blk_ref, cls_lo_ref, cls_hi_ref, xs_ref, wr_ref, br_ref, wg_lo, wu_lo, wd_lo, wg_hi, wu_hi, wd_hi,
                   ys_ref, xbuf, ybuf, xsem, ysem):
    c = pl.program_id(0)
    n_used = first_blk_ref[N_CLASSES]
    n_blocks = ys_ref.shape[0] // BLOCK_M
    rows_of = lambda g: pl.ds(pl.multiple_of(g * BLOCK_M, BLOCK_M), BLOCK_M)
    x_copy = lambda g, slot: pltpu.make_async_copy(xs_ref.at[rows_of(g)], xbuf.at[slot], xsem.at[slot])
    y_copy = lambda g, slot: pltpu.make_async_copy(ybuf.at[slot], ys_ref.at[rows_of(g)], ysem.at[slot])

    @pl.when(c == 0)
    def _():
        x_copy(0, 0).start()

    def block(g, carry):
        slot = g % 2
        x_copy(g, slot).wait()

        @pl.when(g + 1 < n_used)
        def _():
            x_copy(g + 1, 1 - slot).start()

        @pl.when(g >= 2)
        def _():
            y_copy(g - 2, slot).wait()

        xb = _unpack_bf16_pairs(xbuf[slot])

        logits = jnp.dot(xb, wr_ref[0], preferred_element_type=F32) + br_ref[0]
        lane = lax.broadcasted_iota(jnp.int32, logits.shape, 1)
        gl = jnp.where(lane < N_GROUPS, logits, -jnp.inf)
        g_prob = 1.0 / jnp.sum(jnp.exp(gl - jnp.max(gl, axis=-1, keepdims=True)), axis=-1, keepdims=True)
        l_lo = jnp.sum(jnp.where(lane == N_GROUPS + cls_lo_ref[c], logits, 0.0), axis=-1, keepdims=True)
        l_hi = jnp.sum(jnp.where(lane == N_GROUPS + cls_hi_ref[c], logits, 0.0), axis=-1, keepdims=True)
        m = jnp.maximum(l_lo, l_hi)
        p_lo = jnp.exp(l_lo - m)
        p_hi = jnp.exp(l_hi - m)

        h_lo = jnp.dot(xb, wg_lo[0], preferred_element_type=F32)
        u_lo = jnp.dot(xb, wu_lo[0], preferred_element_type=F32)
        h_hi = jnp.dot(xb, wg_hi[0], preferred_element_type=F32)
        u_hi = jnp.dot(xb, wu_hi[0], preferred_element_type=F32)
        a_lo = (_silu(h_lo) * u_lo).astype(BF16)
        a_hi = (_silu(h_hi) * u_hi).astype(BF16)
        y_lo = jnp.dot(a_lo, wd_lo[0], preferred_element_type=F32)
        y_hi = jnp.dot(a_hi, wd_hi[0], preferred_element_type=F32)
        ybuf[slot] = y_lo * (g_prob * (p_lo / (p_lo + p_hi))) + y_hi * (g_prob * (p_hi / (p_lo + p_hi)))
        y_copy(g, slot).start()
        return carry

    lax.fori_loop(first_blk_ref[c], first_blk_ref[c + 1], block, 0)

    @pl.when(c == pl.num_programs(0) - 1)
    def _():
        @pl.when(n_used >= 2)
        def _():
            y_copy(n_used - 2, n_used % 2).wait()

        y_copy(n_used - 1, (n_used - 1) % 2).wait()
        ybuf[0] = jnp.zeros((BLOCK_M, D_MODEL), F32)

        def fill(g, carry):
            y_copy(g, 0).start()
            return carry

        def fill_done(g, carry):
            y_copy(g, 0).wait()
            return carry

        lax.fori_loop(n_used, n_blocks, fill, 0)
        lax.fori_loop(n_used, n_blocks, fill_done, 0)


def _expert_call(layer, first_blk, xs, w_route_bf, b_route, w_gate_bf, w_up_bf, w_down_bf):
    up_spec = lambda table: pl.BlockSpec((1, D_MODEL, EXPERT_HIDDEN), lambda c, fb, lo, hi: ((lo, hi)[table][c], 0, 0))
    down_spec = lambda table: pl.BlockSpec((1, EXPERT_HIDDEN, D_MODEL), lambda c, fb, lo, hi: ((lo, hi)[table][c], 0, 0))
    grid_spec = pltpu.PrefetchScalarGridSpec(
        num_scalar_prefetch=3,
        grid=(N_CLASSES,),
        in_specs=[pl.BlockSpec(memory_space=pl.ANY),
                  pl.BlockSpec((1, D_MODEL, ROUTE_W), lambda c, *_: (layer, 0, 0)),
                  pl.BlockSpec((1, 1, ROUTE_W), lambda c, *_: (layer, 0, 0)),
                  up_spec(0), up_spec(0), down_spec(0), up_spec(1), up_spec(1), down_spec(1)],
        out_specs=pl.BlockSpec(memory_space=pl.ANY),
        scratch_shapes=[pltpu.VMEM((2, BLOCK_M, PACK_W), jnp.uint32), pltpu.VMEM((2, BLOCK_M, D_MODEL), F32),
                        pltpu.SemaphoreType.DMA((2,)), pltpu.SemaphoreType.DMA((2,))],
    )
    return pl.pallas_call(
        _expert_kernel,
        grid_spec=grid_spec,
        out_shape=jax.ShapeDtypeStruct((xs.shape[0], D_MODEL), F32),
        compiler_params=_params(1),
    )(first_blk, jnp.asarray(CLASS_LO), jnp.asarray(CLASS_HI), xs, w_route_bf, b_route, w_gate_bf, w_up_bf, w_down_bf,
      w_gate_bf, w_up_bf, w_down_bf)


def _combine_kernel(tiles_per_batch, dest_ref, ys_ref, x1_ref, mod_ref, ln_ref, out_ref, buf, sem):
    nb = BATCH_PER_STEP
    group, i = pl.program_id(0), pl.program_id(1)
    step = group * tiles_per_batch + i
    slot = step % 2
    has_next = step + 1 < pl.num_programs(0) * tiles_per_batch
    wrap = i + 1 == tiles_per_batch

    def gather(grp, tile, to_slot):
        for j in range(nb):
            base = ((grp * nb + j) * tiles_per_batch + tile) * TILE
            for r in range(TILE):
                pltpu.make_async_copy(ys_ref.at[pl.ds(dest_ref[base + r], 1)], buf.at[to_slot, j, pl.ds(r, 1)],
                                      sem.at[to_slot]).start(priority=r % 2)

    @pl.when(step == 0)
    def _():
        gather(group, i, slot)

    @pl.when(has_next)
    def _():
        gather(jnp.where(wrap, group + 1, group), jnp.where(wrap, 0, i + 1), 1 - slot)

    for r in range(nb * TILE):
        pltpu.make_async_copy(ys_ref.at[pl.ds(0, 1)], buf.at[slot, 0, pl.ds(0, 1)], sem.at[slot]).wait()

    ln = ln_ref[0]
    for j in range(nb):
        out_ref[j] = _standardize(ALPHA * x1_ref[j] + mod_ref[0, j, 0, 5:6] * buf[slot, j]) * ln[0:1] + ln[1:2]


def _combine_call(layer, n_lat_tiles, dest, ys, x1, mod, ln2):
    bsz, length, _ = x1.shape
    nb = BATCH_PER_STEP
    tiles_per_batch = length // TILE
    tok = lambda w: pl.BlockSpec((nb, TILE, w), lambda b, i, *_: (b, i, 0))
    grid_spec = pltpu.PrefetchScalarGridSpec(
        num_scalar_prefetch=1,
        grid=(bsz // nb, tiles_per_batch),
        in_specs=[
            pl.BlockSpec(memory_space=pl.ANY),
            tok(D_MODEL),
            pl.BlockSpec((1, nb, 1, MOD_ROWS, D_MODEL),
                         lambda b, i, *_: (layer, b, jnp.minimum(i // n_lat_tiles, 1), 0, 0)),
            pl.BlockSpec((1, 8, D_MODEL), lambda b, i, *_: (layer, 0, 0)),
        ],
        out_specs=tok(D_MODEL),
        scratch_shapes=[pltpu.VMEM((2, nb, TILE, D_MODEL), F32), pltpu.SemaphoreType.DMA((2,))],
    )
    return pl.pallas_call(
        functools.partial(_combine_kernel, tiles_per_batch),
        grid_spec=grid_spec,
        out_shape=jax.ShapeDtypeStruct(x1.shape, F32),
        compiler_params=_params(2),
    )(dest, ys, x1, mod, ln2)


def _dispatch_plan(class_of_token, rank, counts):
    n_tok = class_of_token.shape[0]
    counts = counts.astype(jnp.int32)
    pcounts = (counts + BLOCK_M - 1) // BLOCK_M * BLOCK_M
    pends = jnp.cumsum(pcounts)
    pstarts = pends - pcounts
    classes = jnp.arange(ROUTE_W, dtype=jnp.int32)
    dest = jnp.sum(jnp.where(class_of_token[:, None] == classes[None, :], pstarts[None, :], 0), axis=1) + rank
    n_blocks = n_tok // BLOCK_M + N_CLASSES
    n_used = pends[-1:] // BLOCK_M
    first_blk = pstarts[:N_CLASSES + 1] // BLOCK_M
    return dest.astype(jnp.int32), pends, pcounts - counts, first_blk, n_used.astype(jnp.int32), n_blocks * BLOCK_M


def _pad_rows(a, rows):
    return jnp.pad(a, [(0, 0)] * (a.ndim - 2) + [(0, rows - a.shape[-2]), (0, 0)])


def kernel(x, c, ctx, c_ctx, w_ada, b_ada, w_in, conv_w, ret_decay_fwd, ret_decay_bwd, sgu_w, sgu_b, w_out, ln1_g, ln1_b,
           router_group_w, router_group_b, router_expert_w, router_expert_b, moe_w_gate, moe_w_up, moe_w_down, ln2_g,
           ln2_b):
    bsz, seq, d = x.shape
    ctx_len = ctx.shape[1]
    assert d == D_MODEL and ctx_len == TILE and seq % TILE == 0 and seq % GRID_W == 0
    n_lat_tiles = seq // TILE

    cond = _pad_rows(jnp.concatenate([c, c_ctx[None, :]], axis=0), 16)
    ada = _ada_call(cond, w_ada, b_ada)
    mod_lat = ada[:, :bsz].reshape(DEPTH, bsz, N_MOD, d)
    mod_ctx = jnp.broadcast_to(ada[:, bsz].reshape(DEPTH, 1, N_MOD, d), mod_lat.shape)
    mod = _pad_rows(jnp.stack([mod_lat, mod_ctx], axis=2), MOD_ROWS)

    w_in_bf = w_in.astype(BF16)
    w_out_bf = w_out.astype(BF16)
    sgu_w_bf = sgu_w.astype(BF16)
    sgu_bias = jnp.repeat(jnp.swapaxes(sgu_b, 1, 2), SGU_W // SGU_GROUPS, axis=2)
    conv_w8 = _pad_rows(conv_w, 8)
    ln1 = _pad_rows(jnp.stack([ln1_g, ln1_b], axis=1), 8)
    ln2 = _pad_rows(jnp.stack([ln2_g, ln2_b], axis=1), 8)
    w_route = jnp.concatenate([router_group_w, jnp.swapaxes(router_expert_w, 1, 2).reshape(DEPTH, d, N_EXPERTS)], axis=2)
    w_route_bf = jnp.pad(w_route, ((0, 0), (0, 0), (0, ROUTE_W - w_route.shape[2]))).astype(BF16)
    b_route = jnp.concatenate([router_group_b, router_expert_b.reshape(DEPTH, N_EXPERTS)], axis=1)
    b_route = jnp.pad(b_route, ((0, 0), (0, ROUTE_W - b_route.shape[1])))[:, None, :].astype(F32)
    w_route_t = jnp.concatenate([_pad_rows(jnp.swapaxes(w_route[:, :, :N_GROUPS], 1, 2), EXPERT_ROW0),
                                 jnp.swapaxes(w_route[:, :, N_GROUPS:], 1, 2)], axis=1)
    w_route_t = _pad_rows(w_route_t, ROUTE_ROWS).astype(BF16)
    b_route_t = jnp.concatenate([_pad_rows(b_route[:, 0, :N_GROUPS, None], EXPERT_ROW0),
                                 b_route[:, 0, N_GROUPS:N_GROUPS + N_EXPERTS, None]], axis=1)
    b_route_t = _pad_rows(b_route_t, ROUTE_ROWS)
    lg = jnp.stack([jax.nn.log_sigmoid(ret_decay_fwd.astype(F32)), jax.nn.log_sigmoid(ret_decay_bwd.astype(F32))],
                   axis=1)
    lg_lanes = jnp.repeat(lg, RET_DK, axis=2).reshape(DEPTH * 2, RET_W)
    lg_rows = lg_lanes[:, None, :]
    lg_cols = lg_lanes[:, :, None]

    x_lat, x_ctx, ctx_tile = x, ctx, 0
    for layer in range(DEPTH):
        last = layer == DEPTH - 1
        pc, q, k, v, g, su, sv, sf, w_gate_bf = _inproj_call(layer, x_lat, x_ctx, ctx_tile, mod, w_in_bf, lg_rows, lg_cols,
                                                             n_lat_tiles, to_cast=(moe_w_gate.reshape(-1, EXPERT_HIDDEN),))
        n_proc = n_lat_tiles if last else n_lat_tiles + 1
        x1, hm, route, w_up_bf = _mix_call(layer, n_lat_tiles, n_proc, lg[layer, 0], lg[layer, 1], pc, q, k, v, g, su, sv,
                                           sf, x_lat, x_ctx, ctx_tile, mod, conv_w8, sgu_w_bf, sgu_bias, w_out_bf, ln1,
                                           w_route_t, b_route_t, lg_rows, lg_cols,
                                           to_cast=(moe_w_up.reshape(-1, EXPERT_HIDDEN),))
        class_of_token, rank, counts = _rank_call(route.reshape(-1, 8, TILE))
        dest, pad_end, pad_len, first_blk, n_used, n_slots = _dispatch_plan(class_of_token.reshape(-1), rank.reshape(-1),
                                                                            counts[:, 0])
        xs, w_down_bf = _dispatch_call(layer, dest, pad_end, pad_len, n_used, hm.reshape(-1, PACK_W), n_slots,
                                       to_cast=(moe_w_down.reshape(-1, D_MODEL),))
        ys = _expert_call(layer, first_blk, xs, w_route_bf, b_route,
                          w_gate_bf.reshape(N_EXPERTS, D_MODEL, EXPERT_HIDDEN), w_up_bf.reshape(N_EXPERTS, D_MODEL, EXPERT_HIDDEN),
                          w_down_bf.reshape(N_EXPERTS, EXPERT_HIDDEN, D_MODEL))
        xa = _combine_call(layer, n_lat_tiles, dest, ys, x1, mod, ln2)
        x_lat, x_ctx, ctx_tile = xa, xa, n_lat_tiles
    return xa
```

```python
import functools
import math

import jax
import jax.numpy as jnp
import numpy as np
from jax import lax
from jax.experimental import pallas as pl
from jax.experimental.pallas import tpu as pltpu

F32 = jnp.float32
BF16 = jnp.bfloat16

D_MODEL = 1024
DEPTH = 2
GRID_W = 64
CONV_W = 256
RET_W = 512
RET_HEADS = 8
RET_DK = 64
PAIR_W = 2 * RET_DK
CHUNK = 128
SGU_W = 256
SGU_GROUPS = 4
IN_COLS = 3 * CONV_W + 4 * RET_W + 2 * SGU_W
N_GROUPS = 4
EXPERTS_PER_GROUP = 8
N_EXPERTS = N_GROUPS * EXPERTS_PER_GROUP
EXPERT_HIDDEN = 512
N_MOD = 6
MOD_ROWS = 8
LN_EPS = 1e-5
ALPHA = (2 * DEPTH) ** 0.25

TILE = 256
BATCH_PER_STEP = 2
COMBINE_BATCH = 4
DISPATCH_TILE = 8 * TILE
HALO = GRID_W
BLOCK_M = 256
ROUTE_W = 128
CAST_STEPS = 128
PACK_W = D_MODEL // 2
ROUTE_ROWS = 48
EXPERT_ROW0 = 8
RANK_TILES = 8
PAIRS_PER_GROUP = EXPERTS_PER_GROUP * (EXPERTS_PER_GROUP - 1) // 2
N_CLASSES = N_GROUPS * PAIRS_PER_GROUP
CLASS_LO = np.array([g * EXPERTS_PER_GROUP + lo for g in range(N_GROUPS) for lo in range(EXPERTS_PER_GROUP)
                     for hi in range(lo + 1, EXPERTS_PER_GROUP)] + [N_EXPERTS - 2] * (ROUTE_W - N_CLASSES), np.int32)
CLASS_HI = np.array([g * EXPERTS_PER_GROUP + hi for g in range(N_GROUPS) for lo in range(EXPERTS_PER_GROUP)
                     for hi in range(lo + 1, EXPERTS_PER_GROUP)] + [N_EXPERTS - 1] * (ROUTE_W - N_CLASSES), np.int32)
VMEM_LIMIT = 56 * 1024 * 1024


def _params(n_axes):
    return pltpu.CompilerParams(dimension_semantics=("arbitrary",) * n_axes, vmem_limit_bytes=VMEM_LIMIT)


def _standardize(v):
    mu = jnp.mean(v, axis=-1, keepdims=True)
    var = jnp.mean(jnp.square(v - mu), axis=-1, keepdims=True)
    return (v - mu) * lax.rsqrt(var + LN_EPS)


def _silu(v):
    return v * jax.nn.sigmoid(v)


def _pack_bf16_pairs(v):
    half = v.shape[1] // 2
    bits = lambda t: pltpu.bitcast(t.astype(BF16).astype(F32), jnp.uint32)
    return bits(v[:, :half]) | (bits(v[:, half:]) >> 16)


def _unpack_bf16_pairs(w):
    hi = pltpu.bitcast(w & jnp.uint32(0xFFFF0000), F32)
    lo = pltpu.bitcast(w << 16, F32)
    return jnp.concatenate([hi, lo], axis=1).astype(BF16)


def _ada_kernel(c_ref, w_ref, b_ref, o_ref):
    a = _silu(c_ref[...]).astype(BF16)
    o_ref[0] = jnp.dot(a, w_ref[0].astype(BF16), preferred_element_type=F32) + b_ref[0]


def _ada_call(cond, w_ada, b_ada):
    rows = cond.shape[0]
    cols = w_ada.shape[-1]
    tn = 1536
    return pl.pallas_call(
        _ada_kernel,
        grid=(DEPTH, cols // tn),
        in_specs=[
            pl.BlockSpec((rows, D_MODEL), lambda l, j: (0, 0)),
            pl.BlockSpec((1, D_MODEL, tn), lambda l, j: (l, 0, j)),
            pl.BlockSpec((1, 1, tn), lambda l, j: (l, 0, j)),
        ],
        out_specs=pl.BlockSpec((1, rows, tn), lambda l, j: (l, 0, j)),
        out_shape=jax.ShapeDtypeStruct((DEPTH, rows, cols), F32),
        compiler_params=_params(2),
    )(cond, w_ada, b_ada.reshape(DEPTH, 1, cols))


def _own_head_mask():
    row_head = lax.broadcasted_iota(jnp.int32, (RET_W, PAIR_W), 0) % PAIR_W // RET_DK
    lane_head = lax.broadcasted_iota(jnp.int32, (RET_W, PAIR_W), 1) // RET_DK
    return row_head == lane_head


def _pair_kv(kc, vc):
    contract_rows = (((0,), (0,)), ((), ()))
    return jnp.concatenate([
        lax.dot_general(kc[:, p * PAIR_W:(p + 1) * PAIR_W], vc[:, p * PAIR_W:(p + 1) * PAIR_W], contract_rows,
                        preferred_element_type=F32) for p in range(RET_W // PAIR_W)], axis=0)


def _inproj_kernel(n_lat_tiles, n_cast, x_ref, ctx_ref, mod_ref, w_ref, lgf_ref, lgfc_ref, *refs):
    cast_in, (pc_ref, q_ref, k_ref, v_ref, g_ref, su_ref, sv_ref, sf_ref), cast_out = (
        refs[:n_cast], refs[n_cast:n_cast + 8], refs[n_cast + 8:-1])
    state = refs[-1]
    for src, dst in zip(cast_in, cast_out):
        dst[...] = src[...].astype(BF16)
    step = pl.program_id(1)
    is_ctx = step == 0
    h = jnp.concatenate([
        (jnp.where(is_ctx, ctx_ref[j], x_ref[j]) * (1.0 + mod_ref[0, j, 0, 1:2]) + mod_ref[0, j, 0, 0:1]).astype(BF16)
        for j in range(BATCH_PER_STEP)], axis=0)

    def proj(lo, hi):
        return jnp.dot(h, w_ref[0, :, lo:hi], preferred_element_type=F32)

    def put(ref, val):
        for j in range(BATCH_PER_STEP):
            ref[j] = val[j * TILE:(j + 1) * TILE].astype(ref.dtype)

    o = 3 * CONV_W
    put(pc_ref, proj(0, o))
    put(q_ref, proj(o, o + RET_W))
    k = proj(o + RET_W, o + 2 * RET_W) * (RET_DK ** -0.5)
    put(k_ref, k)
    v = proj(o + 2 * RET_W, o + 3 * RET_W)
    put(v_ref, v)
    put(g_ref, proj(o + 3 * RET_W, o + 4 * RET_W))
    o += 4 * RET_W
    put(su_ref, proj(o, o + SGU_W))
    put(sv_ref, proj(o + SGU_W, o + 2 * SGU_W))

    @pl.when(step == 0)
    def _():
        state[...] = jnp.zeros_like(state)

    pos = lax.broadcasted_iota(jnp.int32, (CHUNK, 1), 0).astype(F32)
    k_decay = jnp.exp((CHUNK - 1.0 - pos) * lgf_ref[0])
    chunk_decay = jnp.exp(CHUNK * lgfc_ref[0])
    own_head = _own_head_mask()
    for j in range(BATCH_PER_STEP):
        kvs = [_pair_kv((k[rows] * k_decay).astype(BF16), v[rows].astype(BF16))
               for rows in (slice(j * TILE + c * CHUNK, j * TILE + (c + 1) * CHUNK) for c in range(TILE // CHUNK))]
        st = state[j]
        for c, kv in enumerate(kvs):
            sf_ref[j, c] = jnp.where(own_head, st, 0.0).astype(BF16)
            st = chunk_decay * st + kv
        state[j] = st


def _cast_specs(arrays, part, step_of, n_steps):
    steps = min(CAST_STEPS, 1 << (n_steps.bit_length() - 1))
    block = lambda a: (a.shape[0] // (DEPTH * steps), a.shape[1])
    slab = lambda *idx: jnp.minimum(step_of(*idx), steps - 1)
    return ([pl.BlockSpec(block(a), lambda *idx: (part * steps + slab(*idx), 0)) for a in arrays],
            [pl.BlockSpec(block(a), lambda *idx: (slab(*idx), 0)) for a in arrays],
            [jax.ShapeDtypeStruct((a.shape[0] // DEPTH, a.shape[1]), BF16) for a in arrays])


def _inproj_call(layer, x_lat, x_ctx, ctx_tile, mod, w_in_bf, lg_rows, lg_cols, n_lat_tiles, to_cast=()):
    bsz = x_lat.shape[0]
    assert bsz % BATCH_PER_STEP == 0
    nb = BATCH_PER_STEP
    length = (n_lat_tiles + 1) * TILE
    widths = (3 * CONV_W, RET_W, RET_W, RET_W, RET_W, SGU_W, SGU_W)
    cpt = TILE // CHUNK
    n_tiles = n_lat_tiles + 1
    tile_of = lambda s: (s + n_lat_tiles) % n_tiles
    cast_in, cast_out, cast_shapes = _cast_specs(to_cast, layer, lambda b, s: b * n_tiles + s, bsz // nb * n_tiles)
    return pl.pallas_call(
        functools.partial(_inproj_kernel, n_lat_tiles, len(to_cast)),
        grid=(bsz // nb, n_tiles),
        in_specs=[
            pl.BlockSpec((nb, TILE, D_MODEL), lambda b, s: (b, jnp.minimum(tile_of(s), n_lat_tiles - 1), 0)),
            pl.BlockSpec((nb, TILE, D_MODEL), lambda b, s: (b, ctx_tile, 0)),
            pl.BlockSpec((1, nb, 1, MOD_ROWS, D_MODEL), lambda b, s: (layer, b, tile_of(s) // n_lat_tiles, 0, 0)),
            pl.BlockSpec((1, D_MODEL, IN_COLS), lambda b, s: (layer, 0, 0)),
            pl.BlockSpec((1, 1, RET_W), lambda b, s: (2 * layer, 0, 0)),
            pl.BlockSpec((1, RET_W, 1), lambda b, s: (2 * layer, 0, 0)),
        ] + cast_in,
        out_specs=[pl.BlockSpec((nb, TILE, w), lambda b, s: (b, tile_of(s), 0)) for w in widths]
        + [pl.BlockSpec((nb, cpt, RET_W, PAIR_W), lambda b, s: (b, tile_of(s), 0, 0))] + cast_out,
        out_shape=[jax.ShapeDtypeStruct((bsz, length, w), BF16 if 1 <= j <= 3 else F32) for j, w in enumerate(widths)]
        + [jax.ShapeDtypeStruct((bsz, length // CHUNK, RET_W, PAIR_W), BF16)] + cast_shapes,
        scratch_shapes=[pltpu.VMEM((nb, RET_W, PAIR_W), F32)],
        compiler_params=_params(2),
    )(x_lat, x_ctx, mod, w_in_bf, lg_rows, lg_cols, *to_cast)


def _route_class(lt):
    assert EXPERTS_PER_GROUP == 8 and N_GROUPS <= 8
    sub = lax.broadcasted_iota(jnp.int32, (8, lt.shape[1]), 0)
    neg = jnp.float32(-jnp.inf)

    def top(vals):
        m = jnp.max(vals, axis=0, keepdims=True)
        return jnp.min(jnp.where(vals == m, sub, 8), axis=0, keepdims=True)

    gidx = top(jnp.where(sub < N_GROUPS, lt[0:8], neg))
    pair = jnp.zeros_like(gidx)
    for g in range(N_GROUPS):
        e = lt[EXPERT_ROW0 + 8 * g:EXPERT_ROW0 + 8 * (g + 1)]
        i1 = top(e)
        i2 = top(jnp.where(sub == i1, neg, e))
        e_lo = jnp.minimum(i1, i2)
        e_hi = jnp.maximum(i1, i2)
        pair_g = e_lo * (EXPERTS_PER_GROUP - 1) - ((e_lo * (e_lo - 1)) >> 1) + (e_hi - e_lo - 1)
        pair = jnp.where(gidx == g, pair_g, pair)
    return gidx * PAIRS_PER_GROUP + pair


def _mix_kernel(n_lat_tiles, n_cast, lgf_ref, lgb_ref, pc_ref, hp_ref, hn_ref, q_ref, k_ref, v_ref, g_ref, su_ref, sv_ref,
                sf_ref, x_ref, ctx_ref, mod_ref, convw_ref, sguw_ref, sgub_ref, wout_ref, ln_ref, wr_ref, br_ref,
                lgfr_ref, lgbr_ref, lgbc_ref, *refs):
    cast_in, (x1_ref, hm_ref, route_ref), cast_out = refs[:n_cast], refs[n_cast:n_cast + 3], refs[n_cast + 3:-4]
    ycat, dec, qdec, state = refs[-4:]
    for src, dst in zip(cast_in, cast_out):
        dst[...] = src[...].astype(BF16)
    step = pl.program_id(1)
    i = n_lat_tiles - step
    is_ctx = step == 0
    row = lax.broadcasted_iota(jnp.int32, (TILE, 1), 0)
    nb = BATCH_PER_STEP

    @pl.when(step == 0)
    def _():
        state[...] = jnp.zeros_like(state)

    pos_b = lax.broadcasted_iota(jnp.int32, (CHUNK, 1), 0).astype(F32)
    k_decay_b = jnp.exp(pos_b * lgbr_ref[0])
    chunk_decay_b = jnp.exp(CHUNK * lgbc_ref[0])
    own_head = _own_head_mask()
    sb = {}
    for j in range(nb):
        kvs = {c: _pair_kv((k_ref[j, c * CHUNK:(c + 1) * CHUNK, :].astype(F32) * k_decay_b).astype(BF16),
                           v_ref[j, c * CHUNK:(c + 1) * CHUNK, :]) for c in range(TILE // CHUNK)}
        st = state[j]
        for c in reversed(range(TILE // CHUNK)):
            sb[j, c] = jnp.where(own_head, st, 0.0).astype(BF16)
            st = chunk_decay_b * st + kvs[c]
        state[j] = st

    @pl.when((pl.program_id(0) == 0) & (step == 0))
    def _():
        pos = lax.broadcasted_iota(jnp.int32, (CHUNK, 1), 0).astype(F32)
        rel = pos - lax.broadcasted_iota(jnp.int32, (1, CHUNK), 1).astype(F32)
        for h in range(RET_HEADS):
            dec[h // 2, (h % 2) * CHUNK:(h % 2 + 1) * CHUNK, :] = jnp.where(
                rel > 0, jnp.exp(lgf_ref[h] * jnp.maximum(rel, 0.0)),
                jnp.where(rel < 0, jnp.exp(lgb_ref[h] * jnp.maximum(-rel, 0.0)), 2.0))
        qdec[0] = jnp.exp(lgfr_ref[0] * (pos + 1.0))
        qdec[1] = jnp.exp(lgbr_ref[0] * (CHUNK - pos))

    low_head = lax.broadcasted_iota(jnp.int32, (1, PAIR_W), 1) < RET_DK

    def half_norm(o):
        def half_mean(t):
            lo = jnp.sum(jnp.where(low_head, t, 0.0), axis=-1, keepdims=True)
            hi = jnp.sum(jnp.where(low_head, 0.0, t), axis=-1, keepdims=True)
            return jnp.where(low_head, lo, hi) * (1.0 / RET_DK)

        centred = o - half_mean(o)
        return centred * lax.rsqrt(half_mean(jnp.square(centred)) + LN_EPS)

    tiles = [(j, slice(c * CHUNK, (c + 1) * CHUNK), slice(p * PAIR_W, (p + 1) * PAIR_W), c, p)
             for j in range(nb) for c in range(TILE // CHUNK) for p in range(RET_W // PAIR_W)]
    out_rows = lambda j, rows: slice(j * TILE + rows.start, j * TILE + rows.stop)
    scores = []
    for j, rows, cols, c, p in tiles:
        qp = q_ref[j, rows, cols]
        zero = jnp.zeros_like(qp)
        q_split = jnp.concatenate([jnp.where(low_head, qp, zero), jnp.where(low_head, zero, qp)], axis=0)
        scores.append(lax.dot_general(q_split, k_ref[j, rows, cols],
                                      (((1,), (1,)), ((), ())), preferred_element_type=F32))

    line_mask = jnp.where(is_ctx, TILE - 1, GRID_W - 1)
    first = (row & line_mask) == 0
    last = (row & line_mask) == line_mask
    along_seq = lax.broadcasted_iota(jnp.int32, (1, CONV_W), 1) < jnp.where(is_ctx, CONV_W, CONV_W // 2)
    cw = convw_ref[0]
    for j in range(nb):
        pc = pc_ref[j]
        z = pc[:, CONV_W:2 * CONV_W] * pc[:, 2 * CONV_W:3 * CONV_W]
        z_prev = jnp.where(first, 0.0, pltpu.roll(z, 1, 0))
        z_next = jnp.where(last, 0.0, pltpu.roll(z, TILE - 1, 0))
        hp = hp_ref[j]
        hn = hn_ref[j]
        z_top = jnp.where(i == 0, 0.0, hp[:, CONV_W:2 * CONV_W] * hp[:, 2 * CONV_W:3 * CONV_W])
        z_bot = jnp.where(i == n_lat_tiles - 1, 0.0, hn[:, CONV_W:2 * CONV_W] * hn[:, 2 * CONV_W:3 * CONV_W])
        zcat = jnp.concatenate([z_top, z, z_bot], axis=0)
        z_up = zcat[0:TILE]
        z_down = zcat[2 * HALO:2 * HALO + TILE]
        conv = cw[0:1] * jnp.where(along_seq, z_prev, z_up) + cw[1:2] * z
        conv = conv + cw[2:3] * jnp.where(along_seq, z_next, z_down)
        ycat[j * TILE:(j + 1) * TILE, 0:CONV_W] = (pc[:, 0:CONV_W] * conv).astype(BF16)

    outs = []
    for (j, rows, cols, c, p), sc in zip(tiles, scores):
        qp = q_ref[j, rows, cols].astype(F32)
        vp = v_ref[j, rows, cols]
        zero = jnp.zeros_like(vp)
        sc = sc * dec[p]
        lhs = jnp.concatenate([sc[0:CHUNK].astype(BF16), sc[CHUNK:].astype(BF16),
                               (qp * qdec[0, :, cols]).astype(BF16), (qp * qdec[1, :, cols]).astype(BF16)], axis=1)
        rhs = jnp.concatenate([jnp.where(low_head, vp, zero), jnp.where(low_head, zero, vp),
                               sf_ref[j, c, cols, :], sb[j, c][cols, :]], axis=0)
        outs.append(jnp.dot(lhs, rhs, preferred_element_type=F32))

    group = lax.broadcasted_iota(jnp.int32, (1, SGU_W), 1) // (SGU_W // SGU_GROUPS)
    for j in range(nb):
        vn = _standardize(sv_ref[j]).astype(BF16)
        for c in range(TILE // CHUNK):
            rows = slice(c * CHUNK, (c + 1) * CHUNK)
            mixed = jnp.zeros((CHUNK, SGU_W), F32)
            for gi in range(SGU_GROUPS):
                m = jnp.dot(sguw_ref[0, gi], vn[rows], preferred_element_type=F32)
                mixed = jnp.where(group == gi, m, mixed)
            ycat[out_rows(j, rows), CONV_W + RET_W:] = (su_ref[j, rows, :] * (mixed + sgub_ref[0])).astype(BF16)

    for (j, rows, cols, c, p), o in zip(tiles, outs):
        ycat[out_rows(j, rows), CONV_W + p * PAIR_W:CONV_W + (p + 1) * PAIR_W] = (
            _silu(g_ref[j, rows, cols]) * half_norm(o)).astype(BF16)

    y = jnp.dot(ycat[...], wout_ref[0], preferred_element_type=F32)
    ln = ln_ref[0]
    hms = []
    for j in range(nb):
        mod = mod_ref[0, j, 0]
        x1 = _standardize(ALPHA * jnp.where(is_ctx, ctx_ref[j], x_ref[j]) + mod[2:3] * y[j * TILE:(j + 1) * TILE])
        x1 = x1 * ln[0:1] + ln[1:2]
        x1_ref[j] = x1
        hm = x1 * (1.0 + mod[4:5]) + mod[3:4]
        hm_ref[j] = _pack_bf16_pairs(hm)
        hms.append(hm.astype(BF16))
    lt = lax.dot_general(wr_ref[0], jnp.concatenate(hms, axis=0), (((1,), (1,)), ((), ())),
                         preferred_element_type=F32) + br_ref[0]
    cls = _route_class(lt).astype(F32)
    for j in range(nb):
        route_ref[j, 0] = jnp.concatenate([cls[:, j * TILE:(j + 1) * TILE], jnp.zeros((7, TILE), F32)], axis=0)


def _mix_call(layer, n_lat_tiles, n_proc, lgf, lgb, pc, q, k, v, g, su, sv, sf, x_lat, x_ctx, ctx_tile, mod, conv_w,
              sgu_w_bf, sgu_bias, w_out_bf, ln1, w_route_t, b_route_t, lg_rows, lg_cols, to_cast=()):
    bsz, length, _ = q.shape
    nb = BATCH_PER_STEP
    n_tiles = n_lat_tiles + 1
    tile_of = lambda s: n_lat_tiles - s
    cast_in, cast_out, cast_shapes = _cast_specs(to_cast, layer, lambda b, s, *_: b * n_tiles + s, bsz // nb * n_tiles)
    halos_per_tile = TILE // HALO
    n_halo = length // HALO
    cpt = TILE // CHUNK
    tok = lambda w: pl.BlockSpec((nb, TILE, w), lambda b, s, *_: (b, tile_of(s), 0))
    tok_out = lambda w: pl.BlockSpec((nb, TILE, w), lambda b, s, *_: (b, jnp.minimum(tile_of(s), n_proc - 1), 0))
    per_layer = lambda *shape: pl.BlockSpec((1,) + shape, lambda b, s, *_: (layer,) + (0,) * len(shape))
    grid_spec = pltpu.PrefetchScalarGridSpec(
        num_scalar_prefetch=2,
        grid=(bsz // nb, n_tiles),
        in_specs=[
            tok(3 * CONV_W),
            pl.BlockSpec((nb, HALO, 3 * CONV_W),
                         lambda b, s, *_: (b, jnp.maximum(tile_of(s) * halos_per_tile - 1, 0), 0)),
            pl.BlockSpec((nb, HALO, 3 * CONV_W),
                         lambda b, s, *_: (b, jnp.minimum((tile_of(s) + 1) * halos_per_tile, n_halo - 1), 0)),
            tok(RET_W), tok(RET_W), tok(RET_W), tok(RET_W), tok(SGU_W), tok(SGU_W),
            pl.BlockSpec((nb, cpt, RET_W, PAIR_W), lambda b, s, *_: (b, tile_of(s), 0, 0)),
            pl.BlockSpec((nb, TILE, D_MODEL), lambda b, s, *_: (b, jnp.minimum(tile_of(s), n_lat_tiles - 1), 0)),
            pl.BlockSpec((nb, TILE, D_MODEL), lambda b, s, *_: (b, ctx_tile, 0)),
            pl.BlockSpec((1, nb, 1, MOD_ROWS, D_MODEL),
                         lambda b, s, *_: (layer, b, tile_of(s) // n_lat_tiles, 0, 0)),
            per_layer(8, CONV_W),
            per_layer(SGU_GROUPS, CHUNK, CHUNK),
            per_layer(CHUNK, SGU_W),
            per_layer(D_MODEL, D_MODEL),
            per_layer(8, D_MODEL),
            per_layer(ROUTE_ROWS, D_MODEL),
            per_layer(ROUTE_ROWS, 1),
            pl.BlockSpec((1, 1, RET_W), lambda b, s, *_: (2 * layer, 0, 0)),
            pl.BlockSpec((1, 1, RET_W), lambda b, s, *_: (2 * layer + 1, 0, 0)),
            pl.BlockSpec((1, RET_W, 1), lambda b, s, *_: (2 * layer + 1, 0, 0)),
        ] + cast_in,
        out_specs=[tok_out(D_MODEL), tok_out(PACK_W),
                   pl.BlockSpec((nb, 1, 8, TILE), lambda b, s, *_: (b, jnp.minimum(tile_of(s), n_proc - 1), 0, 0))]
        + cast_out,
        scratch_shapes=[pltpu.VMEM((nb * TILE, D_MODEL), BF16),
                        pltpu.VMEM((RET_W // PAIR_W, 2 * CHUNK, CHUNK), F32),
                        pltpu.VMEM((2, CHUNK, RET_W), F32),
                        pltpu.VMEM((nb, RET_W, PAIR_W), F32)],
    )
    return pl.pallas_call(
        functools.partial(_mix_kernel, n_lat_tiles, len(to_cast)),
        grid_spec=grid_spec,
        out_shape=[jax.ShapeDtypeStruct((bsz, n_proc * TILE, D_MODEL), F32),
                   jax.ShapeDtypeStruct((bsz, n_proc * TILE, PACK_W), jnp.uint32),
                   jax.ShapeDtypeStruct((bsz, n_proc, 8, TILE), F32)] + cast_shapes,
        compiler_params=_params(2),
    )(lgf, lgb, pc, pc, pc, q, k, v, g, su, sv, sf, x_lat, x_ctx, mod, conv_w, sgu_w_bf, sgu_bias, w_out_bf,
      ln1, w_route_t, b_route_t, lg_rows, lg_rows, lg_cols, *to_cast)


def _rank_kernel(route_ref, class_ref, rank_ref, counts_ref, running):
    @pl.when(pl.program_id(0) == 0)
    def _():
        running[...] = jnp.zeros_like(running)

    sub = lax.broadcasted_iota(jnp.int32, (ROUTE_W, TILE), 0)
    earlier = (lax.broadcasted_iota(jnp.int32, (TILE, TILE), 0)
               < lax.broadcasted_iota(jnp.int32, (TILE, TILE), 1)).astype(BF16)
    for t in range(route_ref.shape[0]):
        cls = route_ref[t, 0:1, :].astype(jnp.int32)
        class_ref[t] = cls
        onehot = sub == cls
        before = jnp.dot(onehot.astype(BF16), earlier, preferred_element_type=F32) + running[...]
        rank_ref[t] = jnp.sum(jnp.where(onehot, before, 0.0), axis=0, keepdims=True).astype(jnp.int32)
        running[...] += jnp.sum(onehot.astype(F32), axis=1, keepdims=True)
    counts_ref[...] = jnp.broadcast_to(running[...], counts_ref.shape)


def _rank_call(route):
    n_tiles = route.shape[0]
    per_step = math.gcd(n_tiles, RANK_TILES)
    per_tile = pl.BlockSpec((per_step, 1, TILE), lambda i: (i, 0, 0))
    return pl.pallas_call(
        _rank_kernel,
        grid=(n_tiles // per_step,),
        in_specs=[pl.BlockSpec((per_step, 8, TILE), lambda i: (i, 0, 0))],
        out_specs=[per_tile, per_tile, pl.BlockSpec((ROUTE_W, ROUTE_W), lambda i: (0, 0))],
        out_shape=[jax.ShapeDtypeStruct((n_tiles, 1, TILE), jnp.int32), jax.ShapeDtypeStruct((n_tiles, 1, TILE), jnp.int32),
                   jax.ShapeDtypeStruct((ROUTE_W, ROUTE_W), F32)],
        scratch_shapes=[pltpu.VMEM((ROUTE_W, 1), F32)],
        compiler_params=_params(1),
    )(route)


def _dispatch_kernel(n_cast, dest_ref, pad_end_ref, pad_len_ref, n_used_ref, hm_ref, *refs):
    cast_in, xs_ref, cast_out = refs[:n_cast], refs[n_cast], refs[n_cast + 1:-3]
    zeros, sem, pad_sem = refs[-3:]
    for src, dst in zip(cast_in, cast_out):
        dst[...] = src[...].astype(BF16)
    step = pl.program_id(0)
    base = step * DISPATCH_TILE
    half = BLOCK_M // 2

    def for_each_pad_copy(fn):
        def per_class(c, carry):
            off = pad_end_ref[c]
            n = pad_len_ref[c]
            for shift in range(BLOCK_M.bit_length() - 2, -1, -1):
                bit = 1 << shift
                off = off - (n & bit)

                @pl.when((n & bit) != 0)
                def _():
                    if bit >= 8:
                        fn(pltpu.make_async_copy(zeros.at[pl.ds(0, bit)], xs_ref.at[pl.ds(pl.multiple_of(off, 8), bit)],
                                                 pad_sem))
                    else:
                        for j in range(bit):
                            fn(pltpu.make_async_copy(zeros.at[pl.ds(0, 1)], xs_ref.at[pl.ds(off + j, 1)], pad_sem))

            return carry

        lax.fori_loop(0, N_CLASSES, per_class, 0)

        def per_half_block(j, carry):
            fn(pltpu.make_async_copy(zeros, xs_ref.at[pl.ds(pl.multiple_of(j * half, 8), half)], pad_sem))
            return carry

        lax.fori_loop(n_used_ref[0] * 2, xs_ref.shape[0] // half, per_half_block, 0)

    @pl.when(step == 0)
    def _():
        zeros[...] = jnp.zeros_like(zeros)
        for_each_pad_copy(lambda cp: cp.start())

    for r in range(DISPATCH_TILE):
        pltpu.make_async_copy(hm_ref.at[pl.ds(r, 1)], xs_ref.at[pl.ds(dest_ref[base + r], 1)],
                              sem).start(priority=r % 2)
    for r in range(DISPATCH_TILE):
        pltpu.make_async_copy(hm_ref.at[pl.ds(0, 1)], xs_ref.at[pl.ds(0, 1)], sem).wait()

    @pl.when(step == pl.num_programs(0) - 1)
    def _():
        for_each_pad_copy(lambda cp: cp.wait())


def _dispatch_call(layer, dest, pad_end, pad_len, n_used, hm_flat, n_slots, to_cast=()):
    n_tok = hm_flat.shape[0]
    assert n_tok % DISPATCH_TILE == 0
    cast_in, cast_out, cast_shapes = _cast_specs(to_cast, layer, lambda i, *_: i, n_tok // DISPATCH_TILE)
    grid_spec = pltpu.PrefetchScalarGridSpec(
        num_scalar_prefetch=4,
        grid=(n_tok // DISPATCH_TILE,),
        in_specs=[pl.BlockSpec((DISPATCH_TILE, PACK_W), lambda i, *_: (i, 0))] + cast_in,
        out_specs=[pl.BlockSpec(memory_space=pl.ANY)] + cast_out,
        scratch_shapes=[pltpu.VMEM((BLOCK_M // 2, PACK_W), jnp.uint32), pltpu.SemaphoreType.DMA,
                        pltpu.SemaphoreType.DMA],
    )
    return pl.pallas_call(
        functools.partial(_dispatch_kernel, len(to_cast)),
        grid_spec=grid_spec,
        out_shape=[jax.ShapeDtypeStruct((n_slots, PACK_W), jnp.uint32)] + cast_shapes,
        compiler_params=_params(1),
    )(dest, pad_end, pad_len, n_used, hm_flat, *to_cast)


def _expert_kernel(first_blk_ref, cls_lo_ref, cls_hi_ref, xs_ref, wr_ref, br_ref, wg_lo, wu_lo, wd_lo, wg_hi, wu_hi, wd_hi,
                   ys_ref, xbuf, ybuf, xsem, ysem):
    c = pl.program_id(0)
    n_used = first_blk_ref[N_CLASSES]
    n_blocks = ys_ref.shape[0] // BLOCK_M
    rows_of = lambda g: pl.ds(pl.multiple_of(g * BLOCK_M, BLOCK_M), BLOCK_M)
    x_copy = lambda g, slot: pltpu.make_async_copy(xs_ref.at[rows_of(g)], xbuf.at[slot], xsem.at[slot])
    y_copy = lambda g, slot: pltpu.make_async_copy(ybuf.at[slot], ys_ref.at[rows_of(g)], ysem.at[slot])

    @pl.when(c == 0)
    def _():
        x_copy(0, 0).start()

    def block(g, carry):
        slot = g % 2
        x_copy(g, slot).wait()

        @pl.when(g + 1 < n_used)
        def _():
            x_copy(g + 1, 1 - slot).start()

        @pl.when(g >= 2)
        def _():
            y_copy(g - 2, slot).wait()

        xb = _unpack_bf16_pairs(xbuf[slot])

        logits = jnp.dot(xb, wr_ref[0], preferred_element_type=F32) + br_ref[0]
        lane = lax.broadcasted_iota(jnp.int32, logits.shape, 1)
        gl = jnp.where(lane < N_GROUPS, logits, -jnp.inf)
        g_prob = 1.0 / jnp.sum(jnp.exp(gl - jnp.max(gl, axis=-1, keepdims=True)), axis=-1, keepdims=True)
        l_lo = jnp.sum(jnp.where(lane == N_GROUPS + cls_lo_ref[c], logits, 0.0), axis=-1, keepdims=True)
        l_hi = jnp.sum(jnp.where(lane == N_GROUPS + cls_hi_ref[c], logits, 0.0), axis=-1, keepdims=True)
        m = jnp.maximum(l_lo, l_hi)
        p_lo = jnp.exp(l_lo - m)
        p_hi = jnp.exp(l_hi - m)

        h_lo = jnp.dot(xb, wg_lo[0], preferred_element_type=F32)
        u_lo = jnp.dot(xb, wu_lo[0], preferred_element_type=F32)
        h_hi = jnp.dot(xb, wg_hi[0], preferred_element_type=F32)
        u_hi = jnp.dot(xb, wu_hi[0], preferred_element_type=F32)
        a_lo = (_silu(h_lo) * u_lo).astype(BF16)
        a_hi = (_silu(h_hi) * u_hi).astype(BF16)
        y_lo = jnp.dot(a_lo, wd_lo[0], preferred_element_type=F32)
        y_hi = jnp.dot(a_hi, wd_hi[0], preferred_element_type=F32)
        ybuf[slot] = y_lo * (g_prob * (p_lo / (p_lo + p_hi))) + y_hi * (g_prob * (p_hi / (p_lo + p_hi)))
        y_copy(g, slot).start()
        return carry

    lax.fori_loop(first_blk_ref[c], first_blk_ref[c + 1], block, 0)

    @pl.when(c == pl.num_programs(0) - 1)
    def _():
        @pl.when(n_used >= 2)
        def _():
            y_copy(n_used - 2, n_used % 2).wait()

        y_copy(n_used - 1, (n_used - 1) % 2).wait()
        ybuf[0] = jnp.zeros((BLOCK_M, D_MODEL), F32)

        def fill(g, carry):
            y_copy(g, 0).start()
            return carry

        def fill_done(g, carry):
            y_copy(g, 0).wait()
            return carry

        lax.fori_loop(n_used, n_blocks, fill, 0)
        lax.fori_loop(n_used, n_blocks, fill_done, 0)


def _expert_call(layer, first_blk, xs, w_route_bf, b_route, w_gate_bf, w_up_bf, w_down_bf):
    up_spec = lambda table: pl.BlockSpec((1, D_MODEL, EXPERT_HIDDEN), lambda c, fb, lo, hi: ((lo, hi)[table][c], 0, 0))
    down_spec = lambda table: pl.BlockSpec((1, EXPERT_HIDDEN, D_MODEL), lambda c, fb, lo, hi: ((lo, hi)[table][c], 0, 0))
    grid_spec = pltpu.PrefetchScalarGridSpec(
        num_scalar_prefetch=3,
        grid=(N_CLASSES,),
        in_specs=[pl.BlockSpec(memory_space=pl.ANY),
                  pl.BlockSpec((1, D_MODEL, ROUTE_W), lambda c, *_: (layer, 0, 0)),
                  pl.BlockSpec((1, 1, ROUTE_W), lambda c, *_: (layer, 0, 0)),
                  up_spec(0), up_spec(0), down_spec(0), up_spec(1), up_spec(1), down_spec(1)],
        out_specs=pl.BlockSpec(memory_space=pl.ANY),
        scratch_shapes=[pltpu.VMEM((2, BLOCK_M, PACK_W), jnp.uint32), pltpu.VMEM((2, BLOCK_M, D_MODEL), F32),
                        pltpu.SemaphoreType.DMA((2,)), pltpu.SemaphoreType.DMA((2,))],
    )
    return pl.pallas_call(
        _expert_kernel,
        grid_spec=grid_spec,
        out_shape=jax.ShapeDtypeStruct((xs.shape[0], D_MODEL), F32),
        compiler_params=_params(1),
    )(first_blk, jnp.asarray(CLASS_LO), jnp.asarray(CLASS_HI), xs, w_route_bf, b_route, w_gate_bf, w_up_bf, w_down_bf,
      w_gate_bf, w_up_bf, w_down_bf)


def _combine_kernel(tiles_per_batch, dest_ref, ys_ref, x1_ref, mod_ref, ln_ref, out_ref, buf, sem):
    nb = COMBINE_BATCH
    group, i = pl.program_id(0), pl.program_id(1)
    step = group * tiles_per_batch + i
    slot = step % 2
    has_next = step + 1 < pl.num_programs(0) * tiles_per_batch
    wrap = i + 1 == tiles_per_batch

    def gather(grp, tile, to_slot):
        for j in range(nb):
            base = ((grp * nb + j) * tiles_per_batch + tile) * TILE
            for r in range(TILE):
                pltpu.make_async_copy(ys_ref.at[pl.ds(dest_ref[base + r], 1)], buf.at[to_slot, j, pl.ds(r, 1)],
                                      sem.at[to_slot]).start(priority=r % 2)

    @pl.when(step == 0)
    def _():
        gather(group, i, slot)

    @pl.when(has_next)
    def _():
        gather(jnp.where(wrap, group + 1, group), jnp.where(wrap, 0, i + 1), 1 - slot)

    for r in range(nb * TILE):
        pltpu.make_async_copy(ys_ref.at[pl.ds(0, 1)], buf.at[slot, 0, pl.ds(0, 1)], sem.at[slot]).wait()

    ln = ln_ref[0]
    for j in range(nb):
        out_ref[j] = _standardize(ALPHA * x1_ref[j] + mod_ref[0, j, 0, 5:6] * buf[slot, j]) * ln[0:1] + ln[1:2]


def _combine_call(layer, n_lat_tiles, dest, ys, x1, mod, ln2):
    bsz, length, _ = x1.shape
    nb = COMBINE_BATCH
    assert bsz % nb == 0
    tiles_per_batch = length // TILE
    tok = lambda w: pl.BlockSpec((nb, TILE, w), lambda b, i, *_: (b, i, 0))
    grid_spec = pltpu.PrefetchScalarGridSpec(
        num_scalar_prefetch=1,
        grid=(bsz // nb, tiles_per_batch),
        in_specs=[
            pl.BlockSpec(memory_space=pl.ANY),
            tok(D_MODEL),
            pl.BlockSpec((1, nb, 1, MOD_ROWS, D_MODEL),
                         lambda b, i, *_: (layer, b, jnp.minimum(i // n_lat_tiles, 1), 0, 0)),
            pl.BlockSpec((1, 8, D_MODEL), lambda b, i, *_: (layer, 0, 0)),
        ],
        out_specs=tok(D_MODEL),
        scratch_shapes=[pltpu.VMEM((2, nb, TILE, D_MODEL), F32), pltpu.SemaphoreType.DMA((2,))],
    )
    return pl.pallas_call(
        functools.partial(_combine_kernel, tiles_per_batch),
        grid_spec=grid_spec,
        out_shape=jax.ShapeDtypeStruct(x1.shape, F32),
        compiler_params=_params(2),
    )(dest, ys, x1, mod, ln2)


def _dispatch_plan(class_of_token, rank, counts):
    n_tok = class_of_token.shape[0]
    counts = counts.astype(jnp.int32)
    pcounts = (counts + BLOCK_M - 1) // BLOCK_M * BLOCK_M
    pends = jnp.cumsum(pcounts)
    pstarts = pends - pcounts
    classes = jnp.arange(ROUTE_W, dtype=jnp.int32)
    dest = jnp.sum(jnp.where(class_of_token[:, None] == classes[None, :], pstarts[None, :], 0), axis=1) + rank
    n_blocks = n_tok // BLOCK_M + N_CLASSES
    n_used = pends[-1:] // BLOCK_M
    first_blk = pstarts[:N_CLASSES + 1] // BLOCK_M
    return dest.astype(jnp.int32), pends, pcounts - counts, first_blk, n_used.astype(jnp.int32), n_blocks * BLOCK_M


def _pad_rows(a, rows):
    return jnp.pad(a, [(0, 0)] * (a.ndim - 2) + [(0, rows - a.shape[-2]), (0, 0)])


def kernel(x, c, ctx, c_ctx, w_ada, b_ada, w_in, conv_w, ret_decay_fwd, ret_decay_bwd, sgu_w, sgu_b, w_out, ln1_g, ln1_b,
           router_group_w, router_group_b, router_expert_w, router_expert_b, moe_w_gate, moe_w_up, moe_w_down, ln2_g,
           ln2_b):
    bsz, seq, d = x.shape
    ctx_len = ctx.shape[1]
    assert d == D_MODEL and ctx_len == TILE and seq % TILE == 0 and seq % GRID_W == 0
    n_lat_tiles = seq // TILE

    cond = _pad_rows(jnp.concatenate([c, c_ctx[None, :]], axis=0), 16)
    ada = _ada_call(cond, w_ada, b_ada)
    mod_lat = ada[:, :bsz].reshape(DEPTH, bsz, N_MOD, d)
    mod_ctx = jnp.broadcast_to(ada[:, bsz].reshape(DEPTH, 1, N_MOD, d), mod_lat.shape)
    mod = _pad_rows(jnp.stack([mod_lat, mod_ctx], axis=2), MOD_ROWS)

    w_in_bf = w_in.astype(BF16)
    w_out_bf = w_out.astype(BF16)
    sgu_w_bf = sgu_w.astype(BF16)
    sgu_bias = jnp.repeat(jnp.swapaxes(sgu_b, 1, 2), SGU_W // SGU_GROUPS, axis=2)
    conv_w8 = _pad_rows(conv_w, 8)
    ln1 = _pad_rows(jnp.stack([ln1_g, ln1_b], axis=1), 8)
    ln2 = _pad_rows(jnp.stack([ln2_g, ln2_b], axis=1), 8)
    w_route = jnp.concatenate([router_group_w, jnp.swapaxes(router_expert_w, 1, 2).reshape(DEPTH, d, N_EXPERTS)], axis=2)
    w_route_bf = jnp.pad(w_route, ((0, 0), (0, 0), (0, ROUTE_W - w_route.shape[2]))).astype(BF16)
    b_route = jnp.concatenate([router_group_b, router_expert_b.reshape(DEPTH, N_EXPERTS)], axis=1)
    b_route = jnp.pad(b_route, ((0, 0), (0, ROUTE_W - b_route.shape[1])))[:, None, :].astype(F32)
    w_route_t = jnp.concatenate([_pad_rows(jnp.swapaxes(w_route[:, :, :N_GROUPS], 1, 2), EXPERT_ROW0),
                                 jnp.swapaxes(w_route[:, :, N_GROUPS:], 1, 2)], axis=1)
    w_route_t = _pad_rows(w_route_t, ROUTE_ROWS).astype(BF16)
    b_route_t = jnp.concatenate([_pad_rows(b_route[:, 0, :N_GROUPS, None], EXPERT_ROW0),
                                 b_route[:, 0, N_GROUPS:N_GROUPS + N_EXPERTS, None]], axis=1)
    b_route_t = _pad_rows(b_route_t, ROUTE_ROWS)
    lg = jnp.stack([jax.nn.log_sigmoid(ret_decay_fwd.astype(F32)), jax.nn.log_sigmoid(ret_decay_bwd.astype(F32))],
                   axis=1)
    lg_lanes = jnp.repeat(lg, RET_DK, axis=2).reshape(DEPTH * 2, RET_W)
    lg_rows = lg_lanes[:, None, :]
    lg_cols = lg_lanes[:, :, None]

    x_lat, x_ctx, ctx_tile = x, ctx, 0
    for layer in range(DEPTH):
        last = layer == DEPTH - 1
        pc, q, k, v, g, su, sv, sf, w_gate_bf = _inproj_call(layer, x_lat, x_ctx, ctx_tile, mod, w_in_bf, lg_rows, lg_cols,
                                                             n_lat_tiles, to_cast=(moe_w_gate.reshape(-1, EXPERT_HIDDEN),))
        n_proc = n_lat_tiles if last else n_lat_tiles + 1
        x1, hm, route, w_up_bf = _mix_call(layer, n_lat_tiles, n_proc, lg[layer, 0], lg[layer, 1], pc, q, k, v, g, su, sv,
                                           sf, x_lat, x_ctx, ctx_tile, mod, conv_w8, sgu_w_bf, sgu_bias, w_out_bf, ln1,
                                           w_route_t, b_route_t, lg_rows, lg_cols,
                                           to_cast=(moe_w_up.reshape(-1, EXPERT_HIDDEN),))
        class_of_token, rank, counts = _rank_call(route.reshape(-1, 8, TILE))
        dest, pad_end, pad_len, first_blk, n_used, n_slots = _dispatch_plan(class_of_token.reshape(-1), rank.reshape(-1),
                                                                            counts[:, 0])
        xs, w_down_bf = _dispatch_call(layer, dest, pad_end, pad_len, n_used, hm.reshape(-1, PACK_W), n_slots,
                                       to_cast=(moe_w_down.reshape(-1, D_MODEL),))
        ys = _expert_call(layer, first_blk, xs, w_route_bf, b_route,
                          w_gate_bf.reshape(N_EXPERTS, D_MODEL, EXPERT_HIDDEN), w_up_bf.reshape(N_EXPERTS, D_MODEL, EXPERT_HIDDEN),
                          w_down_bf.reshape(N_EXPERTS, EXPERT_HIDDEN, D_MODEL))
        xa = _combine_call(layer, n_lat_tiles, dest, ys, x1, mod, ln2)
        x_lat, x_ctx, ctx_tile = xa, xa, n_lat_tiles
    return xa
```

```python
import functools
import math

import jax
import jax.numpy as jnp
import numpy as np
from jax import lax
from jax.experimental import pallas as pl
from jax.experimental.pallas import tpu as pltpu

F32 = jnp.float32
BF16 = jnp.bfloat16

D_MODEL = 1024
DEPTH = 2
GRID_W = 64
CONV_W = 256
RET_W = 512
RET_HEADS = 8
RET_DK = 64
PAIR_W = 2 * RET_DK
CHUNK = 128
SGU_W = 256
SGU_GROUPS = 4
IN_COLS = 3 * CONV_W + 4 * RET_W + 2 * SGU_W
N_GROUPS = 4
EXPERTS_PER_GROUP = 8
N_EXPERTS = N_GROUPS * EXPERTS_PER_GROUP
EXPERT_HIDDEN = 512
N_MOD = 6
MOD_ROWS = 8
LN_EPS = 1e-5
ALPHA = (2 * DEPTH) ** 0.25

TILE = 256
BATCH_PER_STEP = 2
DISPATCH_TILE = 4 * TILE
HALO = GRID_W
BLOCK_M = 256
ROUTE_W = 128
CAST_STEPS = 128
PACK_W = D_MODEL // 2
ROUTE_ROWS = 48
EXPERT_ROW0 = 8
RANK_TILES = 8
PAIRS_PER_GROUP = EXPERTS_PER_GROUP * (EXPERTS_PER_GROUP - 1) // 2
N_CLASSES = N_GROUPS * PAIRS_PER_GROUP
CLASS_LO = np.array([g * EXPERTS_PER_GROUP + lo for g in range(N_GROUPS) for lo in range(EXPERTS_PER_GROUP)
                     for hi in range(lo + 1, EXPERTS_PER_GROUP)] + [N_EXPERTS - 2] * (ROUTE_W - N_CLASSES), np.int32)
CLASS_HI = np.array([g * EXPERTS_PER_GROUP + hi for g in range(N_GROUPS) for lo in range(EXPERTS_PER_GROUP)
                     for hi in range(lo + 1, EXPERTS_PER_GROUP)] + [N_EXPERTS - 1] * (ROUTE_W - N_CLASSES), np.int32)
VMEM_LIMIT = 56 * 1024 * 1024


def _params(n_axes):
    return pltpu.CompilerParams(dimension_semantics=("arbitrary",) * n_axes, vmem_limit_bytes=VMEM_LIMIT)


def _standardize(v):
    mu = jnp.mean(v, axis=-1, keepdims=True)
    var = jnp.mean(jnp.square(v - mu), axis=-1, keepdims=True)
    return (v - mu) * lax.rsqrt(var + LN_EPS)


def _silu(v):
    return v * jax.nn.sigmoid(v)


def _pack_bf16_pairs(v):
    half = v.shape[1] // 2
    bits = lambda t: pltpu.bitcast(t.astype(BF16).astype(F32), jnp.uint32)
    return bits(v[:, :half]) | (bits(v[:, half:]) >> 16)


def _unpack_bf16_pairs(w):
    hi = pltpu.bitcast(w & jnp.uint32(0xFFFF0000), F32)
    lo = pltpu.bitcast(w << 16, F32)
    return jnp.concatenate([hi, lo], axis=1).astype(BF16)


def _ada_kernel(c_ref, w_ref, b_ref, o_ref):
    a = _silu(c_ref[...]).astype(BF16)
    o_ref[0] = jnp.dot(a, w_ref[0].astype(BF16), preferred_element_type=F32) + b_ref[0]


def _ada_call(cond, w_ada, b_ada):
    rows = cond.shape[0]
    cols = w_ada.shape[-1]
    tn = 1536
    return pl.pallas_call(
        _ada_kernel,
        grid=(DEPTH, cols // tn),
        in_specs=[
            pl.BlockSpec((rows, D_MODEL), lambda l, j: (0, 0)),
            pl.BlockSpec((1, D_MODEL, tn), lambda l, j: (l, 0, j)),
            pl.BlockSpec((1, 1, tn), lambda l, j: (l, 0, j)),
        ],
        out_specs=pl.BlockSpec((1, rows, tn), lambda l, j: (l, 0, j)),
        out_shape=jax.ShapeDtypeStruct((DEPTH, rows, cols), F32),
        compiler_params=_params(2),
    )(cond, w_ada, b_ada.reshape(DEPTH, 1, cols))


def _own_head_mask():
    row_head = lax.broadcasted_iota(jnp.int32, (RET_W, PAIR_W), 0) % PAIR_W // RET_DK
    lane_head = lax.broadcasted_iota(jnp.int32, (RET_W, PAIR_W), 1) // RET_DK
    return row_head == lane_head


def _pair_kv(kc, vc):
    contract_rows = (((0,), (0,)), ((), ()))
    return jnp.concatenate([
        lax.dot_general(kc[:, p * PAIR_W:(p + 1) * PAIR_W], vc[:, p * PAIR_W:(p + 1) * PAIR_W], contract_rows,
                        preferred_element_type=F32) for p in range(RET_W // PAIR_W)], axis=0)


def _inproj_kernel(n_lat_tiles, n_cast, x_ref, ctx_ref, mod_ref, w_ref, lgf_ref, lgfc_ref, *refs):
    cast_in, (pc_ref, q_ref, k_ref, v_ref, g_ref, su_ref, sv_ref, sf_ref), cast_out = (
        refs[:n_cast], refs[n_cast:n_cast + 8], refs[n_cast + 8:-1])
    state = refs[-1]
    for src, dst in zip(cast_in, cast_out):
        dst[...] = src[...].astype(BF16)
    step = pl.program_id(1)
    is_ctx = step == 0
    h = jnp.concatenate([
        (jnp.where(is_ctx, ctx_ref[j], x_ref[j]) * (1.0 + mod_ref[0, j, 0, 1:2]) + mod_ref[0, j, 0, 0:1]).astype(BF16)
        for j in range(BATCH_PER_STEP)], axis=0)

    def proj(lo, hi):
        return jnp.dot(h, w_ref[0, :, lo:hi], preferred_element_type=F32)

    def put(ref, val):
        for j in range(BATCH_PER_STEP):
            ref[j] = val[j * TILE:(j + 1) * TILE].astype(ref.dtype)

    o = 3 * CONV_W
    put(pc_ref, proj(0, o))
    put(q_ref, proj(o, o + RET_W))
    k = proj(o + RET_W, o + 2 * RET_W) * (RET_DK ** -0.5)
    put(k_ref, k)
    v = proj(o + 2 * RET_W, o + 3 * RET_W)
    put(v_ref, v)
    put(g_ref, proj(o + 3 * RET_W, o + 4 * RET_W))
    o += 4 * RET_W
    put(su_ref, proj(o, o + SGU_W))
    put(sv_ref, proj(o + SGU_W, o + 2 * SGU_W))

    @pl.when(step == 0)
    def _():
        state[...] = jnp.zeros_like(state)

    pos = lax.broadcasted_iota(jnp.int32, (CHUNK, 1), 0).astype(F32)
    k_decay = jnp.exp((CHUNK - 1.0 - pos) * lgf_ref[0])
    chunk_decay = jnp.exp(CHUNK * lgfc_ref[0])
    own_head = _own_head_mask()
    for j in range(BATCH_PER_STEP):
        kvs = [_pair_kv((k[rows] * k_decay).astype(BF16), v[rows].astype(BF16))
               for rows in (slice(j * TILE + c * CHUNK, j * TILE + (c + 1) * CHUNK) for c in range(TILE // CHUNK))]
        st = state[j]
        for c, kv in enumerate(kvs):
            sf_ref[j, c] = jnp.where(own_head, st, 0.0).astype(BF16)
            st = chunk_decay * st + kv
        state[j] = st


def _cast_specs(arrays, part, step_of, n_steps):
    steps = min(CAST_STEPS, 1 << (n_steps.bit_length() - 1))
    block = lambda a: (a.shape[0] // (DEPTH * steps), a.shape[1])
    slab = lambda *idx: jnp.minimum(step_of(*idx), steps - 1)
    return ([pl.BlockSpec(block(a), lambda *idx: (part * steps + slab(*idx), 0)) for a in arrays],
            [pl.BlockSpec(block(a), lambda *idx: (slab(*idx), 0)) for a in arrays],
            [jax.ShapeDtypeStruct((a.shape[0] // DEPTH, a.shape[1]), BF16) for a in arrays])


def _inproj_call(layer, x_lat, x_ctx, ctx_tile, mod, w_in_bf, lg_rows, lg_cols, n_lat_tiles, to_cast=()):
    bsz = x_lat.shape[0]
    assert bsz % BATCH_PER_STEP == 0
    nb = BATCH_PER_STEP
    length = (n_lat_tiles + 1) * TILE
    widths = (3 * CONV_W, RET_W, RET_W, RET_W, RET_W, SGU_W, SGU_W)
    cpt = TILE // CHUNK
    n_tiles = n_lat_tiles + 1
    tile_of = lambda s: (s + n_lat_tiles) % n_tiles
    cast_in, cast_out, cast_shapes = _cast_specs(to_cast, layer, lambda b, s: b * n_tiles + s, bsz // nb * n_tiles)
    return pl.pallas_call(
        functools.partial(_inproj_kernel, n_lat_tiles, len(to_cast)),
        grid=(bsz // nb, n_tiles),
        in_specs=[
            pl.BlockSpec((nb, TILE, D_MODEL), lambda b, s: (b, jnp.minimum(tile_of(s), n_lat_tiles - 1), 0)),
            pl.BlockSpec((nb, TILE, D_MODEL), lambda b, s: (b, ctx_tile, 0)),
            pl.BlockSpec((1, nb, 1, MOD_ROWS, D_MODEL), lambda b, s: (layer, b, tile_of(s) // n_lat_tiles, 0, 0)),
            pl.BlockSpec((1, D_MODEL, IN_COLS), lambda b, s: (layer, 0, 0)),
            pl.BlockSpec((1, 1, RET_W), lambda b, s: (2 * layer, 0, 0)),
            pl.BlockSpec((1, RET_W, 1), lambda b, s: (2 * layer, 0, 0)),
        ] + cast_in,
        out_specs=[pl.BlockSpec((nb, TILE, w), lambda b, s: (b, tile_of(s), 0)) for w in widths]
        + [pl.BlockSpec((nb, cpt, RET_W, PAIR_W), lambda b, s: (b, tile_of(s), 0, 0))] + cast_out,
        out_shape=[jax.ShapeDtypeStruct((bsz, length, w), BF16 if 1 <= j <= 3 else F32) for j, w in enumerate(widths)]
        + [jax.ShapeDtypeStruct((bsz, length // CHUNK, RET_W, PAIR_W), BF16)] + cast_shapes,
        scratch_shapes=[pltpu.VMEM((nb, RET_W, PAIR_W), F32)],
        compiler_params=_params(2),
    )(x_lat, x_ctx, mod, w_in_bf, lg_rows, lg_cols, *to_cast)


def _route_class(lt):
    assert EXPERTS_PER_GROUP == 8 and N_GROUPS <= 8
    sub = lax.broadcasted_iota(jnp.int32, (8, lt.shape[1]), 0)
    neg = jnp.float32(-jnp.inf)

    def top(vals):
        m = jnp.max(vals, axis=0, keepdims=True)
        return jnp.min(jnp.where(vals == m, sub, 8), axis=0, keepdims=True)

    gidx = top(jnp.where(sub < N_GROUPS, lt[0:8], neg))
    pair = jnp.zeros_like(gidx)
    for g in range(N_GROUPS):
        e = lt[EXPERT_ROW0 + 8 * g:EXPERT_ROW0 + 8 * (g + 1)]
        i1 = top(e)
        i2 = top(jnp.where(sub == i1, neg, e))
        e_lo = jnp.minimum(i1, i2)
        e_hi = jnp.maximum(i1, i2)
        pair_g = e_lo * (EXPERTS_PER_GROUP - 1) - ((e_lo * (e_lo - 1)) >> 1) + (e_hi - e_lo - 1)
        pair = jnp.where(gidx == g, pair_g, pair)
    return gidx * PAIRS_PER_GROUP + pair


def _mix_kernel(n_lat_tiles, n_cast, lgf_ref, lgb_ref, pc_ref, hp_ref, hn_ref, q_ref, k_ref, v_ref, g_ref, su_ref, sv_ref,
                sf_ref, x_ref, ctx_ref, mod_ref, convw_ref, sguw_ref, sgub_ref, wout_ref, ln_ref, wr_ref, br_ref,
                lgfr_ref, lgbr_ref, lgbc_ref, *refs):
    cast_in, (x1_ref, hm_ref, route_ref), cast_out = refs[:n_cast], refs[n_cast:n_cast + 3], refs[n_cast + 3:-4]
    ycat, dec, qdec, state = refs[-4:]
    for src, dst in zip(cast_in, cast_out):
        dst[...] = src[...].astype(BF16)
    step = pl.program_id(1)
    i = n_lat_tiles - step
    is_ctx = step == 0
    row = lax.broadcasted_iota(jnp.int32, (TILE, 1), 0)
    nb = BATCH_PER_STEP

    @pl.when(step == 0)
    def _():
        state[...] = jnp.zeros_like(state)

    pos_b = lax.broadcasted_iota(jnp.int32, (CHUNK, 1), 0).astype(F32)
    k_decay_b = jnp.exp(pos_b * lgbr_ref[0])
    chunk_decay_b = jnp.exp(CHUNK * lgbc_ref[0])
    own_head = _own_head_mask()
    sb = {}
    for j in range(nb):
        kvs = {c: _pair_kv((k_ref[j, c * CHUNK:(c + 1) * CHUNK, :].astype(F32) * k_decay_b).astype(BF16),
                           v_ref[j, c * CHUNK:(c + 1) * CHUNK, :]) for c in range(TILE // CHUNK)}
        st = state[j]
        for c in reversed(range(TILE // CHUNK)):
            sb[j, c] = jnp.where(own_head, st, 0.0).astype(BF16)
            st = chunk_decay_b * st + kvs[c]
        state[j] = st

    @pl.when((pl.program_id(0) == 0) & (step == 0))
    def _():
        pos = lax.broadcasted_iota(jnp.int32, (CHUNK, 1), 0).astype(F32)
        rel = pos - lax.broadcasted_iota(jnp.int32, (1, CHUNK), 1).astype(F32)
        for h in range(RET_HEADS):
            dec[h // 2, (h % 2) * CHUNK:(h % 2 + 1) * CHUNK, :] = jnp.where(
                rel > 0, jnp.exp(lgf_ref[h] * jnp.maximum(rel, 0.0)),
                jnp.where(rel < 0, jnp.exp(lgb_ref[h] * jnp.maximum(-rel, 0.0)), 2.0))
        qdec[0] = jnp.exp(lgfr_ref[0] * (pos + 1.0))
        qdec[1] = jnp.exp(lgbr_ref[0] * (CHUNK - pos))

    low_head = lax.broadcasted_iota(jnp.int32, (1, PAIR_W), 1) < RET_DK

    def half_norm(o):
        def half_mean(t):
            lo = jnp.sum(jnp.where(low_head, t, 0.0), axis=-1, keepdims=True)
            hi = jnp.sum(jnp.where(low_head, 0.0, t), axis=-1, keepdims=True)
            return jnp.where(low_head, lo, hi) * (1.0 / RET_DK)

        centred = o - half_mean(o)
        return centred * lax.rsqrt(half_mean(jnp.square(centred)) + LN_EPS)

    tiles = [(j, slice(c * CHUNK, (c + 1) * CHUNK), slice(p * PAIR_W, (p + 1) * PAIR_W), c, p)
             for j in range(nb) for c in range(TILE // CHUNK) for p in range(RET_W // PAIR_W)]
    out_rows = lambda j, rows: slice(j * TILE + rows.start, j * TILE + rows.stop)
    scores = []
    for j, rows, cols, c, p in tiles:
        qp = q_ref[j, rows, cols]
        zero = jnp.zeros_like(qp)
        q_split = jnp.concatenate([jnp.where(low_head, qp, zero), jnp.where(low_head, zero, qp)], axis=0)
        scores.append(lax.dot_general(q_split, k_ref[j, rows, cols],
                                      (((1,), (1,)), ((), ())), preferred_element_type=F32))

    line_mask = jnp.where(is_ctx, TILE - 1, GRID_W - 1)
    first = (row & line_mask) == 0
    last = (row & line_mask) == line_mask
    along_seq = lax.broadcasted_iota(jnp.int32, (1, CONV_W), 1) < jnp.where(is_ctx, CONV_W, CONV_W // 2)
    cw = convw_ref[0]
    for j in range(nb):
        pc = pc_ref[j]
        z = pc[:, CONV_W:2 * CONV_W] * pc[:, 2 * CONV_W:3 * CONV_W]
        z_prev = jnp.where(first, 0.0, pltpu.roll(z, 1, 0))
        z_next = jnp.where(last, 0.0, pltpu.roll(z, TILE - 1, 0))
        hp = hp_ref[j]
        hn = hn_ref[j]
        z_top = jnp.where(i == 0, 0.0, hp[:, CONV_W:2 * CONV_W] * hp[:, 2 * CONV_W:3 * CONV_W])
        z_bot = jnp.where(i == n_lat_tiles - 1, 0.0, hn[:, CONV_W:2 * CONV_W] * hn[:, 2 * CONV_W:3 * CONV_W])
        zcat = jnp.concatenate([z_top, z, z_bot], axis=0)
        z_up = zcat[0:TILE]
        z_down = zcat[2 * HALO:2 * HALO + TILE]
        conv = cw[0:1] * jnp.where(along_seq, z_prev, z_up) + cw[1:2] * z
        conv = conv + cw[2:3] * jnp.where(along_seq, z_next, z_down)
        ycat[j * TILE:(j + 1) * TILE, 0:CONV_W] = (pc[:, 0:CONV_W] * conv).astype(BF16)

    outs = []
    for (j, rows, cols, c, p), sc in zip(tiles, scores):
        qp = q_ref[j, rows, cols].astype(F32)
        vp = v_ref[j, rows, cols]
        zero = jnp.zeros_like(vp)
        sc = sc * dec[p]
        lhs = jnp.concatenate([sc[0:CHUNK].astype(BF16), sc[CHUNK:].astype(BF16),
                               (qp * qdec[0, :, cols]).astype(BF16), (qp * qdec[1, :, cols]).astype(BF16)], axis=1)
        rhs = jnp.concatenate([jnp.where(low_head, vp, zero), jnp.where(low_head, zero, vp),
                               sf_ref[j, c, cols, :], sb[j, c][cols, :]], axis=0)
        outs.append(jnp.dot(lhs, rhs, preferred_element_type=F32))

    group = lax.broadcasted_iota(jnp.int32, (1, SGU_W), 1) // (SGU_W // SGU_GROUPS)
    for j in range(nb):
        vn = _standardize(sv_ref[j]).astype(BF16)
        for c in range(TILE // CHUNK):
            rows = slice(c * CHUNK, (c + 1) * CHUNK)
            mixed = jnp.zeros((CHUNK, SGU_W), F32)
            for gi in range(SGU_GROUPS):
                m = jnp.dot(sguw_ref[0, gi], vn[rows], preferred_element_type=F32)
                mixed = jnp.where(group == gi, m, mixed)
            ycat[out_rows(j, rows), CONV_W + RET_W:] = (su_ref[j, rows, :] * (mixed + sgub_ref[0])).astype(BF16)

    for (j, rows, cols, c, p), o in zip(tiles, outs):
        ycat[out_rows(j, rows), CONV_W + p * PAIR_W:CONV_W + (p + 1) * PAIR_W] = (
            _silu(g_ref[j, rows, cols]) * half_norm(o)).astype(BF16)

    y = jnp.dot(ycat[...], wout_ref[0], preferred_element_type=F32)
    ln = ln_ref[0]
    hms = []
    for j in range(nb):
        mod = mod_ref[0, j, 0]
        x1 = _standardize(ALPHA * jnp.where(is_ctx, ctx_ref[j], x_ref[j]) + mod[2:3] * y[j * TILE:(j + 1) * TILE])
        x1 = x1 * ln[0:1] + ln[1:2]
        x1_ref[j] = x1
        hm = x1 * (1.0 + mod[4:5]) + mod[3:4]
        hm_ref[j] = _pack_bf16_pairs(hm)
        hms.append(hm.astype(BF16))
    lt = lax.dot_general(wr_ref[0], jnp.concatenate(hms, axis=0), (((1,), (1,)), ((), ())),
                         preferred_element_type=F32) + br_ref[0]
    cls = _route_class(lt).astype(F32)
    for j in range(nb):
        route_ref[j, 0] = jnp.concatenate([cls[:, j * TILE:(j + 1) * TILE], jnp.zeros((7, TILE), F32)], axis=0)


def _mix_call(layer, n_lat_tiles, n_proc, lgf, lgb, pc, q, k, v, g, su, sv, sf, x_lat, x_ctx, ctx_tile, mod, conv_w,
              sgu_w_bf, sgu_bias, w_out_bf, ln1, w_route_t, b_route_t, lg_rows, lg_cols, to_cast=()):
    bsz, length, _ = q.shape
    nb = BATCH_PER_STEP
    n_tiles = n_lat_tiles + 1
    tile_of = lambda s: n_lat_tiles - s
    cast_in, cast_out, cast_shapes = _cast_specs(to_cast, layer, lambda b, s, *_: b * n_tiles + s, bsz // nb * n_tiles)
    halos_per_tile = TILE // HALO
    n_halo = length // HALO
    cpt = TILE // CHUNK
    tok = lambda w: pl.BlockSpec((nb, TILE, w), lambda b, s, *_: (b, tile_of(s), 0))
    tok_out = lambda w: pl.BlockSpec((nb, TILE, w), lambda b, s, *_: (b, jnp.minimum(tile_of(s), n_proc - 1), 0))
    per_layer = lambda *shape: pl.BlockSpec((1,) + shape, lambda b, s, *_: (layer,) + (0,) * len(shape))
    grid_spec = pltpu.PrefetchScalarGridSpec(
        num_scalar_prefetch=2,
        grid=(bsz // nb, n_tiles),
        in_specs=[
            tok(3 * CONV_W),
            pl.BlockSpec((nb, HALO, 3 * CONV_W),
                         lambda b, s, *_: (b, jnp.maximum(tile_of(s) * halos_per_tile - 1, 0), 0)),
            pl.BlockSpec((nb, HALO, 3 * CONV_W),
                         lambda b, s, *_: (b, jnp.minimum((tile_of(s) + 1) * halos_per_tile, n_halo - 1), 0)),
            tok(RET_W), tok(RET_W), tok(RET_W), tok(RET_W), tok(SGU_W), tok(SGU_W),
            pl.BlockSpec((nb, cpt, RET_W, PAIR_W), lambda b, s, *_: (b, tile_of(s), 0, 0)),
            pl.BlockSpec((nb, TILE, D_MODEL), lambda b, s, *_: (b, jnp.minimum(tile_of(s), n_lat_tiles - 1), 0)),
            pl.BlockSpec((nb, TILE, D_MODEL), lambda b, s, *_: (b, ctx_tile, 0)),
            pl.BlockSpec((1, nb, 1, MOD_ROWS, D_MODEL),
                         lambda b, s, *_: (layer, b, tile_of(s) // n_lat_tiles, 0, 0)),
            per_layer(8, CONV_W),
            per_layer(SGU_GROUPS, CHUNK, CHUNK),
            per_layer(CHUNK, SGU_W),
            per_layer(D_MODEL, D_MODEL),
            per_layer(8, D_MODEL),
            per_layer(ROUTE_ROWS, D_MODEL),
            per_layer(ROUTE_ROWS, 1),
            pl.BlockSpec((1, 1, RET_W), lambda b, s, *_: (2 * layer, 0, 0)),
            pl.BlockSpec((1, 1, RET_W), lambda b, s, *_: (2 * layer + 1, 0, 0)),
            pl.BlockSpec((1, RET_W, 1), lambda b, s, *_: (2 * layer + 1, 0, 0)),
        ] + cast_in,
        out_specs=[tok_out(D_MODEL), tok_out(PACK_W),
                   pl.BlockSpec((nb, 1, 8, TILE), lambda b, s, *_: (b, jnp.minimum(tile_of(s), n_proc - 1), 0, 0))]
        + cast_out,
        scratch_shapes=[pltpu.VMEM((nb * TILE, D_MODEL), BF16),
                        pltpu.VMEM((RET_W // PAIR_W, 2 * CHUNK, CHUNK), F32),
                        pltpu.VMEM((2, CHUNK, RET_W), F32),
                        pltpu.VMEM((nb, RET_W, PAIR_W), F32)],
    )
    return pl.pallas_call(
        functools.partial(_mix_kernel, n_lat_tiles, len(to_cast)),
        grid_spec=grid_spec,
        out_shape=[jax.ShapeDtypeStruct((bsz, n_proc * TILE, D_MODEL), F32),
                   jax.ShapeDtypeStruct((bsz, n_proc * TILE, PACK_W), jnp.uint32),
                   jax.ShapeDtypeStruct((bsz, n_proc, 8, TILE), F32)] + cast_shapes,
        compiler_params=_params(2),
    )(lgf, lgb, pc, pc, pc, q, k, v, g, su, sv, sf, x_lat, x_ctx, mod, conv_w, sgu_w_bf, sgu_bias, w_out_bf,
      ln1, w_route_t, b_route_t, lg_rows, lg_rows, lg_cols, *to_cast)


def _rank_kernel(route_ref, class_ref, rank_ref, counts_ref, running):
    @pl.when(pl.program_id(0) == 0)
    def _():
        running[...] = jnp.zeros_like(running)

    sub = lax.broadcasted_iota(jnp.int32, (ROUTE_W, TILE), 0)
    earlier = (lax.broadcasted_iota(jnp.int32, (TILE, TILE), 0)
               < lax.broadcasted_iota(jnp.int32, (TILE, TILE), 1)).astype(BF16)
    for t in range(route_ref.shape[0]):
        cls = route_ref[t, 0:1, :].astype(jnp.int32)
        class_ref[t] = cls
        onehot = sub == cls
        before = jnp.dot(onehot.astype(BF16), earlier, preferred_element_type=F32) + running[...]
        rank_ref[t] = jnp.sum(jnp.where(onehot, before, 0.0), axis=0, keepdims=True).astype(jnp.int32)
        running[...] += jnp.sum(onehot.astype(F32), axis=1, keepdims=True)
    counts_ref[...] = jnp.broadcast_to(running[...], counts_ref.shape)


def _rank_call(route):
    n_tiles = route.shape[0]
    per_step = math.gcd(n_tiles, RANK_TILES)
    per_tile = pl.BlockSpec((per_step, 1, TILE), lambda i: (i, 0, 0))
    return pl.pallas_call(
        _rank_kernel,
        grid=(n_tiles // per_step,),
        in_specs=[pl.BlockSpec((per_step, 8, TILE), lambda i: (i, 0, 0))],
        out_specs=[per_tile, per_tile, pl.BlockSpec((ROUTE_W, ROUTE_W), lambda i: (0, 0))],
        out_shape=[jax.ShapeDtypeStruct((n_tiles, 1, TILE), jnp.int32), jax.ShapeDtypeStruct((n_tiles, 1, TILE), jnp.int32),
                   jax.ShapeDtypeStruct((ROUTE_W, ROUTE_W), F32)],
        scratch_shapes=[pltpu.VMEM((ROUTE_W, 1), F32)],
        compiler_params=_params(1),
    )(route)


def _dispatch_kernel(n_cast, dest_ref, pad_end_ref, pad_len_ref, n_used_ref, hm_ref, *refs):
    cast_in, xs_ref, cast_out = refs[:n_cast], refs[n_cast], refs[n_cast + 1:-3]
    zeros, sem, pad_sem = refs[-3:]
    for src, dst in zip(cast_in, cast_out):
        dst[...] = src[...].astype(BF16)
    step = pl.program_id(0)
    base = step * DISPATCH_TILE
    half = BLOCK_M // 2

    def for_each_pad_copy(fn):
        def per_class(c, carry):
            off = pad_end_ref[c]
            n = pad_len_ref[c]
            for shift in range(BLOCK_M.bit_length() - 2, -1, -1):
                bit = 1 << shift
                off = off - (n & bit)

                @pl.when((n & bit) != 0)
                def _():
                    if bit >= 8:
                        fn(pltpu.make_async_copy(zeros.at[pl.ds(0, bit)], xs_ref.at[pl.ds(pl.multiple_of(off, 8), bit)],
                                                 pad_sem))
                    else:
                        for j in range(bit):
                            fn(pltpu.make_async_copy(zeros.at[pl.ds(0, 1)], xs_ref.at[pl.ds(off + j, 1)], pad_sem))

            return carry

        lax.fori_loop(0, N_CLASSES, per_class, 0)

        def per_half_block(j, carry):
            fn(pltpu.make_async_copy(zeros, xs_ref.at[pl.ds(pl.multiple_of(j * half, 8), half)], pad_sem))
            return carry

        lax.fori_loop(n_used_ref[0] * 2, xs_ref.shape[0] // half, per_half_block, 0)

    @pl.when(step == 0)
    def _():
        zeros[...] = jnp.zeros_like(zeros)
        for_each_pad_copy(lambda cp: cp.start())

    for r in range(DISPATCH_TILE):
        pltpu.make_async_copy(hm_ref.at[pl.ds(r, 1)], xs_ref.at[pl.ds(dest_ref[base + r], 1)],
                              sem).start(priority=r % 2)
    for r in range(DISPATCH_TILE):
        pltpu.make_async_copy(hm_ref.at[pl.ds(0, 1)], xs_ref.at[pl.ds(0, 1)], sem).wait()

    @pl.when(step == pl.num_programs(0) - 1)
    def _():
        for_each_pad_copy(lambda cp: cp.wait())


def _dispatch_call(layer, dest, pad_end, pad_len, n_used, hm_flat, n_slots, to_cast=()):
    n_tok = hm_flat.shape[0]
    assert n_tok % DISPATCH_TILE == 0
    cast_in, cast_out, cast_shapes = _cast_specs(to_cast, layer, lambda i, *_: i, n_tok // DISPATCH_TILE)
    grid_spec = pltpu.PrefetchScalarGridSpec(
        num_scalar_prefetch=4,
        grid=(n_tok // DISPATCH_TILE,),
        in_specs=[pl.BlockSpec((DISPATCH_TILE, PACK_W), lambda i, *_: (i, 0))] + cast_in,
        out_specs=[pl.BlockSpec(memory_space=pl.ANY)] + cast_out,
        scratch_shapes=[pltpu.VMEM((BLOCK_M // 2, PACK_W), jnp.uint32), pltpu.SemaphoreType.DMA,
                        pltpu.SemaphoreType.DMA],
    )
    return pl.pallas_call(
        functools.partial(_dispatch_kernel, len(to_cast)),
        grid_spec=grid_spec,
        out_shape=[jax.ShapeDtypeStruct((n_slots, PACK_W), jnp.uint32)] + cast_shapes,
        compiler_params=_params(1),
    )(dest, pad_end, pad_len, n_used, hm_flat, *to_cast)


def _expert_kernel(first_blk_ref, count_ref, cls_lo_ref, cls_hi_ref, xs_ref, wr_ref, br_ref, wg_lo, wu_lo, wd_lo, wg_hi,
                   wu_hi, wd_hi, ys_ref, xbuf, ybuf, xsem, ysem):
    c = pl.program_id(0)
    half = BLOCK_M // 2
    n_used = first_blk_ref[N_CLASSES]
    n_blocks = ys_ref.shape[0] // BLOCK_M
    rows_of = lambda g: pl.ds(pl.multiple_of(g * BLOCK_M, BLOCK_M), BLOCK_M)
    x_copy = lambda g, slot: pltpu.make_async_copy(xs_ref.at[rows_of(g)], xbuf.at[slot], xsem.at[slot])
    y_copy = lambda g, slot: pltpu.make_async_copy(ybuf.at[slot], ys_ref.at[rows_of(g)], ysem.at[slot])

    @pl.when(c == 0)
    def _():
        x_copy(0, 0).start()

    def block(g, carry):
        slot = g % 2
        x_copy(g, slot).wait()

        @pl.when(g + 1 < n_used)
        def _():
            x_copy(g + 1, 1 - slot).start()

        @pl.when(g >= 2)
        def _():
            y_copy(g - 2, slot).wait()

        def experts_on(rows):
            xb = _unpack_bf16_pairs(xbuf[slot, 0:rows, :])
            y = two_experts(xb)
            ybuf[slot, 0:rows, :] = y
            if rows < BLOCK_M:
                ybuf[slot, rows:, :] = jnp.zeros((BLOCK_M - rows, D_MODEL), F32)

        live = count_ref[c] - (g - first_blk_ref[c]) * BLOCK_M
        pl.when(live > half)(lambda: experts_on(BLOCK_M))
        pl.when(live <= half)(lambda: experts_on(half))
        y_copy(g, slot).start()
        return carry

    def two_experts(xb):
        logits = jnp.dot(xb, wr_ref[0], preferred_element_type=F32) + br_ref[0]
        lane = lax.broadcasted_iota(jnp.int32, logits.shape, 1)
        gl = jnp.where(lane < N_GROUPS, logits, -jnp.inf)
        g_prob = 1.0 / jnp.sum(jnp.exp(gl - jnp.max(gl, axis=-1, keepdims=True)), axis=-1, keepdims=True)
        l_lo = jnp.sum(jnp.where(lane == N_GROUPS + cls_lo_ref[c], logits, 0.0), axis=-1, keepdims=True)
        l_hi = jnp.sum(jnp.where(lane == N_GROUPS + cls_hi_ref[c], logits, 0.0), axis=-1, keepdims=True)
        m = jnp.maximum(l_lo, l_hi)
        p_lo = jnp.exp(l_lo - m)
        p_hi = jnp.exp(l_hi - m)

        h_lo = jnp.dot(xb, wg_lo[0], preferred_element_type=F32)
        u_lo = jnp.dot(xb, wu_lo[0], preferred_element_type=F32)
        h_hi = jnp.dot(xb, wg_hi[0], preferred_element_type=F32)
        u_hi = jnp.dot(xb, wu_hi[0], preferred_element_type=F32)
        a_lo = (_silu(h_lo) * u_lo).astype(BF16)
        a_hi = (_silu(h_hi) * u_hi).astype(BF16)
        y_lo = jnp.dot(a_lo, wd_lo[0], preferred_element_type=F32)
        y_hi = jnp.dot(a_hi, wd_hi[0], preferred_element_type=F32)
        return y_lo * (g_prob * (p_lo / (p_lo + p_hi))) + y_hi * (g_prob * (p_hi / (p_lo + p_hi)))

    lax.fori_loop(first_blk_ref[c], first_blk_ref[c + 1], block, 0)

    @pl.when(c == pl.num_programs(0) - 1)
    def _():
        @pl.when(n_used >= 2)
        def _():
            y_copy(n_used - 2, n_used % 2).wait()

        y_copy(n_used - 1, (n_used - 1) % 2).wait()
        ybuf[0] = jnp.zeros((BLOCK_M, D_MODEL), F32)

        def fill(g, carry):
            y_copy(g, 0).start()
            return carry

        def fill_done(g, carry):
            y_copy(g, 0).wait()
            return carry

        lax.fori_loop(n_used, n_blocks, fill, 0)
        lax.fori_loop(n_used, n_blocks, fill_done, 0)


def _expert_call(layer, first_blk, counts, xs, w_route_bf, b_route, w_gate_bf, w_up_bf, w_down_bf):
    up_spec = lambda table: pl.BlockSpec((1, D_MODEL, EXPERT_HIDDEN), lambda c, fb, n, lo, hi: ((lo, hi)[table][c], 0, 0))
    down_spec = lambda table: pl.BlockSpec((1, EXPERT_HIDDEN, D_MODEL), lambda c, fb, n, lo, hi: ((lo, hi)[table][c], 0, 0))
    grid_spec = pltpu.PrefetchScalarGridSpec(
        num_scalar_prefetch=4,
        grid=(N_CLASSES,),
        in_specs=[pl.BlockSpec(memory_space=pl.ANY),
                  pl.BlockSpec((1, D_MODEL, ROUTE_W), lambda c, *_: (layer, 0, 0)),
                  pl.BlockSpec((1, 1, ROUTE_W), lambda c, *_: (layer, 0, 0)),
                  up_spec(0), up_spec(0), down_spec(0), up_spec(1), up_spec(1), down_spec(1)],
        out_specs=pl.BlockSpec(memory_space=pl.ANY),
        scratch_shapes=[pltpu.VMEM((2, BLOCK_M, PACK_W), jnp.uint32), pltpu.VMEM((2, BLOCK_M, D_MODEL), F32),
                        pltpu.SemaphoreType.DMA((2,)), pltpu.SemaphoreType.DMA((2,))],
    )
    return pl.pallas_call(
        _expert_kernel,
        grid_spec=grid_spec,
        out_shape=jax.ShapeDtypeStruct((xs.shape[0], D_MODEL), F32),
        compiler_params=_params(1),
    )(first_blk, counts, jnp.asarray(CLASS_LO), jnp.asarray(CLASS_HI), xs, w_route_bf, b_route, w_gate_bf, w_up_bf,
      w_down_bf, w_gate_bf, w_up_bf, w_down_bf)


def _combine_kernel(tiles_per_batch, dest_ref, ys_ref, x1_ref, mod_ref, ln_ref, out_ref, buf, sem):
    nb = BATCH_PER_STEP
    group, i = pl.program_id(0), pl.program_id(1)
    step = group * tiles_per_batch + i
    slot = step % 2
    has_next = step + 1 < pl.num_programs(0) * tiles_per_batch
    wrap = i + 1 == tiles_per_batch

    def gather(grp, tile, to_slot):
        for j in range(nb):
            base = ((grp * nb + j) * tiles_per_batch + tile) * TILE
            for r in range(TILE):
                pltpu.make_async_copy(ys_ref.at[pl.ds(dest_ref[base + r], 1)], buf.at[to_slot, j, pl.ds(r, 1)],
                                      sem.at[to_slot]).start(priority=r % 2)

    @pl.when(step == 0)
    def _():
        gather(group, i, slot)

    @pl.when(has_next)
    def _():
        gather(jnp.where(wrap, group + 1, group), jnp.where(wrap, 0, i + 1), 1 - slot)

    for r in range(nb * TILE):
        pltpu.make_async_copy(ys_ref.at[pl.ds(0, 1)], buf.at[slot, 0, pl.ds(0, 1)], sem.at[slot]).wait()

    ln = ln_ref[0]
    for j in range(nb):
        out_ref[j] = _standardize(ALPHA * x1_ref[j] + mod_ref[0, j, 0, 5:6] * buf[slot, j]) * ln[0:1] + ln[1:2]


def _combine_call(layer, n_lat_tiles, dest, ys, x1, mod, ln2):
    bsz, length, _ = x1.shape
    nb = BATCH_PER_STEP
    tiles_per_batch = length // TILE
    tok = lambda w: pl.BlockSpec((nb, TILE, w), lambda b, i, *_: (b, i, 0))
    grid_spec = pltpu.PrefetchScalarGridSpec(
        num_scalar_prefetch=1,
        grid=(bsz // nb, tiles_per_batch),
        in_specs=[
            pl.BlockSpec(memory_space=pl.ANY),
            tok(D_MODEL),
            pl.BlockSpec((1, nb, 1, MOD_ROWS, D_MODEL),
                         lambda b, i, *_: (layer, b, jnp.minimum(i // n_lat_tiles, 1), 0, 0)),
            pl.BlockSpec((1, 8, D_MODEL), lambda b, i, *_: (layer, 0, 0)),
        ],
        out_specs=tok(D_MODEL),
        scratch_shapes=[pltpu.VMEM((2, nb, TILE, D_MODEL), F32), pltpu.SemaphoreType.DMA((2,))],
    )
    return pl.pallas_call(
        functools.partial(_combine_kernel, tiles_per_batch),
        grid_spec=grid_spec,
        out_shape=jax.ShapeDtypeStruct(x1.shape, F32),
        compiler_params=_params(2),
    )(dest, ys, x1, mod, ln2)


def _dispatch_plan(class_of_token, rank, counts):
    n_tok = class_of_token.shape[0]
    counts = counts.astype(jnp.int32)
    pcounts = (counts + BLOCK_M - 1) // BLOCK_M * BLOCK_M
    pends = jnp.cumsum(pcounts)
    pstarts = pends - pcounts
    classes = jnp.arange(ROUTE_W, dtype=jnp.int32)
    dest = jnp.sum(jnp.where(class_of_token[:, None] == classes[None, :], pstarts[None, :], 0), axis=1) + rank
    n_blocks = n_tok // BLOCK_M + N_CLASSES
    n_used = pends[-1:] // BLOCK_M
    first_blk = pstarts[:N_CLASSES + 1] // BLOCK_M
    return dest.astype(jnp.int32), pends, pcounts - counts, first_blk, counts, n_used.astype(jnp.int32), n_blocks * BLOCK_M


def _pad_rows(a, rows):
    return jnp.pad(a, [(0, 0)] * (a.ndim - 2) + [(0, rows - a.shape[-2]), (0, 0)])


def kernel(x, c, ctx, c_ctx, w_ada, b_ada, w_in, conv_w, ret_decay_fwd, ret_decay_bwd, sgu_w, sgu_b, w_out, ln1_g, ln1_b,
           router_group_w, router_group_b, router_expert_w, router_expert_b, moe_w_gate, moe_w_up, moe_w_down, ln2_g,
           ln2_b):
    bsz, seq, d = x.shape
    ctx_len = ctx.shape[1]
    assert d == D_MODEL and ctx_len == TILE and seq % TILE == 0 and seq % GRID_W == 0
    n_lat_tiles = seq // TILE

    cond = _pad_rows(jnp.concatenate([c, c_ctx[None, :]], axis=0), 16)
    ada = _ada_call(cond, w_ada, b_ada)
    mod_lat = ada[:, :bsz].reshape(DEPTH, bsz, N_MOD, d)
    mod_ctx = jnp.broadcast_to(ada[:, bsz].reshape(DEPTH, 1, N_MOD, d), mod_lat.shape)
    mod = _pad_rows(jnp.stack([mod_lat, mod_ctx], axis=2), MOD_ROWS)

    w_in_bf = w_in.astype(BF16)
    w_out_bf = w_out.astype(BF16)
    sgu_w_bf = sgu_w.astype(BF16)
    sgu_bias = jnp.repeat(jnp.swapaxes(sgu_b, 1, 2), SGU_W // SGU_GROUPS, axis=2)
    conv_w8 = _pad_rows(conv_w, 8)
    ln1 = _pad_rows(jnp.stack([ln1_g, ln1_b], axis=1), 8)
    ln2 = _pad_rows(jnp.stack([ln2_g, ln2_b], axis=1), 8)
    w_route = jnp.concatenate([router_group_w, jnp.swapaxes(router_expert_w, 1, 2).reshape(DEPTH, d, N_EXPERTS)], axis=2)
    w_route_bf = jnp.pad(w_route, ((0, 0), (0, 0), (0, ROUTE_W - w_route.shape[2]))).astype(BF16)
    b_route = jnp.concatenate([router_group_b, router_expert_b.reshape(DEPTH, N_EXPERTS)], axis=1)
    b_route = jnp.pad(b_route, ((0, 0), (0, ROUTE_W - b_route.shape[1])))[:, None, :].astype(F32)
    w_route_t = jnp.concatenate([_pad_rows(jnp.swapaxes(w_route[:, :, :N_GROUPS], 1, 2), EXPERT_ROW0),
                                 jnp.swapaxes(w_route[:, :, N_GROUPS:], 1, 2)], axis=1)
    w_route_t = _pad_rows(w_route_t, ROUTE_ROWS).astype(BF16)
    b_route_t = jnp.concatenate([_pad_rows(b_route[:, 0, :N_GROUPS, None], EXPERT_ROW0),
                                 b_route[:, 0, N_GROUPS:N_GROUPS + N_EXPERTS, None]], axis=1)
    b_route_t = _pad_rows(b_route_t, ROUTE_ROWS)
    lg = jnp.stack([jax.nn.log_sigmoid(ret_decay_fwd.astype(F32)), jax.nn.log_sigmoid(ret_decay_bwd.astype(F32))],
                   axis=1)
    lg_lanes = jnp.repeat(lg, RET_DK, axis=2).reshape(DEPTH * 2, RET_W)
    lg_rows = lg_lanes[:, None, :]
    lg_cols = lg_lanes[:, :, None]

    x_lat, x_ctx, ctx_tile = x, ctx, 0
    for layer in range(DEPTH):
        last = layer == DEPTH - 1
        pc, q, k, v, g, su, sv, sf, w_gate_bf = _inproj_call(layer, x_lat, x_ctx, ctx_tile, mod, w_in_bf, lg_rows, lg_cols,
                                                             n_lat_tiles, to_cast=(moe_w_gate.reshape(-1, EXPERT_HIDDEN),))
        n_proc = n_lat_tiles if last else n_lat_tiles + 1
        x1, hm, route, w_up_bf = _mix_call(layer, n_lat_tiles, n_proc, lg[layer, 0], lg[layer, 1], pc, q, k, v, g, su, sv,
                                           sf, x_lat, x_ctx, ctx_tile, mod, conv_w8, sgu_w_bf, sgu_bias, w_out_bf, ln1,
                                           w_route_t, b_route_t, lg_rows, lg_cols,
                                           to_cast=(moe_w_up.reshape(-1, EXPERT_HIDDEN),))
        class_of_token, rank, counts = _rank_call(route.reshape(-1, 8, TILE))
        dest, pad_end, pad_len, first_blk, counts, n_used, n_slots = _dispatch_plan(class_of_token.reshape(-1),
                                                                                    rank.reshape(-1), counts[:, 0])
        xs, w_down_bf = _dispatch_call(layer, dest, pad_end, pad_len, n_used, hm.reshape(-1, PACK_W), n_slots,
                                       to_cast=(moe_w_down.reshape(-1, D_MODEL),))
        ys = _expert_call(layer, first_blk, counts, xs, w_route_bf, b_route,
                          w_gate_bf.reshape(N_EXPERTS, D_MODEL, EXPERT_HIDDEN), w_up_bf.reshape(N_EXPERTS, D_MODEL, EXPERT_HIDDEN),
                          w_down_bf.reshape(N_EXPERTS, EXPERT_HIDDEN, D_MODEL))
        xa = _combine_call(layer, n_lat_tiles, dest, ys, x1, mod, ln2)
        x_lat, x_ctx, ctx_tile = xa, xa, n_lat_tiles
    return xa
```

```python
import functools
import math

import jax
import jax.numpy as jnp
import numpy as np
from jax import lax
from jax.experimental import pallas as pl
from jax.experimental.pallas import tpu as pltpu

F32 = jnp.float32
BF16 = jnp.bfloat16

D_MODEL = 1024
DEPTH = 2
GRID_W = 64
CONV_W = 256
RET_W = 512
RET_HEADS = 8
RET_DK = 64
PAIR_W = 2 * RET_DK
CHUNK = 128
SGU_W = 256
SGU_GROUPS = 4
IN_COLS = 3 * CONV_W + 4 * RET_W + 2 * SGU_W
N_GROUPS = 4
EXPERTS_PER_GROUP = 8
N_EXPERTS = N_GROUPS * EXPERTS_PER_GROUP
EXPERT_HIDDEN = 512
N_MOD = 6
MOD_ROWS = 8
LN_EPS = 1e-5
ALPHA = (2 * DEPTH) ** 0.25

TILE = 256
BATCH_PER_STEP = 2
DISPATCH_TILE = 8 * TILE
EXPERT_ROWS = 64
HALO = GRID_W
BLOCK_M = 256
ROUTE_W = 128
CAST_STEPS = 128
PACK_W = D_MODEL // 2
ROUTE_ROWS = 48
EXPERT_ROW0 = 8
RANK_TILES = 8
PAIRS_PER_GROUP = EXPERTS_PER_GROUP * (EXPERTS_PER_GROUP - 1) // 2
N_CLASSES = N_GROUPS * PAIRS_PER_GROUP
CLASS_LO = np.array([g * EXPERTS_PER_GROUP + lo for g in range(N_GROUPS) for lo in range(EXPERTS_PER_GROUP)
                     for hi in range(lo + 1, EXPERTS_PER_GROUP)] + [N_EXPERTS - 2] * (ROUTE_W - N_CLASSES), np.int32)
CLASS_HI = np.array([g * EXPERTS_PER_GROUP + hi for g in range(N_GROUPS) for lo in range(EXPERTS_PER_GROUP)
                     for hi in range(lo + 1, EXPERTS_PER_GROUP)] + [N_EXPERTS - 1] * (ROUTE_W - N_CLASSES), np.int32)
VMEM_LIMIT = 56 * 1024 * 1024


def _params(n_axes):
    return pltpu.CompilerParams(dimension_semantics=("arbitrary",) * n_axes, vmem_limit_bytes=VMEM_LIMIT)


def _standardize(v):
    mu = jnp.mean(v, axis=-1, keepdims=True)
    var = jnp.mean(jnp.square(v - mu), axis=-1, keepdims=True)
    return (v - mu) * lax.rsqrt(var + LN_EPS)


def _silu(v):
    return v * jax.nn.sigmoid(v)


def _pack_bf16_pairs(v):
    half = v.shape[1] // 2
    bits = lambda t: pltpu.bitcast(t.astype(BF16).astype(F32), jnp.uint32)
    return bits(v[:, :half]) | (bits(v[:, half:]) >> 16)


def _unpack_bf16_pairs(w):
    hi = pltpu.bitcast(w & jnp.uint32(0xFFFF0000), F32)
    lo = pltpu.bitcast(w << 16, F32)
    return jnp.concatenate([hi, lo], axis=1).astype(BF16)


def _ada_kernel(c_ref, w_ref, b_ref, o_ref):
    a = _silu(c_ref[...]).astype(BF16)
    o_ref[0] = jnp.dot(a, w_ref[0].astype(BF16), preferred_element_type=F32) + b_ref[0]


def _ada_call(cond, w_ada, b_ada):
    rows = cond.shape[0]
    cols = w_ada.shape[-1]
    tn = 1536
    return pl.pallas_call(
        _ada_kernel,
        grid=(DEPTH, cols // tn),
        in_specs=[
            pl.BlockSpec((rows, D_MODEL), lambda l, j: (0, 0)),
            pl.BlockSpec((1, D_MODEL, tn), lambda l, j: (l, 0, j)),
            pl.BlockSpec((1, 1, tn), lambda l, j: (l, 0, j)),
        ],
        out_specs=pl.BlockSpec((1, rows, tn), lambda l, j: (l, 0, j)),
        out_shape=jax.ShapeDtypeStruct((DEPTH, rows, cols), F32),
        compiler_params=_params(2),
    )(cond, w_ada, b_ada.reshape(DEPTH, 1, cols))


def _own_head_mask():
    row_head = lax.broadcasted_iota(jnp.int32, (RET_W, PAIR_W), 0) % PAIR_W // RET_DK
    lane_head = lax.broadcasted_iota(jnp.int32, (RET_W, PAIR_W), 1) // RET_DK
    return row_head == lane_head


def _pair_kv(kc, vc):
    contract_rows = (((0,), (0,)), ((), ()))
    return jnp.concatenate([
        lax.dot_general(kc[:, p * PAIR_W:(p + 1) * PAIR_W], vc[:, p * PAIR_W:(p + 1) * PAIR_W], contract_rows,
                        preferred_element_type=F32) for p in range(RET_W // PAIR_W)], axis=0)


def _inproj_kernel(n_lat_tiles, n_cast, x_ref, ctx_ref, mod_ref, w_ref, lgf_ref, lgfc_ref, *refs):
    cast_in, (pc_ref, q_ref, k_ref, v_ref, g_ref, su_ref, sv_ref, sf_ref), cast_out = (
        refs[:n_cast], refs[n_cast:n_cast + 8], refs[n_cast + 8:-1])
    state = refs[-1]
    for src, dst in zip(cast_in, cast_out):
        dst[...] = src[...].astype(BF16)
    step = pl.program_id(1)
    is_ctx = step == 0
    h = jnp.concatenate([
        (jnp.where(is_ctx, ctx_ref[j], x_ref[j]) * (1.0 + mod_ref[0, j, 0, 1:2]) + mod_ref[0, j, 0, 0:1]).astype(BF16)
        for j in range(BATCH_PER_STEP)], axis=0)

    def proj(lo, hi):
        return jnp.dot(h, w_ref[0, :, lo:hi], preferred_element_type=F32)

    def put(ref, val):
        for j in range(BATCH_PER_STEP):
            ref[j] = val[j * TILE:(j + 1) * TILE].astype(ref.dtype)

    o = 3 * CONV_W
    put(pc_ref, proj(0, o))
    put(q_ref, proj(o, o + RET_W))
    k = proj(o + RET_W, o + 2 * RET_W) * (RET_DK ** -0.5)
    put(k_ref, k)
    v = proj(o + 2 * RET_W, o + 3 * RET_W)
    put(v_ref, v)
    put(g_ref, proj(o + 3 * RET_W, o + 4 * RET_W))
    o += 4 * RET_W
    put(su_ref, proj(o, o + SGU_W))
    put(sv_ref, proj(o + SGU_W, o + 2 * SGU_W))

    @pl.when(step == 0)
    def _():
        state[...] = jnp.zeros_like(state)

    pos = lax.broadcasted_iota(jnp.int32, (CHUNK, 1), 0).astype(F32)
    k_decay = jnp.exp((CHUNK - 1.0 - pos) * lgf_ref[0])
    chunk_decay = jnp.exp(CHUNK * lgfc_ref[0])
    own_head = _own_head_mask()
    for j in range(BATCH_PER_STEP):
        kvs = [_pair_kv((k[rows] * k_decay).astype(BF16), v[rows].astype(BF16))
               for rows in (slice(j * TILE + c * CHUNK, j * TILE + (c + 1) * CHUNK) for c in range(TILE // CHUNK))]
        st = state[j]
        for c, kv in enumerate(kvs):
            sf_ref[j, c] = jnp.where(own_head, st, 0.0).astype(BF16)
            st = chunk_decay * st + kv
        state[j] = st


def _cast_specs(arrays, part, step_of, n_steps):
    steps = min(CAST_STEPS, 1 << (n_steps.bit_length() - 1))
    block = lambda a: (a.shape[0] // (DEPTH * steps), a.shape[1])
    slab = lambda *idx: jnp.minimum(step_of(*idx), steps - 1)
    return ([pl.BlockSpec(block(a), lambda *idx: (part * steps + slab(*idx), 0)) for a in arrays],
            [pl.BlockSpec(block(a), lambda *idx: (slab(*idx), 0)) for a in arrays],
            [jax.ShapeDtypeStruct((a.shape[0] // DEPTH, a.shape[1]), BF16) for a in arrays])


def _inproj_call(layer, x_lat, x_ctx, ctx_tile, mod, w_in_bf, lg_rows, lg_cols, n_lat_tiles, to_cast=()):
    bsz = x_lat.shape[0]
    assert bsz % BATCH_PER_STEP == 0
    nb = BATCH_PER_STEP
    length = (n_lat_tiles + 1) * TILE
    widths = (3 * CONV_W, RET_W, RET_W, RET_W, RET_W, SGU_W, SGU_W)
    cpt = TILE // CHUNK
    n_tiles = n_lat_tiles + 1
    tile_of = lambda s: (s + n_lat_tiles) % n_tiles
    cast_in, cast_out, cast_shapes = _cast_specs(to_cast, layer, lambda b, s: b * n_tiles + s, bsz // nb * n_tiles)
    return pl.pallas_call(
        functools.partial(_inproj_kernel, n_lat_tiles, len(to_cast)),
        grid=(bsz // nb, n_tiles),
        in_specs=[
            pl.BlockSpec((nb, TILE, D_MODEL), lambda b, s: (b, jnp.minimum(tile_of(s), n_lat_tiles - 1), 0)),
            pl.BlockSpec((nb, TILE, D_MODEL), lambda b, s: (b, ctx_tile, 0)),
            pl.BlockSpec((1, nb, 1, MOD_ROWS, D_MODEL), lambda b, s: (layer, b, tile_of(s) // n_lat_tiles, 0, 0)),
            pl.BlockSpec((1, D_MODEL, IN_COLS), lambda b, s: (layer, 0, 0)),
            pl.BlockSpec((1, 1, RET_W), lambda b, s: (2 * layer, 0, 0)),
            pl.BlockSpec((1, RET_W, 1), lambda b, s: (2 * layer, 0, 0)),
        ] + cast_in,
        out_specs=[pl.BlockSpec((nb, TILE, w), lambda b, s: (b, tile_of(s), 0)) for w in widths]
        + [pl.BlockSpec((nb, cpt, RET_W, PAIR_W), lambda b, s: (b, tile_of(s), 0, 0))] + cast_out,
        out_shape=[jax.ShapeDtypeStruct((bsz, length, w), BF16 if 1 <= j <= 3 else F32) for j, w in enumerate(widths)]
        + [jax.ShapeDtypeStruct((bsz, length // CHUNK, RET_W, PAIR_W), BF16)] + cast_shapes,
        scratch_shapes=[pltpu.VMEM((nb, RET_W, PAIR_W), F32)],
        compiler_params=_params(2),
    )(x_lat, x_ctx, mod, w_in_bf, lg_rows, lg_cols, *to_cast)


def _route_class(lt):
    assert EXPERTS_PER_GROUP == 8 and N_GROUPS <= 8
    sub = lax.broadcasted_iota(jnp.int32, (8, lt.shape[1]), 0)
    neg = jnp.float32(-jnp.inf)

    def top(vals):
        m = jnp.max(vals, axis=0, keepdims=True)
        return jnp.min(jnp.where(vals == m, sub, 8), axis=0, keepdims=True)

    gidx = top(jnp.where(sub < N_GROUPS, lt[0:8], neg))
    pair = jnp.zeros_like(gidx)
    for g in range(N_GROUPS):
        e = lt[EXPERT_ROW0 + 8 * g:EXPERT_ROW0 + 8 * (g + 1)]
        i1 = top(e)
        i2 = top(jnp.where(sub == i1, neg, e))
        e_lo = jnp.minimum(i1, i2)
        e_hi = jnp.maximum(i1, i2)
        pair_g = e_lo * (EXPERTS_PER_GROUP - 1) - ((e_lo * (e_lo - 1)) >> 1) + (e_hi - e_lo - 1)
        pair = jnp.where(gidx == g, pair_g, pair)
    return gidx * PAIRS_PER_GROUP + pair


def _mix_kernel(n_lat_tiles, n_cast, lgf_ref, lgb_ref, pc_ref, hp_ref, hn_ref, q_ref, k_ref, v_ref, g_ref, su_ref, sv_ref,
                sf_ref, x_ref, ctx_ref, mod_ref, convw_ref, sguw_ref, sgub_ref, wout_ref, ln_ref, wr_ref, br_ref,
                lgfr_ref, lgbr_ref, lgbc_ref, *refs):
    cast_in, (x1_ref, hm_ref, route_ref), cast_out = refs[:n_cast], refs[n_cast:n_cast + 3], refs[n_cast + 3:-4]
    ycat, dec, qdec, state = refs[-4:]
    for src, dst in zip(cast_in, cast_out):
        dst[...] = src[...].astype(BF16)
    step = pl.program_id(1)
    i = n_lat_tiles - step
    is_ctx = step == 0
    row = lax.broadcasted_iota(jnp.int32, (TILE, 1), 0)
    nb = BATCH_PER_STEP

    @pl.when(step == 0)
    def _():
        state[...] = jnp.zeros_like(state)

    pos_b = lax.broadcasted_iota(jnp.int32, (CHUNK, 1), 0).astype(F32)
    k_decay_b = jnp.exp(pos_b * lgbr_ref[0])
    chunk_decay_b = jnp.exp(CHUNK * lgbc_ref[0])
    own_head = _own_head_mask()
    sb = {}
    for j in range(nb):
        kvs = {c: _pair_kv((k_ref[j, c * CHUNK:(c + 1) * CHUNK, :].astype(F32) * k_decay_b).astype(BF16),
                           v_ref[j, c * CHUNK:(c + 1) * CHUNK, :]) for c in range(TILE // CHUNK)}
        st = state[j]
        for c in reversed(range(TILE // CHUNK)):
            sb[j, c] = jnp.where(own_head, st, 0.0).astype(BF16)
            st = chunk_decay_b * st + kvs[c]
        state[j] = st

    @pl.when((pl.program_id(0) == 0) & (step == 0))
    def _():
        pos = lax.broadcasted_iota(jnp.int32, (CHUNK, 1), 0).astype(F32)
        rel = pos - lax.broadcasted_iota(jnp.int32, (1, CHUNK), 1).astype(F32)
        for h in range(RET_HEADS):
            dec[h // 2, (h % 2) * CHUNK:(h % 2 + 1) * CHUNK, :] = jnp.where(
                rel > 0, jnp.exp(lgf_ref[h] * jnp.maximum(rel, 0.0)),
                jnp.where(rel < 0, jnp.exp(lgb_ref[h] * jnp.maximum(-rel, 0.0)), 2.0))
        qdec[0] = jnp.exp(lgfr_ref[0] * (pos + 1.0))
        qdec[1] = jnp.exp(lgbr_ref[0] * (CHUNK - pos))

    low_head = lax.broadcasted_iota(jnp.int32, (1, PAIR_W), 1) < RET_DK

    def half_norm(o):
        def half_mean(t):
            lo = jnp.sum(jnp.where(low_head, t, 0.0), axis=-1, keepdims=True)
            hi = jnp.sum(jnp.where(low_head, 0.0, t), axis=-1, keepdims=True)
            return jnp.where(low_head, lo, hi) * (1.0 / RET_DK)

        centred = o - half_mean(o)
        return centred * lax.rsqrt(half_mean(jnp.square(centred)) + LN_EPS)

    tiles = [(j, slice(c * CHUNK, (c + 1) * CHUNK), slice(p * PAIR_W, (p + 1) * PAIR_W), c, p)
             for j in range(nb) for c in range(TILE // CHUNK) for p in range(RET_W // PAIR_W)]
    out_rows = lambda j, rows: slice(j * TILE + rows.start, j * TILE + rows.stop)
    scores = []
    for j, rows, cols, c, p in tiles:
        qp = q_ref[j, rows, cols]
        zero = jnp.zeros_like(qp)
        q_split = jnp.concatenate([jnp.where(low_head, qp, zero), jnp.where(low_head, zero, qp)], axis=0)
        scores.append(lax.dot_general(q_split, k_ref[j, rows, cols],
                                      (((1,), (1,)), ((), ())), preferred_element_type=F32))

    line_mask = jnp.where(is_ctx, TILE - 1, GRID_W - 1)
    first = (row & line_mask) == 0
    last = (row & line_mask) == line_mask
    along_seq = lax.broadcasted_iota(jnp.int32, (1, CONV_W), 1) < jnp.where(is_ctx, CONV_W, CONV_W // 2)
    cw = convw_ref[0]
    for j in range(nb):
        pc = pc_ref[j]
        z = pc[:, CONV_W:2 * CONV_W] * pc[:, 2 * CONV_W:3 * CONV_W]
        z_prev = jnp.where(first, 0.0, pltpu.roll(z, 1, 0))
        z_next = jnp.where(last, 0.0, pltpu.roll(z, TILE - 1, 0))
        hp = hp_ref[j]
        hn = hn_ref[j]
        z_top = jnp.where(i == 0, 0.0, hp[:, CONV_W:2 * CONV_W] * hp[:, 2 * CONV_W:3 * CONV_W])
        z_bot = jnp.where(i == n_lat_tiles - 1, 0.0, hn[:, CONV_W:2 * CONV_W] * hn[:, 2 * CONV_W:3 * CONV_W])
        zcat = jnp.concatenate([z_top, z, z_bot], axis=0)
        z_up = zcat[0:TILE]
        z_down = zcat[2 * HALO:2 * HALO + TILE]
        conv = cw[0:1] * jnp.where(along_seq, z_prev, z_up) + cw[1:2] * z
        conv = conv + cw[2:3] * jnp.where(along_seq, z_next, z_down)
        ycat[j * TILE:(j + 1) * TILE, 0:CONV_W] = (pc[:, 0:CONV_W] * conv).astype(BF16)

    outs = []
    for (j, rows, cols, c, p), sc in zip(tiles, scores):
        qp = q_ref[j, rows, cols].astype(F32)
        vp = v_ref[j, rows, cols]
        zero = jnp.zeros_like(vp)
        sc = sc * dec[p]
        lhs = jnp.concatenate([sc[0:CHUNK].astype(BF16), sc[CHUNK:].astype(BF16),
                               (qp * qdec[0, :, cols]).astype(BF16), (qp * qdec[1, :, cols]).astype(BF16)], axis=1)
        rhs = jnp.concatenate([jnp.where(low_head, vp, zero), jnp.where(low_head, zero, vp),
                               sf_ref[j, c, cols, :], sb[j, c][cols, :]], axis=0)
        outs.append(jnp.dot(lhs, rhs, preferred_element_type=F32))

    group = lax.broadcasted_iota(jnp.int32, (1, SGU_W), 1) // (SGU_W // SGU_GROUPS)
    for j in range(nb):
        vn = _standardize(sv_ref[j]).astype(BF16)
        for c in range(TILE // CHUNK):
            rows = slice(c * CHUNK, (c + 1) * CHUNK)
            mixed = jnp.zeros((CHUNK, SGU_W), F32)
            for gi in range(SGU_GROUPS):
                m = jnp.dot(sguw_ref[0, gi], vn[rows], preferred_element_type=F32)
                mixed = jnp.where(group == gi, m, mixed)
            ycat[out_rows(j, rows), CONV_W + RET_W:] = (su_ref[j, rows, :] * (mixed + sgub_ref[0])).astype(BF16)

    for (j, rows, cols, c, p), o in zip(tiles, outs):
        ycat[out_rows(j, rows), CONV_W + p * PAIR_W:CONV_W + (p + 1) * PAIR_W] = (
            _silu(g_ref[j, rows, cols]) * half_norm(o)).astype(BF16)

    y = jnp.dot(ycat[...], wout_ref[0], preferred_element_type=F32)
    ln = ln_ref[0]
    hms = []
    for j in range(nb):
        mod = mod_ref[0, j, 0]
        x1 = _standardize(ALPHA * jnp.where(is_ctx, ctx_ref[j], x_ref[j]) + mod[2:3] * y[j * TILE:(j + 1) * TILE])
        x1 = x1 * ln[0:1] + ln[1:2]
        x1_ref[j] = x1
        hm = x1 * (1.0 + mod[4:5]) + mod[3:4]
        hm_ref[j] = _pack_bf16_pairs(hm)
        hms.append(hm.astype(BF16))
    lt = lax.dot_general(wr_ref[0], jnp.concatenate(hms, axis=0), (((1,), (1,)), ((), ())),
                         preferred_element_type=F32) + br_ref[0]
    cls = _route_class(lt).astype(F32)
    for j in range(nb):
        route_ref[j, 0] = jnp.concatenate([cls[:, j * TILE:(j + 1) * TILE], jnp.zeros((7, TILE), F32)], axis=0)


def _mix_call(layer, n_lat_tiles, n_proc, lgf, lgb, pc, q, k, v, g, su, sv, sf, x_lat, x_ctx, ctx_tile, mod, conv_w,
              sgu_w_bf, sgu_bias, w_out_bf, ln1, w_route_t, b_route_t, lg_rows, lg_cols, to_cast=()):
    bsz, length, _ = q.shape
    nb = BATCH_PER_STEP
    n_tiles = n_lat_tiles + 1
    tile_of = lambda s: n_lat_tiles - s
    cast_in, cast_out, cast_shapes = _cast_specs(to_cast, layer, lambda b, s, *_: b * n_tiles + s, bsz // nb * n_tiles)
    halos_per_tile = TILE // HALO
    n_halo = length // HALO
    cpt = TILE // CHUNK
    tok = lambda w: pl.BlockSpec((nb, TILE, w), lambda b, s, *_: (b, tile_of(s), 0))
    tok_out = lambda w: pl.BlockSpec((nb, TILE, w), lambda b, s, *_: (b, jnp.minimum(tile_of(s), n_proc - 1), 0))
    per_layer = lambda *shape: pl.BlockSpec((1,) + shape, lambda b, s, *_: (layer,) + (0,) * len(shape))
    grid_spec = pltpu.PrefetchScalarGridSpec(
        num_scalar_prefetch=2,
        grid=(bsz // nb, n_tiles),
        in_specs=[
            tok(3 * CONV_W),
            pl.BlockSpec((nb, HALO, 3 * CONV_W),
                         lambda b, s, *_: (b, jnp.maximum(tile_of(s) * halos_per_tile - 1, 0), 0)),
            pl.BlockSpec((nb, HALO, 3 * CONV_W),
                         lambda b, s, *_: (b, jnp.minimum((tile_of(s) + 1) * halos_per_tile, n_halo - 1), 0)),
            tok(RET_W), tok(RET_W), tok(RET_W), tok(RET_W), tok(SGU_W), tok(SGU_W),
            pl.BlockSpec((nb, cpt, RET_W, PAIR_W), lambda b, s, *_: (b, tile_of(s), 0, 0)),
            pl.BlockSpec((nb, TILE, D_MODEL), lambda b, s, *_: (b, jnp.minimum(tile_of(s), n_lat_tiles - 1), 0)),
            pl.BlockSpec((nb, TILE, D_MODEL), lambda b, s, *_: (b, ctx_tile, 0)),
            pl.BlockSpec((1, nb, 1, MOD_ROWS, D_MODEL),
                         lambda b, s, *_: (layer, b, tile_of(s) // n_lat_tiles, 0, 0)),
            per_layer(8, CONV_W),
            per_layer(SGU_GROUPS, CHUNK, CHUNK),
            per_layer(CHUNK, SGU_W),
            per_layer(D_MODEL, D_MODEL),
            per_layer(8, D_MODEL),
            per_layer(ROUTE_ROWS, D_MODEL),
            per_layer(ROUTE_ROWS, 1),
            pl.BlockSpec((1, 1, RET_W), lambda b, s, *_: (2 * layer, 0, 0)),
            pl.BlockSpec((1, 1, RET_W), lambda b, s, *_: (2 * layer + 1, 0, 0)),
            pl.BlockSpec((1, RET_W, 1), lambda b, s, *_: (2 * layer + 1, 0, 0)),
        ] + cast_in,
        out_specs=[tok_out(D_MODEL), tok_out(PACK_W),
                   pl.BlockSpec((nb, 1, 8, TILE), lambda b, s, *_: (b, jnp.minimum(tile_of(s), n_proc - 1), 0, 0))]
        + cast_out,
        scratch_shapes=[pltpu.VMEM((nb * TILE, D_MODEL), BF16),
                        pltpu.VMEM((RET_W // PAIR_W, 2 * CHUNK, CHUNK), F32),
                        pltpu.VMEM((2, CHUNK, RET_W), F32),
                        pltpu.VMEM((nb, RET_W, PAIR_W), F32)],
    )
    return pl.pallas_call(
        functools.partial(_mix_kernel, n_lat_tiles, len(to_cast)),
        grid_spec=grid_spec,
        out_shape=[jax.ShapeDtypeStruct((bsz, n_proc * TILE, D_MODEL), F32),
                   jax.ShapeDtypeStruct((bsz, n_proc * TILE, PACK_W), jnp.uint32),
                   jax.ShapeDtypeStruct((bsz, n_proc, 8, TILE), F32)] + cast_shapes,
        compiler_params=_params(2),
    )(lgf, lgb, pc, pc, pc, q, k, v, g, su, sv, sf, x_lat, x_ctx, mod, conv_w, sgu_w_bf, sgu_bias, w_out_bf,
      ln1, w_route_t, b_route_t, lg_rows, lg_rows, lg_cols, *to_cast)


def _rank_kernel(route_ref, class_ref, rank_ref, counts_ref, running):
    @pl.when(pl.program_id(0) == 0)
    def _():
        running[...] = jnp.zeros_like(running)

    sub = lax.broadcasted_iota(jnp.int32, (ROUTE_W, TILE), 0)
    earlier = (lax.broadcasted_iota(jnp.int32, (TILE, TILE), 0)
               < lax.broadcasted_iota(jnp.int32, (TILE, TILE), 1)).astype(BF16)
    for t in range(route_ref.shape[0]):
        cls = route_ref[t, 0:1, :].astype(jnp.int32)
        class_ref[t] = cls
        onehot = sub == cls
        before = jnp.dot(onehot.astype(BF16), earlier, preferred_element_type=F32) + running[...]
        rank_ref[t] = jnp.sum(jnp.where(onehot, before, 0.0), axis=0, keepdims=True).astype(jnp.int32)
        running[...] += jnp.sum(onehot.astype(F32), axis=1, keepdims=True)
    counts_ref[...] = jnp.broadcast_to(running[...], counts_ref.shape)


def _rank_call(route):
    n_tiles = route.shape[0]
    per_step = math.gcd(n_tiles, RANK_TILES)
    per_tile = pl.BlockSpec((per_step, 1, TILE), lambda i: (i, 0, 0))
    return pl.pallas_call(
        _rank_kernel,
        grid=(n_tiles // per_step,),
        in_specs=[pl.BlockSpec((per_step, 8, TILE), lambda i: (i, 0, 0))],
        out_specs=[per_tile, per_tile, pl.BlockSpec((ROUTE_W, ROUTE_W), lambda i: (0, 0))],
        out_shape=[jax.ShapeDtypeStruct((n_tiles, 1, TILE), jnp.int32), jax.ShapeDtypeStruct((n_tiles, 1, TILE), jnp.int32),
                   jax.ShapeDtypeStruct((ROUTE_W, ROUTE_W), F32)],
        scratch_shapes=[pltpu.VMEM((ROUTE_W, 1), F32)],
        compiler_params=_params(1),
    )(route)


def _dispatch_kernel(n_cast, dest_ref, pad_end_ref, pad_len_ref, n_used_ref, hm_ref, *refs):
    cast_in, xs_ref, cast_out = refs[:n_cast], refs[n_cast], refs[n_cast + 1:-3]
    zeros, sem, pad_sem = refs[-3:]
    for src, dst in zip(cast_in, cast_out):
        dst[...] = src[...].astype(BF16)
    step = pl.program_id(0)
    base = step * DISPATCH_TILE
    half = BLOCK_M // 2

    def for_each_pad_copy(fn):
        def per_class(c, carry):
            off = pad_end_ref[c]
            n = pad_len_ref[c]
            for shift in range(BLOCK_M.bit_length() - 2, -1, -1):
                bit = 1 << shift
                off = off - (n & bit)

                @pl.when((n & bit) != 0)
                def _():
                    if bit >= 8:
                        fn(pltpu.make_async_copy(zeros.at[pl.ds(0, bit)], xs_ref.at[pl.ds(pl.multiple_of(off, 8), bit)],
                                                 pad_sem))
                    else:
                        for j in range(bit):
                            fn(pltpu.make_async_copy(zeros.at[pl.ds(0, 1)], xs_ref.at[pl.ds(off + j, 1)], pad_sem))

            return carry

        lax.fori_loop(0, N_CLASSES, per_class, 0)

        def per_half_block(j, carry):
            fn(pltpu.make_async_copy(zeros, xs_ref.at[pl.ds(pl.multiple_of(j * half, 8), half)], pad_sem))
            return carry

        lax.fori_loop(n_used_ref[0] * 2, xs_ref.shape[0] // half, per_half_block, 0)

    @pl.when(step == 0)
    def _():
        zeros[...] = jnp.zeros_like(zeros)
        for_each_pad_copy(lambda cp: cp.start())

    for r in range(DISPATCH_TILE):
        pltpu.make_async_copy(hm_ref.at[pl.ds(r, 1)], xs_ref.at[pl.ds(dest_ref[base + r], 1)],
                              sem).start(priority=r % 2)
    for r in range(DISPATCH_TILE):
        pltpu.make_async_copy(hm_ref.at[pl.ds(0, 1)], xs_ref.at[pl.ds(0, 1)], sem).wait()

    @pl.when(step == pl.num_programs(0) - 1)
    def _():
        for_each_pad_copy(lambda cp: cp.wait())


def _dispatch_call(layer, dest, pad_end, pad_len, n_used, hm_flat, n_slots, to_cast=()):
    n_tok = hm_flat.shape[0]
    assert n_tok % DISPATCH_TILE == 0
    cast_in, cast_out, cast_shapes = _cast_specs(to_cast, layer, lambda i, *_: i, n_tok // DISPATCH_TILE)
    grid_spec = pltpu.PrefetchScalarGridSpec(
        num_scalar_prefetch=4,
        grid=(n_tok // DISPATCH_TILE,),
        in_specs=[pl.BlockSpec((DISPATCH_TILE, PACK_W), lambda i, *_: (i, 0))] + cast_in,
        out_specs=[pl.BlockSpec(memory_space=pl.ANY)] + cast_out,
        scratch_shapes=[pltpu.VMEM((BLOCK_M // 2, PACK_W), jnp.uint32), pltpu.SemaphoreType.DMA,
                        pltpu.SemaphoreType.DMA],
    )
    return pl.pallas_call(
        functools.partial(_dispatch_kernel, len(to_cast)),
        grid_spec=grid_spec,
        out_shape=[jax.ShapeDtypeStruct((n_slots, PACK_W), jnp.uint32)] + cast_shapes,
        compiler_params=_params(1),
    )(dest, pad_end, pad_len, n_used, hm_flat, *to_cast)


def _expert_kernel(first_blk_ref, count_ref, cls_lo_ref, cls_hi_ref, xs_ref, wr_ref, br_ref, wg_lo, wu_lo, wd_lo, wg_hi,
                   wu_hi, wd_hi, ys_ref, xbuf, ybuf, xsem, ysem):
    c = pl.program_id(0)
    n_used = first_blk_ref[N_CLASSES]
    n_blocks = ys_ref.shape[0] // BLOCK_M
    rows_of = lambda g: pl.ds(pl.multiple_of(g * BLOCK_M, BLOCK_M), BLOCK_M)
    x_copy = lambda g, slot: pltpu.make_async_copy(xs_ref.at[rows_of(g)], xbuf.at[slot], xsem.at[slot])
    y_copy = lambda g, slot: pltpu.make_async_copy(ybuf.at[slot], ys_ref.at[rows_of(g)], ysem.at[slot])

    @pl.when(c == 0)
    def _():
        x_copy(0, 0).start()

    def block(g, carry):
        slot = g % 2
        x_copy(g, slot).wait()

        @pl.when(g + 1 < n_used)
        def _():
            x_copy(g + 1, 1 - slot).start()

        @pl.when(g >= 2)
        def _():
            y_copy(g - 2, slot).wait()

        def experts_on(rows):
            xb = _unpack_bf16_pairs(xbuf[slot, 0:rows, :])
            y = two_experts(xb)
            ybuf[slot, 0:rows, :] = y
            if rows < BLOCK_M:
                ybuf[slot, rows:, :] = jnp.zeros((BLOCK_M - rows, D_MODEL), F32)

        live = count_ref[c] - (g - first_blk_ref[c]) * BLOCK_M
        for rows in range(EXPERT_ROWS, BLOCK_M + 1, EXPERT_ROWS):
            pl.when((live > rows - EXPERT_ROWS) & ((live <= rows) | (rows == BLOCK_M)))(
                functools.partial(experts_on, rows))
        y_copy(g, slot).start()
        return carry

    def two_experts(xb):
        logits = jnp.dot(xb, wr_ref[0], preferred_element_type=F32) + br_ref[0]
        lane = lax.broadcasted_iota(jnp.int32, logits.shape, 1)
        gl = jnp.where(lane < N_GROUPS, logits, -jnp.inf)
        g_prob = 1.0 / jnp.sum(jnp.exp(gl - jnp.max(gl, axis=-1, keepdims=True)), axis=-1, keepdims=True)
        l_lo = jnp.sum(jnp.where(lane == N_GROUPS + cls_lo_ref[c], logits, 0.0), axis=-1, keepdims=True)
        l_hi = jnp.sum(jnp.where(lane == N_GROUPS + cls_hi_ref[c], logits, 0.0), axis=-1, keepdims=True)
        m = jnp.maximum(l_lo, l_hi)
        p_lo = jnp.exp(l_lo - m)
        p_hi = jnp.exp(l_hi - m)

        h_lo = jnp.dot(xb, wg_lo[0], preferred_element_type=F32)
        u_lo = jnp.dot(xb, wu_lo[0], preferred_element_type=F32)
        h_hi = jnp.dot(xb, wg_hi[0], preferred_element_type=F32)
        u_hi = jnp.dot(xb, wu_hi[0], preferred_element_type=F32)
        a_lo = (_silu(h_lo) * u_lo).astype(BF16)
        a_hi = (_silu(h_hi) * u_hi).astype(BF16)
        y_lo = jnp.dot(a_lo, wd_lo[0], preferred_element_type=F32)
        y_hi = jnp.dot(a_hi, wd_hi[0], preferred_element_type=F32)
        return y_lo * (g_prob * (p_lo / (p_lo + p_hi))) + y_hi * (g_prob * (p_hi / (p_lo + p_hi)))

    lax.fori_loop(first_blk_ref[c], first_blk_ref[c + 1], block, 0)

    @pl.when(c == pl.num_programs(0) - 1)
    def _():
        @pl.when(n_used >= 2)
        def _():
            y_copy(n_used - 2, n_used % 2).wait()

        y_copy(n_used - 1, (n_used - 1) % 2).wait()
        ybuf[0] = jnp.zeros((BLOCK_M, D_MODEL), F32)

        def fill(g, carry):
            y_copy(g, 0).start()
            return carry

        def fill_done(g, carry):
            y_copy(g, 0).wait()
            return carry

        lax.fori_loop(n_used, n_blocks, fill, 0)
        lax.fori_loop(n_used, n_blocks, fill_done, 0)


def _expert_call(layer, first_blk, counts, xs, w_route_bf, b_route, w_gate_bf, w_up_bf, w_down_bf):
    up_spec = lambda table: pl.BlockSpec((1, D_MODEL, EXPERT_HIDDEN), lambda c, fb, n, lo, hi: ((lo, hi)[table][c], 0, 0))
    down_spec = lambda table: pl.BlockSpec((1, EXPERT_HIDDEN, D_MODEL), lambda c, fb, n, lo, hi: ((lo, hi)[table][c], 0, 0))
    grid_spec = pltpu.PrefetchScalarGridSpec(
        num_scalar_prefetch=4,
        grid=(N_CLASSES,),
        in_specs=[pl.BlockSpec(memory_space=pl.ANY),
                  pl.BlockSpec((1, D_MODEL, ROUTE_W), lambda c, *_: (layer, 0, 0)),
                  pl.BlockSpec((1, 1, ROUTE_W), lambda c, *_: (layer, 0, 0)),
                  up_spec(0), up_spec(0), down_spec(0), up_spec(1), up_spec(1), down_spec(1)],
        out_specs=pl.BlockSpec(memory_space=pl.ANY),
        scratch_shapes=[pltpu.VMEM((2, BLOCK_M, PACK_W), jnp.uint32), pltpu.VMEM((2, BLOCK_M, D_MODEL), F32),
                        pltpu.SemaphoreType.DMA((2,)), pltpu.SemaphoreType.DMA((2,))],
    )
    return pl.pallas_call(
        _expert_kernel,
        grid_spec=grid_spec,
        out_shape=jax.ShapeDtypeStruct((xs.shape[0], D_MODEL), F32),
        compiler_params=_params(1),
    )(first_blk, counts, jnp.asarray(CLASS_LO), jnp.asarray(CLASS_HI), xs, w_route_bf, b_route, w_gate_bf, w_up_bf,
      w_down_bf, w_gate_bf, w_up_bf, w_down_bf)


def _combine_kernel(tiles_per_batch, dest_ref, ys_ref, x1_ref, mod_ref, ln_ref, out_ref, buf, sem):
    nb = BATCH_PER_STEP
    group, i = pl.program_id(0), pl.program_id(1)
    step = group * tiles_per_batch + i
    slot = step % 2
    has_next = step + 1 < pl.num_programs(0) * tiles_per_batch
    wrap = i + 1 == tiles_per_batch

    def gather(grp, tile, to_slot):
        for j in range(nb):
            base = ((grp * nb + j) * tiles_per_batch + tile) * TILE
            for r in range(TILE):
                pltpu.make_async_copy(ys_ref.at[pl.ds(dest_ref[base + r], 1)], buf.at[to_slot, j, pl.ds(r, 1)],
                                      sem.at[to_slot]).start(priority=r % 2)

    @pl.when(step == 0)
    def _():
        gather(group, i, slot)

    @pl.when(has_next)
    def _():
        gather(jnp.where(wrap, group + 1, group), jnp.where(wrap, 0, i + 1), 1 - slot)

    for r in range(nb * TILE):
        pltpu.make_async_copy(ys_ref.at[pl.ds(0, 1)], buf.at[slot, 0, pl.ds(0, 1)], sem.at[slot]).wait()

    ln = ln_ref[0]
    for j in range(nb):
        out_ref[j] = _standardize(ALPHA * x1_ref[j] + mod_ref[0, j, 0, 5:6] * buf[slot, j]) * ln[0:1] + ln[1:2]


def _combine_call(layer, n_lat_tiles, dest, ys, x1, mod, ln2):
    bsz, length, _ = x1.shape
    nb = BATCH_PER_STEP
    tiles_per_batch = length // TILE
    tok = lambda w: pl.BlockSpec((nb, TILE, w), lambda b, i, *_: (b, i, 0))
    grid_spec = pltpu.PrefetchScalarGridSpec(
        num_scalar_prefetch=1,
        grid=(bsz // nb, tiles_per_batch),
        in_specs=[
            pl.BlockSpec(memory_space=pl.ANY),
            tok(D_MODEL),
            pl.BlockSpec((1, nb, 1, MOD_ROWS, D_MODEL),
                         lambda b, i, *_: (layer, b, jnp.minimum(i // n_lat_tiles, 1), 0, 0)),
            pl.BlockSpec((1, 8, D_MODEL), lambda b, i, *_: (layer, 0, 0)),
        ],
        out_specs=tok(D_MODEL),
        scratch_shapes=[pltpu.VMEM((2, nb, TILE, D_MODEL), F32), pltpu.SemaphoreType.DMA((2,))],
    )
    return pl.pallas_call(
        functools.partial(_combine_kernel, tiles_per_batch),
        grid_spec=grid_spec,
        out_shape=jax.ShapeDtypeStruct(x1.shape, F32),
        compiler_params=_params(2),
    )(dest, ys, x1, mod, ln2)


def _dispatch_plan(class_of_token, rank, counts):
    n_tok = class_of_token.shape[0]
    counts = counts.astype(jnp.int32)
    pcounts = (counts + BLOCK_M - 1) // BLOCK_M * BLOCK_M
    pends = jnp.cumsum(pcounts)
    pstarts = pends - pcounts
    classes = jnp.arange(ROUTE_W, dtype=jnp.int32)
    dest = jnp.sum(jnp.where(class_of_token[:, None] == classes[None, :], pstarts[None, :], 0), axis=1) + rank
    n_blocks = n_tok // BLOCK_M + N_CLASSES
    n_used = pends[-1:] // BLOCK_M
    first_blk = pstarts[:N_CLASSES + 1] // BLOCK_M
    return dest.astype(jnp.int32), pends, pcounts - counts, first_blk, counts, n_used.astype(jnp.int32), n_blocks * BLOCK_M


def _pad_rows(a, rows):
    return jnp.pad(a, [(0, 0)] * (a.ndim - 2) + [(0, rows - a.shape[-2]), (0, 0)])


def kernel(x, c, ctx, c_ctx, w_ada, b_ada, w_in, conv_w, ret_decay_fwd, ret_decay_bwd, sgu_w, sgu_b, w_out, ln1_g, ln1_b,
           router_group_w, router_group_b, router_expert_w, router_expert_b, moe_w_gate, moe_w_up, moe_w_down, ln2_g,
           ln2_b):
    bsz, seq, d = x.shape
    ctx_len = ctx.shape[1]
    assert d == D_MODEL and ctx_len == TILE and seq % TILE == 0 and seq % GRID_W == 0
    n_lat_tiles = seq // TILE

    cond = _pad_rows(jnp.concatenate([c, c_ctx[None, :]], axis=0), 16)
    ada = _ada_call(cond, w_ada, b_ada)
    mod_lat = ada[:, :bsz].reshape(DEPTH, bsz, N_MOD, d)
    mod_ctx = jnp.broadcast_to(ada[:, bsz].reshape(DEPTH, 1, N_MOD, d), mod_lat.shape)
    mod = _pad_rows(jnp.stack([mod_lat, mod_ctx], axis=2), MOD_ROWS)

    w_in_bf = w_in.astype(BF16)
    w_out_bf = w_out.astype(BF16)
    sgu_w_bf = sgu_w.astype(BF16)
    sgu_bias = jnp.repeat(jnp.swapaxes(sgu_b, 1, 2), SGU_W // SGU_GROUPS, axis=2)
    conv_w8 = _pad_rows(conv_w, 8)
    ln1 = _pad_rows(jnp.stack([ln1_g, ln1_b], axis=1), 8)
    ln2 = _pad_rows(jnp.stack([ln2_g, ln2_b], axis=1), 8)
    w_route = jnp.concatenate([router_group_w, jnp.swapaxes(router_expert_w, 1, 2).reshape(DEPTH, d, N_EXPERTS)], axis=2)
    w_route_bf = jnp.pad(w_route, ((0, 0), (0, 0), (0, ROUTE_W - w_route.shape[2]))).astype(BF16)
    b_route = jnp.concatenate([router_group_b, router_expert_b.reshape(DEPTH, N_EXPERTS)], axis=1)
    b_route = jnp.pad(b_route, ((0, 0), (0, ROUTE_W - b_route.shape[1])))[:, None, :].astype(F32)
    w_route_t = jnp.concatenate([_pad_rows(jnp.swapaxes(w_route[:, :, :N_GROUPS], 1, 2), EXPERT_ROW0),
                                 jnp.swapaxes(w_route[:, :, N_GROUPS:], 1, 2)], axis=1)
    w_route_t = _pad_rows(w_route_t, ROUTE_ROWS).astype(BF16)
    b_route_t = jnp.concatenate([_pad_rows(b_route[:, 0, :N_GROUPS, None], EXPERT_ROW0),
                                 b_route[:, 0, N_GROUPS:N_GROUPS + N_EXPERTS, None]], axis=1)
    b_route_t = _pad_rows(b_route_t, ROUTE_ROWS)
    lg = jnp.stack([jax.nn.log_sigmoid(ret_decay_fwd.astype(F32)), jax.nn.log_sigmoid(ret_decay_bwd.astype(F32))],
                   axis=1)
    lg_lanes = jnp.repeat(lg, RET_DK, axis=2).reshape(DEPTH * 2, RET_W)
    lg_rows = lg_lanes[:, None, :]
    lg_cols = lg_lanes[:, :, None]

    x_lat, x_ctx, ctx_tile = x, ctx, 0
    for layer in range(DEPTH):
        last = layer == DEPTH - 1
        pc, q, k, v, g, su, sv, sf, w_gate_bf = _inproj_call(layer, x_lat, x_ctx, ctx_tile, mod, w_in_bf, lg_rows, lg_cols,
                                                             n_lat_tiles, to_cast=(moe_w_gate.reshape(-1, EXPERT_HIDDEN),))
        n_proc = n_lat_tiles if last else n_lat_tiles + 1
        x1, hm, route, w_up_bf = _mix_call(layer, n_lat_tiles, n_proc, lg[layer, 0], lg[layer, 1], pc, q, k, v, g, su, sv,
                                           sf, x_lat, x_ctx, ctx_tile, mod, conv_w8, sgu_w_bf, sgu_bias, w_out_bf, ln1,
                                           w_route_t, b_route_t, lg_rows, lg_cols,
                                           to_cast=(moe_w_up.reshape(-1, EXPERT_HIDDEN),))
        class_of_token, rank, counts = _rank_call(route.reshape(-1, 8, TILE))
        dest, pad_end, pad_len, first_blk, counts, n_used, n_slots = _dispatch_plan(class_of_token.reshape(-1),
                                                                                    rank.reshape(-1), counts[:, 0])
        xs, w_down_bf = _dispatch_call(layer, dest, pad_end, pad_len, n_used, hm.reshape(-1, PACK_W), n_slots,
                                       to_cast=(moe_w_down.reshape(-1, D_MODEL),))
        ys = _expert_call(layer, first_blk, counts, xs, w_route_bf, b_route,
                          w_gate_bf.reshape(N_EXPERTS, D_MODEL, EXPERT_HIDDEN), w_up_bf.reshape(N_EXPERTS, D_MODEL, EXPERT_HIDDEN),
                          w_down_bf.reshape(N_EXPERTS, EXPERT_HIDDEN, D_MODEL))
        xa = _combine_call(layer, n_lat_tiles, dest, ys, x1, mod, ln2)
        x_lat, x_ctx, ctx_tile = xa, xa, n_lat_tiles
    return xa
```

```python
import functools
import math

import jax
import jax.numpy as jnp
import numpy as np
from jax import lax
from jax.experimental import pallas as pl
from jax.experimental.pallas import tpu as pltpu

F32 = jnp.float32
BF16 = jnp.bfloat16

D_MODEL = 1024
DEPTH = 2
GRID_W = 64
CONV_W = 256
RET_W = 512
RET_HEADS = 8
RET_DK = 64
PAIR_W = 2 * RET_DK
CHUNK = 128
SGU_W = 256
SGU_GROUPS = 4
IN_COLS = 3 * CONV_W + 4 * RET_W + 2 * SGU_W
N_GROUPS = 4
EXPERTS_PER_GROUP = 8
N_EXPERTS = N_GROUPS * EXPERTS_PER_GROUP
EXPERT_HIDDEN = 512
N_MOD = 6
MOD_ROWS = 8
LN_EPS = 1e-5
ALPHA = (2 * DEPTH) ** 0.25

TILE = 256
BATCH_PER_STEP = 2
DISPATCH_TILE = 8 * TILE
COMBINE_ROWS = 32
EXPERT_ROWS = 64
HALO = GRID_W
BLOCK_M = 256
ROUTE_W = 128
CAST_STEPS = 128
PACK_W = D_MODEL // 2
ROUTE_ROWS = 48
EXPERT_ROW0 = 8
RANK_TILES = 8
PAIRS_PER_GROUP = EXPERTS_PER_GROUP * (EXPERTS_PER_GROUP - 1) // 2
N_CLASSES = N_GROUPS * PAIRS_PER_GROUP
CLASS_LO = np.array([g * EXPERTS_PER_GROUP + lo for g in range(N_GROUPS) for lo in range(EXPERTS_PER_GROUP)
                     for hi in range(lo + 1, EXPERTS_PER_GROUP)] + [N_EXPERTS - 2] * (ROUTE_W - N_CLASSES), np.int32)
CLASS_HI = np.array([g * EXPERTS_PER_GROUP + hi for g in range(N_GROUPS) for lo in range(EXPERTS_PER_GROUP)
                     for hi in range(lo + 1, EXPERTS_PER_GROUP)] + [N_EXPERTS - 1] * (ROUTE_W - N_CLASSES), np.int32)
VMEM_LIMIT = 56 * 1024 * 1024


def _params(n_axes):
    return pltpu.CompilerParams(dimension_semantics=("arbitrary",) * n_axes, vmem_limit_bytes=VMEM_LIMIT)


def _standardize(v):
    mu = jnp.mean(v, axis=-1, keepdims=True)
    var = jnp.mean(jnp.square(v - mu), axis=-1, keepdims=True)
    return (v - mu) * lax.rsqrt(var + LN_EPS)


def _silu(v):
    return v * jax.nn.sigmoid(v)


def _pack_bf16_pairs(v):
    half = v.shape[1] // 2
    bits = lambda t: pltpu.bitcast(t.astype(BF16).astype(F32), jnp.uint32)
    return bits(v[:, :half]) | (bits(v[:, half:]) >> 16)


def _unpack_bf16_pairs(w):
    hi = pltpu.bitcast(w & jnp.uint32(0xFFFF0000), F32)
    lo = pltpu.bitcast(w << 16, F32)
    return jnp.concatenate([hi, lo], axis=1).astype(BF16)


def _ada_kernel(c_ref, w_ref, b_ref, o_ref):
    a = _silu(c_ref[...]).astype(BF16)
    o_ref[0] = jnp.dot(a, w_ref[0].astype(BF16), preferred_element_type=F32) + b_ref[0]


def _ada_call(cond, w_ada, b_ada):
    rows = cond.shape[0]
    cols = w_ada.shape[-1]
    tn = 1536
    return pl.pallas_call(
        _ada_kernel,
        grid=(DEPTH, cols // tn),
        in_specs=[
            pl.BlockSpec((rows, D_MODEL), lambda l, j: (0, 0)),
            pl.BlockSpec((1, D_MODEL, tn), lambda l, j: (l, 0, j)),
            pl.BlockSpec((1, 1, tn), lambda l, j: (l, 0, j)),
        ],
        out_specs=pl.BlockSpec((1, rows, tn), lambda l, j: (l, 0, j)),
        out_shape=jax.ShapeDtypeStruct((DEPTH, rows, cols), F32),
        compiler_params=_params(2),
    )(cond, w_ada, b_ada.reshape(DEPTH, 1, cols))


def _own_head_mask():
    row_head = lax.broadcasted_iota(jnp.int32, (RET_W, PAIR_W), 0) % PAIR_W // RET_DK
    lane_head = lax.broadcasted_iota(jnp.int32, (RET_W, PAIR_W), 1) // RET_DK
    return row_head == lane_head


def _pair_kv(kc, vc):
    contract_rows = (((0,), (0,)), ((), ()))
    return jnp.concatenate([
        lax.dot_general(kc[:, p * PAIR_W:(p + 1) * PAIR_W], vc[:, p * PAIR_W:(p + 1) * PAIR_W], contract_rows,
                        preferred_element_type=F32) for p in range(RET_W // PAIR_W)], axis=0)


def _inproj_kernel(n_lat_tiles, n_cast, x_ref, ctx_ref, mod_ref, w_ref, lgf_ref, lgfc_ref, *refs):
    cast_in, (pc_ref, q_ref, k_ref, v_ref, g_ref, su_ref, sv_ref, sf_ref), cast_out = (
        refs[:n_cast], refs[n_cast:n_cast + 8], refs[n_cast + 8:-1])
    state = refs[-1]
    for src, dst in zip(cast_in, cast_out):
        dst[...] = src[...].astype(BF16)
    step = pl.program_id(1)
    is_ctx = step == 0
    h = jnp.concatenate([
        (jnp.where(is_ctx, ctx_ref[j], x_ref[j]) * (1.0 + mod_ref[0, j, 0, 1:2]) + mod_ref[0, j, 0, 0:1]).astype(BF16)
        for j in range(BATCH_PER_STEP)], axis=0)

    def proj(lo, hi):
        return jnp.dot(h, w_ref[0, :, lo:hi], preferred_element_type=F32)

    def put(ref, val):
        for j in range(BATCH_PER_STEP):
            ref[j] = val[j * TILE:(j + 1) * TILE].astype(ref.dtype)

    o = 3 * CONV_W
    put(pc_ref, proj(0, o))
    put(q_ref, proj(o, o + RET_W))
    k = proj(o + RET_W, o + 2 * RET_W) * (RET_DK ** -0.5)
    put(k_ref, k)
    v = proj(o + 2 * RET_W, o + 3 * RET_W)
    put(v_ref, v)
    put(g_ref, proj(o + 3 * RET_W, o + 4 * RET_W))
    o += 4 * RET_W
    put(su_ref, proj(o, o + SGU_W))
    put(sv_ref, proj(o + SGU_W, o + 2 * SGU_W))

    @pl.when(step == 0)
    def _():
        state[...] = jnp.zeros_like(state)

    pos = lax.broadcasted_iota(jnp.int32, (CHUNK, 1), 0).astype(F32)
    k_decay = jnp.exp((CHUNK - 1.0 - pos) * lgf_ref[0])
    chunk_decay = jnp.exp(CHUNK * lgfc_ref[0])
    own_head = _own_head_mask()
    for j in range(BATCH_PER_STEP):
        kvs = [_pair_kv((k[rows] * k_decay).astype(BF16), v[rows].astype(BF16))
               for rows in (slice(j * TILE + c * CHUNK, j * TILE + (c + 1) * CHUNK) for c in range(TILE // CHUNK))]
        st = state[j]
        for c, kv in enumerate(kvs):
            sf_ref[j, c] = jnp.where(own_head, st, 0.0).astype(BF16)
            st = chunk_decay * st + kv
        state[j] = st


def _cast_specs(arrays, part, step_of, n_steps):
    steps = min(CAST_STEPS, 1 << (n_steps.bit_length() - 1))
    block = lambda a: (a.shape[0] // (DEPTH * steps), a.shape[1])
    slab = lambda *idx: jnp.minimum(step_of(*idx), steps - 1)
    return ([pl.BlockSpec(block(a), lambda *idx: (part * steps + slab(*idx), 0)) for a in arrays],
            [pl.BlockSpec(block(a), lambda *idx: (slab(*idx), 0)) for a in arrays],
            [jax.ShapeDtypeStruct((a.shape[0] // DEPTH, a.shape[1]), BF16) for a in arrays])


def _inproj_call(layer, x_lat, x_ctx, ctx_tile, mod, w_in_bf, lg_rows, lg_cols, n_lat_tiles, to_cast=()):
    bsz = x_lat.shape[0]
    assert bsz % BATCH_PER_STEP == 0
    nb = BATCH_PER_STEP
    length = (n_lat_tiles + 1) * TILE
    widths = (3 * CONV_W, RET_W, RET_W, RET_W, RET_W, SGU_W, SGU_W)
    cpt = TILE // CHUNK
    n_tiles = n_lat_tiles + 1
    tile_of = lambda s: (s + n_lat_tiles) % n_tiles
    cast_in, cast_out, cast_shapes = _cast_specs(to_cast, layer, lambda b, s: b * n_tiles + s, bsz // nb * n_tiles)
    return pl.pallas_call(
        functools.partial(_inproj_kernel, n_lat_tiles, len(to_cast)),
        grid=(bsz // nb, n_tiles),
        in_specs=[
            pl.BlockSpec((nb, TILE, D_MODEL), lambda b, s: (b, jnp.minimum(tile_of(s), n_lat_tiles - 1), 0)),
            pl.BlockSpec((nb, TILE, D_MODEL), lambda b, s: (b, ctx_tile, 0)),
            pl.BlockSpec((1, nb, 1, MOD_ROWS, D_MODEL), lambda b, s: (layer, b, tile_of(s) // n_lat_tiles, 0, 0)),
            pl.BlockSpec((1, D_MODEL, IN_COLS), lambda b, s: (layer, 0, 0)),
            pl.BlockSpec((1, 1, RET_W), lambda b, s: (2 * layer, 0, 0)),
            pl.BlockSpec((1, RET_W, 1), lambda b, s: (2 * layer, 0, 0)),
        ] + cast_in,
        out_specs=[pl.BlockSpec((nb, TILE, w), lambda b, s: (b, tile_of(s), 0)) for w in widths]
        + [pl.BlockSpec((nb, cpt, RET_W, PAIR_W), lambda b, s: (b, tile_of(s), 0, 0))] + cast_out,
        out_shape=[jax.ShapeDtypeStruct((bsz, length, w), BF16 if 1 <= j <= 3 else F32) for j, w in enumerate(widths)]
        + [jax.ShapeDtypeStruct((bsz, length // CHUNK, RET_W, PAIR_W), BF16)] + cast_shapes,
        scratch_shapes=[pltpu.VMEM((nb, RET_W, PAIR_W), F32)],
        compiler_params=_params(2),
    )(x_lat, x_ctx, mod, w_in_bf, lg_rows, lg_cols, *to_cast)


def _route_class(lt):
    assert EXPERTS_PER_GROUP == 8 and N_GROUPS <= 8
    sub = lax.broadcasted_iota(jnp.int32, (8, lt.shape[1]), 0)
    neg = jnp.float32(-jnp.inf)

    def top(vals):
        m = jnp.max(vals, axis=0, keepdims=True)
        return jnp.min(jnp.where(vals == m, sub, 8), axis=0, keepdims=True)

    gidx = top(jnp.where(sub < N_GROUPS, lt[0:8], neg))
    pair = jnp.zeros_like(gidx)
    for g in range(N_GROUPS):
        e = lt[EXPERT_ROW0 + 8 * g:EXPERT_ROW0 + 8 * (g + 1)]
        i1 = top(e)
        i2 = top(jnp.where(sub == i1, neg, e))
        e_lo = jnp.minimum(i1, i2)
        e_hi = jnp.maximum(i1, i2)
        pair_g = e_lo * (EXPERTS_PER_GROUP - 1) - ((e_lo * (e_lo - 1)) >> 1) + (e_hi - e_lo - 1)
        pair = jnp.where(gidx == g, pair_g, pair)
    return gidx * PAIRS_PER_GROUP + pair


def _mix_kernel(n_lat_tiles, n_cast, lgf_ref, lgb_ref, pc_ref, hp_ref, hn_ref, q_ref, k_ref, v_ref, g_ref, su_ref, sv_ref,
                sf_ref, x_ref, ctx_ref, mod_ref, convw_ref, sguw_ref, sgub_ref, wout_ref, ln_ref, wr_ref, br_ref,
                lgfr_ref, lgbr_ref, lgbc_ref, *refs):
    cast_in, (x1_ref, hm_ref, route_ref), cast_out = refs[:n_cast], refs[n_cast:n_cast + 3], refs[n_cast + 3:-4]
    ycat, dec, qdec, state = refs[-4:]
    for src, dst in zip(cast_in, cast_out):
        dst[...] = src[...].astype(BF16)
    step = pl.program_id(1)
    i = n_lat_tiles - step
    is_ctx = step == 0
    row = lax.broadcasted_iota(jnp.int32, (TILE, 1), 0)
    nb = BATCH_PER_STEP

    @pl.when(step == 0)
    def _():
        state[...] = jnp.zeros_like(state)

    pos_b = lax.broadcasted_iota(jnp.int32, (CHUNK, 1), 0).astype(F32)
    k_decay_b = jnp.exp(pos_b * lgbr_ref[0])
    chunk_decay_b = jnp.exp(CHUNK * lgbc_ref[0])
    own_head = _own_head_mask()
    sb = {}
    for j in range(nb):
        kvs = {c: _pair_kv((k_ref[j, c * CHUNK:(c + 1) * CHUNK, :].astype(F32) * k_decay_b).astype(BF16),
                           v_ref[j, c * CHUNK:(c + 1) * CHUNK, :]) for c in range(TILE // CHUNK)}
        st = state[j]
        for c in reversed(range(TILE // CHUNK)):
            sb[j, c] = jnp.where(own_head, st, 0.0).astype(BF16)
            st = chunk_decay_b * st + kvs[c]
        state[j] = st

    @pl.when((pl.program_id(0) == 0) & (step == 0))
    def _():
        pos = lax.broadcasted_iota(jnp.int32, (CHUNK, 1), 0).astype(F32)
        rel = pos - lax.broadcasted_iota(jnp.int32, (1, CHUNK), 1).astype(F32)
        for h in range(RET_HEADS):
            dec[h // 2, (h % 2) * CHUNK:(h % 2 + 1) * CHUNK, :] = jnp.where(
                rel > 0, jnp.exp(lgf_ref[h] * jnp.maximum(rel, 0.0)),
                jnp.where(rel < 0, jnp.exp(lgb_ref[h] * jnp.maximum(-rel, 0.0)), 2.0))
        qdec[0] = jnp.exp(lgfr_ref[0] * (pos + 1.0))
        qdec[1] = jnp.exp(lgbr_ref[0] * (CHUNK - pos))

    low_head = lax.broadcasted_iota(jnp.int32, (1, PAIR_W), 1) < RET_DK

    def half_norm(o):
        def half_mean(t):
            lo = jnp.sum(jnp.where(low_head, t, 0.0), axis=-1, keepdims=True)
            hi = jnp.sum(jnp.where(low_head, 0.0, t), axis=-1, keepdims=True)
            return jnp.where(low_head, lo, hi) * (1.0 / RET_DK)

        centred = o - half_mean(o)
        return centred * lax.rsqrt(half_mean(jnp.square(centred)) + LN_EPS)

    tiles = [(j, slice(c * CHUNK, (c + 1) * CHUNK), slice(p * PAIR_W, (p + 1) * PAIR_W), c, p)
             for j in range(nb) for c in range(TILE // CHUNK) for p in range(RET_W // PAIR_W)]
    out_rows = lambda j, rows: slice(j * TILE + rows.start, j * TILE + rows.stop)
    scores = []
    for j, rows, cols, c, p in tiles:
        qp = q_ref[j, rows, cols]
        zero = jnp.zeros_like(qp)
        q_split = jnp.concatenate([jnp.where(low_head, qp, zero), jnp.where(low_head, zero, qp)], axis=0)
        scores.append(lax.dot_general(q_split, k_ref[j, rows, cols],
                                      (((1,), (1,)), ((), ())), preferred_element_type=F32))

    line_mask = jnp.where(is_ctx, TILE - 1, GRID_W - 1)
    first = (row & line_mask) == 0
    last = (row & line_mask) == line_mask
    along_seq = lax.broadcasted_iota(jnp.int32, (1, CONV_W), 1) < jnp.where(is_ctx, CONV_W, CONV_W // 2)
    cw = convw_ref[0]
    for j in range(nb):
        pc = pc_ref[j]
        z = pc[:, CONV_W:2 * CONV_W] * pc[:, 2 * CONV_W:3 * CONV_W]
        z_prev = jnp.where(first, 0.0, pltpu.roll(z, 1, 0))
        z_next = jnp.where(last, 0.0, pltpu.roll(z, TILE - 1, 0))
        hp = hp_ref[j]
        hn = hn_ref[j]
        z_top = jnp.where(i == 0, 0.0, hp[:, CONV_W:2 * CONV_W] * hp[:, 2 * CONV_W:3 * CONV_W])
        z_bot = jnp.where(i == n_lat_tiles - 1, 0.0, hn[:, CONV_W:2 * CONV_W] * hn[:, 2 * CONV_W:3 * CONV_W])
        zcat = jnp.concatenate([z_top, z, z_bot], axis=0)
        z_up = zcat[0:TILE]
        z_down = zcat[2 * HALO:2 * HALO + TILE]
        conv = cw[0:1] * jnp.where(along_seq, z_prev, z_up) + cw[1:2] * z
        conv = conv + cw[2:3] * jnp.where(along_seq, z_next, z_down)
        ycat[j * TILE:(j + 1) * TILE, 0:CONV_W] = (pc[:, 0:CONV_W] * conv).astype(BF16)

    outs = []
    for (j, rows, cols, c, p), sc in zip(tiles, scores):
        qp = q_ref[j, rows, cols].astype(F32)
        vp = v_ref[j, rows, cols]
        zero = jnp.zeros_like(vp)
        sc = sc * dec[p]
        lhs = jnp.concatenate([sc[0:CHUNK].astype(BF16), sc[CHUNK:].astype(BF16),
                               (qp * qdec[0, :, cols]).astype(BF16), (qp * qdec[1, :, cols]).astype(BF16)], axis=1)
        rhs = jnp.concatenate([jnp.where(low_head, vp, zero), jnp.where(low_head, zero, vp),
                               sf_ref[j, c, cols, :], sb[j, c][cols, :]], axis=0)
        outs.append(jnp.dot(lhs, rhs, preferred_element_type=F32))

    group = lax.broadcasted_iota(jnp.int32, (1, SGU_W), 1) // (SGU_W // SGU_GROUPS)
    for j in range(nb):
        vn = _standardize(sv_ref[j]).astype(BF16)
        for c in range(TILE // CHUNK):
            rows = slice(c * CHUNK, (c + 1) * CHUNK)
            mixed = jnp.zeros((CHUNK, SGU_W), F32)
            for gi in range(SGU_GROUPS):
                m = jnp.dot(sguw_ref[0, gi], vn[rows], preferred_element_type=F32)
                mixed = jnp.where(group == gi, m, mixed)
            ycat[out_rows(j, rows), CONV_W + RET_W:] = (su_ref[j, rows, :] * (mixed + sgub_ref[0])).astype(BF16)

    for (j, rows, cols, c, p), o in zip(tiles, outs):
        ycat[out_rows(j, rows), CONV_W + p * PAIR_W:CONV_W + (p + 1) * PAIR_W] = (
            _silu(g_ref[j, rows, cols]) * half_norm(o)).astype(BF16)

    y = jnp.dot(ycat[...], wout_ref[0], preferred_element_type=F32)
    ln = ln_ref[0]
    hms = []
    for j in range(nb):
        mod = mod_ref[0, j, 0]
        x1 = _standardize(ALPHA * jnp.where(is_ctx, ctx_ref[j], x_ref[j]) + mod[2:3] * y[j * TILE:(j + 1) * TILE])
        x1 = x1 * ln[0:1] + ln[1:2]
        x1_ref[j] = x1
        hm = x1 * (1.0 + mod[4:5]) + mod[3:4]
        hm_ref[j] = _pack_bf16_pairs(hm)
        hms.append(hm.astype(BF16))
    lt = lax.dot_general(wr_ref[0], jnp.concatenate(hms, axis=0), (((1,), (1,)), ((), ())),
                         preferred_element_type=F32) + br_ref[0]
    cls = _route_class(lt).astype(F32)
    for j in range(nb):
        route_ref[j, 0] = jnp.concatenate([cls[:, j * TILE:(j + 1) * TILE], jnp.zeros((7, TILE), F32)], axis=0)


def _mix_call(layer, n_lat_tiles, n_proc, lgf, lgb, pc, q, k, v, g, su, sv, sf, x_lat, x_ctx, ctx_tile, mod, conv_w,
              sgu_w_bf, sgu_bias, w_out_bf, ln1, w_route_t, b_route_t, lg_rows, lg_cols, to_cast=()):
    bsz, length, _ = q.shape
    nb = BATCH_PER_STEP
    n_tiles = n_lat_tiles + 1
    tile_of = lambda s: n_lat_tiles - s
    cast_in, cast_out, cast_shapes = _cast_specs(to_cast, layer, lambda b, s, *_: b * n_tiles + s, bsz // nb * n_tiles)
    halos_per_tile = TILE // HALO
    n_halo = length // HALO
    cpt = TILE // CHUNK
    tok = lambda w: pl.BlockSpec((nb, TILE, w), lambda b, s, *_: (b, tile_of(s), 0))
    tok_out = lambda w: pl.BlockSpec((nb, TILE, w), lambda b, s, *_: (b, jnp.minimum(tile_of(s), n_proc - 1), 0))
    per_layer = lambda *shape: pl.BlockSpec((1,) + shape, lambda b, s, *_: (layer,) + (0,) * len(shape))
    grid_spec = pltpu.PrefetchScalarGridSpec(
        num_scalar_prefetch=2,
        grid=(bsz // nb, n_tiles),
        in_specs=[
            tok(3 * CONV_W),
            pl.BlockSpec((nb, HALO, 3 * CONV_W),
                         lambda b, s, *_: (b, jnp.maximum(tile_of(s) * halos_per_tile - 1, 0), 0)),
            pl.BlockSpec((nb, HALO, 3 * CONV_W),
                         lambda b, s, *_: (b, jnp.minimum((tile_of(s) + 1) * halos_per_tile, n_halo - 1), 0)),
            tok(RET_W), tok(RET_W), tok(RET_W), tok(RET_W), tok(SGU_W), tok(SGU_W),
            pl.BlockSpec((nb, cpt, RET_W, PAIR_W), lambda b, s, *_: (b, tile_of(s), 0, 0)),
            pl.BlockSpec((nb, TILE, D_MODEL), lambda b, s, *_: (b, jnp.minimum(tile_of(s), n_lat_tiles - 1), 0)),
            pl.BlockSpec((nb, TILE, D_MODEL), lambda b, s, *_: (b, ctx_tile, 0)),
            pl.BlockSpec((1, nb, 1, MOD_ROWS, D_MODEL),
                         lambda b, s, *_: (layer, b, tile_of(s) // n_lat_tiles, 0, 0)),
            per_layer(8, CONV_W),
            per_layer(SGU_GROUPS, CHUNK, CHUNK),
            per_layer(CHUNK, SGU_W),
            per_layer(D_MODEL, D_MODEL),
            per_layer(8, D_MODEL),
            per_layer(ROUTE_ROWS, D_MODEL),
            per_layer(ROUTE_ROWS, 1),
            pl.BlockSpec((1, 1, RET_W), lambda b, s, *_: (2 * layer, 0, 0)),
            pl.BlockSpec((1, 1, RET_W), lambda b, s, *_: (2 * layer + 1, 0, 0)),
            pl.BlockSpec((1, RET_W, 1), lambda b, s, *_: (2 * layer + 1, 0, 0)),
        ] + cast_in,
        out_specs=[tok_out(D_MODEL), tok_out(PACK_W),
                   pl.BlockSpec((nb, 1, 8, TILE), lambda b, s, *_: (b, jnp.minimum(tile_of(s), n_proc - 1), 0, 0))]
        + cast_out,
        scratch_shapes=[pltpu.VMEM((nb * TILE, D_MODEL), BF16),
                        pltpu.VMEM((RET_W // PAIR_W, 2 * CHUNK, CHUNK), F32),
                        pltpu.VMEM((2, CHUNK, RET_W), F32),
                        pltpu.VMEM((nb, RET_W, PAIR_W), F32)],
    )
    return pl.pallas_call(
        functools.partial(_mix_kernel, n_lat_tiles, len(to_cast)),
        grid_spec=grid_spec,
        out_shape=[jax.ShapeDtypeStruct((bsz, n_proc * TILE, D_MODEL), F32),
                   jax.ShapeDtypeStruct((bsz, n_proc * TILE, PACK_W), jnp.uint32),
                   jax.ShapeDtypeStruct((bsz, n_proc, 8, TILE), F32)] + cast_shapes,
        compiler_params=_params(2),
    )(lgf, lgb, pc, pc, pc, q, k, v, g, su, sv, sf, x_lat, x_ctx, mod, conv_w, sgu_w_bf, sgu_bias, w_out_bf,
      ln1, w_route_t, b_route_t, lg_rows, lg_rows, lg_cols, *to_cast)


def _rank_kernel(route_ref, class_ref, rank_ref, counts_ref, running):
    @pl.when(pl.program_id(0) == 0)
    def _():
        running[...] = jnp.zeros_like(running)

    sub = lax.broadcasted_iota(jnp.int32, (ROUTE_W, TILE), 0)
    earlier = (lax.broadcasted_iota(jnp.int32, (TILE, TILE), 0)
               < lax.broadcasted_iota(jnp.int32, (TILE, TILE), 1)).astype(BF16)
    for t in range(route_ref.shape[0]):
        cls = route_ref[t, 0:1, :].astype(jnp.int32)
        class_ref[t] = cls
        onehot = sub == cls
        before = jnp.dot(onehot.astype(BF16), earlier, preferred_element_type=F32) + running[...]
        rank_ref[t] = jnp.sum(jnp.where(onehot, before, 0.0), axis=0, keepdims=True).astype(jnp.int32)
        running[...] += jnp.sum(onehot.astype(F32), axis=1, keepdims=True)
    counts_ref[...] = jnp.broadcast_to(running[...], counts_ref.shape)


def _rank_call(route):
    n_tiles = route.shape[0]
    per_step = math.gcd(n_tiles, RANK_TILES)
    per_tile = pl.BlockSpec((per_step, 1, TILE), lambda i: (i, 0, 0))
    return pl.pallas_call(
        _rank_kernel,
        grid=(n_tiles // per_step,),
        in_specs=[pl.BlockSpec((per_step, 8, TILE), lambda i: (i, 0, 0))],
        out_specs=[per_tile, per_tile, pl.BlockSpec((ROUTE_W, ROUTE_W), lambda i: (0, 0))],
        out_shape=[jax.ShapeDtypeStruct((n_tiles, 1, TILE), jnp.int32), jax.ShapeDtypeStruct((n_tiles, 1, TILE), jnp.int32),
                   jax.ShapeDtypeStruct((ROUTE_W, ROUTE_W), F32)],
        scratch_shapes=[pltpu.VMEM((ROUTE_W, 1), F32)],
        compiler_params=_params(1),
    )(route)


def _dispatch_kernel(n_cast, dest_ref, pad_end_ref, pad_len_ref, n_used_ref, hm_ref, *refs):
    cast_in, xs_ref, cast_out = refs[:n_cast], refs[n_cast], refs[n_cast + 1:-3]
    zeros, sem, pad_sem = refs[-3:]
    for src, dst in zip(cast_in, cast_out):
        dst[...] = src[...].astype(BF16)
    step = pl.program_id(0)
    base = step * DISPATCH_TILE
    half = BLOCK_M // 2

    def for_each_pad_copy(fn):
        def per_class(c, carry):
            off = pad_end_ref[c]
            n = pad_len_ref[c]
            for shift in range(BLOCK_M.bit_length() - 2, -1, -1):
                bit = 1 << shift
                off = off - (n & bit)

                @pl.when((n & bit) != 0)
                def _():
                    if bit >= 8:
                        fn(pltpu.make_async_copy(zeros.at[pl.ds(0, bit)], xs_ref.at[pl.ds(pl.multiple_of(off, 8), bit)],
                                                 pad_sem))
                    else:
                        for j in range(bit):
                            fn(pltpu.make_async_copy(zeros.at[pl.ds(0, 1)], xs_ref.at[pl.ds(off + j, 1)], pad_sem))

            return carry

        lax.fori_loop(0, N_CLASSES, per_class, 0)

        def per_half_block(j, carry):
            fn(pltpu.make_async_copy(zeros, xs_ref.at[pl.ds(pl.multiple_of(j * half, 8), half)], pad_sem))
            return carry

        lax.fori_loop(n_used_ref[0] * 2, xs_ref.shape[0] // half, per_half_block, 0)

    @pl.when(step == 0)
    def _():
        zeros[...] = jnp.zeros_like(zeros)
        for_each_pad_copy(lambda cp: cp.start())

    for r in range(DISPATCH_TILE):
        pltpu.make_async_copy(hm_ref.at[pl.ds(r, 1)], xs_ref.at[pl.ds(dest_ref[base + r], 1)],
                              sem).start(priority=r % 2)
    for r in range(DISPATCH_TILE):
        pltpu.make_async_copy(hm_ref.at[pl.ds(0, 1)], xs_ref.at[pl.ds(0, 1)], sem).wait()

    @pl.when(step == pl.num_programs(0) - 1)
    def _():
        for_each_pad_copy(lambda cp: cp.wait())


def _dispatch_call(layer, dest, pad_end, pad_len, n_used, hm_flat, n_slots, to_cast=()):
    n_tok = hm_flat.shape[0]
    assert n_tok % DISPATCH_TILE == 0
    cast_in, cast_out, cast_shapes = _cast_specs(to_cast, layer, lambda i, *_: i, n_tok // DISPATCH_TILE)
    grid_spec = pltpu.PrefetchScalarGridSpec(
        num_scalar_prefetch=4,
        grid=(n_tok // DISPATCH_TILE,),
        in_specs=[pl.BlockSpec((DISPATCH_TILE, PACK_W), lambda i, *_: (i, 0))] + cast_in,
        out_specs=[pl.BlockSpec(memory_space=pl.ANY)] + cast_out,
        scratch_shapes=[pltpu.VMEM((BLOCK_M // 2, PACK_W), jnp.uint32), pltpu.SemaphoreType.DMA,
                        pltpu.SemaphoreType.DMA],
    )
    return pl.pallas_call(
        functools.partial(_dispatch_kernel, len(to_cast)),
        grid_spec=grid_spec,
        out_shape=[jax.ShapeDtypeStruct((n_slots, PACK_W), jnp.uint32)] + cast_shapes,
        compiler_params=_params(1),
    )(dest, pad_end, pad_len, n_used, hm_flat, *to_cast)


def _expert_kernel(first_blk_ref, count_ref, cls_lo_ref, cls_hi_ref, xs_ref, wr_ref, br_ref, wg_lo, wu_lo, wd_lo, wg_hi,
                   wu_hi, wd_hi, ys_ref, xbuf, ybuf, xsem, ysem):
    c = pl.program_id(0)
    n_used = first_blk_ref[N_CLASSES]
    n_blocks = ys_ref.shape[0] // BLOCK_M
    rows_of = lambda g: pl.ds(pl.multiple_of(g * BLOCK_M, BLOCK_M), BLOCK_M)
    x_copy = lambda g, slot: pltpu.make_async_copy(xs_ref.at[rows_of(g)], xbuf.at[slot], xsem.at[slot])
    y_copy = lambda g, slot: pltpu.make_async_copy(ybuf.at[slot], ys_ref.at[rows_of(g)], ysem.at[slot])

    @pl.when(c == 0)
    def _():
        x_copy(0, 0).start()

    def block(g, carry):
        slot = g % 2
        x_copy(g, slot).wait()

        @pl.when(g + 1 < n_used)
        def _():
            x_copy(g + 1, 1 - slot).start()

        @pl.when(g >= 2)
        def _():
            y_copy(g - 2, slot).wait()

        def experts_on(rows):
            xb = _unpack_bf16_pairs(xbuf[slot, 0:rows, :])
            y = two_experts(xb)
            ybuf[slot, 0:rows, :] = y
            if rows < BLOCK_M:
                ybuf[slot, rows:, :] = jnp.zeros((BLOCK_M - rows, D_MODEL), F32)

        live = count_ref[c] - (g - first_blk_ref[c]) * BLOCK_M
        for rows in range(EXPERT_ROWS, BLOCK_M + 1, EXPERT_ROWS):
            pl.when((live > rows - EXPERT_ROWS) & ((live <= rows) | (rows == BLOCK_M)))(
                functools.partial(experts_on, rows))
        y_copy(g, slot).start()
        return carry

    def two_experts(xb):
        logits = jnp.dot(xb, wr_ref[0], preferred_element_type=F32) + br_ref[0]
        lane = lax.broadcasted_iota(jnp.int32, logits.shape, 1)
        gl = jnp.where(lane < N_GROUPS, logits, -jnp.inf)
        g_prob = 1.0 / jnp.sum(jnp.exp(gl - jnp.max(gl, axis=-1, keepdims=True)), axis=-1, keepdims=True)
        l_lo = jnp.sum(jnp.where(lane == N_GROUPS + cls_lo_ref[c], logits, 0.0), axis=-1, keepdims=True)
        l_hi = jnp.sum(jnp.where(lane == N_GROUPS + cls_hi_ref[c], logits, 0.0), axis=-1, keepdims=True)
        m = jnp.maximum(l_lo, l_hi)
        p_lo = jnp.exp(l_lo - m)
        p_hi = jnp.exp(l_hi - m)

        h_lo = jnp.dot(xb, wg_lo[0], preferred_element_type=F32)
        u_lo = jnp.dot(xb, wu_lo[0], preferred_element_type=F32)
        h_hi = jnp.dot(xb, wg_hi[0], preferred_element_type=F32)
        u_hi = jnp.dot(xb, wu_hi[0], preferred_element_type=F32)
        a_lo = (_silu(h_lo) * u_lo).astype(BF16)
        a_hi = (_silu(h_hi) * u_hi).astype(BF16)
        y_lo = jnp.dot(a_lo, wd_lo[0], preferred_element_type=F32)
        y_hi = jnp.dot(a_hi, wd_hi[0], preferred_element_type=F32)
        return y_lo * (g_prob * (p_lo / (p_lo + p_hi))) + y_hi * (g_prob * (p_hi / (p_lo + p_hi)))

    lax.fori_loop(first_blk_ref[c], first_blk_ref[c + 1], block, 0)

    @pl.when(c == pl.num_programs(0) - 1)
    def _():
        @pl.when(n_used >= 2)
        def _():
            y_copy(n_used - 2, n_used % 2).wait()

        y_copy(n_used - 1, (n_used - 1) % 2).wait()
        ybuf[0] = jnp.zeros((BLOCK_M, D_MODEL), F32)

        def fill(g, carry):
            y_copy(g, 0).start()
            return carry

        def fill_done(g, carry):
            y_copy(g, 0).wait()
            return carry

        lax.fori_loop(n_used, n_blocks, fill, 0)
        lax.fori_loop(n_used, n_blocks, fill_done, 0)


def _expert_call(layer, first_blk, counts, xs, w_route_bf, b_route, w_gate_bf, w_up_bf, w_down_bf):
    up_spec = lambda table: pl.BlockSpec((1, D_MODEL, EXPERT_HIDDEN), lambda c, fb, n, lo, hi: ((lo, hi)[table][c], 0, 0))
    down_spec = lambda table: pl.BlockSpec((1, EXPERT_HIDDEN, D_MODEL), lambda c, fb, n, lo, hi: ((lo, hi)[table][c], 0, 0))
    grid_spec = pltpu.PrefetchScalarGridSpec(
        num_scalar_prefetch=4,
        grid=(N_CLASSES,),
        in_specs=[pl.BlockSpec(memory_space=pl.ANY),
                  pl.BlockSpec((1, D_MODEL, ROUTE_W), lambda c, *_: (layer, 0, 0)),
                  pl.BlockSpec((1, 1, ROUTE_W), lambda c, *_: (layer, 0, 0)),
                  up_spec(0), up_spec(0), down_spec(0), up_spec(1), up_spec(1), down_spec(1)],
        out_specs=pl.BlockSpec(memory_space=pl.ANY),
        scratch_shapes=[pltpu.VMEM((2, BLOCK_M, PACK_W), jnp.uint32), pltpu.VMEM((2, BLOCK_M, D_MODEL), F32),
                        pltpu.SemaphoreType.DMA((2,)), pltpu.SemaphoreType.DMA((2,))],
    )
    return pl.pallas_call(
        _expert_kernel,
        grid_spec=grid_spec,
        out_shape=jax.ShapeDtypeStruct((xs.shape[0], D_MODEL), F32),
        compiler_params=_params(1),
    )(first_blk, counts, jnp.asarray(CLASS_LO), jnp.asarray(CLASS_HI), xs, w_route_bf, b_route, w_gate_bf, w_up_bf,
      w_down_bf, w_gate_bf, w_up_bf, w_down_bf)


def _combine_kernel(tiles_per_batch, dest_ref, ys_ref, x1_ref, mod_ref, ln_ref, out_ref, buf, sem):
    nb = BATCH_PER_STEP
    n_steps = pl.num_programs(0) * tiles_per_batch
    step = pl.program_id(0) * tiles_per_batch + pl.program_id(1)
    slot = step % 3
    ahead = jnp.minimum(step + 2, n_steps - 1)
    ahead_slot = (step + 2) % 3

    def gather(of_step, to_slot, j, rows):
        base = (((of_step // tiles_per_batch) * nb + j) * tiles_per_batch + of_step % tiles_per_batch) * TILE
        for r in rows:
            pltpu.make_async_copy(ys_ref.at[pl.ds(dest_ref[base + r], 1)], buf.at[to_slot, j, pl.ds(r, 1)],
                                  sem.at[to_slot]).start(priority=r % 2)

    def drain(from_slot):
        for r in range(nb * TILE):
            pltpu.make_async_copy(ys_ref.at[pl.ds(0, 1)], buf.at[from_slot, 0, pl.ds(0, 1)], sem.at[from_slot]).wait()

    @pl.when(step == 0)
    def _():
        for j in range(nb):
            gather(0, 0, j, range(TILE))
            gather(1, 1, j, range(TILE))

    drain(slot)
    ln = ln_ref[0]
    for j in range(nb):
        for r0 in range(0, TILE, COMBINE_ROWS):
            rows = slice(r0, r0 + COMBINE_ROWS)
            out_ref[j, rows, :] = _standardize(ALPHA * x1_ref[j, rows, :] + mod_ref[0, j, 0, 5:6] * buf[slot, j, rows, :]
                                               ) * ln[0:1] + ln[1:2]
            gather(ahead, ahead_slot, j, range(r0, r0 + COMBINE_ROWS))

    @pl.when(step == n_steps - 1)
    def _():
        drain((step + 1) % 3)
        drain((step + 2) % 3)


def _combine_call(layer, n_lat_tiles, dest, ys, x1, mod, ln2):
    bsz, length, _ = x1.shape
    nb = BATCH_PER_STEP
    tiles_per_batch = length // TILE
    assert bsz // nb * tiles_per_batch >= 3
    tok = lambda w: pl.BlockSpec((nb, TILE, w), lambda b, i, *_: (b, i, 0))
    grid_spec = pltpu.PrefetchScalarGridSpec(
        num_scalar_prefetch=1,
        grid=(bsz // nb, tiles_per_batch),
        in_specs=[
            pl.BlockSpec(memory_space=pl.ANY),
            tok(D_MODEL),
            pl.BlockSpec((1, nb, 1, MOD_ROWS, D_MODEL),
                         lambda b, i, *_: (layer, b, jnp.minimum(i // n_lat_tiles, 1), 0, 0)),
            pl.BlockSpec((1, 8, D_MODEL), lambda b, i, *_: (layer, 0, 0)),
        ],
        out_specs=tok(D_MODEL),
        scratch_shapes=[pltpu.VMEM((3, nb, TILE, D_MODEL), F32), pltpu.SemaphoreType.DMA((3,))],
    )
    return pl.pallas_call(
        functools.partial(_combine_kernel, tiles_per_batch),
        grid_spec=grid_spec,
        out_shape=jax.ShapeDtypeStruct(x1.shape, F32),
        compiler_params=_params(2),
    )(dest, ys, x1, mod, ln2)


def _dispatch_plan(class_of_token, rank, counts):
    n_tok = class_of_token.shape[0]
    counts = counts.astype(jnp.int32)
    pcounts = (counts + BLOCK_M - 1) // BLOCK_M * BLOCK_M
    pends = jnp.cumsum(pcounts)
    pstarts = pends - pcounts
    classes = jnp.arange(ROUTE_W, dtype=jnp.int32)
    dest = jnp.sum(jnp.where(class_of_token[:, None] == classes[None, :], pstarts[None, :], 0), axis=1) + rank
    n_blocks = n_tok // BLOCK_M + N_CLASSES
    n_used = pends[-1:] // BLOCK_M
    first_blk = pstarts[:N_CLASSES + 1] // BLOCK_M
    return dest.astype(jnp.int32), pends, pcounts - counts, first_blk, counts, n_used.astype(jnp.int32), n_blocks * BLOCK_M


def _pad_rows(a, rows):
    return jnp.pad(a, [(0, 0)] * (a.ndim - 2) + [(0, rows - a.shape[-2]), (0, 0)])


def kernel(x, c, ctx, c_ctx, w_ada, b_ada, w_in, conv_w, ret_decay_fwd, ret_decay_bwd, sgu_w, sgu_b, w_out, ln1_g, ln1_b,
           router_group_w, router_group_b, router_expert_w, router_expert_b, moe_w_gate, moe_w_up, moe_w_down, ln2_g,
           ln2_b):
    bsz, seq, d = x.shape
    ctx_len = ctx.shape[1]
    assert d == D_MODEL and ctx_len == TILE and seq % TILE == 0 and seq % GRID_W == 0
    n_lat_tiles = seq // TILE

    cond = _pad_rows(jnp.concatenate([c, c_ctx[None, :]], axis=0), 16)
    ada = _ada_call(cond, w_ada, b_ada)
    mod_lat = ada[:, :bsz].reshape(DEPTH, bsz, N_MOD, d)
    mod_ctx = jnp.broadcast_to(ada[:, bsz].reshape(DEPTH, 1, N_MOD, d), mod_lat.shape)
    mod = _pad_rows(jnp.stack([mod_lat, mod_ctx], axis=2), MOD_ROWS)

    w_in_bf = w_in.astype(BF16)
    w_out_bf = w_out.astype(BF16)
    sgu_w_bf = sgu_w.astype(BF16)
    sgu_bias = jnp.repeat(jnp.swapaxes(sgu_b, 1, 2), SGU_W // SGU_GROUPS, axis=2)
    conv_w8 = _pad_rows(conv_w, 8)
    ln1 = _pad_rows(jnp.stack([ln1_g, ln1_b], axis=1), 8)
    ln2 = _pad_rows(jnp.stack([ln2_g, ln2_b], axis=1), 8)
    w_route = jnp.concatenate([router_group_w, jnp.swapaxes(router_expert_w, 1, 2).reshape(DEPTH, d, N_EXPERTS)], axis=2)
    w_route_bf = jnp.pad(w_route, ((0, 0), (0, 0), (0, ROUTE_W - w_route.shape[2]))).astype(BF16)
    b_route = jnp.concatenate([router_group_b, router_expert_b.reshape(DEPTH, N_EXPERTS)], axis=1)
    b_route = jnp.pad(b_route, ((0, 0), (0, ROUTE_W - b_route.shape[1])))[:, None, :].astype(F32)
    w_route_t = jnp.concatenate([_pad_rows(jnp.swapaxes(w_route[:, :, :N_GROUPS], 1, 2), EXPERT_ROW0),
                                 jnp.swapaxes(w_route[:, :, N_GROUPS:], 1, 2)], axis=1)
    w_route_t = _pad_rows(w_route_t, ROUTE_ROWS).astype(BF16)
    b_route_t = jnp.concatenate([_pad_rows(b_route[:, 0, :N_GROUPS, None], EXPERT_ROW0),
                                 b_route[:, 0, N_GROUPS:N_GROUPS + N_EXPERTS, None]], axis=1)
    b_route_t = _pad_rows(b_route_t, ROUTE_ROWS)
    lg = jnp.stack([jax.nn.log_sigmoid(ret_decay_fwd.astype(F32)), jax.nn.log_sigmoid(ret_decay_bwd.astype(F32))],
                   axis=1)
    lg_lanes = jnp.repeat(lg, RET_DK, axis=2).reshape(DEPTH * 2, RET_W)
    lg_rows = lg_lanes[:, None, :]
    lg_cols = lg_lanes[:, :, None]

    x_lat, x_ctx, ctx_tile = x, ctx, 0
    for layer in range(DEPTH):
        last = layer == DEPTH - 1
        pc, q, k, v, g, su, sv, sf, w_gate_bf = _inproj_call(layer, x_lat, x_ctx, ctx_tile, mod, w_in_bf, lg_rows, lg_cols,
                                                             n_lat_tiles, to_cast=(moe_w_gate.reshape(-1, EXPERT_HIDDEN),))
        n_proc = n_lat_tiles if last else n_lat_tiles + 1
        x1, hm, route, w_up_bf = _mix_call(layer, n_lat_tiles, n_proc, lg[layer, 0], lg[layer, 1], pc, q, k, v, g, su, sv,
                                           sf, x_lat, x_ctx, ctx_tile, mod, conv_w8, sgu_w_bf, sgu_bias, w_out_bf, ln1,
                                           w_route_t, b_route_t, lg_rows, lg_cols,
                                           to_cast=(moe_w_up.reshape(-1, EXPERT_HIDDEN),))
        class_of_token, rank, counts = _rank_call(route.reshape(-1, 8, TILE))
        dest, pad_end, pad_len, first_blk, counts, n_used, n_slots = _dispatch_plan(class_of_token.reshape(-1),
                                                                                    rank.reshape(-1), counts[:, 0])
        xs, w_down_bf = _dispatch_call(layer, dest, pad_end, pad_len, n_used, hm.reshape(-1, PACK_W), n_slots,
                                       to_cast=(moe_w_down.reshape(-1, D_MODEL),))
        ys = _expert_call(layer, first_blk, counts, xs, w_route_bf, b_route,
                          w_gate_bf.reshape(N_EXPERTS, D_MODEL, EXPERT_HIDDEN), w_up_bf.reshape(N_EXPERTS, D_MODEL, EXPERT_HIDDEN),
                          w_down_bf.reshape(N_EXPERTS, EXPERT_HIDDEN, D_MODEL))
        xa = _combine_call(layer, n_lat_tiles, dest, ys, x1, mod, ln2)
        x_lat, x_ctx, ctx_tile = xa, xa, n_lat_tiles
    return xa
```
